```python
import math
import jax
import jax.numpy as jnp
from jax import lax
import numpy as np

D_MODEL = 1024
BATCH = 4
SEQ = 4096
DEPTH = 1

MIX_W = D_MODEL
RET_HEADS = 4
RET_V_W = MIX_W // 2
RET_VAL_DIM = RET_V_W // RET_HEADS
RET_KEY_DIM = RET_VAL_DIM // 2
RET_QK_W = RET_HEADS * RET_KEY_DIM
RET_CHUNK = 128
DIFF_HEADS = 4
DIFF_V_W = MIX_W - RET_V_W
DIFF_HEAD_DIM = DIFF_V_W // (2 * DIFF_HEADS)
DIFF_QK_W = DIFF_HEADS * 2 * DIFF_HEAD_DIM
Q_BLOCK = 128
IN_SIZES = (RET_QK_W, RET_QK_W, RET_V_W, RET_V_W, DIFF_QK_W, DIFF_QK_W, DIFF_V_W)
IN_W = sum(IN_SIZES)
REL_BUCKETS = 32
REL_MAX_DIST = 128
N_EXPERTS = 32
TOP_K = 4
D_FF = D_MODEL
SWIGLU_ALPHA = 1.702
SWIGLU_LIMIT = 7.0
MOE_BLOCK = 128
NORM_EPS = 1e-6

kernel_name = "hymba_style_retention_diffattn_moe_adaln"


def rms_norm(x, w=None):
    xf = x.astype(jnp.float32)
    y = xf * lax.rsqrt(jnp.mean(xf * xf, axis=-1, keepdims=True) + NORM_EPS)
    if w is not None:
        y = y * w
    return y.astype(x.dtype)


def rotate_every_two(x):
    x1 = x[..., ::2]
    x2 = x[..., 1::2]
    return jnp.stack((-x2, x1), axis=-1).reshape(x.shape)


def retention(q, k, v):
    B, S, H, dk = q.shape
    dv = v.shape[-1]
    C = RET_CHUNK
    N = S // C
    pos = jnp.arange(S, dtype=jnp.float32)
    inv_freq = 1.0 / (10000.0 ** jnp.linspace(0.0, 1.0, dk // 2, dtype=jnp.float32))
    ang = pos[:, None] * jnp.repeat(inv_freq, 2)[None, :]
    sin = jnp.sin(ang)[:, None, :].astype(q.dtype)
    cos = jnp.cos(ang)[:, None, :].astype(q.dtype)
    q = q * cos + rotate_every_two(q) * sin
    k = (k * cos + rotate_every_two(k) * sin) * (dk ** -0.5)
    log_g = jnp.log(1.0 - 2.0 ** (-5.0 - jnp.arange(H, dtype=jnp.float32)))
    i = jnp.arange(C, dtype=jnp.float32)
    rel = i[:, None] - i[None, :]
    dmask = jnp.where(rel[None] >= 0,
                      jnp.exp(jnp.maximum(rel, 0.0)[None] * log_g[:, None, None]),
                      0.0).astype(q.dtype)
    zeta = jnp.exp((C - 1.0 - i)[:, None] * log_g[None, :]).astype(q.dtype)
    xi = jnp.exp((i + 1.0)[:, None] * log_g[None, :]).astype(q.dtype)
    g_chunk = jnp.exp(C * log_g).astype(q.dtype)[:, None, None]
    qc = q.reshape(B, N, C, H, dk)
    kc = k.reshape(B, N, C, H, dk)
    vc = v.reshape(B, N, C, H, dv)
    scores = jnp.einsum('bnihd,bnjhd->bnhij', qc, kc) * dmask
    inner = jnp.einsum('bnhij,bnjhe->bnihe', scores, vc)
    kv = jnp.einsum('bnjhd,bnjhe->nbhde', kc * zeta[:, :, None], vc)

    def step(state, kv_n):
        return state * g_chunk + kv_n, state

    _, prev = lax.scan(step, jnp.zeros((B, H, dk, dv), kv.dtype), kv)
    cross = jnp.einsum('bnihd,nbhde->bnihe', qc * xi[:, :, None], prev)
    return (inner + cross).reshape(B, S, H, dv)


def t5_bucket(rel):
    n = jnp.maximum(rel, 0)
    max_exact = REL_BUCKETS // 2
    nf = jnp.maximum(n, 1).astype(jnp.float32)
    large = max_exact + (jnp.log(nf / max_exact) / math.log(REL_MAX_DIST / max_exact)
                         * (REL_BUCKETS - max_exact)).astype(jnp.int32)
    large = jnp.minimum(large, REL_BUCKETS - 1)
    return jnp.where(n < max_exact, n, large)


def diff_attention(q, k, v, lam, rel_bias):
    B, S, H, _, d = q.shape
    N = S // Q_BLOCK
    q = q * (d ** -0.5)
    qb = jnp.moveaxis(q.reshape(B, N, Q_BLOCK, H, 2, d), 1, 0)
    starts = jnp.arange(N, dtype=jnp.int32) * Q_BLOCK
    k_pos = jnp.arange(S, dtype=jnp.int32)

    def block(args):
        qblk, start = args
        q_pos = start + jnp.arange(Q_BLOCK, dtype=jnp.int32)
        rel = q_pos[:, None] - k_pos[None, :]
        bias = jnp.transpose(rel_bias[t5_bucket(rel)], (2, 0, 1)).astype(jnp.float32)
        s = jnp.einsum('bqhmd,bkhmd->bhmqk', qblk, k).astype(jnp.float32)
        s = s + bias[None, :, None]
        s = jnp.where(rel[None, None, None] >= 0, s, -jnp.inf)
        p = jax.nn.softmax(s, axis=-1)
        a = p[:, :, 0] - lam * p[:, :, 1]
        return jnp.einsum('bhqk,bkhe->bqhe', a.astype(v.dtype), v)

    out = lax.map(block, (qb, starts))
    return jnp.moveaxis(out, 0, 1).reshape(B, S, H, 2 * d)


def swiglu_clamped(h):
    glu = jnp.minimum(h[..., ::2], SWIGLU_LIMIT)
    lin = jnp.clip(h[..., 1::2], -SWIGLU_LIMIT, SWIGLU_LIMIT)
    return glu * jax.nn.sigmoid(SWIGLU_ALPHA * glu) * (lin + 1.0)


def moe_ffn(h, w_router, b_router, w1, b1, w2, b2):
    B, S, D = h.shape
    T = B * S
    xt = h.reshape(T, D)
    logits = (xt @ w_router + b_router).astype(jnp.float32)
    top_v, top_i = lax.top_k(logits, TOP_K)
    gate = jax.nn.softmax(top_v, axis=-1)
    TK = T * TOP_K
    flat_e = top_i.reshape(TK)
    flat_tok = jnp.arange(TK, dtype=jnp.int32) // TOP_K
    flat_w = gate.reshape(TK)
    order = jnp.argsort(flat_e)
    sorted_e = flat_e[order]
    counts = jnp.bincount(flat_e, length=N_EXPERTS)
    padded = (counts + MOE_BLOCK - 1) // MOE_BLOCK * MOE_BLOCK
    pad_end = jnp.cumsum(padded)
    pad_start = pad_end - padded
    sort_start = jnp.cumsum(counts) - counts
    rank = jnp.arange(TK, dtype=jnp.int32) - sort_start[sorted_e]
    dest = pad_start[sorted_e] + rank
    P = TK + N_EXPERTS * MOE_BLOCK
    nb = P // MOE_BLOCK
    slot_tok = jnp.zeros((P,), jnp.int32).at[dest].set(flat_tok[order])
    slot_w = jnp.zeros((P,), jnp.float32).at[dest].set(flat_w[order])
    block_e = jnp.minimum(
        jnp.searchsorted(pad_end, jnp.arange(nb, dtype=pad_end.dtype) * MOE_BLOCK, side='right'),
        N_EXPERTS - 1)
    xs = xt[slot_tok].reshape(nb, MOE_BLOCK, D)

    def expert_block(args):
        xb, e = args
        hid = xb @ w1[e] + b1[e]
        return swiglu_clamped(hid) @ w2[e] + b2[e]

    ys = lax.map(expert_block, (xs, block_e)).reshape(P, D)
    out = jax.ops.segment_sum(ys.astype(jnp.float32) * slot_w[:, None], slot_tok, num_segments=T)
    return out.astype(h.dtype).reshape(B, S, D)


def setup_inputs(seed: int = 0) -> dict:
    key = jax.random.key(seed)
    ks = jax.random.split(key, 22)
    D, E, F = D_MODEL, N_EXPERTS, D_FF
    nrm = lambda k, shape, s: jax.random.normal(k, shape, jnp.float32) * s
    return {
        "x": nrm(ks[0], (BATCH, SEQ, D), 1.0),
        "c": nrm(ks[1], (BATCH, D), 1.0),
        "w_ada": nrm(ks[2], (DEPTH, D, 6 * D), 0.5 * D ** -0.5),
        "b_ada": nrm(ks[3], (DEPTH, 6 * D), 0.01),
        "norm1_w": 1.0 + nrm(ks[4], (DEPTH, D), 0.02),
        "w_in": nrm(ks[5], (DEPTH, D, IN_W), D ** -0.5),
        "lam_q1": nrm(ks[6], (DEPTH, DIFF_HEAD_DIM), 0.1),
        "lam_k1": nrm(ks[7], (DEPTH, DIFF_HEAD_DIM), 0.1),
        "lam_q2": nrm(ks[8], (DEPTH, DIFF_HEAD_DIM), 0.1),
        "lam_k2": nrm(ks[9], (DEPTH, DIFF_HEAD_DIM), 0.1),
        "subln_w": 1.0 + nrm(ks[10], (DEPTH, 2 * DIFF_HEAD_DIM), 0.02),
        "rel_bias": nrm(ks[11], (REL_BUCKETS, DIFF_HEADS), 0.5),
        "w_out": nrm(ks[12], (DEPTH, MIX_W, D), MIX_W ** -0.5),
        "norm2_w": 1.0 + nrm(ks[13], (DEPTH, D), 0.02),
        "w_router": nrm(ks[14], (DEPTH, D, E), D ** -0.5),
        "b_router": nrm(ks[15], (DEPTH, E), 0.01),
        "w1": nrm(ks[16], (DEPTH, E, D, 2 * F), D ** -0.5),
        "b1": nrm(ks[17], (DEPTH, E, 2 * F), 0.01),
        "w2": nrm(ks[18], (DEPTH, E, F, D), F ** -0.5),
        "b2": nrm(ks[19], (DEPTH, E, D), 0.01),
        "normf_w": 1.0 + nrm(ks[20], (D,), 0.02),
    }


def reference(x, c, w_ada, b_ada, norm1_w, w_in, lam_q1, lam_k1, lam_q2, lam_k2, subln_w,
              rel_bias, w_out, norm2_w, w_router, b_router, w1, b1, w2, b2, normf_w):
    B, S, D = x.shape
    split_idx = list(np.cumsum(IN_SIZES)[:-1])
    cond = jax.nn.silu(c)
    for l in range(DEPTH):
        mod = (cond @ w_ada[l] + b_ada[l]).reshape(B, 6, D)
        shift1, scale1, gate1, shift2, scale2, gate2 = [mod[:, j][:, None, :] for j in range(6)]

        h = rms_norm(x, norm1_w[l]) * (1.0 + scale1) + shift1
        proj = h @ w_in[l]
        q_r, k_r, v_r, g_r, q_d, k_d, v_d = jnp.split(proj, split_idx, axis=-1)

        y_r = retention(q_r.reshape(B, S, RET_HEADS, RET_KEY_DIM),
                        k_r.reshape(B, S, RET_HEADS, RET_KEY_DIM),
                        v_r.reshape(B, S, RET_HEADS, RET_VAL_DIM))
        y_r = jax.nn.silu(g_r) * rms_norm(y_r).reshape(B, S, RET_V_W)

        lambda_init = 0.8 - 0.6 * math.exp(-0.3 * l)
        lam = (jnp.exp(jnp.sum(lam_q1[l] * lam_k1[l]).astype(jnp.float32))
               - jnp.exp(jnp.sum(lam_q2[l] * lam_k2[l]).astype(jnp.float32)) + lambda_init)
        y_d = diff_attention(q_d.reshape(B, S, DIFF_HEADS, 2, DIFF_HEAD_DIM),
                             k_d.reshape(B, S, DIFF_HEADS, 2, DIFF_HEAD_DIM),
                             v_d.reshape(B, S, DIFF_HEADS, 2 * DIFF_HEAD_DIM),
                             lam, rel_bias)
        y_d = (rms_norm(y_d, subln_w[l]) * (1.0 - lambda_init)).reshape(B, S, DIFF_V_W)

        mixed = jnp.concatenate([y_r.astype(x.dtype), y_d.astype(x.dtype)], axis=-1) @ w_out[l]
        x = x + gate1 * mixed

        h2 = rms_norm(x, norm2_w[l]) * (1.0 + scale2) + shift2
        x = x + gate2 * moe_ffn(h2, w_router[l], b_router[l], w1[l], b1[l], w2[l], b2[l])
    return rms_norm(x, normf_w)
```

```python
import functools
import math

import jax
import jax.numpy as jnp
from jax import lax
from jax.experimental import pallas as pl
from jax.experimental.pallas import tpu as pltpu

F32 = jnp.float32
BF16 = jnp.bfloat16

D_MODEL = 1024
RET_HEADS = 4
RET_KEY_DIM = 64
RET_VAL_DIM = 128
RET_QK_W = RET_HEADS * RET_KEY_DIM
RET_V_W = RET_HEADS * RET_VAL_DIM
RET_CHUNK = 128
DIFF_HEADS = 4
DIFF_HEAD_DIM = 64
DIFF_QK_W = DIFF_HEADS * 2 * DIFF_HEAD_DIM
DIFF_V_W = DIFF_HEADS * 2 * DIFF_HEAD_DIM
IN_SIZES = (RET_QK_W, RET_QK_W, RET_V_W, RET_V_W, DIFF_QK_W, DIFF_QK_W, DIFF_V_W)
REL_BUCKETS = 32
REL_MAX_DIST = 128
N_EXPERTS = 32
TOP_K = 4
SWIGLU_ALPHA = 1.702
SWIGLU_LIMIT = 7.0
NORM_EPS = 1e-6
LAMBDA_INIT = 0.8 - 0.6 * math.exp(-0.3 * 0)

LANES = 128
NEG_BIG = -1e30
VMEM_LIMIT = 56 * 1024 * 1024

ROW_TILE = 512
RET_ROWS = 512
ATT_BLOCK = 256
EXPERT_ROWS = 256
DISPATCH_TOKENS = 256
COMBINE_TOKENS = 128


def _rms(x):
    return x * lax.rsqrt(jnp.mean(x * x, axis=-1, keepdims=True) + NORM_EPS)


def _ada_kernel(c_ref, w_ref, b_ref, o_ref):
    c = c_ref[...]
    cond = c * jax.nn.sigmoid(c)
    o_ref[...] = jnp.dot(cond, w_ref[...], precision=lax.Precision.HIGHEST,
                         preferred_element_type=F32) + b_ref[...]


def _ada(c_pad, w_ada, b_ada):
    rows, d = c_pad.shape
    n = w_ada.shape[1]
    tn = 1024
    return pl.pallas_call(
        _ada_kernel,
        grid=(n // tn,),
        in_specs=[pl.BlockSpec((rows, d), lambda j: (0, 0)),
                  pl.BlockSpec((d, tn), lambda j: (0, j)),
                  pl.BlockSpec((1, tn), lambda j: (0, j))],
        out_specs=pl.BlockSpec((rows, tn), lambda j: (0, j)),
        out_shape=jax.ShapeDtypeStruct((rows, n), F32),
        name="ada",
    )(c_pad, w_ada, b_ada)


def _in_proj_kernel(x_ref, mod_ref, nw_ref, w_ref, *o_refs):
    x = x_ref[...]
    shift = mod_ref[0, 0:1, :]
    scale = mod_ref[0, 1:2, :]
    h = (_rms(x) * nw_ref[...]) * (1.0 + scale) + shift
    hb = h.astype(BF16)
    off = 0
    for o_ref, width in zip(o_refs, IN_SIZES):
        o_ref[...] = jnp.dot(hb, w_ref[:, off:off + width],
                             preferred_element_type=F32).astype(o_ref.dtype)
        off += width


def _in_proj(x2, mod, norm_w, w_in_bf16, seq):
    t, d = x2.shape
    tm = min(ROW_TILE, seq)
    per_batch = seq // tm
    in_w = w_in_bf16.shape[1]
    return pl.pallas_call(
        _in_proj_kernel,
        grid=(t // tm,),
        in_specs=[pl.BlockSpec((tm, d), lambda i: (i, 0)),
                  pl.BlockSpec((1, 6, d), lambda i: (i // per_batch, 0, 0)),
                  pl.BlockSpec((1, d), lambda i: (0, 0)),
                  pl.BlockSpec((d, in_w), lambda i: (0, 0))],
        out_specs=[pl.BlockSpec((tm, w), lambda i: (i, 0)) for w in IN_SIZES],
        out_shape=[jax.ShapeDtypeStruct((t, w), BF16) for w in IN_SIZES],
        compiler_params=pltpu.CompilerParams(vmem_limit_bytes=VMEM_LIMIT),
        name="in_proj",
    )(x2, mod, norm_w, w_in_bf16)


def _rotary(x, cos, sin_even, sin_odd):
    nxt = pltpu.roll(x, LANES - 1, 1)
    prv = pltpu.roll(x, 1, 1)
    return x * cos + nxt * sin_even + prv * sin_odd


def _ret_kernel(q_ref, k_ref, v_ref, g_ref, cos_ref, sine_ref, sino_ref,
                hmask_ref, xi_ref, zeta_ref, dmask_ref, gch_ref, o_ref, state_ref):
    @pl.when(pl.program_id(1) == 0)
    def _():
        state_ref[...] = jnp.zeros_like(state_ref)

    n_sub = q_ref.shape[0] // RET_CHUNK
    for c in range(n_sub):
        rows = slice(c * RET_CHUNK, (c + 1) * RET_CHUNK)
        for pair in range(RET_HEADS // 2):
            lanes = slice(pair * LANES, (pair + 1) * LANES)
            cos = cos_ref[rows, lanes]
            sine = sine_ref[rows, lanes]
            sino = sino_ref[rows, lanes]
            qr = _rotary(q_ref[rows, lanes].astype(F32), cos, sine, sino)
            kr = _rotary(k_ref[rows, lanes].astype(F32), cos, sine, sino) * (RET_KEY_DIM ** -0.5)
            qb = qr.astype(BF16)
            for hh in range(2):
                h = 2 * pair + hh
                vcols = slice(h * RET_VAL_DIM, (h + 1) * RET_VAL_DIM)
                v = v_ref[rows, vcols]
                km = (kr * hmask_ref[h]).astype(BF16)
                scores = lax.dot_general(qb, km, (((1,), (1,)), ((), ())),
                                         preferred_element_type=F32) * dmask_ref[h]
                inner = jnp.dot(scores.astype(BF16), v, preferred_element_type=F32)
                qx = (qr * xi_ref[h]).astype(BF16)
                state = state_ref[h]
                cross = jnp.dot(qx, state.astype(BF16), preferred_element_type=F32)
                kz = (kr * zeta_ref[h]).astype(BF16)
                kv = lax.dot_general(kz, v, (((0,), (0,)), ((), ())),
                                     preferred_element_type=F32)
                state_ref[h] = state * gch_ref[h] + kv
                y = _rms(inner + cross)
                g = g_ref[rows, vcols].astype(F32)
                o_ref[rows, vcols] = (g * jax.nn.sigmoid(g) * y).astype(o_ref.dtype)


def _retention_tables(seq):
    dk, c, nh = RET_KEY_DIM, RET_CHUNK, RET_HEADS
    pos = jnp.arange(seq, dtype=F32)
    inv_freq = 1.0 / (10000.0 ** jnp.linspace(0.0, 1.0, dk // 2, dtype=F32))
    ang = pos[:, None] * jnp.repeat(inv_freq, 2)[None, :]
    sin = jnp.tile(jnp.sin(ang), (1, nh))
    cos = jnp.tile(jnp.cos(ang), (1, nh))
    even = (jnp.arange(nh * dk) % 2 == 0)[None, :]
    sin_even = jnp.where(even, -sin, 0.0)
    sin_odd = jnp.where(even, 0.0, sin)
    log_g = jnp.log(1.0 - 2.0 ** (-5.0 - jnp.arange(nh, dtype=F32)))
    i = jnp.arange(c, dtype=F32)
    rel = i[:, None] - i[None, :]
    dmask = jnp.where(rel[None] >= 0,
                      jnp.exp(jnp.maximum(rel, 0.0)[None] * log_g[:, None, None]), 0.0)
    zeta = jnp.exp((c - 1.0 - i)[None, :] * log_g[:, None])
    xi = jnp.exp((i + 1.0)[None, :] * log_g[:, None])
    g_chunk = jnp.exp(c * log_g)
    lane = jnp.arange(LANES)
    hmask = jnp.stack([((lane // dk) == (h % 2)).astype(F32) for h in range(nh)])[:, None, :]
    xi_t = xi[:, :, None] * hmask
    zeta_t = zeta[:, :, None] * hmask
    gch = jnp.broadcast_to(g_chunk[:, None, None], (nh, 1, LANES))
    return cos, sin_even, sin_odd, hmask, xi_t, zeta_t, dmask, gch


def _retention(q, k, v, g, batch, seq):
    t = q.shape[0]
    rb = min(RET_ROWS, seq)
    per_batch = seq // rb
    cos, sin_even, sin_odd, hmask, xi_t, zeta_t, dmask, gch = _retention_tables(seq)
    row = lambda w: pl.BlockSpec((rb, w), lambda b, j: (b * per_batch + j, 0))
    tab = lambda w: pl.BlockSpec((rb, w), lambda b, j: (j, 0))
    full = lambda a: pl.BlockSpec(a.shape, lambda b, j: (0,) * a.ndim)
    return pl.pallas_call(
        _ret_kernel,
        grid=(batch, per_batch),
        in_specs=[row(RET_QK_W), row(RET_QK_W), row(RET_V_W), row(RET_V_W),
                  tab(RET_QK_W), tab(RET_QK_W), tab(RET_QK_W),
                  full(hmask), full(xi_t), full(zeta_t), full(dmask), full(gch)],
        out_specs=row(RET_V_W),
        out_shape=jax.ShapeDtypeStruct((t, RET_V_W), BF16),
        scratch_shapes=[pltpu.VMEM((RET_HEADS, LANES, RET_VAL_DIM), F32)],
        compiler_params=pltpu.CompilerParams(
            dimension_semantics=("arbitrary", "arbitrary"), vmem_limit_bytes=VMEM_LIMIT),
        name="retention",
    )(q, k, v, g, cos, sin_even, sin_odd, hmask, xi_t, zeta_t, dmask, gch)


def _t5_bucket(rel):
    n = jnp.maximum(rel, 0)
    max_exact = REL_BUCKETS // 2
    nf = jnp.maximum(n, 1).astype(F32)
    large = max_exact + (jnp.log(nf / max_exact) / math.log(REL_MAX_DIST / max_exact)
                         * (REL_BUCKETS - max_exact)).astype(jnp.int32)
    large = jnp.minimum(large, REL_BUCKETS - 1)
    return jnp.where(n < max_exact, n, large)


def _bias_tiles(rel_bias, blk):
    r = jnp.arange(blk, dtype=jnp.int32)
    far = rel_bias[REL_BUCKETS - 1]
    rel0 = r[:, None] - r[None, :]
    rel1 = rel0 + blk
    b0 = jnp.transpose(rel_bias[_t5_bucket(rel0)], (2, 0, 1)) - far[:, None, None]
    b0 = jnp.where(rel0[None] >= 0, b0, NEG_BIG)
    b1 = jnp.transpose(rel_bias[_t5_bucket(rel1)], (2, 0, 1)) - far[:, None, None]
    return b0.astype(F32), b1.astype(F32)


def _attn_kernel(q_ref, k_ref, v_ref, b0_ref, b1_ref, lq1_ref, lk1_ref, lq2_ref, lk2_ref,
                 sw_ref, o_ref, m_ref, l_ref, acc_ref):
    blk = q_ref.shape[0]
    i = pl.program_id(2)
    lane = lax.broadcasted_iota(jnp.int32, (1, LANES), 1)
    q = q_ref[...] * (DIFF_HEAD_DIM ** -0.5)
    zero = jnp.zeros_like(q)
    qm = (jnp.where(lane < DIFF_HEAD_DIM, q, zero), jnp.where(lane >= DIFF_HEAD_DIM, q, zero))

    m_ref[...] = jnp.full_like(m_ref, NEG_BIG)
    l_ref[...] = jnp.zeros_like(l_ref)
    acc_ref[...] = jnp.zeros_like(acc_ref)

    def step(j, bias):
        start = pl.multiple_of(j * blk, blk)
        kb = k_ref[pl.ds(start, blk), :]
        vb = v_ref[pl.ds(start, blk), :]
        for mi in range(2):
            s = lax.dot_general(qm[mi], kb, (((1,), (1,)), ((), ())),
                                preferred_element_type=F32)
            if bias is not None:
                s = s + bias
            m_old = m_ref[mi]
            m_new = jnp.maximum(m_old, jnp.max(s, axis=-1, keepdims=True))
            alpha = jnp.exp(m_old - m_new)
            p = jnp.exp(s - m_new[:, 0:1])
            l_ref[mi] = alpha * l_ref[mi] + jnp.sum(p, axis=-1, keepdims=True)
            acc_ref[mi] = alpha * acc_ref[mi] + jnp.dot(p.astype(BF16), vb,
                                                        preferred_element_type=F32)
            m_ref[mi] = m_new

    def far_body(j, carry):
        step(j, None)
        return carry

    lax.fori_loop(0, jnp.maximum(i - 1, 0), far_body, 0)

    @pl.when(i >= 1)
    def _():
        step(i - 1, b1_ref[0])

    step(i, b0_ref[0])

    lam = (jnp.exp(jnp.sum(lq1_ref[...] * lk1_ref[...], axis=-1, keepdims=True))
           - jnp.exp(jnp.sum(lq2_ref[...] * lk2_ref[...], axis=-1, keepdims=True))
           + LAMBDA_INIT)
    a = acc_ref[0] / l_ref[0] - lam * (acc_ref[1] / l_ref[1])
    o_ref[...] = (_rms(a) * sw_ref[...] * (1.0 - LAMBDA_INIT)).astype(o_ref.dtype)


def _diff_attention(q, k, v, rel_bias, lq1, lk1, lq2, lk2, subln_w, batch, seq):
    t = q.shape[0]
    blk = min(ATT_BLOCK, seq)
    nq = seq // blk
    hw = 2 * DIFF_HEAD_DIM
    b0, b1 = _bias_tiles(rel_bias, blk)
    small = lambda a: pl.BlockSpec(a.shape, lambda b, h, i: (0,) * a.ndim)
    return pl.pallas_call(
        _attn_kernel,
        grid=(batch, DIFF_HEADS, nq),
        in_specs=[pl.BlockSpec((blk, hw), lambda b, h, i: (b * nq + i, h)),
                  pl.BlockSpec((seq, hw), lambda b, h, i: (b, h)),
                  pl.BlockSpec((seq, hw), lambda b, h, i: (b, h)),
                  pl.BlockSpec((1, blk, blk), lambda b, h, i: (h, 0, 0)),
                  pl.BlockSpec((1, blk, blk), lambda b, h, i: (h, 0, 0)),
                  small(lq1), small(lk1), small(lq2), small(lk2), small(subln_w)],
        out_specs=pl.BlockSpec((blk, hw), lambda b, h, i: (b * nq + i, h)),
        out_shape=jax.ShapeDtypeStruct((t, DIFF_V_W), BF16),
        scratch_shapes=[pltpu.VMEM((2, blk, 1), F32), pltpu.VMEM((2, blk, 1), F32),
                        pltpu.VMEM((2, blk, hw), F32)],
        compiler_params=pltpu.CompilerParams(
            dimension_semantics=("arbitrary", "arbitrary", "arbitrary"),
            vmem_limit_bytes=VMEM_LIMIT),
        name="diff_attn",
    )(q, k, v, b0, b1, lq1, lk1, lq2, lk2, subln_w)


def _out_kernel(yr_ref, yd_ref, x_ref, mod_ref, nw_ref, wo_ref, wr_ref, br_ref,
                x1_ref, h2_ref, meta_ref, gate_ref, cnt_ref, run_ref):
    tm = x_ref.shape[0]

    @pl.when(pl.program_id(0) == 0)
    def _():
        run_ref[...] = jnp.zeros_like(run_ref)

    mixed = (jnp.dot(yr_ref[...], wo_ref[0:RET_V_W, :], preferred_element_type=F32)
             + jnp.dot(yd_ref[...], wo_ref[RET_V_W:, :], preferred_element_type=F32))
    gate1 = mod_ref[0, 2:3, :]
    shift2 = mod_ref[0, 3:4, :]
    scale2 = mod_ref[0, 4:5, :]
    x1 = x_ref[...] + gate1 * mixed
    x1_ref[...] = x1
    h2 = (_rms(x1) * nw_ref[...]) * (1.0 + scale2) + shift2
    h2_ref[...] = h2

    logits = jnp.dot(h2, wr_ref[...], precision=lax.Precision.HIGHEST,
                     preferred_element_type=F32) + br_ref[...]
    lane = lax.broadcasted_iota(jnp.int32, logits.shape, 1)
    work = logits
    vals, idxs, hots = [], [], []
    for _ in range(TOP_K):
        mx = jnp.max(work, axis=-1, keepdims=True)
        idx = jnp.min(jnp.where(work == mx, lane, N_EXPERTS), axis=-1, keepdims=True)
        hot = lane == idx
        vals.append(mx)
        idxs.append(idx)
        hots.append(hot)
        work = jnp.where(hot, -jnp.inf, work)
    exps = [jnp.exp(v - vals[0]) for v in vals]
    denom = exps[0] + exps[1] + exps[2] + exps[3]

    sel = jnp.zeros(logits.shape, F32)
    for hot in hots:
        sel = sel + hot.astype(F32)
    r = lax.broadcasted_iota(jnp.int32, (tm, tm), 0)
    c = lax.broadcasted_iota(jnp.int32, (tm, tm), 1)
    before = (c < r).astype(BF16)
    prefix = jnp.dot(before, sel.astype(BF16), preferred_element_type=F32) + run_ref[...]
    ranks = [jnp.sum(jnp.where(hot, prefix, 0.0), axis=-1, keepdims=True) for hot in hots]
    run_ref[...] = run_ref[...] + jnp.sum(sel, axis=0, keepdims=True)
    cnt_ref[...] = run_ref[...].astype(jnp.int32)

    lane8 = lax.broadcasted_iota(jnp.int32, (tm, 2 * TOP_K), 1)
    meta = jnp.zeros((tm, 2 * TOP_K), jnp.int32)
    gates = jnp.zeros((tm, 2 * TOP_K), F32)
    for kk in range(TOP_K):
        meta = jnp.where(lane8 == kk, idxs[kk], meta)
        meta = jnp.where(lane8 == TOP_K + kk, ranks[kk].astype(jnp.int32), meta)
        gates = jnp.where(lane8 == kk, exps[kk] / denom, gates)
    meta_ref[...] = meta
    gate_ref[...] = gates


def _out_router(y_r, y_d, x2, mod, norm_w, w_out_bf16, w_router, b_router, seq):
    t, d = x2.shape
    tm = min(ROW_TILE, seq)
    per_batch = seq // tm
    ne = w_router.shape[1]
    row = lambda w: pl.BlockSpec((tm, w), lambda i: (i, 0))
    const = lambda a: pl.BlockSpec(a.shape, lambda i: (0,) * a.ndim)
    return pl.pallas_call(
        _out_kernel,
        grid=(t // tm,),
        in_specs=[row(RET_V_W), row(DIFF_V_W), row(d),
                  pl.BlockSpec((1, 6, d), lambda i: (i // per_batch, 0, 0)),
                  const(norm_w), const(w_out_bf16), const(w_router), const(b_router)],
        out_specs=[row(d), row(d), row(2 * TOP_K), row(2 * TOP_K),
                   pl.BlockSpec((1, ne), lambda i: (0, 0))],
        out_shape=[jax.ShapeDtypeStruct((t, d), F32), jax.ShapeDtypeStruct((t, d), F32),
                   jax.ShapeDtypeStruct((t, 2 * TOP_K), jnp.int32),
                   jax.ShapeDtypeStruct((t, 2 * TOP_K), F32),
                   jax.ShapeDtypeStruct((1, ne), jnp.int32)],
        scratch_shapes=[pltpu.VMEM((1, ne), F32)],
        compiler_params=pltpu.CompilerParams(
            dimension_semantics=("arbitrary",), vmem_limit_bytes=VMEM_LIMIT),
        name="out_router",
    )(y_r, y_d, x2, mod, norm_w, w_out_bf16, w_router, b_router)


def _dispatch_kernel(dest_ref, h_ref, xs_ref, sem):
    nt = dest_ref.shape[2] // TOP_K
    base = pl.program_id(0) * nt

    def row_copy(tok, slot):
        return pltpu.make_async_copy(h_ref.at[pl.ds(tok, 1), :], xs_ref.at[pl.ds(slot, 1), :], sem)

    def issue(j, carry):
        for kk in range(TOP_K):
            row_copy(base + j, dest_ref[0, 0, j * TOP_K + kk]).start()
        return carry

    lax.fori_loop(0, nt, issue, 0)

    def drain(j, carry):
        row_copy(0, 0).wait()
        return carry

    lax.fori_loop(0, nt * TOP_K, drain, 0)


def _dispatch(dest, h2, n_rows):
    t, d = h2.shape
    nt = min(DISPATCH_TOKENS, t)
    dest2 = dest.reshape(t // nt, 1, nt * TOP_K)
    return pl.pallas_call(
        _dispatch_kernel,
        grid=(t // nt,),
        in_specs=[pl.BlockSpec((1, 1, nt * TOP_K), lambda i: (i, 0, 0), memory_space=pltpu.SMEM),
                  pl.BlockSpec(memory_space=pl.ANY)],
        out_specs=pl.BlockSpec(memory_space=pl.ANY),
        out_shape=jax.ShapeDtypeStruct((n_rows, d), h2.dtype),
        scratch_shapes=[pltpu.SemaphoreType.DMA(())],
        compiler_params=pltpu.CompilerParams(dimension_semantics=("arbitrary",)),
        name="dispatch",
    )(dest2, h2)


def _expert_kernel(be_ref, nv_ref, x_ref, w1g_ref, w1l_ref, b1g_ref, b1l_ref, w2_ref, b2_ref, o_ref):
    @pl.when(pl.program_id(0) < nv_ref[0])
    def _():
        x = x_ref[...].astype(BF16)
        hg = jnp.dot(x, w1g_ref[0], preferred_element_type=F32) + b1g_ref[0]
        hl = jnp.dot(x, w1l_ref[0], preferred_element_type=F32) + b1l_ref[0]
        glu = jnp.minimum(hg, SWIGLU_LIMIT)
        lin = jnp.clip(hl, -SWIGLU_LIMIT, SWIGLU_LIMIT)
        act = glu * jax.nn.sigmoid(SWIGLU_ALPHA * glu) * (lin + 1.0)
        o_ref[...] = jnp.dot(act.astype(BF16), w2_ref[0], preferred_element_type=F32) + b2_ref[0]


def _experts(block_e, n_valid, xs, w1g, w1l, b1g, b1l, w2, b2):
    p, d = xs.shape
    f = w1g.shape[2]
    bm = EXPERT_ROWS
    nb = p // bm
    blk = lambda b, be, nv: jnp.minimum(b, nv[0] - 1)
    exp = lambda b, be, nv: be[jnp.minimum(b, nv[0] - 1)]
    grid_spec = pltpu.PrefetchScalarGridSpec(
        num_scalar_prefetch=2,
        grid=(nb,),
        in_specs=[pl.BlockSpec((bm, d), lambda b, be, nv: (blk(b, be, nv), 0)),
                  pl.BlockSpec((1, d, f), lambda b, be, nv: (exp(b, be, nv), 0, 0)),
                  pl.BlockSpec((1, d, f), lambda b, be, nv: (exp(b, be, nv), 0, 0)),
                  pl.BlockSpec((1, 1, f), lambda b, be, nv: (exp(b, be, nv), 0, 0)),
                  pl.BlockSpec((1, 1, f), lambda b, be, nv: (exp(b, be, nv), 0, 0)),
                  pl.BlockSpec((1, f, d), lambda b, be, nv: (exp(b, be, nv), 0, 0)),
                  pl.BlockSpec((1, 1, d), lambda b, be, nv: (exp(b, be, nv), 0, 0))],
        out_specs=pl.BlockSpec((bm, d), lambda b, be, nv: (blk(b, be, nv), 0)),
    )
    return pl.pallas_call(
        _expert_kernel,
        grid_spec=grid_spec,
        out_shape=jax.ShapeDtypeStruct((p, d), F32),
        compiler_params=pltpu.CompilerParams(
            dimension_semantics=("arbitrary",), vmem_limit_bytes=VMEM_LIMIT),
        name="experts",
    )(block_e, n_valid, xs, w1g, w1l, b1g, b1l, w2, b2)


def _combine_kernel(dcur_ref, dnxt_ref, gate_ref, x1_ref, mod_ref, nf_ref, y_ref, o_ref, buf, sems):
    nt = x1_ref.shape[0]
    i = pl.program_id(0)
    n = pl.num_programs(0)
    slot = lax.rem(i, 2)

    def row_copy(src_row, s, kk, r):
        return pltpu.make_async_copy(y_ref.at[pl.ds(src_row, 1), :],
                                     buf.at[s, kk, pl.ds(r, 1), :], sems.at[s])

    def issue_all(d_ref, s):
        def body(j, carry):
            for kk in range(TOP_K):
                row_copy(d_ref[0, 0, j * TOP_K + kk], s, kk, j).start()
            return carry
        lax.fori_loop(0, nt, body, 0)

    @pl.when(i == 0)
    def _():
        issue_all(dcur_ref, 0)

    @pl.when(i + 1 < n)
    def _():
        issue_all(dnxt_ref, 1 - slot)

    def drain(j, carry):
        row_copy(0, slot, 0, 0).wait()
        return carry

    lax.fori_loop(0, nt * TOP_K, drain, 0)

    g = gate_ref[...]
    moe = buf[slot, 0] * g[:, 0:1]
    for kk in range(1, TOP_K):
        moe = moe + buf[slot, kk] * g[:, kk:kk + 1]
    gate2 = mod_ref[0, 5:6, :]
    x2 = x1_ref[...] + gate2 * moe
    o_ref[...] = _rms(x2) * nf_ref[...]


def _combine(dest, gates, x1, mod, normf_w, y, seq):
    t, d = x1.shape
    nt = min(COMBINE_TOKENS, seq)
    steps = t // nt
    per_batch = seq // nt
    dest2 = dest.reshape(steps, 1, nt * TOP_K)
    return pl.pallas_call(
        _combine_kernel,
        grid=(steps,),
        in_specs=[pl.BlockSpec((1, 1, nt * TOP_K), lambda i: (i, 0, 0), memory_space=pltpu.SMEM),
                  pl.BlockSpec((1, 1, nt * TOP_K), lambda i: (jnp.minimum(i + 1, steps - 1), 0, 0),
                               memory_space=pltpu.SMEM),
                  pl.BlockSpec((nt, 2 * TOP_K), lambda i: (i, 0)),
                  pl.BlockSpec((nt, d), lambda i: (i, 0)),
                  pl.BlockSpec((1, 6, d), lambda i: (i // per_batch, 0, 0)),
                  pl.BlockSpec((1, d), lambda i: (0, 0)),
                  pl.BlockSpec(memory_space=pl.ANY)],
        out_specs=pl.BlockSpec((nt, d), lambda i: (i, 0)),
        out_shape=jax.ShapeDtypeStruct((t, d), F32),
        scratch_shapes=[pltpu.VMEM((2, TOP_K, nt, d), F32), pltpu.SemaphoreType.DMA((2,))],
        compiler_params=pltpu.CompilerParams(
            dimension_semantics=("arbitrary",), vmem_limit_bytes=VMEM_LIMIT),
        name="combine",
    )(dest2, dest2, gates, x1, mod, normf_w, y)


def kernel(x, c, w_ada, b_ada, norm1_w, w_in, lam_q1, lam_k1, lam_q2, lam_k2, subln_w, rel_bias,
           w_out, norm2_w, w_router, b_router, w1, b1, w2, b2, normf_w):
    batch, seq, d = x.shape
    t = batch * seq
    x2 = x.reshape(t, d)

    c_pad = jnp.zeros((8, d), F32).at[:batch].set(c)
    mod = _ada(c_pad, w_ada[0], b_ada[0][None, :])[:batch].reshape(batch, 6, d)

    q_r, k_r, v_r, g_r, q_d, k_d, v_d = _in_proj(x2, mod, norm1_w[0][None, :],
                                                 w_in[0].astype(BF16), seq)
    y_r = _retention(q_r, k_r, v_r, g_r, batch, seq)
    y_d = _diff_attention(q_d, k_d, v_d, rel_bias, lam_q1, lam_k1, lam_q2, lam_k2,
                          subln_w, batch, seq)

    x1, h2, meta, gates, counts = _out_router(y_r, y_d, x2, mod, norm2_w[0][None, :],
                                              w_out[0].astype(BF16), w_router[0],
                                              b_router[0][None, :], seq)

    bm = EXPERT_ROWS
    counts = counts[0]
    padded = (counts + bm - 1) // bm * bm
    pad_end = jnp.cumsum(padded)
    pad_start = pad_end - padded
    dest = pad_start[meta[:, :TOP_K]] + meta[:, TOP_K:]
    n_rows = (t * TOP_K // bm + N_EXPERTS) * bm
    nb = n_rows // bm
    block_e = jnp.minimum(
        jnp.searchsorted(pad_end, jnp.arange(nb, dtype=pad_end.dtype) * bm, side='right'),
        N_EXPERTS - 1).astype(jnp.int32)
    n_valid = (pad_end[-1:] // bm).astype(jnp.int32)

    xs = _dispatch(dest, h2, n_rows)
    w1b = w1[0].astype(BF16)
    ys = _experts(block_e, n_valid, xs, w1b[:, :, 0::2], w1b[:, :, 1::2],
                  b1[0][:, None, 0::2], b1[0][:, None, 1::2], w2[0].astype(BF16),
                  b2[0][:, None, :])
    out = _combine(dest, gates, x1, mod, normf_w[None, :], ys, seq)
    return out.reshape(batch, seq, d)
```

```python
import functools
import math

import jax
import jax.numpy as jnp
from jax import lax
from jax.experimental import pallas as pl
from jax.experimental.pallas import tpu as pltpu

F32 = jnp.float32
BF16 = jnp.bfloat16

D_MODEL = 1024
RET_HEADS = 4
RET_KEY_DIM = 64
RET_VAL_DIM = 128
RET_QK_W = RET_HEADS * RET_KEY_DIM
RET_V_W = RET_HEADS * RET_VAL_DIM
RET_CHUNK = 128
DIFF_HEADS = 4
DIFF_HEAD_DIM = 64
DIFF_QK_W = DIFF_HEADS * 2 * DIFF_HEAD_DIM
DIFF_V_W = DIFF_HEADS * 2 * DIFF_HEAD_DIM
IN_SIZES = (RET_QK_W, RET_QK_W, RET_V_W, RET_V_W, DIFF_QK_W, DIFF_QK_W, DIFF_V_W)
REL_BUCKETS = 32
REL_MAX_DIST = 128
N_EXPERTS = 32
TOP_K = 4
SWIGLU_ALPHA = 1.702
SWIGLU_LIMIT = 7.0
NORM_EPS = 1e-6
LAMBDA_INIT = 0.8 - 0.6 * math.exp(-0.3 * 0)

LANES = 128
NEG_BIG = -1e30
VMEM_LIMIT = 56 * 1024 * 1024

ROW_TILE = 512
RET_ROWS = 512
ATT_BLOCK = 256
EXPERT_ROWS = 256
PERM_TILE = 256
COMBINE_TOKENS = 128


def _rms(x):
    return x * lax.rsqrt(jnp.mean(x * x, axis=-1, keepdims=True) + NORM_EPS)


def _ada_kernel(c_ref, w_ref, b_ref, o_ref):
    c = c_ref[...]
    cond = c * jax.nn.sigmoid(c)
    o_ref[...] = jnp.dot(cond, w_ref[...], precision=lax.Precision.HIGHEST,
                         preferred_element_type=F32) + b_ref[...]


def _ada(c_pad, w_ada, b_ada):
    rows, d = c_pad.shape
    n = w_ada.shape[1]
    tn = 1024
    return pl.pallas_call(
        _ada_kernel,
        grid=(n // tn,),
        in_specs=[pl.BlockSpec((rows, d), lambda j: (0, 0)),
                  pl.BlockSpec((d, tn), lambda j: (0, j)),
                  pl.BlockSpec((1, tn), lambda j: (0, j))],
        out_specs=pl.BlockSpec((rows, tn), lambda j: (0, j)),
        out_shape=jax.ShapeDtypeStruct((rows, n), F32),
        name="ada",
    )(c_pad, w_ada, b_ada)


def _in_proj_kernel(x_ref, mod_ref, nw_ref, w_ref, *o_refs):
    x = x_ref[...]
    shift = mod_ref[0, 0:1, :]
    scale = mod_ref[0, 1:2, :]
    h = (_rms(x) * nw_ref[...]) * (1.0 + scale) + shift
    hb = h.astype(BF16)
    off = 0
    for o_ref, width in zip(o_refs, IN_SIZES):
        o_ref[...] = jnp.dot(hb, w_ref[:, off:off + width],
                             preferred_element_type=F32).astype(o_ref.dtype)
        off += width


def _in_proj(x2, mod, norm_w, w_in_bf16, seq):
    t, d = x2.shape
    tm = min(ROW_TILE, seq)
    per_batch = seq // tm
    in_w = w_in_bf16.shape[1]
    return pl.pallas_call(
        _in_proj_kernel,
        grid=(t // tm,),
        in_specs=[pl.BlockSpec((tm, d), lambda i: (i, 0)),
                  pl.BlockSpec((1, 6, d), lambda i: (i // per_batch, 0, 0)),
                  pl.BlockSpec((1, d), lambda i: (0, 0)),
                  pl.BlockSpec((d, in_w), lambda i: (0, 0))],
        out_specs=[pl.BlockSpec((tm, w), lambda i: (i, 0)) for w in IN_SIZES],
        out_shape=[jax.ShapeDtypeStruct((t, w), BF16) for w in IN_SIZES],
        compiler_params=pltpu.CompilerParams(vmem_limit_bytes=VMEM_LIMIT),
        name="in_proj",
    )(x2, mod, norm_w, w_in_bf16)


def _rotary(x, cos, sin_even, sin_odd):
    nxt = pltpu.roll(x, LANES - 1, 1)
    prv = pltpu.roll(x, 1, 1)
    return x * cos + nxt * sin_even + prv * sin_odd


def _ret_kernel(q_ref, k_ref, v_ref, g_ref, cos_ref, sine_ref, sino_ref,
                hmask_ref, xi_ref, zeta_ref, dmask_ref, gch_ref, o_ref, state_ref):
    @pl.when(pl.program_id(1) == 0)
    def _():
        state_ref[...] = jnp.zeros_like(state_ref)

    n_sub = q_ref.shape[0] // RET_CHUNK
    for c in range(n_sub):
        rows = slice(c * RET_CHUNK, (c + 1) * RET_CHUNK)
        for pair in range(RET_HEADS // 2):
            lanes = slice(pair * LANES, (pair + 1) * LANES)
            cos = cos_ref[rows, lanes]
            sine = sine_ref[rows, lanes]
            sino = sino_ref[rows, lanes]
            qr = _rotary(q_ref[rows, lanes].astype(F32), cos, sine, sino)
            kr = _rotary(k_ref[rows, lanes].astype(F32), cos, sine, sino) * (RET_KEY_DIM ** -0.5)
            qb = qr.astype(BF16)
            for hh in range(2):
                h = 2 * pair + hh
                vcols = slice(h * RET_VAL_DIM, (h + 1) * RET_VAL_DIM)
                v = v_ref[rows, vcols]
                km = (kr * hmask_ref[h]).astype(BF16)
                scores = lax.dot_general(qb, km, (((1,), (1,)), ((), ())),
                                         preferred_element_type=F32) * dmask_ref[h]
                inner = jnp.dot(scores.astype(BF16), v, preferred_element_type=F32)
                qx = (qr * xi_ref[h]).astype(BF16)
                state = state_ref[h]
                cross = jnp.dot(qx, state.astype(BF16), preferred_element_type=F32)
                kz = (kr * zeta_ref[h]).astype(BF16)
                kv = lax.dot_general(kz, v, (((0,), (0,)), ((), ())),
                                     preferred_element_type=F32)
                state_ref[h] = state * gch_ref[h] + kv
                y = _rms(inner + cross)
                g = g_ref[rows, vcols].astype(F32)
                o_ref[rows, vcols] = (g * jax.nn.sigmoid(g) * y).astype(o_ref.dtype)


def _retention_tables(seq):
    dk, c, nh = RET_KEY_DIM, RET_CHUNK, RET_HEADS
    pos = jnp.arange(seq, dtype=F32)
    inv_freq = 1.0 / (10000.0 ** jnp.linspace(0.0, 1.0, dk // 2, dtype=F32))
    ang = pos[:, None] * jnp.repeat(inv_freq, 2)[None, :]
    sin = jnp.tile(jnp.sin(ang), (1, nh))
    cos = jnp.tile(jnp.cos(ang), (1, nh))
    even = (jnp.arange(nh * dk) % 2 == 0)[None, :]
    sin_even = jnp.where(even, -sin, 0.0)
    sin_odd = jnp.where(even, 0.0, sin)
    log_g = jnp.log(1.0 - 2.0 ** (-5.0 - jnp.arange(nh, dtype=F32)))
    i = jnp.arange(c, dtype=F32)
    rel = i[:, None] - i[None, :]
    dmask = jnp.where(rel[None] >= 0,
                      jnp.exp(jnp.maximum(rel, 0.0)[None] * log_g[:, None, None]), 0.0)
    zeta = jnp.exp((c - 1.0 - i)[None, :] * log_g[:, None])
    xi = jnp.exp((i + 1.0)[None, :] * log_g[:, None])
    g_chunk = jnp.exp(c * log_g)
    lane = jnp.arange(LANES)
    hmask = jnp.stack([((lane // dk) == (h % 2)).astype(F32) for h in range(nh)])[:, None, :]
    xi_t = xi[:, :, None] * hmask
    zeta_t = zeta[:, :, None] * hmask
    gch = jnp.broadcast_to(g_chunk[:, None, None], (nh, 1, LANES))
    return cos, sin_even, sin_odd, hmask, xi_t, zeta_t, dmask, gch


def _retention(q, k, v, g, batch, seq):
    t = q.shape[0]
    rb = min(RET_ROWS, seq)
    per_batch = seq // rb
    cos, sin_even, sin_odd, hmask, xi_t, zeta_t, dmask, gch = _retention_tables(seq)
    row = lambda w: pl.BlockSpec((rb, w), lambda b, j: (b * per_batch + j, 0))
    tab = lambda w: pl.BlockSpec((rb, w), lambda b, j: (j, 0))
    full = lambda a: pl.BlockSpec(a.shape, lambda b, j: (0,) * a.ndim)
    return pl.pallas_call(
        _ret_kernel,
        grid=(batch, per_batch),
        in_specs=[row(RET_QK_W), row(RET_QK_W), row(RET_V_W), row(RET_V_W),
                  tab(RET_QK_W), tab(RET_QK_W), tab(RET_QK_W),
                  full(hmask), full(xi_t), full(zeta_t), full(dmask), full(gch)],
        out_specs=row(RET_V_W),
        out_shape=jax.ShapeDtypeStruct((t, RET_V_W), BF16),
        scratch_shapes=[pltpu.VMEM((RET_HEADS, LANES, RET_VAL_DIM), F32)],
        compiler_params=pltpu.CompilerParams(
            dimension_semantics=("arbitrary", "arbitrary"), vmem_limit_bytes=VMEM_LIMIT),
        name="retention",
    )(q, k, v, g, cos, sin_even, sin_odd, hmask, xi_t, zeta_t, dmask, gch)


def _t5_bucket(rel):
    n = jnp.maximum(rel, 0)
    max_exact = REL_BUCKETS // 2
    nf = jnp.maximum(n, 1).astype(F32)
    large = max_exact + (jnp.log(nf / max_exact) / math.log(REL_MAX_DIST / max_exact)
                         * (REL_BUCKETS - max_exact)).astype(jnp.int32)
    large = jnp.minimum(large, REL_BUCKETS - 1)
    return jnp.where(n < max_exact, n, large)


def _bias_tiles(rel_bias, blk):
    r = jnp.arange(blk, dtype=jnp.int32)
    far = rel_bias[REL_BUCKETS - 1]
    rel0 = r[:, None] - r[None, :]
    rel1 = rel0 + blk
    b0 = jnp.transpose(rel_bias[_t5_bucket(rel0)], (2, 0, 1)) - far[:, None, None]
    b0 = jnp.where(rel0[None] >= 0, b0, NEG_BIG)
    b1 = jnp.transpose(rel_bias[_t5_bucket(rel1)], (2, 0, 1)) - far[:, None, None]
    return b0.astype(F32), b1.astype(F32)


def _attn_kernel(q_ref, k_ref, v_ref, b0_ref, b1_ref, lq1_ref, lk1_ref, lq2_ref, lk2_ref,
                 sw_ref, o_ref, m_ref, l_ref, acc_ref):
    blk = q_ref.shape[0]
    i = pl.program_id(2)
    lane = lax.broadcasted_iota(jnp.int32, (1, LANES), 1)
    q = q_ref[...] * (DIFF_HEAD_DIM ** -0.5)
    zero = jnp.zeros_like(q)
    qm = (jnp.where(lane < DIFF_HEAD_DIM, q, zero), jnp.where(lane >= DIFF_HEAD_DIM, q, zero))

    m_ref[...] = jnp.full_like(m_ref, NEG_BIG)
    l_ref[...] = jnp.zeros_like(l_ref)
    acc_ref[...] = jnp.zeros_like(acc_ref)

    def step(j, bias):
        start = pl.multiple_of(j * blk, blk)
        kb = k_ref[pl.ds(start, blk), :]
        vb = v_ref[pl.ds(start, blk), :]
        for mi in range(2):
            s = lax.dot_general(qm[mi], kb, (((1,), (1,)), ((), ())),
                                preferred_element_type=F32)
            if bias is not None:
                s = s + bias
            m_old = m_ref[mi]
            m_new = jnp.maximum(m_old, jnp.max(s, axis=-1, keepdims=True))
            alpha = jnp.exp(m_old - m_new)
            p = jnp.exp(s - m_new[:, 0:1])
            l_ref[mi] = alpha * l_ref[mi] + jnp.sum(p, axis=-1, keepdims=True)
            acc_ref[mi] = alpha * acc_ref[mi] + jnp.dot(p.astype(BF16), vb,
                                                        preferred_element_type=F32)
            m_ref[mi] = m_new

    def far_body(j, carry):
        step(j, None)
        return carry

    lax.fori_loop(0, jnp.maximum(i - 1, 0), far_body, 0)

    @pl.when(i >= 1)
    def _():
        step(i - 1, b1_ref[0])

    step(i, b0_ref[0])

    lam = (jnp.exp(jnp.sum(lq1_ref[...] * lk1_ref[...], axis=-1, keepdims=True))
           - jnp.exp(jnp.sum(lq2_ref[...] * lk2_ref[...], axis=-1, keepdims=True))
           + LAMBDA_INIT)
    a = acc_ref[0] / l_ref[0] - lam * (acc_ref[1] / l_ref[1])
    o_ref[...] = (_rms(a) * sw_ref[...] * (1.0 - LAMBDA_INIT)).astype(o_ref.dtype)


def _diff_attention(q, k, v, rel_bias, lq1, lk1, lq2, lk2, subln_w, batch, seq):
    t = q.shape[0]
    blk = min(ATT_BLOCK, seq)
    nq = seq // blk
    hw = 2 * DIFF_HEAD_DIM
    b0, b1 = _bias_tiles(rel_bias, blk)
    small = lambda a: pl.BlockSpec(a.shape, lambda b, h, i: (0,) * a.ndim)
    return pl.pallas_call(
        _attn_kernel,
        grid=(batch, DIFF_HEADS, nq),
        in_specs=[pl.BlockSpec((blk, hw), lambda b, h, i: (b * nq + i, h)),
                  pl.BlockSpec((seq, hw), lambda b, h, i: (b, h)),
                  pl.BlockSpec((seq, hw), lambda b, h, i: (b, h)),
                  pl.BlockSpec((1, blk, blk), lambda b, h, i: (h, 0, 0)),
                  pl.BlockSpec((1, blk, blk), lambda b, h, i: (h, 0, 0)),
                  small(lq1), small(lk1), small(lq2), small(lk2), small(subln_w)],
        out_specs=pl.BlockSpec((blk, hw), lambda b, h, i: (b * nq + i, h)),
        out_shape=jax.ShapeDtypeStruct((t, DIFF_V_W), BF16),
        scratch_shapes=[pltpu.VMEM((2, blk, 1), F32), pltpu.VMEM((2, blk, 1), F32),
                        pltpu.VMEM((2, blk, hw), F32)],
        compiler_params=pltpu.CompilerParams(
            dimension_semantics=("arbitrary", "arbitrary", "arbitrary"),
            vmem_limit_bytes=VMEM_LIMIT),
        name="diff_attn",
    )(q, k, v, b0, b1, lq1, lk1, lq2, lk2, subln_w)


def _out_kernel(yr_ref, yd_ref, x_ref, mod_ref, nw_ref, wo_ref, wr_ref, br_ref,
                x1_ref, h2_ref, meta_ref, gate_ref, cnt_ref, run_ref):
    tm = x_ref.shape[0]

    @pl.when(pl.program_id(0) == 0)
    def _():
        run_ref[...] = jnp.zeros_like(run_ref)

    mixed = (jnp.dot(yr_ref[...], wo_ref[0:RET_V_W, :], preferred_element_type=F32)
             + jnp.dot(yd_ref[...], wo_ref[RET_V_W:, :], preferred_element_type=F32))
    gate1 = mod_ref[0, 2:3, :]
    shift2 = mod_ref[0, 3:4, :]
    scale2 = mod_ref[0, 4:5, :]
    x1 = x_ref[...] + gate1 * mixed
    x1_ref[...] = x1
    h2 = (_rms(x1) * nw_ref[...]) * (1.0 + scale2) + shift2
    h2_ref[...] = h2

    logits = jnp.dot(h2, wr_ref[...], precision=lax.Precision.HIGHEST,
                     preferred_element_type=F32) + br_ref[...]
    lane = lax.broadcasted_iota(jnp.int32, logits.shape, 1)
    work = logits
    vals, idxs, hots = [], [], []
    for _ in range(TOP_K):
        mx = jnp.max(work, axis=-1, keepdims=True)
        idx = jnp.min(jnp.where(work == mx, lane, N_EXPERTS), axis=-1, keepdims=True)
        hot = lane == idx
        vals.append(mx)
        idxs.append(idx)
        hots.append(hot)
        work = jnp.where(hot, -jnp.inf, work)
    exps = [jnp.exp(v - vals[0]) for v in vals]
    denom = exps[0] + exps[1] + exps[2] + exps[3]

    sel = jnp.zeros(logits.shape, F32)
    for hot in hots:
        sel = sel + hot.astype(F32)
    r = lax.broadcasted_iota(jnp.int32, (tm, tm), 0)
    c = lax.broadcasted_iota(jnp.int32, (tm, tm), 1)
    before = (c < r).astype(BF16)
    prefix = jnp.dot(before, sel.astype(BF16), preferred_element_type=F32) + run_ref[...]
    ranks = [jnp.sum(jnp.where(hot, prefix, 0.0), axis=-1, keepdims=True) for hot in hots]
    run_ref[...] = run_ref[...] + jnp.sum(sel, axis=0, keepdims=True)
    cnt_ref[...] = run_ref[...].astype(jnp.int32)

    lane8 = lax.broadcasted_iota(jnp.int32, (tm, 2 * TOP_K), 1)
    meta = jnp.zeros((tm, 2 * TOP_K), jnp.int32)
    gates = jnp.zeros((tm, 2 * TOP_K), F32)
    for kk in range(TOP_K):
        meta = jnp.where(lane8 == kk, idxs[kk], meta)
        meta = jnp.where(lane8 == TOP_K + kk, ranks[kk].astype(jnp.int32), meta)
        gates = jnp.where(lane8 == kk, exps[kk] / denom, gates)
    meta_ref[...] = meta
    gate_ref[...] = gates


def _out_router(y_r, y_d, x2, mod, norm_w, w_out_bf16, w_router, b_router, seq):
    t, d = x2.shape
    tm = min(ROW_TILE, seq)
    per_batch = seq // tm
    ne = w_router.shape[1]
    row = lambda w: pl.BlockSpec((tm, w), lambda i: (i, 0))
    const = lambda a: pl.BlockSpec(a.shape, lambda i: (0,) * a.ndim)
    return pl.pallas_call(
        _out_kernel,
        grid=(t // tm,),
        in_specs=[row(RET_V_W), row(DIFF_V_W), row(d),
                  pl.BlockSpec((1, 6, d), lambda i: (i // per_batch, 0, 0)),
                  const(norm_w), const(w_out_bf16), const(w_router), const(b_router)],
        out_specs=[row(d), row(d), row(2 * TOP_K), row(2 * TOP_K),
                   pl.BlockSpec((1, ne), lambda i: (0, 0))],
        out_shape=[jax.ShapeDtypeStruct((t, d), F32), jax.ShapeDtypeStruct((t, d), F32),
                   jax.ShapeDtypeStruct((t, 2 * TOP_K), jnp.int32),
                   jax.ShapeDtypeStruct((t, 2 * TOP_K), F32),
                   jax.ShapeDtypeStruct((1, ne), jnp.int32)],
        scratch_shapes=[pltpu.VMEM((1, ne), F32)],
        compiler_params=pltpu.CompilerParams(
            dimension_semantics=("arbitrary",), vmem_limit_bytes=VMEM_LIMIT),
        name="out_router",
    )(y_r, y_d, x2, mod, norm_w, w_out_bf16, w_router, b_router)


def _w1_prep_kernel(w_ref, p_ref, o_ref):
    for s in range(w_ref.shape[2] // PERM_TILE):
        cols = slice(s * PERM_TILE, (s + 1) * PERM_TILE)
        o_ref[0, :, cols] = jnp.dot(w_ref[0, :, cols].astype(BF16), p_ref[...],
                                    preferred_element_type=F32).astype(BF16)


def _pair_split_matrix():
    i = jnp.arange(PERM_TILE)[:, None]
    j = jnp.arange(PERM_TILE)[None, :]
    half = PERM_TILE // 2
    src = jnp.where(j < half, 2 * j, 2 * (j - half) + 1)
    return (i == src).astype(BF16)


def _w1_prep(w1):
    e, d, f2 = w1.shape
    tn = 1024
    return pl.pallas_call(
        _w1_prep_kernel,
        grid=(e, f2 // tn),
        in_specs=[pl.BlockSpec((1, d, tn), lambda i, j: (i, 0, j)),
                  pl.BlockSpec((PERM_TILE, PERM_TILE), lambda i, j: (0, 0))],
        out_specs=pl.BlockSpec((1, d, tn), lambda i, j: (i, 0, j)),
        out_shape=jax.ShapeDtypeStruct((e, d, f2), BF16),
        compiler_params=pltpu.CompilerParams(vmem_limit_bytes=VMEM_LIMIT),
        name="w1_prep",
    )(w1, _pair_split_matrix())


def _pair_split_bias(b1):
    e, f2 = b1.shape
    nt = f2 // PERM_TILE
    g = b1[:, 0::2].reshape(e, nt, 1, PERM_TILE // 2)
    l = b1[:, 1::2].reshape(e, nt, 1, PERM_TILE // 2)
    return jnp.concatenate([g, l], axis=2).reshape(e, 1, f2)


def _expert_kernel(be_ref, nv_ref, tcur_ref, tnxt_ref, h_ref, w1_ref, b1_ref, w2_ref, b2_ref,
                   o_ref, xbuf, sems):
    b = pl.program_id(0)
    nv = nv_ref[0]
    bm = xbuf.shape[1]
    slot = lax.rem(b, 2)

    def row_copy(tok, s, r):
        return pltpu.make_async_copy(h_ref.at[pl.ds(tok, 1), :], xbuf.at[s, pl.ds(r, 1), :],
                                     sems.at[s])

    def gather(tok_ref, s):
        def body(r, carry):
            row_copy(tok_ref[0, 0, r], s, r).start()
            return carry
        lax.fori_loop(0, bm, body, 0, unroll=8)

    @pl.when(b == 0)
    def _():
        gather(tcur_ref, 0)

    @pl.when(b + 1 < nv)
    def _():
        gather(tnxt_ref, 1 - slot)

    @pl.when(b < nv)
    def _():
        pltpu.make_async_copy(h_ref.at[pl.ds(0, bm), :], xbuf.at[slot], sems.at[slot]).wait()
        x = xbuf[slot].astype(BF16)
        h = jnp.dot(x, w1_ref[0], preferred_element_type=F32) + b1_ref[0]
        half = PERM_TILE // 2
        acts = []
        for j in range(h.shape[1] // PERM_TILE):
            glu = jnp.minimum(h[:, j * PERM_TILE:j * PERM_TILE + half], SWIGLU_LIMIT)
            lin = jnp.clip(h[:, j * PERM_TILE + half:(j + 1) * PERM_TILE],
                           -SWIGLU_LIMIT, SWIGLU_LIMIT)
            acts.append((glu * jax.nn.sigmoid(SWIGLU_ALPHA * glu) * (lin + 1.0)).astype(BF16))
        act = jnp.concatenate(acts, axis=1)
        o_ref[...] = jnp.dot(act, w2_ref[0], preferred_element_type=F32) + b2_ref[0]

    @pl.when(b >= nv)
    def _():
        o_ref[...] = jnp.zeros_like(o_ref)


def _experts(block_e, n_valid, slot_tok, h2, w1p, b1p, w2, b2):
    t, d = h2.shape
    f2 = w1p.shape[2]
    f = w2.shape[1]
    bm = EXPERT_ROWS
    nb = slot_tok.shape[0] // bm
    tok3 = slot_tok.reshape(nb, 1, bm)
    exp = lambda b, be, nv: be[jnp.minimum(b, nv[0] - 1)]
    grid_spec = pltpu.PrefetchScalarGridSpec(
        num_scalar_prefetch=2,
        grid=(nb,),
        in_specs=[pl.BlockSpec((1, 1, bm), lambda b, be, nv: (b, 0, 0), memory_space=pltpu.SMEM),
                  pl.BlockSpec((1, 1, bm), lambda b, be, nv: (jnp.minimum(b + 1, nb - 1), 0, 0),
                               memory_space=pltpu.SMEM),
                  pl.BlockSpec(memory_space=pl.ANY),
                  pl.BlockSpec((1, d, f2), lambda b, be, nv: (exp(b, be, nv), 0, 0)),
                  pl.BlockSpec((1, 1, f2), lambda b, be, nv: (exp(b, be, nv), 0, 0)),
                  pl.BlockSpec((1, f, d), lambda b, be, nv: (exp(b, be, nv), 0, 0)),
                  pl.BlockSpec((1, 1, d), lambda b, be, nv: (exp(b, be, nv), 0, 0))],
        out_specs=pl.BlockSpec((bm, d), lambda b, be, nv: (b, 0)),
        scratch_shapes=[pltpu.VMEM((2, bm, d), F32), pltpu.SemaphoreType.DMA((2,))],
    )
    return pl.pallas_call(
        _expert_kernel,
        grid_spec=grid_spec,
        out_shape=jax.ShapeDtypeStruct((nb * bm, d), F32),
        compiler_params=pltpu.CompilerParams(
            dimension_semantics=("arbitrary",), vmem_limit_bytes=VMEM_LIMIT),
        name="experts",
    )(block_e, n_valid, tok3, tok3, h2, w1p, b1p, w2, b2)


def _combine_kernel(dcur_ref, dnxt_ref, gate_ref, x1_ref, mod_ref, nf_ref, y_ref, o_ref, buf, sems):
    nt = x1_ref.shape[0]
    i = pl.program_id(0)
    n = pl.num_programs(0)
    slot = lax.rem(i, 2)

    def row_copy(src_row, s, kk, r):
        return pltpu.make_async_copy(y_ref.at[pl.ds(src_row, 1), :],
                                     buf.at[s, kk, pl.ds(r, 1), :], sems.at[s])

    def issue_all(d_ref, s):
        def body(j, carry):
            for kk in range(TOP_K):
                row_copy(d_ref[0, 0, j * TOP_K + kk], s, kk, j).start()
            return carry
        lax.fori_loop(0, nt, body, 0)

    @pl.when(i == 0)
    def _():
        issue_all(dcur_ref, 0)

    @pl.when(i + 1 < n)
    def _():
        issue_all(dnxt_ref, 1 - slot)

    for kk in range(TOP_K):
        pltpu.make_async_copy(y_ref.at[pl.ds(0, nt), :], buf.at[slot, kk], sems.at[slot]).wait()

    g = gate_ref[...]
    moe = buf[slot, 0] * g[:, 0:1]
    for kk in range(1, TOP_K):
        moe = moe + buf[slot, kk] * g[:, kk:kk + 1]
    gate2 = mod_ref[0, 5:6, :]
    x2 = x1_ref[...] + gate2 * moe
    o_ref[...] = _rms(x2) * nf_ref[...]


def _combine(dest, gates, x1, mod, normf_w, y, seq):
    t, d = x1.shape
    nt = min(COMBINE_TOKENS, seq)
    steps = t // nt
    per_batch = seq // nt
    dest2 = dest.reshape(steps, 1, nt * TOP_K)
    return pl.pallas_call(
        _combine_kernel,
        grid=(steps,),
        in_specs=[pl.BlockSpec((1, 1, nt * TOP_K), lambda i: (i, 0, 0), memory_space=pltpu.SMEM),
                  pl.BlockSpec((1, 1, nt * TOP_K), lambda i: (jnp.minimum(i + 1, steps - 1), 0, 0),
                               memory_space=pltpu.SMEM),
                  pl.BlockSpec((nt, 2 * TOP_K), lambda i: (i, 0)),
                  pl.BlockSpec((nt, d), lambda i: (i, 0)),
                  pl.BlockSpec((1, 6, d), lambda i: (i // per_batch, 0, 0)),
                  pl.BlockSpec((1, d), lambda i: (0, 0)),
                  pl.BlockSpec(memory_space=pl.ANY)],
        out_specs=pl.BlockSpec((nt, d), lambda i: (i, 0)),
        out_shape=jax.ShapeDtypeStruct((t, d), F32),
        scratch_shapes=[pltpu.VMEM((2, TOP_K, nt, d), F32), pltpu.SemaphoreType.DMA((2,))],
        compiler_params=pltpu.CompilerParams(
            dimension_semantics=("arbitrary",), vmem_limit_bytes=VMEM_LIMIT),
        name="combine",
    )(dest2, dest2, gates, x1, mod, normf_w, y)


def kernel(x, c, w_ada, b_ada, norm1_w, w_in, lam_q1, lam_k1, lam_q2, lam_k2, subln_w, rel_bias,
           w_out, norm2_w, w_router, b_router, w1, b1, w2, b2, normf_w):
    batch, seq, d = x.shape
    t = batch * seq
    x2 = x.reshape(t, d)

    c_pad = jnp.zeros((8, d), F32).at[:batch].set(c)
    mod = _ada(c_pad, w_ada[0], b_ada[0][None, :])[:batch].reshape(batch, 6, d)

    q_r, k_r, v_r, g_r, q_d, k_d, v_d = _in_proj(x2, mod, norm1_w[0][None, :],
                                                 w_in[0].astype(BF16), seq)
    y_r = _retention(q_r, k_r, v_r, g_r, batch, seq)
    y_d = _diff_attention(q_d, k_d, v_d, rel_bias, lam_q1, lam_k1, lam_q2, lam_k2,
                          subln_w, batch, seq)

    x1, h2, meta, gates, counts = _out_router(y_r, y_d, x2, mod, norm2_w[0][None, :],
                                              w_out[0].astype(BF16), w_router[0],
                                              b_router[0][None, :], seq)

    bm = EXPERT_ROWS
    counts = counts[0]
    padded = (counts + bm - 1) // bm * bm
    pad_end = jnp.cumsum(padded)
    pad_start = pad_end - padded
    sel_e = meta[:, :TOP_K]
    hot_e = sel_e[:, :, None] == jnp.arange(N_EXPERTS, dtype=jnp.int32)[None, None, :]
    dest = jnp.sum(jnp.where(hot_e, pad_start[None, None, :], 0), axis=-1) + meta[:, TOP_K:]
    n_rows = (t * TOP_K // bm + N_EXPERTS) * bm
    nb = n_rows // bm
    block_start = jnp.arange(nb, dtype=jnp.int32) * bm
    block_e = jnp.minimum(jnp.sum((pad_end[None, :] <= block_start[:, None]).astype(jnp.int32), axis=1),
                          N_EXPERTS - 1)
    n_valid = (pad_end[-1:] // bm).astype(jnp.int32)
    slot_tok = jnp.zeros((n_rows,), jnp.int32).at[dest.reshape(-1)].set(
        jnp.arange(t * TOP_K, dtype=jnp.int32) // TOP_K)

    ys = _experts(block_e, n_valid, slot_tok, h2, _w1_prep(w1[0]), _pair_split_bias(b1[0]),
                  w2[0].astype(BF16), b2[0][:, None, :])
    out = _combine(dest, gates, x1, mod, normf_w[None, :], ys, seq)
    return out.reshape(batch, seq, d)
```

```python
import functools
import math

import jax
import jax.numpy as jnp
from jax import lax
from jax.experimental import pallas as pl
from jax.experimental.pallas import tpu as pltpu

F32 = jnp.float32
BF16 = jnp.bfloat16

D_MODEL = 1024
RET_HEADS = 4
RET_KEY_DIM = 64
RET_VAL_DIM = 128
RET_QK_W = RET_HEADS * RET_KEY_DIM
RET_V_W = RET_HEADS * RET_VAL_DIM
RET_CHUNK = 128
DIFF_HEADS = 4
DIFF_HEAD_DIM = 64
DIFF_QK_W = DIFF_HEADS * 2 * DIFF_HEAD_DIM
DIFF_V_W = DIFF_HEADS * 2 * DIFF_HEAD_DIM
IN_SIZES = (RET_QK_W, RET_QK_W, RET_V_W, RET_V_W, DIFF_QK_W, DIFF_QK_W, DIFF_V_W)
REL_BUCKETS = 32
REL_MAX_DIST = 128
N_EXPERTS = 32
TOP_K = 4
SWIGLU_ALPHA = 1.702
SWIGLU_LIMIT = 7.0
NORM_EPS = 1e-6
LAMBDA_INIT = 0.8 - 0.6 * math.exp(-0.3 * 0)

LANES = 128
ROW_SUB = D_MODEL // LANES
NEG_BIG = -1e30
VMEM_LIMIT = 56 * 1024 * 1024

ROW_TILE = 512
RET_ROWS = 512
ATT_BLOCK = 512
EXPERT_ROWS = 256
PERM_TILE = 256
COMBINE_TOKENS = 128


def _rms(x):
    return x * lax.rsqrt(jnp.mean(x * x, axis=-1, keepdims=True) + NORM_EPS)


def _store_row_tiles(ref, x):
    rows = x.shape[0]
    for s in range(ROW_SUB):
        ref[pl.ds(s, rows, stride=ROW_SUB), :] = x[:, s * LANES:(s + 1) * LANES]


def _load_row_tiles(ref, rows):
    return jnp.concatenate([ref[pl.ds(s, rows, stride=ROW_SUB), :] for s in range(ROW_SUB)], axis=1)


def _ada_kernel(c_ref, w_ref, b_ref, o_ref):
    c = c_ref[...]
    cond = c * jax.nn.sigmoid(c)
    o_ref[...] = jnp.dot(cond, w_ref[...], precision=lax.Precision.HIGHEST,
                         preferred_element_type=F32) + b_ref[...]


def _ada(c_pad, w_ada, b_ada):
    rows, d = c_pad.shape
    n = w_ada.shape[1]
    tn = 1024
    return pl.pallas_call(
        _ada_kernel,
        grid=(n // tn,),
        in_specs=[pl.BlockSpec((rows, d), lambda j: (0, 0)),
                  pl.BlockSpec((d, tn), lambda j: (0, j)),
                  pl.BlockSpec((1, tn), lambda j: (0, j))],
        out_specs=pl.BlockSpec((rows, tn), lambda j: (0, j)),
        out_shape=jax.ShapeDtypeStruct((rows, n), F32),
        name="ada",
    )(c_pad, w_ada, b_ada)


def _in_proj_kernel(x_ref, mod_ref, nw_ref, w_ref, *o_refs):
    x = x_ref[...]
    shift = mod_ref[0, 0:1, :]
    scale = mod_ref[0, 1:2, :]
    h = (_rms(x) * nw_ref[...]) * (1.0 + scale) + shift
    hb = h.astype(BF16)
    off = 0
    for o_ref, width in zip(o_refs, IN_SIZES):
        o_ref[...] = jnp.dot(hb, w_ref[:, off:off + width],
                             preferred_element_type=F32).astype(o_ref.dtype)
        off += width


def _in_proj(x2, mod, norm_w, w_in_bf16, seq):
    t, d = x2.shape
    tm = min(ROW_TILE, seq)
    per_batch = seq // tm
    in_w = w_in_bf16.shape[1]
    return pl.pallas_call(
        _in_proj_kernel,
        grid=(t // tm,),
        in_specs=[pl.BlockSpec((tm, d), lambda i: (i, 0)),
                  pl.BlockSpec((1, 6, d), lambda i: (i // per_batch, 0, 0)),
                  pl.BlockSpec((1, d), lambda i: (0, 0)),
                  pl.BlockSpec((d, in_w), lambda i: (0, 0))],
        out_specs=[pl.BlockSpec((tm, w), lambda i: (i, 0)) for w in IN_SIZES],
        out_shape=[jax.ShapeDtypeStruct((t, w), BF16) for w in IN_SIZES],
        compiler_params=pltpu.CompilerParams(vmem_limit_bytes=VMEM_LIMIT),
        name="in_proj",
    )(x2, mod, norm_w, w_in_bf16)


def _rotary(x, cos, sin_even, sin_odd):
    nxt = pltpu.roll(x, LANES - 1, 1)
    prv = pltpu.roll(x, 1, 1)
    return x * cos + nxt * sin_even + prv * sin_odd


def _ret_kernel(q_ref, k_ref, v_ref, g_ref, cos_ref, sine_ref, sino_ref,
                hmask_ref, xi_ref, zeta_ref, dmask_ref, gch_ref, o_ref, state_ref):
    @pl.when(pl.program_id(1) == 0)
    def _():
        state_ref[...] = jnp.zeros_like(state_ref)

    n_sub = q_ref.shape[0] // RET_CHUNK
    for c in range(n_sub):
        rows = slice(c * RET_CHUNK, (c + 1) * RET_CHUNK)
        for pair in range(RET_HEADS // 2):
            lanes = slice(pair * LANES, (pair + 1) * LANES)
            cos = cos_ref[rows, lanes]
            sine = sine_ref[rows, lanes]
            sino = sino_ref[rows, lanes]
            qr = _rotary(q_ref[rows, lanes].astype(F32), cos, sine, sino)
            kr = _rotary(k_ref[rows, lanes].astype(F32), cos, sine, sino) * (RET_KEY_DIM ** -0.5)
            qb = qr.astype(BF16)
            for hh in range(2):
                h = 2 * pair + hh
                vcols = slice(h * RET_VAL_DIM, (h + 1) * RET_VAL_DIM)
                v = v_ref[rows, vcols]
                km = (kr * hmask_ref[h]).astype(BF16)
                scores = lax.dot_general(qb, km, (((1,), (1,)), ((), ())),
                                         preferred_element_type=F32) * dmask_ref[h]
                inner = jnp.dot(scores.astype(BF16), v, preferred_element_type=F32)
                qx = (qr * xi_ref[h]).astype(BF16)
                state = state_ref[h]
                cross = jnp.dot(qx, state.astype(BF16), preferred_element_type=F32)
                kz = (kr * zeta_ref[h]).astype(BF16)
                kv = lax.dot_general(kz, v, (((0,), (0,)), ((), ())),
                                     preferred_element_type=F32)
                state_ref[h] = state * gch_ref[h] + kv
                y = _rms(inner + cross)
                g = g_ref[rows, vcols].astype(F32)
                o_ref[rows, vcols] = (g * jax.nn.sigmoid(g) * y).astype(o_ref.dtype)


def _retention_tables(seq):
    dk, c, nh = RET_KEY_DIM, RET_CHUNK, RET_HEADS
    pos = jnp.arange(seq, dtype=F32)
    inv_freq = 1.0 / (10000.0 ** jnp.linspace(0.0, 1.0, dk // 2, dtype=F32))
    ang = pos[:, None] * jnp.repeat(inv_freq, 2)[None, :]
    sin = jnp.tile(jnp.sin(ang), (1, nh))
    cos = jnp.tile(jnp.cos(ang), (1, nh))
    even = (jnp.arange(nh * dk) % 2 == 0)[None, :]
    sin_even = jnp.where(even, -sin, 0.0)
    sin_odd = jnp.where(even, 0.0, sin)
    log_g = jnp.log(1.0 - 2.0 ** (-5.0 - jnp.arange(nh, dtype=F32)))
    i = jnp.arange(c, dtype=F32)
    rel = i[:, None] - i[None, :]
    dmask = jnp.where(rel[None] >= 0,
                      jnp.exp(jnp.maximum(rel, 0.0)[None] * log_g[:, None, None]), 0.0)
    zeta = jnp.exp((c - 1.0 - i)[None, :] * log_g[:, None])
    xi = jnp.exp((i + 1.0)[None, :] * log_g[:, None])
    g_chunk = jnp.exp(c * log_g)
    lane = jnp.arange(LANES)
    hmask = jnp.stack([((lane // dk) == (h % 2)).astype(F32) for h in range(nh)])[:, None, :]
    xi_t = xi[:, :, None] * hmask
    zeta_t = zeta[:, :, None] * hmask
    gch = jnp.broadcast_to(g_chunk[:, None, None], (nh, 1, LANES))
    return cos, sin_even, sin_odd, hmask, xi_t, zeta_t, dmask, gch


def _retention(q, k, v, g, batch, seq):
    t = q.shape[0]
    rb = min(RET_ROWS, seq)
    per_batch = seq // rb
    cos, sin_even, sin_odd, hmask, xi_t, zeta_t, dmask, gch = _retention_tables(seq)
    row = lambda w: pl.BlockSpec((rb, w), lambda b, j: (b * per_batch + j, 0))
    tab = lambda w: pl.BlockSpec((rb, w), lambda b, j: (j, 0))
    full = lambda a: pl.BlockSpec(a.shape, lambda b, j: (0,) * a.ndim)
    return pl.pallas_call(
        _ret_kernel,
        grid=(batch, per_batch),
        in_specs=[row(RET_QK_W), row(RET_QK_W), row(RET_V_W), row(RET_V_W),
                  tab(RET_QK_W), tab(RET_QK_W), tab(RET_QK_W),
                  full(hmask), full(xi_t), full(zeta_t), full(dmask), full(gch)],
        out_specs=row(RET_V_W),
        out_shape=jax.ShapeDtypeStruct((t, RET_V_W), BF16),
        scratch_shapes=[pltpu.VMEM((RET_HEADS, LANES, RET_VAL_DIM), F32)],
        compiler_params=pltpu.CompilerParams(
            dimension_semantics=("arbitrary", "arbitrary"), vmem_limit_bytes=VMEM_LIMIT),
        name="retention",
    )(q, k, v, g, cos, sin_even, sin_odd, hmask, xi_t, zeta_t, dmask, gch)


def _t5_bucket(rel):
    n = jnp.maximum(rel, 0)
    max_exact = REL_BUCKETS // 2
    nf = jnp.maximum(n, 1).astype(F32)
    large = max_exact + (jnp.log(nf / max_exact) / math.log(REL_MAX_DIST / max_exact)
                         * (REL_BUCKETS - max_exact)).astype(jnp.int32)
    large = jnp.minimum(large, REL_BUCKETS - 1)
    return jnp.where(n < max_exact, n, large)


def _bias_tiles(rel_bias, blk):
    r = jnp.arange(blk, dtype=jnp.int32)
    far = rel_bias[REL_BUCKETS - 1]
    rel0 = r[None, :] - r[:, None]
    rel1 = rel0 + blk
    buckets = jnp.arange(REL_BUCKETS, dtype=jnp.int32)

    def tile(rel):
        hot = (_t5_bucket(rel)[:, :, None] == buckets).astype(F32)
        return jnp.einsum('krb,bh->hkr', hot, rel_bias, precision=lax.Precision.HIGHEST)

    b0 = jnp.where(rel0[None] >= 0, tile(rel0) - far[:, None, None], NEG_BIG)
    b1 = tile(rel1) - far[:, None, None]
    return b0, b1


def _attn_kernel(q_ref, k_ref, vt_ref, b0_ref, b1_ref, lq1_ref, lk1_ref, lq2_ref, lk2_ref,
                 sw_ref, o_ref, m_ref, l_ref, acc_ref):
    blk = q_ref.shape[0]
    i = pl.program_id(2)
    lane = lax.broadcasted_iota(jnp.int32, (1, LANES), 1)
    q = q_ref[...] * (DIFF_HEAD_DIM ** -0.5)
    zero = jnp.zeros_like(q)
    qm = (jnp.where(lane < DIFF_HEAD_DIM, q, zero), jnp.where(lane >= DIFF_HEAD_DIM, q, zero))

    m_ref[...] = jnp.full_like(m_ref, NEG_BIG)
    l_ref[...] = jnp.zeros_like(l_ref)
    acc_ref[...] = jnp.zeros_like(acc_ref)

    def step(j, bias):
        start = pl.multiple_of(j * blk, blk)
        kb = k_ref[pl.ds(start, blk), :]
        vt = vt_ref[0, 0, j]
        for mi in range(2):
            s = lax.dot_general(kb, qm[mi], (((1,), (1,)), ((), ())),
                                preferred_element_type=F32)
            if bias is not None:
                s = s + bias
            m_old = m_ref[mi]
            m_new = jnp.maximum(m_old, jnp.max(s, axis=0, keepdims=True))
            alpha = jnp.exp(m_old - m_new)
            p = jnp.exp(s - m_new)
            l_ref[mi] = alpha * l_ref[mi] + jnp.sum(p, axis=0, keepdims=True)
            acc_ref[mi] = alpha * acc_ref[mi] + jnp.dot(vt, p.astype(BF16),
                                                        preferred_element_type=F32)
            m_ref[mi] = m_new

    def far_body(j, carry):
        step(j, None)
        return carry

    lax.fori_loop(0, jnp.maximum(i - 1, 0), far_body, 0)

    @pl.when(i >= 1)
    def _():
        step(i - 1, b1_ref[0])

    step(i, b0_ref[0])

    lam = (jnp.exp(jnp.sum(lq1_ref[...] * lk1_ref[...], axis=-1, keepdims=True))
           - jnp.exp(jnp.sum(lq2_ref[...] * lk2_ref[...], axis=-1, keepdims=True))
           + LAMBDA_INIT)
    a = acc_ref[0] / l_ref[0] - lam * (acc_ref[1] / l_ref[1])
    a = a * lax.rsqrt(jnp.mean(a * a, axis=0, keepdims=True) + NORM_EPS)
    o_ref[...] = (a.T * sw_ref[...] * (1.0 - LAMBDA_INIT)).astype(o_ref.dtype)


def _diff_attention(q, k, v, rel_bias, lq1, lk1, lq2, lk2, subln_w, batch, seq):
    t = q.shape[0]
    blk = min(ATT_BLOCK, seq)
    nq = seq // blk
    hw = 2 * DIFF_HEAD_DIM
    b0, b1 = _bias_tiles(rel_bias, blk)
    vt = jnp.transpose(v.reshape(batch, nq, blk, DIFF_HEADS, hw), (0, 3, 1, 4, 2))
    small = lambda a: pl.BlockSpec(a.shape, lambda b, h, i: (0,) * a.ndim)
    return pl.pallas_call(
        _attn_kernel,
        grid=(batch, DIFF_HEADS, nq),
        in_specs=[pl.BlockSpec((blk, hw), lambda b, h, i: (b * nq + i, h)),
                  pl.BlockSpec((seq, hw), lambda b, h, i: (b, h)),
                  pl.BlockSpec((1, 1, nq, hw, blk), lambda b, h, i: (b, h, 0, 0, 0)),
                  pl.BlockSpec((1, blk, blk), lambda b, h, i: (h, 0, 0)),
                  pl.BlockSpec((1, blk, blk), lambda b, h, i: (h, 0, 0)),
                  small(lq1), small(lk1), small(lq2), small(lk2), small(subln_w)],
        out_specs=pl.BlockSpec((blk, hw), lambda b, h, i: (b * nq + i, h)),
        out_shape=jax.ShapeDtypeStruct((t, DIFF_V_W), BF16),
        scratch_shapes=[pltpu.VMEM((2, 1, blk), F32), pltpu.VMEM((2, 1, blk), F32),
                        pltpu.VMEM((2, hw, blk), F32)],
        compiler_params=pltpu.CompilerParams(
            dimension_semantics=("arbitrary", "arbitrary", "arbitrary"),
            vmem_limit_bytes=VMEM_LIMIT),
        name="diff_attn",
    )(q, k, vt, b0, b1, lq1, lk1, lq2, lk2, subln_w)


def _out_kernel(yr_ref, yd_ref, x_ref, mod_ref, nw_ref, wo_ref, wr_ref, br_ref,
                x1_ref, h2_ref, meta_ref, gate_ref, cnt_ref, run_ref):
    tm = x_ref.shape[0]

    @pl.when(pl.program_id(0) == 0)
    def _():
        run_ref[...] = jnp.zeros_like(run_ref)

    mixed = (jnp.dot(yr_ref[...], wo_ref[0:RET_V_W, :], preferred_element_type=F32)
             + jnp.dot(yd_ref[...], wo_ref[RET_V_W:, :], preferred_element_type=F32))
    gate1 = mod_ref[0, 2:3, :]
    shift2 = mod_ref[0, 3:4, :]
    scale2 = mod_ref[0, 4:5, :]
    x1 = x_ref[...] + gate1 * mixed
    x1_ref[...] = x1
    h2 = (_rms(x1) * nw_ref[...]) * (1.0 + scale2) + shift2
    _store_row_tiles(h2_ref, h2)

    logits = jnp.dot(h2, wr_ref[...], precision=lax.Precision.HIGHEST,
                     preferred_element_type=F32) + br_ref[...]
    lane = lax.broadcasted_iota(jnp.int32, logits.shape, 1)
    work = logits
    vals, idxs, hots = [], [], []
    for _ in range(TOP_K):
        mx = jnp.max(work, axis=-1, keepdims=True)
        idx = jnp.min(jnp.where(work == mx, lane, N_EXPERTS), axis=-1, keepdims=True)
        hot = lane == idx
        vals.append(mx)
        idxs.append(idx)
        hots.append(hot)
        work = jnp.where(hot, -jnp.inf, work)
    exps = [jnp.exp(v - vals[0]) for v in vals]
    denom = exps[0] + exps[1] + exps[2] + exps[3]

    sel = jnp.zeros(logits.shape, F32)
    for hot in hots:
        sel = sel + hot.astype(F32)
    r = lax.broadcasted_iota(jnp.int32, (tm, tm), 0)
    c = lax.broadcasted_iota(jnp.int32, (tm, tm), 1)
    before = (c < r).astype(BF16)
    prefix = jnp.dot(before, sel.astype(BF16), preferred_element_type=F32) + run_ref[...]
    ranks = [jnp.sum(jnp.where(hot, prefix, 0.0), axis=-1, keepdims=True) for hot in hots]
    run_ref[...] = run_ref[...] + jnp.sum(sel, axis=0, keepdims=True)
    cnt_ref[...] = run_ref[...].astype(jnp.int32)

    lane8 = lax.broadcasted_iota(jnp.int32, (tm, 2 * TOP_K), 1)
    meta = jnp.zeros((tm, 2 * TOP_K), jnp.int32)
    gates = jnp.zeros((tm, 2 * TOP_K), F32)
    for kk in range(TOP_K):
        meta = jnp.where(lane8 == kk, idxs[kk], meta)
        meta = jnp.where(lane8 == TOP_K + kk, ranks[kk].astype(jnp.int32), meta)
        gates = jnp.where(lane8 == kk, exps[kk] / denom, gates)
    meta_ref[...] = meta
    gate_ref[...] = gates


def _out_router(y_r, y_d, x2, mod, norm_w, w_out_bf16, w_router, b_router, seq):
    t, d = x2.shape
    tm = min(ROW_TILE, seq)
    per_batch = seq // tm
    ne = w_router.shape[1]
    row = lambda w: pl.BlockSpec((tm, w), lambda i: (i, 0))
    const = lambda a: pl.BlockSpec(a.shape, lambda i: (0,) * a.ndim)
    return pl.pallas_call(
        _out_kernel,
        grid=(t // tm,),
        in_specs=[row(RET_V_W), row(DIFF_V_W), row(d),
                  pl.BlockSpec((1, 6, d), lambda i: (i // per_batch, 0, 0)),
                  const(norm_w), const(w_out_bf16), const(w_router), const(b_router)],
        out_specs=[row(d), pl.BlockSpec((tm * ROW_SUB, LANES), lambda i: (i, 0)),
                   row(2 * TOP_K), row(2 * TOP_K),
                   pl.BlockSpec((1, ne), lambda i: (0, 0))],
        out_shape=[jax.ShapeDtypeStruct((t, d), F32),
                   jax.ShapeDtypeStruct((t * ROW_SUB, LANES), F32),
                   jax.ShapeDtypeStruct((t, 2 * TOP_K), jnp.int32),
                   jax.ShapeDtypeStruct((t, 2 * TOP_K), F32),
                   jax.ShapeDtypeStruct((1, ne), jnp.int32)],
        scratch_shapes=[pltpu.VMEM((1, ne), F32)],
        compiler_params=pltpu.CompilerParams(
            dimension_semantics=("arbitrary",), vmem_limit_bytes=VMEM_LIMIT),
        name="out_router",
    )(y_r, y_d, x2, mod, norm_w, w_out_bf16, w_router, b_router)


def _w1_prep_kernel(w_ref, p_ref, o_ref):
    for s in range(w_ref.shape[2] // PERM_TILE):
        cols = slice(s * PERM_TILE, (s + 1) * PERM_TILE)
        o_ref[0, :, cols] = jnp.dot(w_ref[0, :, cols].astype(BF16), p_ref[...],
                                    preferred_element_type=F32).astype(BF16)


def _pair_split_matrix():
    i = jnp.arange(PERM_TILE)[:, None]
    j = jnp.arange(PERM_TILE)[None, :]
    half = PERM_TILE // 2
    src = jnp.where(j < half, 2 * j, 2 * (j - half) + 1)
    return (i == src).astype(BF16)


def _w1_prep(w1):
    e, d, f2 = w1.shape
    tn = 1024
    return pl.pallas_call(
        _w1_prep_kernel,
        grid=(e, f2 // tn),
        in_specs=[pl.BlockSpec((1, d, tn), lambda i, j: (i, 0, j)),
                  pl.BlockSpec((PERM_TILE, PERM_TILE), lambda i, j: (0, 0))],
        out_specs=pl.BlockSpec((1, d, tn), lambda i, j: (i, 0, j)),
        out_shape=jax.ShapeDtypeStruct((e, d, f2), BF16),
        compiler_params=pltpu.CompilerParams(vmem_limit_bytes=VMEM_LIMIT),
        name="w1_prep",
    )(w1, _pair_split_matrix())


def _pair_split_bias(b1):
    e, f2 = b1.shape
    nt = f2 // PERM_TILE
    g = b1[:, 0::2].reshape(e, nt, 1, PERM_TILE // 2)
    l = b1[:, 1::2].reshape(e, nt, 1, PERM_TILE // 2)
    return jnp.concatenate([g, l], axis=2).reshape(e, 1, f2)


def _expert_kernel(be_ref, nv_ref, tcur_ref, tnxt_ref, h_ref, w1_ref, b1_ref, w2_ref, b2_ref,
                   o_ref, xbuf, sems):
    b = pl.program_id(0)
    nb = pl.num_programs(0)
    nv = nv_ref[0]
    bm = xbuf.shape[1] // ROW_SUB
    slot = lax.rem(b, 2)

    def row_copy(tok, s, r):
        return pltpu.make_async_copy(h_ref.at[pl.ds(tok * ROW_SUB, ROW_SUB), :],
                                     xbuf.at[s, pl.ds(r * ROW_SUB, ROW_SUB), :], sems.at[s])

    def fetch(tok_ref, s, r0, r1):
        for r in range(r0, r1):
            row_copy(tok_ref[0, 0, r], s, r).start()

    def wait_block(s):
        pltpu.make_async_copy(h_ref.at[pl.ds(0, bm * ROW_SUB), :], xbuf.at[s], sems.at[s]).wait()

    @pl.when(b == 0)
    def _():
        fetch(tcur_ref, 0, 0, bm)

    n_tiles = w1_ref.shape[2] // PERM_TILE
    per_tile = bm // n_tiles

    @pl.when(b < nv)
    def _():
        wait_block(slot)
        x = _load_row_tiles(xbuf.at[slot], bm).astype(BF16)
        half = PERM_TILE // 2
        acts = []
        for j in range(n_tiles):
            fetch(tnxt_ref, 1 - slot, j * per_tile, (j + 1) * per_tile)
            cols = slice(j * PERM_TILE, (j + 1) * PERM_TILE)
            h = jnp.dot(x, w1_ref[0, :, cols], preferred_element_type=F32) + b1_ref[0, :, cols]
            glu = jnp.minimum(h[:, :half], SWIGLU_LIMIT)
            lin = jnp.clip(h[:, half:], -SWIGLU_LIMIT, SWIGLU_LIMIT)
            acts.append((glu * jax.nn.sigmoid(SWIGLU_ALPHA * glu) * (lin + 1.0)).astype(BF16))
        act = jnp.concatenate(acts, axis=1)
        y = jnp.dot(act, w2_ref[0], preferred_element_type=F32) + b2_ref[0]
        _store_row_tiles(o_ref, y)

    @pl.when(b >= nv)
    def _():
        wait_block(slot)
        fetch(tnxt_ref, 1 - slot, 0, bm)
        o_ref[...] = jnp.zeros_like(o_ref)

    @pl.when(b == nb - 1)
    def _():
        wait_block(1 - slot)


def _experts(block_e, n_valid, slot_tok, h2, w1p, b1p, w2, b2):
    d = D_MODEL
    f2 = w1p.shape[2]
    f = w2.shape[1]
    bm = EXPERT_ROWS
    nb = slot_tok.shape[0] // bm
    tok3 = slot_tok.reshape(nb, 1, bm)
    exp = lambda b, be, nv: be[jnp.maximum(jnp.minimum(b, nv[0] - 1), 0)]
    grid_spec = pltpu.PrefetchScalarGridSpec(
        num_scalar_prefetch=2,
        grid=(nb,),
        in_specs=[pl.BlockSpec((1, 1, bm), lambda b, be, nv: (b, 0, 0), memory_space=pltpu.SMEM),
                  pl.BlockSpec((1, 1, bm), lambda b, be, nv: (jnp.minimum(b + 1, nb - 1), 0, 0),
                               memory_space=pltpu.SMEM),
                  pl.BlockSpec(memory_space=pl.ANY),
                  pl.BlockSpec((1, d, f2), lambda b, be, nv: (exp(b, be, nv), 0, 0)),
                  pl.BlockSpec((1, 1, f2), lambda b, be, nv: (exp(b, be, nv), 0, 0)),
                  pl.BlockSpec((1, f, d), lambda b, be, nv: (exp(b, be, nv), 0, 0)),
                  pl.BlockSpec((1, 1, d), lambda b, be, nv: (exp(b, be, nv), 0, 0))],
        out_specs=pl.BlockSpec((bm * ROW_SUB, LANES), lambda b, be, nv: (b, 0)),
        scratch_shapes=[pltpu.VMEM((2, bm * ROW_SUB, LANES), F32), pltpu.SemaphoreType.DMA((2,))],
    )
    return pl.pallas_call(
        _expert_kernel,
        grid_spec=grid_spec,
        out_shape=jax.ShapeDtypeStruct((nb * bm * ROW_SUB, LANES), F32),
        compiler_params=pltpu.CompilerParams(
            dimension_semantics=("arbitrary",), vmem_limit_bytes=VMEM_LIMIT),
        name="experts",
    )(block_e, n_valid, tok3, tok3, h2, w1p, b1p, w2, b2)


def _combine_kernel(dcur_ref, dnxt_ref, gate_ref, x1_ref, mod_ref, nf_ref, y_ref, o_ref, buf, sems):
    nt = x1_ref.shape[0]
    i = pl.program_id(0)
    n = pl.num_programs(0)
    slot = lax.rem(i, 2)

    def row_copy(src_row, s, kk, r):
        return pltpu.make_async_copy(y_ref.at[pl.ds(src_row * ROW_SUB, ROW_SUB), :],
                                     buf.at[s, kk, pl.ds(r * ROW_SUB, ROW_SUB), :], sems.at[s])

    def fetch(d_ref, s):
        for j in range(nt):
            for kk in range(TOP_K):
                row_copy(d_ref[0, 0, j * TOP_K + kk], s, kk, j).start()

    def wait_all(s):
        for kk in range(TOP_K):
            pltpu.make_async_copy(y_ref.at[pl.ds(0, nt * ROW_SUB), :], buf.at[s, kk], sems.at[s]).wait()

    @pl.when(i == 0)
    def _():
        fetch(dcur_ref, 0)

    fetch(dnxt_ref, 1 - slot)
    wait_all(slot)

    g = gate_ref[...]
    parts = []
    for s in range(ROW_SUB):
        acc = buf[slot, 0, pl.ds(s, nt, stride=ROW_SUB), :] * g[:, 0:1]
        for kk in range(1, TOP_K):
            acc = acc + buf[slot, kk, pl.ds(s, nt, stride=ROW_SUB), :] * g[:, kk:kk + 1]
        parts.append(acc)
    moe = jnp.concatenate(parts, axis=1)
    gate2 = mod_ref[0, 5:6, :]
    x2 = x1_ref[...] + gate2 * moe
    o_ref[...] = _rms(x2) * nf_ref[...]

    @pl.when(i == n - 1)
    def _():
        wait_all(1 - slot)


def _combine(dest, gates, x1, mod, normf_w, y, seq):
    t, d = x1.shape
    nt = min(COMBINE_TOKENS, seq)
    steps = t // nt
    per_batch = seq // nt
    dest2 = dest.reshape(steps, 1, nt * TOP_K)
    return pl.pallas_call(
        _combine_kernel,
        grid=(steps,),
        in_specs=[pl.BlockSpec((1, 1, nt * TOP_K), lambda i: (i, 0, 0), memory_space=pltpu.SMEM),
                  pl.BlockSpec((1, 1, nt * TOP_K), lambda i: (jnp.minimum(i + 1, steps - 1), 0, 0),
                               memory_space=pltpu.SMEM),
                  pl.BlockSpec((nt, 2 * TOP_K), lambda i: (i, 0)),
                  pl.BlockSpec((nt, d), lambda i: (i, 0)),
                  pl.BlockSpec((1, 6, d), lambda i: (i // per_batch, 0, 0)),
                  pl.BlockSpec((1, d), lambda i: (0, 0)),
                  pl.BlockSpec(memory_space=pl.ANY)],
        out_specs=pl.BlockSpec((nt, d), lambda i: (i, 0)),
        out_shape=jax.ShapeDtypeStruct((t, d), F32),
        scratch_shapes=[pltpu.VMEM((2, TOP_K, nt * ROW_SUB, LANES), F32),
                        pltpu.SemaphoreType.DMA((2,))],
        compiler_params=pltpu.CompilerParams(
            dimension_semantics=("arbitrary",), vmem_limit_bytes=VMEM_LIMIT),
        name="combine",
    )(dest2, dest2, gates, x1, mod, normf_w, y)


def kernel(x, c, w_ada, b_ada, norm1_w, w_in, lam_q1, lam_k1, lam_q2, lam_k2, subln_w, rel_bias,
           w_out, norm2_w, w_router, b_router, w1, b1, w2, b2, normf_w):
    batch, seq, d = x.shape
    t = batch * seq
    x2 = x.reshape(t, d)

    c_pad = jnp.zeros((8, d), F32).at[:batch].set(c)
    mod = _ada(c_pad, w_ada[0], b_ada[0][None, :])[:batch].reshape(batch, 6, d)

    q_r, k_r, v_r, g_r, q_d, k_d, v_d = _in_proj(x2, mod, norm1_w[0][None, :],
                                                 w_in[0].astype(BF16), seq)
    y_r = _retention(q_r, k_r, v_r, g_r, batch, seq)
    y_d = _diff_attention(q_d, k_d, v_d, rel_bias, lam_q1, lam_k1, lam_q2, lam_k2,
                          subln_w, batch, seq)

    x1, h2, meta, gates, counts = _out_router(y_r, y_d, x2, mod, norm2_w[0][None, :],
                                              w_out[0].astype(BF16), w_router[0],
                                              b_router[0][None, :], seq)

    bm = EXPERT_ROWS
    counts = counts[0]
    padded = (counts + bm - 1) // bm * bm
    pad_end = jnp.cumsum(padded)
    pad_start = pad_end - padded
    sel_e = meta[:, :TOP_K]
    hot_e = sel_e[:, :, None] == jnp.arange(N_EXPERTS, dtype=jnp.int32)[None, None, :]
    dest = jnp.sum(jnp.where(hot_e, pad_start[None, None, :], 0), axis=-1) + meta[:, TOP_K:]
    n_rows = (t * TOP_K // bm + N_EXPERTS) * bm
    nb = n_rows // bm
    block_start = jnp.arange(nb, dtype=jnp.int32) * bm
    block_e = jnp.minimum(jnp.sum((pad_end[None, :] <= block_start[:, None]).astype(jnp.int32), axis=1),
                          N_EXPERTS - 1)
    n_valid = (pad_end[-1:] // bm).astype(jnp.int32)
    slot_tok = jnp.zeros((n_rows,), jnp.int32).at[dest.reshape(-1)].set(
        jnp.arange(t * TOP_K, dtype=jnp.int32) // TOP_K)

    ys = _experts(block_e, n_valid, slot_tok, h2, _w1_prep(w1[0]), _pair_split_bias(b1[0]),
                  w2[0].astype(BF16), b2[0][:, None, :])
    out = _combine(dest, gates, x1, mod, normf_w[None, :], ys, seq)
    return out.reshape(batch, seq, d)
```

```python
import functools
import math

import jax
import jax.numpy as jnp
from jax import lax
from jax.experimental import pallas as pl
from jax.experimental.pallas import tpu as pltpu

F32 = jnp.float32
BF16 = jnp.bfloat16

D_MODEL = 1024
RET_HEADS = 4
RET_KEY_DIM = 64
RET_VAL_DIM = 128
RET_QK_W = RET_HEADS * RET_KEY_DIM
RET_V_W = RET_HEADS * RET_VAL_DIM
RET_CHUNK = 128
DIFF_HEADS = 4
DIFF_HEAD_DIM = 64
DIFF_QK_W = DIFF_HEADS * 2 * DIFF_HEAD_DIM
DIFF_V_W = DIFF_HEADS * 2 * DIFF_HEAD_DIM
IN_SIZES = (RET_QK_W, RET_QK_W, RET_V_W, RET_V_W, DIFF_QK_W, DIFF_QK_W, DIFF_V_W)
REL_BUCKETS = 32
REL_MAX_DIST = 128
N_EXPERTS = 32
TOP_K = 4
SWIGLU_ALPHA = 1.702
SWIGLU_LIMIT = 7.0
NORM_EPS = 1e-6
LAMBDA_INIT = 0.8 - 0.6 * math.exp(-0.3 * 0)

LANES = 128
ROW_SUB = D_MODEL // LANES
NEG_BIG = -1e30
VMEM_LIMIT = 56 * 1024 * 1024

ROW_TILE = 512
RET_ROWS = 512
ATT_BLOCK = 512
EXPERT_ROWS = 256
PERM_TILE = 256
COMBINE_TOKENS = 128


def _rms(x):
    return x * lax.rsqrt(jnp.mean(x * x, axis=-1, keepdims=True) + NORM_EPS)


def _store_row_tiles(ref, x):
    rows = x.shape[0]
    for s in range(ROW_SUB):
        ref[pl.ds(s, rows, stride=ROW_SUB), :] = x[:, s * LANES:(s + 1) * LANES]


def _load_row_tiles(ref, rows):
    return jnp.concatenate([ref[pl.ds(s, rows, stride=ROW_SUB), :] for s in range(ROW_SUB)], axis=1)


def _ada_kernel(c_ref, w_ref, b_ref, o_ref):
    c = c_ref[...]
    cond = c * jax.nn.sigmoid(c)
    o_ref[...] = jnp.dot(cond, w_ref[...], precision=lax.Precision.HIGHEST,
                         preferred_element_type=F32) + b_ref[...]


def _ada(c_pad, w_ada, b_ada):
    rows, d = c_pad.shape
    n = w_ada.shape[1]
    tn = 1024
    return pl.pallas_call(
        _ada_kernel,
        grid=(n // tn,),
        in_specs=[pl.BlockSpec((rows, d), lambda j: (0, 0)),
                  pl.BlockSpec((d, tn), lambda j: (0, j)),
                  pl.BlockSpec((1, tn), lambda j: (0, j))],
        out_specs=pl.BlockSpec((rows, tn), lambda j: (0, j)),
        out_shape=jax.ShapeDtypeStruct((rows, n), F32),
        name="ada",
    )(c_pad, w_ada, b_ada)


def _in_proj_kernel(x_ref, mod_ref, nw_ref, w_ref, *o_refs):
    x = x_ref[...]
    shift = mod_ref[0, 0:1, :]
    scale = mod_ref[0, 1:2, :]
    h = (_rms(x) * nw_ref[...]) * (1.0 + scale) + shift
    hb = h.astype(BF16)
    off = 0
    for o_ref, width in zip(o_refs, IN_SIZES):
        o_ref[...] = jnp.dot(hb, w_ref[:, off:off + width],
                             preferred_element_type=F32).astype(o_ref.dtype)
        off += width


def _in_proj(x2, mod, norm_w, w_in_bf16, seq):
    t, d = x2.shape
    tm = min(ROW_TILE, seq)
    per_batch = seq // tm
    in_w = w_in_bf16.shape[1]
    return pl.pallas_call(
        _in_proj_kernel,
        grid=(t // tm,),
        in_specs=[pl.BlockSpec((tm, d), lambda i: (i, 0)),
                  pl.BlockSpec((1, 6, d), lambda i: (i // per_batch, 0, 0)),
                  pl.BlockSpec((1, d), lambda i: (0, 0)),
                  pl.BlockSpec((d, in_w), lambda i: (0, 0))],
        out_specs=[pl.BlockSpec((tm, w), lambda i: (i, 0)) for w in IN_SIZES],
        out_shape=[jax.ShapeDtypeStruct((t, w), BF16) for w in IN_SIZES],
        compiler_params=pltpu.CompilerParams(vmem_limit_bytes=VMEM_LIMIT),
        name="in_proj",
    )(x2, mod, norm_w, w_in_bf16)


def _rotary(x, cos, sin_even, sin_odd):
    nxt = pltpu.roll(x, LANES - 1, 1)
    prv = pltpu.roll(x, 1, 1)
    return x * cos + nxt * sin_even + prv * sin_odd


def _ret_kernel(q_ref, k_ref, v_ref, g_ref, cos_ref, sine_ref, sino_ref,
                hmask_ref, xi_ref, zeta_ref, dmask_ref, gch_ref, o_ref, state_ref):
    @pl.when(pl.program_id(1) == 0)
    def _():
        state_ref[...] = jnp.zeros_like(state_ref)

    n_sub = q_ref.shape[0] // RET_CHUNK
    for c in range(n_sub):
        rows = slice(c * RET_CHUNK, (c + 1) * RET_CHUNK)
        for pair in range(RET_HEADS // 2):
            lanes = slice(pair * LANES, (pair + 1) * LANES)
            cos = cos_ref[rows, lanes]
            sine = sine_ref[rows, lanes]
            sino = sino_ref[rows, lanes]
            qr = _rotary(q_ref[rows, lanes].astype(F32), cos, sine, sino)
            kr = _rotary(k_ref[rows, lanes].astype(F32), cos, sine, sino) * (RET_KEY_DIM ** -0.5)
            qb = qr.astype(BF16)
            for hh in range(2):
                h = 2 * pair + hh
                vcols = slice(h * RET_VAL_DIM, (h + 1) * RET_VAL_DIM)
                v = v_ref[rows, vcols]
                km = (kr * hmask_ref[h]).astype(BF16)
                scores = lax.dot_general(qb, km, (((1,), (1,)), ((), ())),
                                         preferred_element_type=F32) * dmask_ref[h]
                inner = jnp.dot(scores.astype(BF16), v, preferred_element_type=F32)
                qx = (qr * xi_ref[h]).astype(BF16)
                state = state_ref[h]
                cross = jnp.dot(qx, state.astype(BF16), preferred_element_type=F32)
                kz = (kr * zeta_ref[h]).astype(BF16)
                kv = lax.dot_general(kz, v, (((0,), (0,)), ((), ())),
                                     preferred_element_type=F32)
                state_ref[h] = state * gch_ref[h] + kv
                y = _rms(inner + cross)
                g = g_ref[rows, vcols].astype(F32)
                o_ref[rows, vcols] = (g * jax.nn.sigmoid(g) * y).astype(o_ref.dtype)


def _retention_tables(seq):
    dk, c, nh = RET_KEY_DIM, RET_CHUNK, RET_HEADS
    pos = jnp.arange(seq, dtype=F32)
    inv_freq = 1.0 / (10000.0 ** jnp.linspace(0.0, 1.0, dk // 2, dtype=F32))
    ang = pos[:, None] * jnp.repeat(inv_freq, 2)[None, :]
    sin = jnp.tile(jnp.sin(ang), (1, nh))
    cos = jnp.tile(jnp.cos(ang), (1, nh))
    even = (jnp.arange(nh * dk) % 2 == 0)[None, :]
    sin_even = jnp.where(even, -sin, 0.0)
    sin_odd = jnp.where(even, 0.0, sin)
    log_g = jnp.log(1.0 - 2.0 ** (-5.0 - jnp.arange(nh, dtype=F32)))
    i = jnp.arange(c, dtype=F32)
    rel = i[:, None] - i[None, :]
    dmask = jnp.where(rel[None] >= 0,
                      jnp.exp(jnp.maximum(rel, 0.0)[None] * log_g[:, None, None]), 0.0)
    zeta = jnp.exp((c - 1.0 - i)[None, :] * log_g[:, None])
    xi = jnp.exp((i + 1.0)[None, :] * log_g[:, None])
    g_chunk = jnp.exp(c * log_g)
    lane = jnp.arange(LANES)
    hmask = jnp.stack([((lane // dk) == (h % 2)).astype(F32) for h in range(nh)])[:, None, :]
    xi_t = xi[:, :, None] * hmask
    zeta_t = zeta[:, :, None] * hmask
    gch = jnp.broadcast_to(g_chunk[:, None, None], (nh, 1, LANES))
    return cos, sin_even, sin_odd, hmask, xi_t, zeta_t, dmask, gch


def _retention(q, k, v, g, batch, seq):
    t = q.shape[0]
    rb = min(RET_ROWS, seq)
    per_batch = seq // rb
    cos, sin_even, sin_odd, hmask, xi_t, zeta_t, dmask, gch = _retention_tables(seq)
    row = lambda w: pl.BlockSpec((rb, w), lambda b, j: (b * per_batch + j, 0))
    tab = lambda w: pl.BlockSpec((rb, w), lambda b, j: (j, 0))
    full = lambda a: pl.BlockSpec(a.shape, lambda b, j: (0,) * a.ndim)
    return pl.pallas_call(
        _ret_kernel,
        grid=(batch, per_batch),
        in_specs=[row(RET_QK_W), row(RET_QK_W), row(RET_V_W), row(RET_V_W),
                  tab(RET_QK_W), tab(RET_QK_W), tab(RET_QK_W),
                  full(hmask), full(xi_t), full(zeta_t), full(dmask), full(gch)],
        out_specs=row(RET_V_W),
        out_shape=jax.ShapeDtypeStruct((t, RET_V_W), BF16),
        scratch_shapes=[pltpu.VMEM((RET_HEADS, LANES, RET_VAL_DIM), F32)],
        compiler_params=pltpu.CompilerParams(
            dimension_semantics=("arbitrary", "arbitrary"), vmem_limit_bytes=VMEM_LIMIT),
        name="retention",
    )(q, k, v, g, cos, sin_even, sin_odd, hmask, xi_t, zeta_t, dmask, gch)


def _t5_bucket(rel):
    n = jnp.maximum(rel, 0)
    max_exact = REL_BUCKETS // 2
    nf = jnp.maximum(n, 1).astype(F32)
    large = max_exact + (jnp.log(nf / max_exact) / math.log(REL_MAX_DIST / max_exact)
                         * (REL_BUCKETS - max_exact)).astype(jnp.int32)
    large = jnp.minimum(large, REL_BUCKETS - 1)
    return jnp.where(n < max_exact, n, large)


def _bias_tiles(rel_bias, blk):
    r = jnp.arange(blk, dtype=jnp.int32)
    far = rel_bias[REL_BUCKETS - 1]
    rel0 = r[None, :] - r[:, None]
    rel1 = rel0 + blk
    buckets = jnp.arange(REL_BUCKETS, dtype=jnp.int32)

    def tile(rel):
        hot = (_t5_bucket(rel)[:, :, None] == buckets).astype(F32)
        return jnp.einsum('krb,bh->hkr', hot, rel_bias, precision=lax.Precision.HIGHEST)

    b0 = jnp.where(rel0[None] >= 0, tile(rel0) - far[:, None, None], NEG_BIG)
    b1 = tile(rel1) - far[:, None, None]
    return b0, b1


def _attn_kernel(q_ref, k_ref, vt_ref, b0_ref, b1_ref, lq1_ref, lk1_ref, lq2_ref, lk2_ref,
                 sw_ref, o_ref, m_ref, l_ref, acc_ref):
    blk = q_ref.shape[0]
    i = pl.program_id(2)
    lane = lax.broadcasted_iota(jnp.int32, (1, LANES), 1)
    q = q_ref[...] * (DIFF_HEAD_DIM ** -0.5)
    zero = jnp.zeros_like(q)
    qm = (jnp.where(lane < DIFF_HEAD_DIM, q, zero), jnp.where(lane >= DIFF_HEAD_DIM, q, zero))

    m_ref[...] = jnp.full_like(m_ref, NEG_BIG)
    l_ref[...] = jnp.zeros_like(l_ref)
    acc_ref[...] = jnp.zeros_like(acc_ref)

    def step(j, bias):
        start = pl.multiple_of(j * blk, blk)
        kb = k_ref[pl.ds(start, blk), :]
        vt = vt_ref[0, 0, j]
        for mi in range(2):
            s = lax.dot_general(kb, qm[mi], (((1,), (1,)), ((), ())),
                                preferred_element_type=F32)
            if bias is not None:
                s = s + bias
            m_old = m_ref[mi]
            m_new = jnp.maximum(m_old, jnp.max(s, axis=0, keepdims=True))
            alpha = jnp.exp(m_old - m_new)
            p = jnp.exp(s - m_new)
            l_ref[mi] = alpha * l_ref[mi] + jnp.sum(p, axis=0, keepdims=True)
            acc_ref[mi] = alpha * acc_ref[mi] + jnp.dot(vt, p.astype(BF16),
                                                        preferred_element_type=F32)
            m_ref[mi] = m_new

    def far_body(j, carry):
        step(j, None)
        return carry

    lax.fori_loop(0, jnp.maximum(i - 1, 0), far_body, 0)

    @pl.when(i >= 1)
    def _():
        step(i - 1, b1_ref[0])

    step(i, b0_ref[0])

    lam = (jnp.exp(jnp.sum(lq1_ref[...] * lk1_ref[...], axis=-1, keepdims=True))
           - jnp.exp(jnp.sum(lq2_ref[...] * lk2_ref[...], axis=-1, keepdims=True))
           + LAMBDA_INIT)
    a = acc_ref[0] / l_ref[0] - lam * (acc_ref[1] / l_ref[1])
    a = a * lax.rsqrt(jnp.mean(a * a, axis=0, keepdims=True) + NORM_EPS)
    o_ref[...] = (a.T * sw_ref[...] * (1.0 - LAMBDA_INIT)).astype(o_ref.dtype)


def _diff_attention(q, k, v, rel_bias, lq1, lk1, lq2, lk2, subln_w, batch, seq):
    t = q.shape[0]
    blk = min(ATT_BLOCK, seq)
    nq = seq // blk
    hw = 2 * DIFF_HEAD_DIM
    b0, b1 = _bias_tiles(rel_bias, blk)
    vt = jnp.transpose(v.reshape(batch, nq, blk, DIFF_HEADS, hw), (0, 3, 1, 4, 2))
    small = lambda a: pl.BlockSpec(a.shape, lambda b, h, i: (0,) * a.ndim)
    return pl.pallas_call(
        _attn_kernel,
        grid=(batch, DIFF_HEADS, nq),
        in_specs=[pl.BlockSpec((blk, hw), lambda b, h, i: (b * nq + i, h)),
                  pl.BlockSpec((seq, hw), lambda b, h, i: (b, h)),
                  pl.BlockSpec((1, 1, nq, hw, blk), lambda b, h, i: (b, h, 0, 0, 0)),
                  pl.BlockSpec((1, blk, blk), lambda b, h, i: (h, 0, 0)),
                  pl.BlockSpec((1, blk, blk), lambda b, h, i: (h, 0, 0)),
                  small(lq1), small(lk1), small(lq2), small(lk2), small(subln_w)],
        out_specs=pl.BlockSpec((blk, hw), lambda b, h, i: (b * nq + i, h)),
        out_shape=jax.ShapeDtypeStruct((t, DIFF_V_W), BF16),
        scratch_shapes=[pltpu.VMEM((2, 1, blk), F32), pltpu.VMEM((2, 1, blk), F32),
                        pltpu.VMEM((2, hw, blk), F32)],
        compiler_params=pltpu.CompilerParams(
            dimension_semantics=("arbitrary", "arbitrary", "arbitrary"),
            vmem_limit_bytes=VMEM_LIMIT),
        name="diff_attn",
    )(q, k, vt, b0, b1, lq1, lk1, lq2, lk2, subln_w)


def _out_kernel(yr_ref, yd_ref, x_ref, mod_ref, nw_ref, wo_ref, wr_ref, br_ref,
                x1_ref, h2_ref, meta_ref, gate_ref, cnt_ref, run_ref):
    tm = x_ref.shape[0]

    @pl.when(pl.program_id(0) == 0)
    def _():
        run_ref[...] = jnp.zeros_like(run_ref)

    mixed = (jnp.dot(yr_ref[...], wo_ref[0:RET_V_W, :], preferred_element_type=F32)
             + jnp.dot(yd_ref[...], wo_ref[RET_V_W:, :], preferred_element_type=F32))
    gate1 = mod_ref[0, 2:3, :]
    shift2 = mod_ref[0, 3:4, :]
    scale2 = mod_ref[0, 4:5, :]
    x1 = x_ref[...] + gate1 * mixed
    x1_ref[...] = x1
    h2 = (_rms(x1) * nw_ref[...]) * (1.0 + scale2) + shift2
    _store_row_tiles(h2_ref, h2)

    logits = jnp.dot(h2, wr_ref[...], precision=lax.Precision.HIGHEST,
                     preferred_element_type=F32) + br_ref[...]
    lane = lax.broadcasted_iota(jnp.int32, logits.shape, 1)
    work = logits
    vals, idxs, hots = [], [], []
    for _ in range(TOP_K):
        mx = jnp.max(work, axis=-1, keepdims=True)
        idx = jnp.min(jnp.where(work == mx, lane, N_EXPERTS), axis=-1, keepdims=True)
        hot = lane == idx
        vals.append(mx)
        idxs.append(idx)
        hots.append(hot)
        work = jnp.where(hot, -jnp.inf, work)
    exps = [jnp.exp(v - vals[0]) for v in vals]
    denom = exps[0] + exps[1] + exps[2] + exps[3]

    sel = jnp.zeros(logits.shape, F32)
    for hot in hots:
        sel = sel + hot.astype(F32)
    r = lax.broadcasted_iota(jnp.int32, (tm, tm), 0)
    c = lax.broadcasted_iota(jnp.int32, (tm, tm), 1)
    before = (c < r).astype(BF16)
    prefix = jnp.dot(before, sel.astype(BF16), preferred_element_type=F32) + run_ref[...]
    ranks = [jnp.sum(jnp.where(hot, prefix, 0.0), axis=-1, keepdims=True) for hot in hots]
    run_ref[...] = run_ref[...] + jnp.sum(sel, axis=0, keepdims=True)
    cnt_ref[...] = run_ref[...].astype(jnp.int32)

    lane8 = lax.broadcasted_iota(jnp.int32, (tm, 2 * TOP_K), 1)
    meta = jnp.zeros((tm, 2 * TOP_K), jnp.int32)
    gates = jnp.zeros((tm, 2 * TOP_K), F32)
    for kk in range(TOP_K):
        meta = jnp.where(lane8 == kk, idxs[kk], meta)
        meta = jnp.where(lane8 == TOP_K + kk, ranks[kk].astype(jnp.int32), meta)
        gates = jnp.where(lane8 == kk, exps[kk] / denom, gates)
    meta_ref[...] = meta
    gate_ref[...] = gates


def _out_router(y_r, y_d, x2, mod, norm_w, w_out_bf16, w_router, b_router, seq):
    t, d = x2.shape
    tm = min(ROW_TILE, seq)
    per_batch = seq // tm
    ne = w_router.shape[1]
    row = lambda w: pl.BlockSpec((tm, w), lambda i: (i, 0))
    const = lambda a: pl.BlockSpec(a.shape, lambda i: (0,) * a.ndim)
    return pl.pallas_call(
        _out_kernel,
        grid=(t // tm,),
        in_specs=[row(RET_V_W), row(DIFF_V_W), row(d),
                  pl.BlockSpec((1, 6, d), lambda i: (i // per_batch, 0, 0)),
                  const(norm_w), const(w_out_bf16), const(w_router), const(b_router)],
        out_specs=[row(d), pl.BlockSpec((tm * ROW_SUB, LANES), lambda i: (i, 0)),
                   row(2 * TOP_K), row(2 * TOP_K),
                   pl.BlockSpec((1, ne), lambda i: (0, 0))],
        out_shape=[jax.ShapeDtypeStruct((t, d), F32),
                   jax.ShapeDtypeStruct((t * ROW_SUB, LANES), F32),
                   jax.ShapeDtypeStruct((t, 2 * TOP_K), jnp.int32),
                   jax.ShapeDtypeStruct((t, 2 * TOP_K), F32),
                   jax.ShapeDtypeStruct((1, ne), jnp.int32)],
        scratch_shapes=[pltpu.VMEM((1, ne), F32)],
        compiler_params=pltpu.CompilerParams(
            dimension_semantics=("arbitrary",), vmem_limit_bytes=VMEM_LIMIT),
        name="out_router",
    )(y_r, y_d, x2, mod, norm_w, w_out_bf16, w_router, b_router)


def _w1_prep_kernel(w_ref, p_ref, o_ref):
    for s in range(w_ref.shape[2] // PERM_TILE):
        cols = slice(s * PERM_TILE, (s + 1) * PERM_TILE)
        o_ref[0, :, cols] = jnp.dot(w_ref[0, :, cols].astype(BF16), p_ref[...],
                                    preferred_element_type=F32).astype(BF16)


def _pair_split_matrix():
    i = jnp.arange(PERM_TILE)[:, None]
    j = jnp.arange(PERM_TILE)[None, :]
    half = PERM_TILE // 2
    src = jnp.where(j < half, 2 * j, 2 * (j - half) + 1)
    return (i == src).astype(BF16)


def _w1_prep(w1):
    e, d, f2 = w1.shape
    tn = 1024
    return pl.pallas_call(
        _w1_prep_kernel,
        grid=(e, f2 // tn),
        in_specs=[pl.BlockSpec((1, d, tn), lambda i, j: (i, 0, j)),
                  pl.BlockSpec((PERM_TILE, PERM_TILE), lambda i, j: (0, 0))],
        out_specs=pl.BlockSpec((1, d, tn), lambda i, j: (i, 0, j)),
        out_shape=jax.ShapeDtypeStruct((e, d, f2), BF16),
        compiler_params=pltpu.CompilerParams(vmem_limit_bytes=VMEM_LIMIT),
        name="w1_prep",
    )(w1, _pair_split_matrix())


def _pair_split_bias(b1):
    e, f2 = b1.shape
    nt = f2 // PERM_TILE
    g = b1[:, 0::2].reshape(e, nt, 1, PERM_TILE // 2)
    l = b1[:, 1::2].reshape(e, nt, 1, PERM_TILE // 2)
    return jnp.concatenate([g, l], axis=2).reshape(e, 1, f2)


def _expert_kernel(be_ref, nv_ref, tcur_ref, tnxt_ref, h_ref, w1_ref, b1_ref, w2_ref, b2_ref,
                   o_ref, xbuf, sems):
    b = pl.program_id(0)
    nb = pl.num_programs(0)
    nv = nv_ref[0]
    bm = xbuf.shape[1] // ROW_SUB
    slot = lax.rem(b, 2)

    def row_copy(tok, s, r):
        return pltpu.make_async_copy(h_ref.at[pl.ds(tok * ROW_SUB, ROW_SUB), :],
                                     xbuf.at[s, pl.ds(r * ROW_SUB, ROW_SUB), :], sems.at[s])

    def fetch(tok_ref, s, r0, r1):
        for r in range(r0, r1):
            row_copy(tok_ref[0, 0, r], s, r).start()

    def wait_block(s):
        pltpu.make_async_copy(h_ref.at[pl.ds(0, bm * ROW_SUB), :], xbuf.at[s], sems.at[s]).wait()

    @pl.when(b == 0)
    def _():
        fetch(tcur_ref, 0, 0, bm)

    n_tiles = w1_ref.shape[2] // PERM_TILE

    @pl.when(b < nv)
    def _():
        wait_block(slot)
        fetch(tnxt_ref, 1 - slot, 0, bm)
        x = _load_row_tiles(xbuf.at[slot], bm).astype(BF16)
        half = PERM_TILE // 2
        acts = []
        for j in range(n_tiles):
            cols = slice(j * PERM_TILE, (j + 1) * PERM_TILE)
            h = jnp.dot(x, w1_ref[0, :, cols], preferred_element_type=F32) + b1_ref[0, :, cols]
            glu = jnp.minimum(h[:, :half], SWIGLU_LIMIT)
            lin = jnp.clip(h[:, half:], -SWIGLU_LIMIT, SWIGLU_LIMIT)
            acts.append((glu * jax.nn.sigmoid(SWIGLU_ALPHA * glu) * (lin + 1.0)).astype(BF16))
        act = jnp.concatenate(acts, axis=1)
        y = jnp.dot(act, w2_ref[0], preferred_element_type=F32) + b2_ref[0]
        _store_row_tiles(o_ref, y)

    @pl.when(b >= nv)
    def _():
        wait_block(slot)
        fetch(tnxt_ref, 1 - slot, 0, bm)
        o_ref[...] = jnp.zeros_like(o_ref)

    @pl.when(b == nb - 1)
    def _():
        wait_block(1 - slot)


def _experts(block_e, n_valid, slot_tok, h2, w1p, b1p, w2, b2):
    d = D_MODEL
    f2 = w1p.shape[2]
    f = w2.shape[1]
    bm = EXPERT_ROWS
    nb = slot_tok.shape[0] // bm
    tok3 = slot_tok.reshape(nb, 1, bm)
    exp = lambda b, be, nv: be[jnp.maximum(jnp.minimum(b, nv[0] - 1), 0)]
    grid_spec = pltpu.PrefetchScalarGridSpec(
        num_scalar_prefetch=2,
        grid=(nb,),
        in_specs=[pl.BlockSpec((1, 1, bm), lambda b, be, nv: (b, 0, 0), memory_space=pltpu.SMEM),
                  pl.BlockSpec((1, 1, bm), lambda b, be, nv: (jnp.minimum(b + 1, nb - 1), 0, 0),
                               memory_space=pltpu.SMEM),
                  pl.BlockSpec(memory_space=pl.ANY),
                  pl.BlockSpec((1, d, f2), lambda b, be, nv: (exp(b, be, nv), 0, 0)),
                  pl.BlockSpec((1, 1, f2), lambda b, be, nv: (exp(b, be, nv), 0, 0)),
                  pl.BlockSpec((1, f, d), lambda b, be, nv: (exp(b, be, nv), 0, 0)),
                  pl.BlockSpec((1, 1, d), lambda b, be, nv: (exp(b, be, nv), 0, 0))],
        out_specs=pl.BlockSpec((bm * ROW_SUB, LANES), lambda b, be, nv: (b, 0)),
        scratch_shapes=[pltpu.VMEM((2, bm * ROW_SUB, LANES), F32), pltpu.SemaphoreType.DMA((2,))],
    )
    return pl.pallas_call(
        _expert_kernel,
        grid_spec=grid_spec,
        out_shape=jax.ShapeDtypeStruct((nb * bm * ROW_SUB, LANES), F32),
        compiler_params=pltpu.CompilerParams(
            dimension_semantics=("arbitrary",), vmem_limit_bytes=VMEM_LIMIT),
        name="experts",
    )(block_e, n_valid, tok3, tok3, h2, w1p, b1p, w2, b2)


def _combine_kernel(dcur_ref, dnxt_ref, gate_ref, x1_ref, mod_ref, nf_ref, y_ref, o_ref, buf, sems):
    nt = x1_ref.shape[0]
    i = pl.program_id(0)
    n = pl.num_programs(0)
    slot = lax.rem(i, 2)

    def row_copy(src_row, s, kk, r):
        return pltpu.make_async_copy(y_ref.at[pl.ds(src_row * ROW_SUB, ROW_SUB), :],
                                     buf.at[s, kk, pl.ds(r * ROW_SUB, ROW_SUB), :], sems.at[s])

    def fetch(d_ref, s):
        for j in range(nt):
            for kk in range(TOP_K):
                row_copy(d_ref[0, 0, j * TOP_K + kk], s, kk, j).start()

    def wait_all(s):
        for kk in range(TOP_K):
            pltpu.make_async_copy(y_ref.at[pl.ds(0, nt * ROW_SUB), :], buf.at[s, kk], sems.at[s]).wait()

    @pl.when(i == 0)
    def _():
        fetch(dcur_ref, 0)

    fetch(dnxt_ref, 1 - slot)
    wait_all(slot)

    g = gate_ref[...]
    parts = []
    for s in range(ROW_SUB):
        acc = buf[slot, 0, pl.ds(s, nt, stride=ROW_SUB), :] * g[:, 0:1]
        for kk in range(1, TOP_K):
            acc = acc + buf[slot, kk, pl.ds(s, nt, stride=ROW_SUB), :] * g[:, kk:kk + 1]
        parts.append(acc)
    moe = jnp.concatenate(parts, axis=1)
    gate2 = mod_ref[0, 5:6, :]
    x2 = x1_ref[...] + gate2 * moe
    o_ref[...] = _rms(x2) * nf_ref[...]

    @pl.when(i == n - 1)
    def _():
        wait_all(1 - slot)


def _combine(dest, gates, x1, mod, normf_w, y, seq):
    t, d = x1.shape
    nt = min(COMBINE_TOKENS, seq)
    steps = t // nt
    per_batch = seq // nt
    dest2 = dest.reshape(steps, 1, nt * TOP_K)
    return pl.pallas_call(
        _combine_kernel,
        grid=(steps,),
        in_specs=[pl.BlockSpec((1, 1, nt * TOP_K), lambda i: (i, 0, 0), memory_space=pltpu.SMEM),
                  pl.BlockSpec((1, 1, nt * TOP_K), lambda i: (jnp.minimum(i + 1, steps - 1), 0, 0),
                               memory_space=pltpu.SMEM),
                  pl.BlockSpec((nt, 2 * TOP_K), lambda i: (i, 0)),
                  pl.BlockSpec((nt, d), lambda i: (i, 0)),
                  pl.BlockSpec((1, 6, d), lambda i: (i // per_batch, 0, 0)),
                  pl.BlockSpec((1, d), lambda i: (0, 0)),
                  pl.BlockSpec(memory_space=pl.ANY)],
        out_specs=pl.BlockSpec((nt, d), lambda i: (i, 0)),
        out_shape=jax.ShapeDtypeStruct((t, d), F32),
        scratch_shapes=[pltpu.VMEM((2, TOP_K, nt * ROW_SUB, LANES), F32),
                        pltpu.SemaphoreType.DMA((2,))],
        compiler_params=pltpu.CompilerParams(
            dimension_semantics=("arbitrary",), vmem_limit_bytes=VMEM_LIMIT),
        name="combine",
    )(dest2, dest2, gates, x1, mod, normf_w, y)


def kernel(x, c, w_ada, b_ada, norm1_w, w_in, lam_q1, lam_k1, lam_q2, lam_k2, subln_w, rel_bias,
           w_out, norm2_w, w_router, b_router, w1, b1, w2, b2, normf_w):
    batch, seq, d = x.shape
    t = batch * seq
    x2 = x.reshape(t, d)

    c_pad = jnp.zeros((8, d), F32).at[:batch].set(c)
    mod = _ada(c_pad, w_ada[0], b_ada[0][None, :])[:batch].reshape(batch, 6, d)

    q_r, k_r, v_r, g_r, q_d, k_d, v_d = _in_proj(x2, mod, norm1_w[0][None, :],
                                                 w_in[0].astype(BF16), seq)
    y_r = _retention(q_r, k_r, v_r, g_r, batch, seq)
    y_d = _diff_attention(q_d, k_d, v_d, rel_bias, lam_q1, lam_k1, lam_q2, lam_k2,
                          subln_w, batch, seq)

    x1, h2, meta, gates, counts = _out_router(y_r, y_d, x2, mod, norm2_w[0][None, :],
                                              w_out[0].astype(BF16), w_router[0],
                                              b_router[0][None, :], seq)

    bm = EXPERT_ROWS
    counts = counts[0]
    padded = (counts + bm - 1) // bm * bm
    pad_end = jnp.cumsum(padded)
    pad_start = pad_end - padded
    sel_e = meta[:, :TOP_K]
    hot_e = sel_e[:, :, None] == jnp.arange(N_EXPERTS, dtype=jnp.int32)[None, None, :]
    dest = jnp.sum(jnp.where(hot_e, pad_start[None, None, :], 0), axis=-1) + meta[:, TOP_K:]
    n_rows = (t * TOP_K // bm + N_EXPERTS) * bm
    nb = n_rows // bm
    block_start = jnp.arange(nb, dtype=jnp.int32) * bm
    block_e = jnp.minimum(jnp.sum((pad_end[None, :] <= block_start[:, None]).astype(jnp.int32), axis=1),
                          N_EXPERTS - 1)
    n_valid = (pad_end[-1:] // bm).astype(jnp.int32)
    slot_tok = jnp.zeros((n_rows,), jnp.int32).at[dest.reshape(-1)].set(
        jnp.arange(t * TOP_K, dtype=jnp.int32) // TOP_K, unique_indices=True,
        mode='promise_in_bounds')

    ys = _experts(block_e, n_valid, slot_tok, h2, _w1_prep(w1[0]), _pair_split_bias(b1[0]),
                  w2[0].astype(BF16), b2[0][:, None, :])
    out = _combine(dest, gates, x1, mod, normf_w[None, :], ys, seq)
    return out.reshape(batch, seq, d)
```

```python
import functools
import math

import jax
import jax.numpy as jnp
from jax import lax
from jax.experimental import pallas as pl
from jax.experimental.pallas import tpu as pltpu
from jax.experimental.pallas import tpu_sc as plsc

F32 = jnp.float32
BF16 = jnp.bfloat16

D_MODEL = 1024
RET_HEADS = 4
RET_KEY_DIM = 64
RET_VAL_DIM = 128
RET_QK_W = RET_HEADS * RET_KEY_DIM
RET_V_W = RET_HEADS * RET_VAL_DIM
RET_CHUNK = 128
DIFF_HEADS = 4
DIFF_HEAD_DIM = 64
DIFF_QK_W = DIFF_HEADS * 2 * DIFF_HEAD_DIM
DIFF_V_W = DIFF_HEADS * 2 * DIFF_HEAD_DIM
IN_SIZES = (RET_QK_W, RET_QK_W, RET_V_W, RET_V_W, DIFF_QK_W, DIFF_QK_W, DIFF_V_W)
REL_BUCKETS = 32
REL_MAX_DIST = 128
N_EXPERTS = 32
TOP_K = 4
SWIGLU_ALPHA = 1.702
SWIGLU_LIMIT = 7.0
NORM_EPS = 1e-6
LAMBDA_INIT = 0.8 - 0.6 * math.exp(-0.3 * 0)

LANES = 128
ROW_SUB = D_MODEL // LANES
NEG_BIG = -1e30
VMEM_LIMIT = 56 * 1024 * 1024

ROW_TILE = 512
RET_ROWS = 512
ATT_BLOCK = 512
EXPERT_ROWS = 256
PERM_TILE = 256
COMBINE_TOKENS = 256
SC_CORES = 2
SC_SUBCORES = 16
SC_CHUNK = 32


def _rms(x):
    return x * lax.rsqrt(jnp.mean(x * x, axis=-1, keepdims=True) + NORM_EPS)


def _store_row_tiles(ref, x):
    rows = x.shape[0]
    for s in range(ROW_SUB):
        ref[pl.ds(s, rows, stride=ROW_SUB), :] = x[:, s * LANES:(s + 1) * LANES]


def _load_row_tiles(ref, rows):
    return jnp.concatenate([ref[pl.ds(s, rows, stride=ROW_SUB), :] for s in range(ROW_SUB)], axis=1)


def _ada_kernel(c_ref, w_ref, b_ref, o_ref):
    c = c_ref[...]
    cond = c * jax.nn.sigmoid(c)
    o_ref[...] = jnp.dot(cond, w_ref[...], precision=lax.Precision.HIGHEST,
                         preferred_element_type=F32) + b_ref[...]


def _ada(c_pad, w_ada, b_ada):
    rows, d = c_pad.shape
    n = w_ada.shape[1]
    tn = 1024
    return pl.pallas_call(
        _ada_kernel,
        grid=(n // tn,),
        in_specs=[pl.BlockSpec((rows, d), lambda j: (0, 0)),
                  pl.BlockSpec((d, tn), lambda j: (0, j)),
                  pl.BlockSpec((1, tn), lambda j: (0, j))],
        out_specs=pl.BlockSpec((rows, tn), lambda j: (0, j)),
        out_shape=jax.ShapeDtypeStruct((rows, n), F32),
        name="ada",
    )(c_pad, w_ada, b_ada)


def _in_proj_kernel(x_ref, mod_ref, nw_ref, w_ref, *o_refs):
    x = x_ref[...]
    shift = mod_ref[0, 0:1, :]
    scale = mod_ref[0, 1:2, :]
    h = (_rms(x) * nw_ref[...]) * (1.0 + scale) + shift
    hb = h.astype(BF16)
    off = 0
    for o_ref, width in zip(o_refs, IN_SIZES):
        o_ref[...] = jnp.dot(hb, w_ref[:, off:off + width],
                             preferred_element_type=F32).astype(o_ref.dtype)
        off += width


def _in_proj(x2, mod, norm_w, w_in_bf16, seq):
    t, d = x2.shape
    tm = min(ROW_TILE, seq)
    per_batch = seq // tm
    in_w = w_in_bf16.shape[1]
    return pl.pallas_call(
        _in_proj_kernel,
        grid=(t // tm,),
        in_specs=[pl.BlockSpec((tm, d), lambda i: (i, 0)),
                  pl.BlockSpec((1, 6, d), lambda i: (i // per_batch, 0, 0)),
                  pl.BlockSpec((1, d), lambda i: (0, 0)),
                  pl.BlockSpec((d, in_w), lambda i: (0, 0))],
        out_specs=[pl.BlockSpec((tm, w), lambda i: (i, 0)) for w in IN_SIZES],
        out_shape=[jax.ShapeDtypeStruct((t, w), BF16) for w in IN_SIZES],
        compiler_params=pltpu.CompilerParams(vmem_limit_bytes=VMEM_LIMIT),
        name="in_proj",
    )(x2, mod, norm_w, w_in_bf16)


def _rotary(x, cos, sin_even, sin_odd):
    nxt = pltpu.roll(x, LANES - 1, 1)
    prv = pltpu.roll(x, 1, 1)
    return x * cos + nxt * sin_even + prv * sin_odd


def _ret_kernel(q_ref, k_ref, v_ref, g_ref, cos_ref, sine_ref, sino_ref,
                hmask_ref, xi_ref, zeta_ref, dmask_ref, gch_ref, o_ref, state_ref):
    @pl.when(pl.program_id(1) == 0)
    def _():
        state_ref[...] = jnp.zeros_like(state_ref)

    n_sub = q_ref.shape[0] // RET_CHUNK
    for c in range(n_sub):
        rows = slice(c * RET_CHUNK, (c + 1) * RET_CHUNK)
        for pair in range(RET_HEADS // 2):
            lanes = slice(pair * LANES, (pair + 1) * LANES)
            cos = cos_ref[rows, lanes]
            sine = sine_ref[rows, lanes]
            sino = sino_ref[rows, lanes]
            qr = _rotary(q_ref[rows, lanes].astype(F32), cos, sine, sino)
            kr = _rotary(k_ref[rows, lanes].astype(F32), cos, sine, sino) * (RET_KEY_DIM ** -0.5)
            qb = qr.astype(BF16)
            for hh in range(2):
                h = 2 * pair + hh
                vcols = slice(h * RET_VAL_DIM, (h + 1) * RET_VAL_DIM)
                v = v_ref[rows, vcols]
                km = (kr * hmask_ref[h]).astype(BF16)
                scores = lax.dot_general(qb, km, (((1,), (1,)), ((), ())),
                                         preferred_element_type=F32) * dmask_ref[h]
                inner = jnp.dot(scores.astype(BF16), v, preferred_element_type=F32)
                qx = (qr * xi_ref[h]).astype(BF16)
                state = state_ref[h]
                cross = jnp.dot(qx, state.astype(BF16), preferred_element_type=F32)
                kz = (kr * zeta_ref[h]).astype(BF16)
                kv = lax.dot_general(kz, v, (((0,), (0,)), ((), ())),
                                     preferred_element_type=F32)
                state_ref[h] = state * gch_ref[h] + kv
                y = _rms(inner + cross)
                g = g_ref[rows, vcols].astype(F32)
                o_ref[rows, vcols] = (g * jax.nn.sigmoid(g) * y).astype(o_ref.dtype)


def _retention_tables(seq):
    dk, c, nh = RET_KEY_DIM, RET_CHUNK, RET_HEADS
    pos = jnp.arange(seq, dtype=F32)
    inv_freq = 1.0 / (10000.0 ** jnp.linspace(0.0, 1.0, dk // 2, dtype=F32))
    ang = pos[:, None] * jnp.repeat(inv_freq, 2)[None, :]
    sin = jnp.tile(jnp.sin(ang), (1, nh))
    cos = jnp.tile(jnp.cos(ang), (1, nh))
    even = (jnp.arange(nh * dk) % 2 == 0)[None, :]
    sin_even = jnp.where(even, -sin, 0.0)
    sin_odd = jnp.where(even, 0.0, sin)
    log_g = jnp.log(1.0 - 2.0 ** (-5.0 - jnp.arange(nh, dtype=F32)))
    i = jnp.arange(c, dtype=F32)
    rel = i[:, None] - i[None, :]
    dmask = jnp.where(rel[None] >= 0,
                      jnp.exp(jnp.maximum(rel, 0.0)[None] * log_g[:, None, None]), 0.0)
    zeta = jnp.exp((c - 1.0 - i)[None, :] * log_g[:, None])
    xi = jnp.exp((i + 1.0)[None, :] * log_g[:, None])
    g_chunk = jnp.exp(c * log_g)
    lane = jnp.arange(LANES)
    hmask = jnp.stack([((lane // dk) == (h % 2)).astype(F32) for h in range(nh)])[:, None, :]
    xi_t = xi[:, :, None] * hmask
    zeta_t = zeta[:, :, None] * hmask
    gch = jnp.broadcast_to(g_chunk[:, None, None], (nh, 1, LANES))
    return cos, sin_even, sin_odd, hmask, xi_t, zeta_t, dmask, gch


def _retention(q, k, v, g, batch, seq):
    t = q.shape[0]
    rb = min(RET_ROWS, seq)
    per_batch = seq // rb
    cos, sin_even, sin_odd, hmask, xi_t, zeta_t, dmask, gch = _retention_tables(seq)
    row = lambda w: pl.BlockSpec((rb, w), lambda b, j: (b * per_batch + j, 0))
    tab = lambda w: pl.BlockSpec((rb, w), lambda b, j: (j, 0))
    full = lambda a: pl.BlockSpec(a.shape, lambda b, j: (0,) * a.ndim)
    return pl.pallas_call(
        _ret_kernel,
        grid=(batch, per_batch),
        in_specs=[row(RET_QK_W), row(RET_QK_W), row(RET_V_W), row(RET_V_W),
                  tab(RET_QK_W), tab(RET_QK_W), tab(RET_QK_W),
                  full(hmask), full(xi_t), full(zeta_t), full(dmask), full(gch)],
        out_specs=row(RET_V_W),
        out_shape=jax.ShapeDtypeStruct((t, RET_V_W), BF16),
        scratch_shapes=[pltpu.VMEM((RET_HEADS, LANES, RET_VAL_DIM), F32)],
        compiler_params=pltpu.CompilerParams(
            dimension_semantics=("arbitrary", "arbitrary"), vmem_limit_bytes=VMEM_LIMIT),
        name="retention",
    )(q, k, v, g, cos, sin_even, sin_odd, hmask, xi_t, zeta_t, dmask, gch)


def _t5_bucket(rel):
    n = jnp.maximum(rel, 0)
    max_exact = REL_BUCKETS // 2
    nf = jnp.maximum(n, 1).astype(F32)
    large = max_exact + (jnp.log(nf / max_exact) / math.log(REL_MAX_DIST / max_exact)
                         * (REL_BUCKETS - max_exact)).astype(jnp.int32)
    large = jnp.minimum(large, REL_BUCKETS - 1)
    return jnp.where(n < max_exact, n, large)


def _bias_tiles(rel_bias, blk):
    r = jnp.arange(blk, dtype=jnp.int32)
    far = rel_bias[REL_BUCKETS - 1]
    rel0 = r[None, :] - r[:, None]
    rel1 = rel0 + blk
    buckets = jnp.arange(REL_BUCKETS, dtype=jnp.int32)

    def tile(rel):
        hot = (_t5_bucket(rel)[:, :, None] == buckets).astype(F32)
        return jnp.einsum('krb,bh->hkr', hot, rel_bias, precision=lax.Precision.HIGHEST)

    b0 = jnp.where(rel0[None] >= 0, tile(rel0) - far[:, None, None], NEG_BIG)
    b1 = tile(rel1) - far[:, None, None]
    return b0, b1


def _attn_kernel(q_ref, k_ref, vt_ref, b0_ref, b1_ref, lq1_ref, lk1_ref, lq2_ref, lk2_ref,
                 sw_ref, o_ref, m_ref, l_ref, acc_ref):
    blk = q_ref.shape[0]
    i = pl.program_id(2)
    lane = lax.broadcasted_iota(jnp.int32, (1, LANES), 1)
    q = q_ref[...] * (DIFF_HEAD_DIM ** -0.5)
    zero = jnp.zeros_like(q)
    qm = (jnp.where(lane < DIFF_HEAD_DIM, q, zero), jnp.where(lane >= DIFF_HEAD_DIM, q, zero))

    m_ref[...] = jnp.full_like(m_ref, NEG_BIG)
    l_ref[...] = jnp.zeros_like(l_ref)
    acc_ref[...] = jnp.zeros_like(acc_ref)

    def step(j, bias):
        start = pl.multiple_of(j * blk, blk)
        kb = k_ref[pl.ds(start, blk), :]
        vt = vt_ref[0, 0, j]
        for mi in range(2):
            s = lax.dot_general(kb, qm[mi], (((1,), (1,)), ((), ())),
                                preferred_element_type=F32)
            if bias is not None:
                s = s + bias
            m_old = m_ref[mi]
            m_new = jnp.maximum(m_old, jnp.max(s, axis=0, keepdims=True))
            alpha = jnp.exp(m_old - m_new)
            p = jnp.exp(s - m_new)
            l_ref[mi] = alpha * l_ref[mi] + jnp.sum(p, axis=0, keepdims=True)
            acc_ref[mi] = alpha * acc_ref[mi] + jnp.dot(vt, p.astype(BF16),
                                                        preferred_element_type=F32)
            m_ref[mi] = m_new

    def far_body(j, carry):
        step(j, None)
        return carry

    lax.fori_loop(0, jnp.maximum(i - 1, 0), far_body, 0)

    @pl.when(i >= 1)
    def _():
        step(i - 1, b1_ref[0])

    step(i, b0_ref[0])

    lam = (jnp.exp(jnp.sum(lq1_ref[...] * lk1_ref[...], axis=-1, keepdims=True))
           - jnp.exp(jnp.sum(lq2_ref[...] * lk2_ref[...], axis=-1, keepdims=True))
           + LAMBDA_INIT)
    a = acc_ref[0] / l_ref[0] - lam * (acc_ref[1] / l_ref[1])
    a = a * lax.rsqrt(jnp.mean(a * a, axis=0, keepdims=True) + NORM_EPS)
    o_ref[...] = (a.T * sw_ref[...] * (1.0 - LAMBDA_INIT)).astype(o_ref.dtype)


def _diff_attention(q, k, v, rel_bias, lq1, lk1, lq2, lk2, subln_w, batch, seq):
    t = q.shape[0]
    blk = min(ATT_BLOCK, seq)
    nq = seq // blk
    hw = 2 * DIFF_HEAD_DIM
    b0, b1 = _bias_tiles(rel_bias, blk)
    vt = jnp.transpose(v.reshape(batch, nq, blk, DIFF_HEADS, hw), (0, 3, 1, 4, 2))
    small = lambda a: pl.BlockSpec(a.shape, lambda b, h, i: (0,) * a.ndim)
    return pl.pallas_call(
        _attn_kernel,
        grid=(batch, DIFF_HEADS, nq),
        in_specs=[pl.BlockSpec((blk, hw), lambda b, h, i: (b * nq + i, h)),
                  pl.BlockSpec((seq, hw), lambda b, h, i: (b, h)),
                  pl.BlockSpec((1, 1, nq, hw, blk), lambda b, h, i: (b, h, 0, 0, 0)),
                  pl.BlockSpec((1, blk, blk), lambda b, h, i: (h, 0, 0)),
                  pl.BlockSpec((1, blk, blk), lambda b, h, i: (h, 0, 0)),
                  small(lq1), small(lk1), small(lq2), small(lk2), small(subln_w)],
        out_specs=pl.BlockSpec((blk, hw), lambda b, h, i: (b * nq + i, h)),
        out_shape=jax.ShapeDtypeStruct((t, DIFF_V_W), BF16),
        scratch_shapes=[pltpu.VMEM((2, 1, blk), F32), pltpu.VMEM((2, 1, blk), F32),
                        pltpu.VMEM((2, hw, blk), F32)],
        compiler_params=pltpu.CompilerParams(
            dimension_semantics=("arbitrary", "arbitrary", "arbitrary"),
            vmem_limit_bytes=VMEM_LIMIT),
        name="diff_attn",
    )(q, k, vt, b0, b1, lq1, lk1, lq2, lk2, subln_w)


def _out_kernel(yr_ref, yd_ref, x_ref, mod_ref, nw_ref, wo_ref, wr_ref, br_ref,
                x1_ref, h2_ref, meta_ref, gate_ref, cnt_ref, run_ref):
    tm = x_ref.shape[0]

    @pl.when(pl.program_id(0) == 0)
    def _():
        run_ref[...] = jnp.zeros_like(run_ref)

    mixed = (jnp.dot(yr_ref[...], wo_ref[0:RET_V_W, :], preferred_element_type=F32)
             + jnp.dot(yd_ref[...], wo_ref[RET_V_W:, :], preferred_element_type=F32))
    gate1 = mod_ref[0, 2:3, :]
    shift2 = mod_ref[0, 3:4, :]
    scale2 = mod_ref[0, 4:5, :]
    x1 = x_ref[...] + gate1 * mixed
    x1_ref[...] = x1
    h2 = (_rms(x1) * nw_ref[...]) * (1.0 + scale2) + shift2
    _store_row_tiles(h2_ref, h2)

    logits = jnp.dot(h2, wr_ref[...], precision=lax.Precision.HIGHEST,
                     preferred_element_type=F32) + br_ref[...]
    lane = lax.broadcasted_iota(jnp.int32, logits.shape, 1)
    work = logits
    vals, idxs, hots = [], [], []
    for _ in range(TOP_K):
        mx = jnp.max(work, axis=-1, keepdims=True)
        idx = jnp.min(jnp.where(work == mx, lane, N_EXPERTS), axis=-1, keepdims=True)
        hot = lane == idx
        vals.append(mx)
        idxs.append(idx)
        hots.append(hot)
        work = jnp.where(hot, -jnp.inf, work)
    exps = [jnp.exp(v - vals[0]) for v in vals]
    denom = exps[0] + exps[1] + exps[2] + exps[3]

    sel = jnp.zeros(logits.shape, F32)
    for hot in hots:
        sel = sel + hot.astype(F32)
    r = lax.broadcasted_iota(jnp.int32, (tm, tm), 0)
    c = lax.broadcasted_iota(jnp.int32, (tm, tm), 1)
    before = (c < r).astype(BF16)
    prefix = jnp.dot(before, sel.astype(BF16), preferred_element_type=F32) + run_ref[...]
    ranks = [jnp.sum(jnp.where(hot, prefix, 0.0), axis=-1, keepdims=True) for hot in hots]
    run_ref[...] = run_ref[...] + jnp.sum(sel, axis=0, keepdims=True)
    cnt_ref[...] = run_ref[...].astype(jnp.int32)

    lane8 = lax.broadcasted_iota(jnp.int32, (tm, 2 * TOP_K), 1)
    meta = jnp.zeros((tm, 2 * TOP_K), jnp.int32)
    gates = jnp.zeros((tm, 2 * TOP_K), F32)
    for kk in range(TOP_K):
        meta = jnp.where(lane8 == kk, idxs[kk], meta)
        meta = jnp.where(lane8 == TOP_K + kk, ranks[kk].astype(jnp.int32), meta)
        gates = jnp.where(lane8 == kk, exps[kk] / denom, gates)
    meta_ref[...] = meta
    gate_ref[...] = gates


def _out_router(y_r, y_d, x2, mod, norm_w, w_out_bf16, w_router, b_router, seq):
    t, d = x2.shape
    tm = min(ROW_TILE, seq)
    per_batch = seq // tm
    ne = w_router.shape[1]
    row = lambda w: pl.BlockSpec((tm, w), lambda i: (i, 0))
    const = lambda a: pl.BlockSpec(a.shape, lambda i: (0,) * a.ndim)
    return pl.pallas_call(
        _out_kernel,
        grid=(t // tm,),
        in_specs=[row(RET_V_W), row(DIFF_V_W), row(d),
                  pl.BlockSpec((1, 6, d), lambda i: (i // per_batch, 0, 0)),
                  const(norm_w), const(w_out_bf16), const(w_router), const(b_router)],
        out_specs=[row(d), pl.BlockSpec((tm * ROW_SUB, LANES), lambda i: (i, 0)),
                   row(2 * TOP_K), row(2 * TOP_K),
                   pl.BlockSpec((1, ne), lambda i: (0, 0))],
        out_shape=[jax.ShapeDtypeStruct((t, d), F32),
                   jax.ShapeDtypeStruct((t * ROW_SUB, LANES), F32),
                   jax.ShapeDtypeStruct((t, 2 * TOP_K), jnp.int32),
                   jax.ShapeDtypeStruct((t, 2 * TOP_K), F32),
                   jax.ShapeDtypeStruct((1, ne), jnp.int32)],
        scratch_shapes=[pltpu.VMEM((1, ne), F32)],
        compiler_params=pltpu.CompilerParams(
            dimension_semantics=("arbitrary",), vmem_limit_bytes=VMEM_LIMIT),
        name="out_router",
    )(y_r, y_d, x2, mod, norm_w, w_out_bf16, w_router, b_router)


def _w1_prep_kernel(w_ref, p_ref, o_ref):
    for s in range(w_ref.shape[2] // PERM_TILE):
        cols = slice(s * PERM_TILE, (s + 1) * PERM_TILE)
        o_ref[0, :, cols] = jnp.dot(w_ref[0, :, cols].astype(BF16), p_ref[...],
                                    preferred_element_type=F32).astype(BF16)


def _pair_split_matrix():
    i = jnp.arange(PERM_TILE)[:, None]
    j = jnp.arange(PERM_TILE)[None, :]
    half = PERM_TILE // 2
    src = jnp.where(j < half, 2 * j, 2 * (j - half) + 1)
    return (i == src).astype(BF16)


def _w1_prep(w1):
    e, d, f2 = w1.shape
    tn = 1024
    return pl.pallas_call(
        _w1_prep_kernel,
        grid=(e, f2 // tn),
        in_specs=[pl.BlockSpec((1, d, tn), lambda i, j: (i, 0, j)),
                  pl.BlockSpec((PERM_TILE, PERM_TILE), lambda i, j: (0, 0))],
        out_specs=pl.BlockSpec((1, d, tn), lambda i, j: (i, 0, j)),
        out_shape=jax.ShapeDtypeStruct((e, d, f2), BF16),
        compiler_params=pltpu.CompilerParams(vmem_limit_bytes=VMEM_LIMIT),
        name="w1_prep",
    )(w1, _pair_split_matrix())


def _pair_split_bias(b1):
    e, f2 = b1.shape
    nt = f2 // PERM_TILE
    g = b1[:, 0::2].reshape(e, nt, 1, PERM_TILE // 2)
    l = b1[:, 1::2].reshape(e, nt, 1, PERM_TILE // 2)
    return jnp.concatenate([g, l], axis=2).reshape(e, 1, f2)


def _sc_gather_rows(table, idx):
    n = idx.shape[0]
    row_shape = table.shape[1:]
    n_workers = SC_CORES * SC_SUBCORES
    per_w = n // n_workers
    n_chunks = per_w // SC_CHUNK
    assert per_w * n_workers == n and n_chunks * SC_CHUNK == per_w and n_chunks % 2 == 0
    mesh = plsc.VectorSubcoreMesh(core_axis_name="c", subcore_axis_name="s",
                                  num_cores=SC_CORES, num_subcores=SC_SUBCORES)

    def body(table_hbm, idx_hbm, out_hbm, idx_v, buf0, buf1, sem0, sem1):
        wid = lax.axis_index("s") * SC_CORES + lax.axis_index("c")
        base = wid * per_w
        pltpu.sync_copy(idx_hbm.at[pl.ds(base, per_w)], idx_v)
        bufs = (buf0, buf1)
        sems = (sem0, sem1)

        def gather(c, s):
            rows = idx_v.at[pl.ds(c * SC_CHUNK, SC_CHUNK)]
            return pltpu.make_async_copy(table_hbm.at[rows], bufs[s], sems[s])

        gather(0, 0).start()

        @pl.loop(0, n_chunks, step=2)
        def _(g):
            for s in range(2):
                c = g + s
                gather(c, s).wait()

                @pl.when(c + 1 < n_chunks)
                def _():
                    gather(c + 1, 1 - s).start()

                pltpu.sync_copy(bufs[s], out_hbm.at[pl.ds(base + c * SC_CHUNK, SC_CHUNK)])

    return pl.kernel(
        body,
        out_type=jax.ShapeDtypeStruct((n,) + row_shape, table.dtype),
        mesh=mesh,
        scratch_types=[pltpu.VMEM((per_w,), jnp.int32),
                       pltpu.VMEM((SC_CHUNK,) + row_shape, table.dtype),
                       pltpu.VMEM((SC_CHUNK,) + row_shape, table.dtype),
                       pltpu.SemaphoreType.DMA, pltpu.SemaphoreType.DMA],
        name="sc_gather_rows",
    )(table, idx)


def _expert_kernel(be_ref, nv_ref, x_ref, w1_ref, b1_ref, w2_ref, b2_ref, o_ref):
    b = pl.program_id(0)
    nv = nv_ref[0]
    bm = x_ref.shape[0] // ROW_SUB

    @pl.when(b < nv)
    def _():
        x = _load_row_tiles(x_ref, bm).astype(BF16)
        half = PERM_TILE // 2
        acts = []
        for j in range(w1_ref.shape[2] // PERM_TILE):
            cols = slice(j * PERM_TILE, (j + 1) * PERM_TILE)
            h = jnp.dot(x, w1_ref[0, :, cols], preferred_element_type=F32) + b1_ref[0, :, cols]
            glu = jnp.minimum(h[:, :half], SWIGLU_LIMIT)
            lin = jnp.clip(h[:, half:], -SWIGLU_LIMIT, SWIGLU_LIMIT)
            acts.append((glu * jax.nn.sigmoid(SWIGLU_ALPHA * glu) * (lin + 1.0)).astype(BF16))
        act = jnp.concatenate(acts, axis=1)
        y = jnp.dot(act, w2_ref[0], preferred_element_type=F32) + b2_ref[0]
        _store_row_tiles(o_ref, y)

    @pl.when(b >= nv)
    def _():
        o_ref[...] = jnp.zeros_like(o_ref)


def _experts(block_e, n_valid, xs, w1p, b1p, w2, b2):
    d = D_MODEL
    f2 = w1p.shape[2]
    f = w2.shape[1]
    bm = EXPERT_ROWS
    nb = xs.shape[0] // (bm * ROW_SUB)
    last = lambda b, nv: jnp.maximum(jnp.minimum(b, nv[0] - 1), 0)
    exp = lambda b, be, nv: be[last(b, nv)]
    grid_spec = pltpu.PrefetchScalarGridSpec(
        num_scalar_prefetch=2,
        grid=(nb,),
        in_specs=[pl.BlockSpec((bm * ROW_SUB, LANES), lambda b, be, nv: (last(b, nv), 0)),
                  pl.BlockSpec((1, d, f2), lambda b, be, nv: (exp(b, be, nv), 0, 0)),
                  pl.BlockSpec((1, 1, f2), lambda b, be, nv: (exp(b, be, nv), 0, 0)),
                  pl.BlockSpec((1, f, d), lambda b, be, nv: (exp(b, be, nv), 0, 0)),
                  pl.BlockSpec((1, 1, d), lambda b, be, nv: (exp(b, be, nv), 0, 0))],
        out_specs=pl.BlockSpec((bm * ROW_SUB, LANES), lambda b, be, nv: (b, 0)),
    )
    return pl.pallas_call(
        _expert_kernel,
        grid_spec=grid_spec,
        out_shape=jax.ShapeDtypeStruct((nb * bm * ROW_SUB, LANES), F32),
        compiler_params=pltpu.CompilerParams(
            dimension_semantics=("arbitrary",), vmem_limit_bytes=VMEM_LIMIT),
        name="experts",
    )(block_e, n_valid, xs, w1p, b1p, w2, b2)


def _combine_kernel(y_ref, gate_ref, x1_ref, mod_ref, nf_ref, o_ref):
    nt = x1_ref.shape[0]
    g = gate_ref[...]
    parts = []
    for s in range(ROW_SUB):
        acc = y_ref[0, 0, pl.ds(s, nt, stride=ROW_SUB), :] * g[:, 0:1]
        for kk in range(1, TOP_K):
            acc = acc + y_ref[0, kk, pl.ds(s, nt, stride=ROW_SUB), :] * g[:, kk:kk + 1]
        parts.append(acc)
    moe = jnp.concatenate(parts, axis=1)
    gate2 = mod_ref[0, 5:6, :]
    x2 = x1_ref[...] + gate2 * moe
    o_ref[...] = _rms(x2) * nf_ref[...]


def _combine(yg, gates, x1, mod, normf_w, seq):
    t, d = x1.shape
    steps, _, rows, _ = yg.shape
    nt = rows // ROW_SUB
    per_batch = seq // nt
    return pl.pallas_call(
        _combine_kernel,
        grid=(steps,),
        in_specs=[pl.BlockSpec((1, TOP_K, rows, LANES), lambda i: (i, 0, 0, 0)),
                  pl.BlockSpec((nt, 2 * TOP_K), lambda i: (i, 0)),
                  pl.BlockSpec((nt, d), lambda i: (i, 0)),
                  pl.BlockSpec((1, 6, d), lambda i: (i // per_batch, 0, 0)),
                  pl.BlockSpec((1, d), lambda i: (0, 0))],
        out_specs=pl.BlockSpec((nt, d), lambda i: (i, 0)),
        out_shape=jax.ShapeDtypeStruct((t, d), F32),
        compiler_params=pltpu.CompilerParams(vmem_limit_bytes=VMEM_LIMIT),
        name="combine",
    )(yg, gates, x1, mod, normf_w)


def kernel(x, c, w_ada, b_ada, norm1_w, w_in, lam_q1, lam_k1, lam_q2, lam_k2, subln_w, rel_bias,
           w_out, norm2_w, w_router, b_router, w1, b1, w2, b2, normf_w):
    batch, seq, d = x.shape
    t = batch * seq
    x2 = x.reshape(t, d)

    c_pad = jnp.zeros((8, d), F32).at[:batch].set(c)
    mod = _ada(c_pad, w_ada[0], b_ada[0][None, :])[:batch].reshape(batch, 6, d)

    q_r, k_r, v_r, g_r, q_d, k_d, v_d = _in_proj(x2, mod, norm1_w[0][None, :],
                                                 w_in[0].astype(BF16), seq)
    y_r = _retention(q_r, k_r, v_r, g_r, batch, seq)
    y_d = _diff_attention(q_d, k_d, v_d, rel_bias, lam_q1, lam_k1, lam_q2, lam_k2,
                          subln_w, batch, seq)

    x1, h2, meta, gates, counts = _out_router(y_r, y_d, x2, mod, norm2_w[0][None, :],
                                              w_out[0].astype(BF16), w_router[0],
                                              b_router[0][None, :], seq)

    bm = EXPERT_ROWS
    counts = counts[0]
    padded = (counts + bm - 1) // bm * bm
    pad_end = jnp.cumsum(padded)
    pad_start = pad_end - padded
    sel_e = meta[:, :TOP_K]
    hot_e = sel_e[:, :, None] == jnp.arange(N_EXPERTS, dtype=jnp.int32)[None, None, :]
    dest = jnp.sum(jnp.where(hot_e, pad_start[None, None, :], 0), axis=-1) + meta[:, TOP_K:]
    n_rows = (t * TOP_K // bm + N_EXPERTS) * bm
    nb = n_rows // bm
    block_start = jnp.arange(nb, dtype=jnp.int32) * bm
    block_e = jnp.minimum(jnp.sum((pad_end[None, :] <= block_start[:, None]).astype(jnp.int32), axis=1),
                          N_EXPERTS - 1)
    n_valid = (pad_end[-1:] // bm).astype(jnp.int32)
    slot_tok = jnp.zeros((n_rows,), jnp.int32).at[dest.reshape(-1)].set(
        jnp.arange(t * TOP_K, dtype=jnp.int32) // TOP_K, unique_indices=True,
        mode='promise_in_bounds')

    xs = _sc_gather_rows(h2.reshape(t, ROW_SUB, LANES), slot_tok).reshape(n_rows * ROW_SUB, LANES)
    ys = _experts(block_e, n_valid, xs, _w1_prep(w1[0]), _pair_split_bias(b1[0]),
                  w2[0].astype(BF16), b2[0][:, None, :])
    nt = min(COMBINE_TOKENS, seq)
    steps = t // nt
    slot_major = jnp.transpose(dest.reshape(steps, nt, TOP_K), (0, 2, 1)).reshape(-1)
    yg = _sc_gather_rows(ys.reshape(n_rows, ROW_SUB, LANES), slot_major)
    yg = yg.reshape(steps, TOP_K, nt * ROW_SUB, LANES)
    out = _combine(yg, gates, x1, mod, normf_w[None, :], seq)
    return out.reshape(batch, seq, d)
```

```python
import functools
import math

import jax
import jax.numpy as jnp
from jax import lax
from jax.experimental import pallas as pl
from jax.experimental.pallas import tpu as pltpu
from jax.experimental.pallas import tpu_sc as plsc

F32 = jnp.float32
BF16 = jnp.bfloat16

D_MODEL = 1024
RET_HEADS = 4
RET_KEY_DIM = 64
RET_VAL_DIM = 128
RET_QK_W = RET_HEADS * RET_KEY_DIM
RET_V_W = RET_HEADS * RET_VAL_DIM
RET_CHUNK = 128
DIFF_HEADS = 4
DIFF_HEAD_DIM = 64
DIFF_QK_W = DIFF_HEADS * 2 * DIFF_HEAD_DIM
DIFF_V_W = DIFF_HEADS * 2 * DIFF_HEAD_DIM
IN_SIZES = (RET_QK_W, RET_QK_W, RET_V_W, RET_V_W, DIFF_QK_W, DIFF_QK_W, DIFF_V_W)
REL_BUCKETS = 32
REL_MAX_DIST = 128
N_EXPERTS = 32
TOP_K = 4
SWIGLU_ALPHA = 1.702
SWIGLU_LIMIT = 7.0
NORM_EPS = 1e-6
LAMBDA_INIT = 0.8 - 0.6 * math.exp(-0.3 * 0)

LANES = 128
ROW_SUB = D_MODEL // LANES
NEG_BIG = -1e30
VMEM_LIMIT = 56 * 1024 * 1024

ROW_TILE = 512
RET_ROWS = 512
ATT_BLOCK = 512
EXPERT_ROWS = 256
PERM_TILE = 256
COMBINE_TOKENS = 256
SC_CORES = 2
SC_SUBCORES = 16
SC_CHUNK = 16
SC_BUFS = 7
SC_AHEAD = 4


def _rms(x):
    return x * lax.rsqrt(jnp.mean(x * x, axis=-1, keepdims=True) + NORM_EPS)


def _store_row_tiles(ref, x):
    rows = x.shape[0]
    for s in range(ROW_SUB):
        ref[pl.ds(s, rows, stride=ROW_SUB), :] = x[:, s * LANES:(s + 1) * LANES]


def _load_row_tiles(ref, rows):
    return jnp.concatenate([ref[pl.ds(s, rows, stride=ROW_SUB), :] for s in range(ROW_SUB)], axis=1)


def _ada_kernel(c_ref, w_ref, b_ref, o_ref):
    c = c_ref[...]
    cond = c * jax.nn.sigmoid(c)
    o_ref[...] = jnp.dot(cond, w_ref[...], precision=lax.Precision.HIGHEST,
                         preferred_element_type=F32) + b_ref[...]


def _ada(c_pad, w_ada, b_ada):
    rows, d = c_pad.shape
    n = w_ada.shape[1]
    tn = 1024
    return pl.pallas_call(
        _ada_kernel,
        grid=(n // tn,),
        in_specs=[pl.BlockSpec((rows, d), lambda j: (0, 0)),
                  pl.BlockSpec((d, tn), lambda j: (0, j)),
                  pl.BlockSpec((1, tn), lambda j: (0, j))],
        out_specs=pl.BlockSpec((rows, tn), lambda j: (0, j)),
        out_shape=jax.ShapeDtypeStruct((rows, n), F32),
        name="ada",
    )(c_pad, w_ada, b_ada)


def _in_proj_kernel(x_ref, mod_ref, nw_ref, w_ref, *o_refs):
    x = x_ref[...]
    shift = mod_ref[0, 0:1, :]
    scale = mod_ref[0, 1:2, :]
    h = (_rms(x) * nw_ref[...]) * (1.0 + scale) + shift
    hb = h.astype(BF16)
    off = 0
    for o_ref, width in zip(o_refs, IN_SIZES):
        o_ref[...] = jnp.dot(hb, w_ref[:, off:off + width],
                             preferred_element_type=F32).astype(o_ref.dtype)
        off += width


def _in_proj(x2, mod, norm_w, w_in_bf16, seq):
    t, d = x2.shape
    tm = min(ROW_TILE, seq)
    per_batch = seq // tm
    in_w = w_in_bf16.shape[1]
    return pl.pallas_call(
        _in_proj_kernel,
        grid=(t // tm,),
        in_specs=[pl.BlockSpec((tm, d), lambda i: (i, 0)),
                  pl.BlockSpec((1, 6, d), lambda i: (i // per_batch, 0, 0)),
                  pl.BlockSpec((1, d), lambda i: (0, 0)),
                  pl.BlockSpec((d, in_w), lambda i: (0, 0))],
        out_specs=[pl.BlockSpec((tm, w), lambda i: (i, 0)) for w in IN_SIZES],
        out_shape=[jax.ShapeDtypeStruct((t, w), BF16) for w in IN_SIZES],
        compiler_params=pltpu.CompilerParams(vmem_limit_bytes=VMEM_LIMIT),
        name="in_proj",
    )(x2, mod, norm_w, w_in_bf16)


def _rotary(x, cos, sin_even, sin_odd):
    nxt = pltpu.roll(x, LANES - 1, 1)
    prv = pltpu.roll(x, 1, 1)
    return x * cos + nxt * sin_even + prv * sin_odd


def _ret_kernel(q_ref, k_ref, v_ref, g_ref, cos_ref, sine_ref, sino_ref,
                hmask_ref, xi_ref, zeta_ref, dmask_ref, gch_ref, o_ref, state_ref):
    @pl.when(pl.program_id(1) == 0)
    def _():
        state_ref[...] = jnp.zeros_like(state_ref)

    n_sub = q_ref.shape[0] // RET_CHUNK
    for c in range(n_sub):
        rows = slice(c * RET_CHUNK, (c + 1) * RET_CHUNK)
        for pair in range(RET_HEADS // 2):
            lanes = slice(pair * LANES, (pair + 1) * LANES)
            cos = cos_ref[rows, lanes]
            sine = sine_ref[rows, lanes]
            sino = sino_ref[rows, lanes]
            qr = _rotary(q_ref[rows, lanes].astype(F32), cos, sine, sino)
            kr = _rotary(k_ref[rows, lanes].astype(F32), cos, sine, sino) * (RET_KEY_DIM ** -0.5)
            qb = qr.astype(BF16)
            for hh in range(2):
                h = 2 * pair + hh
                vcols = slice(h * RET_VAL_DIM, (h + 1) * RET_VAL_DIM)
                v = v_ref[rows, vcols]
                km = (kr * hmask_ref[h]).astype(BF16)
                scores = lax.dot_general(qb, km, (((1,), (1,)), ((), ())),
                                         preferred_element_type=F32) * dmask_ref[h]
                inner = jnp.dot(scores.astype(BF16), v, preferred_element_type=F32)
                qx = (qr * xi_ref[h]).astype(BF16)
                state = state_ref[h]
                cross = jnp.dot(qx, state.astype(BF16), preferred_element_type=F32)
                kz = (kr * zeta_ref[h]).astype(BF16)
                kv = lax.dot_general(kz, v, (((0,), (0,)), ((), ())),
                                     preferred_element_type=F32)
                state_ref[h] = state * gch_ref[h] + kv
                y = _rms(inner + cross)
                g = g_ref[rows, vcols].astype(F32)
                o_ref[rows, vcols] = (g * jax.nn.sigmoid(g) * y).astype(o_ref.dtype)


def _retention_tables(seq):
    dk, c, nh = RET_KEY_DIM, RET_CHUNK, RET_HEADS
    pos = jnp.arange(seq, dtype=F32)
    inv_freq = 1.0 / (10000.0 ** jnp.linspace(0.0, 1.0, dk // 2, dtype=F32))
    ang = pos[:, None] * jnp.repeat(inv_freq, 2)[None, :]
    sin = jnp.tile(jnp.sin(ang), (1, nh))
    cos = jnp.tile(jnp.cos(ang), (1, nh))
    even = (jnp.arange(nh * dk) % 2 == 0)[None, :]
    sin_even = jnp.where(even, -sin, 0.0)
    sin_odd = jnp.where(even, 0.0, sin)
    log_g = jnp.log(1.0 - 2.0 ** (-5.0 - jnp.arange(nh, dtype=F32)))
    i = jnp.arange(c, dtype=F32)
    rel = i[:, None] - i[None, :]
    dmask = jnp.where(rel[None] >= 0,
                      jnp.exp(jnp.maximum(rel, 0.0)[None] * log_g[:, None, None]), 0.0)
    zeta = jnp.exp((c - 1.0 - i)[None, :] * log_g[:, None])
    xi = jnp.exp((i + 1.0)[None, :] * log_g[:, None])
    g_chunk = jnp.exp(c * log_g)
    lane = jnp.arange(LANES)
    hmask = jnp.stack([((lane // dk) == (h % 2)).astype(F32) for h in range(nh)])[:, None, :]
    xi_t = xi[:, :, None] * hmask
    zeta_t = zeta[:, :, None] * hmask
    gch = jnp.broadcast_to(g_chunk[:, None, None], (nh, 1, LANES))
    return cos, sin_even, sin_odd, hmask, xi_t, zeta_t, dmask, gch


def _retention(q, k, v, g, batch, seq):
    t = q.shape[0]
    rb = min(RET_ROWS, seq)
    per_batch = seq // rb
    cos, sin_even, sin_odd, hmask, xi_t, zeta_t, dmask, gch = _retention_tables(seq)
    row = lambda w: pl.BlockSpec((rb, w), lambda b, j: (b * per_batch + j, 0))
    tab = lambda w: pl.BlockSpec((rb, w), lambda b, j: (j, 0))
    full = lambda a: pl.BlockSpec(a.shape, lambda b, j: (0,) * a.ndim)
    return pl.pallas_call(
        _ret_kernel,
        grid=(batch, per_batch),
        in_specs=[row(RET_QK_W), row(RET_QK_W), row(RET_V_W), row(RET_V_W),
                  tab(RET_QK_W), tab(RET_QK_W), tab(RET_QK_W),
                  full(hmask), full(xi_t), full(zeta_t), full(dmask), full(gch)],
        out_specs=row(RET_V_W),
        out_shape=jax.ShapeDtypeStruct((t, RET_V_W), BF16),
        scratch_shapes=[pltpu.VMEM((RET_HEADS, LANES, RET_VAL_DIM), F32)],
        compiler_params=pltpu.CompilerParams(
            dimension_semantics=("arbitrary", "arbitrary"), vmem_limit_bytes=VMEM_LIMIT),
        name="retention",
    )(q, k, v, g, cos, sin_even, sin_odd, hmask, xi_t, zeta_t, dmask, gch)


def _t5_bucket(rel):
    n = jnp.maximum(rel, 0)
    max_exact = REL_BUCKETS // 2
    nf = jnp.maximum(n, 1).astype(F32)
    large = max_exact + (jnp.log(nf / max_exact) / math.log(REL_MAX_DIST / max_exact)
                         * (REL_BUCKETS - max_exact)).astype(jnp.int32)
    large = jnp.minimum(large, REL_BUCKETS - 1)
    return jnp.where(n < max_exact, n, large)


def _bias_tiles(rel_bias, blk):
    r = jnp.arange(blk, dtype=jnp.int32)
    far = rel_bias[REL_BUCKETS - 1]
    rel0 = r[None, :] - r[:, None]
    rel1 = rel0 + blk
    buckets = jnp.arange(REL_BUCKETS, dtype=jnp.int32)

    def tile(rel):
        hot = (_t5_bucket(rel)[:, :, None] == buckets).astype(F32)
        return jnp.einsum('krb,bh->hkr', hot, rel_bias, precision=lax.Precision.HIGHEST)

    b0 = jnp.where(rel0[None] >= 0, tile(rel0) - far[:, None, None], NEG_BIG)
    b1 = tile(rel1) - far[:, None, None]
    return b0, b1


def _attn_kernel(q_ref, k_ref, vt_ref, b0_ref, b1_ref, lq1_ref, lk1_ref, lq2_ref, lk2_ref,
                 sw_ref, o_ref, m_ref, l_ref, acc_ref):
    blk = q_ref.shape[0]
    i = pl.program_id(2)
    lane = lax.broadcasted_iota(jnp.int32, (1, LANES), 1)
    q = q_ref[...] * (DIFF_HEAD_DIM ** -0.5)
    zero = jnp.zeros_like(q)
    qm = (jnp.where(lane < DIFF_HEAD_DIM, q, zero), jnp.where(lane >= DIFF_HEAD_DIM, q, zero))

    m_ref[...] = jnp.full_like(m_ref, NEG_BIG)
    l_ref[...] = jnp.zeros_like(l_ref)
    acc_ref[...] = jnp.zeros_like(acc_ref)

    def step(j, bias):
        start = pl.multiple_of(j * blk, blk)
        kb = k_ref[pl.ds(start, blk), :]
        vt = vt_ref[0, 0, j]
        for mi in range(2):
            s = lax.dot_general(kb, qm[mi], (((1,), (1,)), ((), ())),
                                preferred_element_type=F32)
            if bias is not None:
                s = s + bias
            m_old = m_ref[mi]
            m_new = jnp.maximum(m_old, jnp.max(s, axis=0, keepdims=True))
            alpha = jnp.exp(m_old - m_new)
            p = jnp.exp(s - m_new)
            l_ref[mi] = alpha * l_ref[mi] + jnp.sum(p, axis=0, keepdims=True)
            acc_ref[mi] = alpha * acc_ref[mi] + jnp.dot(vt, p.astype(BF16),
                                                        preferred_element_type=F32)
            m_ref[mi] = m_new

    def far_body(j, carry):
        step(j, None)
        return carry

    lax.fori_loop(0, jnp.maximum(i - 1, 0), far_body, 0)

    @pl.when(i >= 1)
    def _():
        step(i - 1, b1_ref[0])

    step(i, b0_ref[0])

    lam = (jnp.exp(jnp.sum(lq1_ref[...] * lk1_ref[...], axis=-1, keepdims=True))
           - jnp.exp(jnp.sum(lq2_ref[...] * lk2_ref[...], axis=-1, keepdims=True))
           + LAMBDA_INIT)
    a = acc_ref[0] / l_ref[0] - lam * (acc_ref[1] / l_ref[1])
    a = a * lax.rsqrt(jnp.mean(a * a, axis=0, keepdims=True) + NORM_EPS)
    o_ref[...] = (a.T * sw_ref[...] * (1.0 - LAMBDA_INIT)).astype(o_ref.dtype)


def _diff_attention(q, k, v, rel_bias, lq1, lk1, lq2, lk2, subln_w, batch, seq):
    t = q.shape[0]
    blk = min(ATT_BLOCK, seq)
    nq = seq // blk
    hw = 2 * DIFF_HEAD_DIM
    b0, b1 = _bias_tiles(rel_bias, blk)
    vt = jnp.transpose(v.reshape(batch, nq, blk, DIFF_HEADS, hw), (0, 3, 1, 4, 2))
    small = lambda a: pl.BlockSpec(a.shape, lambda b, h, i: (0,) * a.ndim)
    return pl.pallas_call(
        _attn_kernel,
        grid=(batch, DIFF_HEADS, nq),
        in_specs=[pl.BlockSpec((blk, hw), lambda b, h, i: (b * nq + i, h)),
                  pl.BlockSpec((seq, hw), lambda b, h, i: (b, h)),
                  pl.BlockSpec((1, 1, nq, hw, blk), lambda b, h, i: (b, h, 0, 0, 0)),
                  pl.BlockSpec((1, blk, blk), lambda b, h, i: (h, 0, 0)),
                  pl.BlockSpec((1, blk, blk), lambda b, h, i: (h, 0, 0)),
                  small(lq1), small(lk1), small(lq2), small(lk2), small(subln_w)],
        out_specs=pl.BlockSpec((blk, hw), lambda b, h, i: (b * nq + i, h)),
        out_shape=jax.ShapeDtypeStruct((t, DIFF_V_W), BF16),
        scratch_shapes=[pltpu.VMEM((2, 1, blk), F32), pltpu.VMEM((2, 1, blk), F32),
                        pltpu.VMEM((2, hw, blk), F32)],
        compiler_params=pltpu.CompilerParams(
            dimension_semantics=("arbitrary", "arbitrary", "arbitrary"),
            vmem_limit_bytes=VMEM_LIMIT),
        name="diff_attn",
    )(q, k, vt, b0, b1, lq1, lk1, lq2, lk2, subln_w)


def _out_kernel(yr_ref, yd_ref, x_ref, mod_ref, nw_ref, wo_ref, wr_ref, br_ref,
                x1_ref, h2_ref, meta_ref, gate_ref, cnt_ref, run_ref):
    tm = x_ref.shape[0]

    @pl.when(pl.program_id(0) == 0)
    def _():
        run_ref[...] = jnp.zeros_like(run_ref)

    mixed = (jnp.dot(yr_ref[...], wo_ref[0:RET_V_W, :], preferred_element_type=F32)
             + jnp.dot(yd_ref[...], wo_ref[RET_V_W:, :], preferred_element_type=F32))
    gate1 = mod_ref[0, 2:3, :]
    shift2 = mod_ref[0, 3:4, :]
    scale2 = mod_ref[0, 4:5, :]
    x1 = x_ref[...] + gate1 * mixed
    x1_ref[...] = x1
    h2 = (_rms(x1) * nw_ref[...]) * (1.0 + scale2) + shift2
    _store_row_tiles(h2_ref, h2)

    logits = jnp.dot(h2, wr_ref[...], precision=lax.Precision.HIGHEST,
                     preferred_element_type=F32) + br_ref[...]
    lane = lax.broadcasted_iota(jnp.int32, logits.shape, 1)
    work = logits
    vals, idxs, hots = [], [], []
    for _ in range(TOP_K):
        mx = jnp.max(work, axis=-1, keepdims=True)
        idx = jnp.min(jnp.where(work == mx, lane, N_EXPERTS), axis=-1, keepdims=True)
        hot = lane == idx
        vals.append(mx)
        idxs.append(idx)
        hots.append(hot)
        work = jnp.where(hot, -jnp.inf, work)
    exps = [jnp.exp(v - vals[0]) for v in vals]
    denom = exps[0] + exps[1] + exps[2] + exps[3]

    sel = jnp.zeros(logits.shape, F32)
    for hot in hots:
        sel = sel + hot.astype(F32)
    r = lax.broadcasted_iota(jnp.int32, (tm, tm), 0)
    c = lax.broadcasted_iota(jnp.int32, (tm, tm), 1)
    before = (c < r).astype(BF16)
    prefix = jnp.dot(before, sel.astype(BF16), preferred_element_type=F32) + run_ref[...]
    ranks = [jnp.sum(jnp.where(hot, prefix, 0.0), axis=-1, keepdims=True) for hot in hots]
    run_ref[...] = run_ref[...] + jnp.sum(sel, axis=0, keepdims=True)
    cnt_ref[...] = run_ref[...].astype(jnp.int32)

    lane8 = lax.broadcasted_iota(jnp.int32, (tm, 2 * TOP_K), 1)
    meta = jnp.zeros((tm, 2 * TOP_K), jnp.int32)
    gates = jnp.zeros((tm, 2 * TOP_K), F32)
    for kk in range(TOP_K):
        meta = jnp.where(lane8 == kk, idxs[kk], meta)
        meta = jnp.where(lane8 == TOP_K + kk, ranks[kk].astype(jnp.int32), meta)
        gates = jnp.where(lane8 == kk, exps[kk] / denom, gates)
    meta_ref[...] = meta
    gate_ref[...] = gates


def _out_router(y_r, y_d, x2, mod, norm_w, w_out_bf16, w_router, b_router, seq):
    t, d = x2.shape
    tm = min(ROW_TILE, seq)
    per_batch = seq // tm
    ne = w_router.shape[1]
    row = lambda w: pl.BlockSpec((tm, w), lambda i: (i, 0))
    const = lambda a: pl.BlockSpec(a.shape, lambda i: (0,) * a.ndim)
    return pl.pallas_call(
        _out_kernel,
        grid=(t // tm,),
        in_specs=[row(RET_V_W), row(DIFF_V_W), row(d),
                  pl.BlockSpec((1, 6, d), lambda i: (i // per_batch, 0, 0)),
                  const(norm_w), const(w_out_bf16), const(w_router), const(b_router)],
        out_specs=[row(d), pl.BlockSpec((tm * ROW_SUB, LANES), lambda i: (i, 0)),
                   row(2 * TOP_K), row(2 * TOP_K),
                   pl.BlockSpec((1, ne), lambda i: (0, 0))],
        out_shape=[jax.ShapeDtypeStruct((t, d), F32),
                   jax.ShapeDtypeStruct((t * ROW_SUB, LANES), F32),
                   jax.ShapeDtypeStruct((t, 2 * TOP_K), jnp.int32),
                   jax.ShapeDtypeStruct((t, 2 * TOP_K), F32),
                   jax.ShapeDtypeStruct((1, ne), jnp.int32)],
        scratch_shapes=[pltpu.VMEM((1, ne), F32)],
        compiler_params=pltpu.CompilerParams(
            dimension_semantics=("arbitrary",), vmem_limit_bytes=VMEM_LIMIT),
        name="out_router",
    )(y_r, y_d, x2, mod, norm_w, w_out_bf16, w_router, b_router)


def _w1_prep_kernel(w_ref, p_ref, o_ref):
    for s in range(w_ref.shape[2] // PERM_TILE):
        cols = slice(s * PERM_TILE, (s + 1) * PERM_TILE)
        o_ref[0, :, cols] = jnp.dot(w_ref[0, :, cols].astype(BF16), p_ref[...],
                                    preferred_element_type=F32).astype(BF16)


def _pair_split_matrix():
    i = jnp.arange(PERM_TILE)[:, None]
    j = jnp.arange(PERM_TILE)[None, :]
    half = PERM_TILE // 2
    src = jnp.where(j < half, 2 * j, 2 * (j - half) + 1)
    return (i == src).astype(BF16)


def _w1_prep(w1):
    e, d, f2 = w1.shape
    tn = 1024
    return pl.pallas_call(
        _w1_prep_kernel,
        grid=(e, f2 // tn),
        in_specs=[pl.BlockSpec((1, d, tn), lambda i, j: (i, 0, j)),
                  pl.BlockSpec((PERM_TILE, PERM_TILE), lambda i, j: (0, 0))],
        out_specs=pl.BlockSpec((1, d, tn), lambda i, j: (i, 0, j)),
        out_shape=jax.ShapeDtypeStruct((e, d, f2), BF16),
        compiler_params=pltpu.CompilerParams(vmem_limit_bytes=VMEM_LIMIT),
        name="w1_prep",
    )(w1, _pair_split_matrix())


def _pair_split_bias(b1):
    e, f2 = b1.shape
    nt = f2 // PERM_TILE
    g = b1[:, 0::2].reshape(e, nt, 1, PERM_TILE // 2)
    l = b1[:, 1::2].reshape(e, nt, 1, PERM_TILE // 2)
    return jnp.concatenate([g, l], axis=2).reshape(e, 1, f2)


def _sc_gather_rows(table, idx):
    n = idx.shape[0]
    row_shape = table.shape[1:]
    n_workers = SC_CORES * SC_SUBCORES
    per_w = n // n_workers
    n_chunks = per_w // SC_CHUNK
    assert per_w * n_workers == n and n_chunks * SC_CHUNK == per_w and n_chunks >= SC_AHEAD
    n_outer = -(-n_chunks // SC_BUFS)
    mesh = plsc.VectorSubcoreMesh(core_axis_name="c", subcore_axis_name="s",
                                  num_cores=SC_CORES, num_subcores=SC_SUBCORES)

    def body(table_hbm, idx_hbm, out_hbm, idx_v, *scratch):
        bufs = scratch[:SC_BUFS]
        gsem = scratch[SC_BUFS:2 * SC_BUFS]
        wsem = scratch[2 * SC_BUFS:]
        wid = lax.axis_index("s") * SC_CORES + lax.axis_index("c")
        base = wid * per_w
        pltpu.sync_copy(idx_hbm.at[pl.ds(base, per_w)], idx_v)

        def gather(c, s):
            rows = idx_v.at[pl.ds(c * SC_CHUNK, SC_CHUNK)]
            return pltpu.make_async_copy(table_hbm.at[rows], bufs[s], gsem[s])

        def write(c, s):
            return pltpu.make_async_copy(bufs[s], out_hbm.at[pl.ds(base + c * SC_CHUNK, SC_CHUNK)],
                                         wsem[s])

        for c in range(SC_AHEAD):
            gather(c, c).start()

        @pl.loop(0, n_outer * SC_BUFS, step=SC_BUFS)
        def _(g):
            for s in range(SC_BUFS):
                c = g + s

                @pl.when(c < n_chunks)
                def _():
                    gather(c, s).wait()
                    write(c, s).start()

                nxt = c + SC_AHEAD
                ns = (s + SC_AHEAD) % SC_BUFS

                @pl.when(nxt < n_chunks)
                def _():
                    @pl.when(nxt >= SC_BUFS)
                    def _():
                        write(nxt - SC_BUFS, ns).wait()

                    gather(nxt, ns).start()

        for j in range(max(0, n_chunks - SC_BUFS), n_chunks):
            write(j, j % SC_BUFS).wait()

    return pl.kernel(
        body,
        out_type=jax.ShapeDtypeStruct((n,) + row_shape, table.dtype),
        mesh=mesh,
        scratch_types=([pltpu.VMEM((per_w,), jnp.int32)]
                       + [pltpu.VMEM((SC_CHUNK,) + row_shape, table.dtype)] * SC_BUFS
                       + [pltpu.SemaphoreType.DMA] * (2 * SC_BUFS)),
        name="sc_gather_rows",
    )(table, idx)


def _expert_kernel(be_ref, nv_ref, x_ref, w1_ref, b1_ref, w2_ref, b2_ref, o_ref):
    b = pl.program_id(0)
    nv = nv_ref[0]
    bm = x_ref.shape[0] // ROW_SUB

    @pl.when(b < nv)
    def _():
        x = _load_row_tiles(x_ref, bm).astype(BF16)
        half = PERM_TILE // 2
        acts = []
        for j in range(w1_ref.shape[2] // PERM_TILE):
            cols = slice(j * PERM_TILE, (j + 1) * PERM_TILE)
            h = jnp.dot(x, w1_ref[0, :, cols], preferred_element_type=F32) + b1_ref[0, :, cols]
            glu = jnp.minimum(h[:, :half], SWIGLU_LIMIT)
            lin = jnp.clip(h[:, half:], -SWIGLU_LIMIT, SWIGLU_LIMIT)
            acts.append((glu * jax.nn.sigmoid(SWIGLU_ALPHA * glu) * (lin + 1.0)).astype(BF16))
        act = jnp.concatenate(acts, axis=1)
        y = jnp.dot(act, w2_ref[0], preferred_element_type=F32) + b2_ref[0]
        _store_row_tiles(o_ref, y)

    @pl.when(b >= nv)
    def _():
        o_ref[...] = jnp.zeros_like(o_ref)


def _experts(block_e, n_valid, xs, w1p, b1p, w2, b2):
    d = D_MODEL
    f2 = w1p.shape[2]
    f = w2.shape[1]
    bm = EXPERT_ROWS
    nb = xs.shape[0] // (bm * ROW_SUB)
    last = lambda b, nv: jnp.maximum(jnp.minimum(b, nv[0] - 1), 0)
    exp = lambda b, be, nv: be[last(b, nv)]
    grid_spec = pltpu.PrefetchScalarGridSpec(
        num_scalar_prefetch=2,
        grid=(nb,),
        in_specs=[pl.BlockSpec((bm * ROW_SUB, LANES), lambda b, be, nv: (last(b, nv), 0)),
                  pl.BlockSpec((1, d, f2), lambda b, be, nv: (exp(b, be, nv), 0, 0)),
                  pl.BlockSpec((1, 1, f2), lambda b, be, nv: (exp(b, be, nv), 0, 0)),
                  pl.BlockSpec((1, f, d), lambda b, be, nv: (exp(b, be, nv), 0, 0)),
                  pl.BlockSpec((1, 1, d), lambda b, be, nv: (exp(b, be, nv), 0, 0))],
        out_specs=pl.BlockSpec((bm * ROW_SUB, LANES), lambda b, be, nv: (b, 0)),
    )
    return pl.pallas_call(
        _expert_kernel,
        grid_spec=grid_spec,
        out_shape=jax.ShapeDtypeStruct((nb * bm * ROW_SUB, LANES), F32),
        compiler_params=pltpu.CompilerParams(
            dimension_semantics=("arbitrary",), vmem_limit_bytes=VMEM_LIMIT),
        name="experts",
    )(block_e, n_valid, xs, w1p, b1p, w2, b2)


def _combine_kernel(y_ref, gate_ref, x1_ref, mod_ref, nf_ref, o_ref):
    nt = x1_ref.shape[0]
    g = gate_ref[...]
    parts = []
    for s in range(ROW_SUB):
        acc = y_ref[0, 0, pl.ds(s, nt, stride=ROW_SUB), :] * g[:, 0:1]
        for kk in range(1, TOP_K):
            acc = acc + y_ref[0, kk, pl.ds(s, nt, stride=ROW_SUB), :] * g[:, kk:kk + 1]
        parts.append(acc)
    moe = jnp.concatenate(parts, axis=1)
    gate2 = mod_ref[0, 5:6, :]
    x2 = x1_ref[...] + gate2 * moe
    o_ref[...] = _rms(x2) * nf_ref[...]


def _combine(yg, gates, x1, mod, normf_w, seq):
    t, d = x1.shape
    steps, _, rows, _ = yg.shape
    nt = rows // ROW_SUB
    per_batch = seq // nt
    return pl.pallas_call(
        _combine_kernel,
        grid=(steps,),
        in_specs=[pl.BlockSpec((1, TOP_K, rows, LANES), lambda i: (i, 0, 0, 0)),
                  pl.BlockSpec((nt, 2 * TOP_K), lambda i: (i, 0)),
                  pl.BlockSpec((nt, d), lambda i: (i, 0)),
                  pl.BlockSpec((1, 6, d), lambda i: (i // per_batch, 0, 0)),
                  pl.BlockSpec((1, d), lambda i: (0, 0))],
        out_specs=pl.BlockSpec((nt, d), lambda i: (i, 0)),
        out_shape=jax.ShapeDtypeStruct((t, d), F32),
        compiler_params=pltpu.CompilerParams(vmem_limit_bytes=VMEM_LIMIT),
        name="combine",
    )(yg, gates, x1, mod, normf_w)


def kernel(x, c, w_ada, b_ada, norm1_w, w_in, lam_q1, lam_k1, lam_q2, lam_k2, subln_w, rel_bias,
           w_out, norm2_w, w_router, b_router, w1, b1, w2, b2, normf_w):
    batch, seq, d = x.shape
    t = batch * seq
    x2 = x.reshape(t, d)

    c_pad = jnp.zeros((8, d), F32).at[:batch].set(c)
    mod = _ada(c_pad, w_ada[0], b_ada[0][None, :])[:batch].reshape(batch, 6, d)

    q_r, k_r, v_r, g_r, q_d, k_d, v_d = _in_proj(x2, mod, norm1_w[0][None, :],
                                                 w_in[0].astype(BF16), seq)
    y_r = _retention(q_r, k_r, v_r, g_r, batch, seq)
    y_d = _diff_attention(q_d, k_d, v_d, rel_bias, lam_q1, lam_k1, lam_q2, lam_k2,
                          subln_w, batch, seq)

    x1, h2, meta, gates, counts = _out_router(y_r, y_d, x2, mod, norm2_w[0][None, :],
                                              w_out[0].astype(BF16), w_router[0],
                                              b_router[0][None, :], seq)

    bm = EXPERT_ROWS
    counts = counts[0]
    padded = (counts + bm - 1) // bm * bm
    pad_end = jnp.cumsum(padded)
    pad_start = pad_end - padded
    sel_e = meta[:, :TOP_K]
    hot_e = sel_e[:, :, None] == jnp.arange(N_EXPERTS, dtype=jnp.int32)[None, None, :]
    dest = jnp.sum(jnp.where(hot_e, pad_start[None, None, :], 0), axis=-1) + meta[:, TOP_K:]
    n_rows = (t * TOP_K // bm + N_EXPERTS) * bm
    nb = n_rows // bm
    block_start = jnp.arange(nb, dtype=jnp.int32) * bm
    block_e = jnp.minimum(jnp.sum((pad_end[None, :] <= block_start[:, None]).astype(jnp.int32), axis=1),
                          N_EXPERTS - 1)
    n_valid = (pad_end[-1:] // bm).astype(jnp.int32)
    slot_tok = jnp.zeros((n_rows,), jnp.int32).at[dest.reshape(-1)].set(
        jnp.arange(t * TOP_K, dtype=jnp.int32) // TOP_K, unique_indices=True,
        mode='promise_in_bounds')

    xs = _sc_gather_rows(h2.reshape(t, ROW_SUB, LANES), slot_tok).reshape(n_rows * ROW_SUB, LANES)
    ys = _experts(block_e, n_valid, xs, _w1_prep(w1[0]), _pair_split_bias(b1[0]),
                  w2[0].astype(BF16), b2[0][:, None, :])
    nt = min(COMBINE_TOKENS, seq)
    steps = t // nt
    slot_major = jnp.transpose(dest.reshape(steps, nt, TOP_K), (0, 2, 1)).reshape(-1)
    yg = _sc_gather_rows(ys.reshape(n_rows, ROW_SUB, LANES), slot_major)
    yg = yg.reshape(steps, TOP_K, nt * ROW_SUB, LANES)
    out = _combine(yg, gates, x1, mod, normf_w[None, :], seq)
    return out.reshape(batch, seq, d)
```

```python
import functools
import math

import jax
import jax.numpy as jnp
from jax import lax
from jax.experimental import pallas as pl
from jax.experimental.pallas import tpu as pltpu

F32 = jnp.float32
BF16 = jnp.bfloat16

D_MODEL = 1024
RET_HEADS = 4
RET_KEY_DIM = 64
RET_VAL_DIM = 128
RET_QK_W = RET_HEADS * RET_KEY_DIM
RET_V_W = RET_HEADS * RET_VAL_DIM
RET_CHUNK = 128
DIFF_HEADS = 4
DIFF_HEAD_DIM = 64
DIFF_QK_W = DIFF_HEADS * 2 * DIFF_HEAD_DIM
DIFF_V_W = DIFF_HEADS * 2 * DIFF_HEAD_DIM
IN_SIZES = (RET_QK_W, RET_QK_W, RET_V_W, RET_V_W, DIFF_QK_W, DIFF_QK_W, DIFF_V_W)
REL_BUCKETS = 32
REL_MAX_DIST = 128
N_EXPERTS = 32
TOP_K = 4
SWIGLU_ALPHA = 1.702
SWIGLU_LIMIT = 7.0
NORM_EPS = 1e-6
LAMBDA_INIT = 0.8 - 0.6 * math.exp(-0.3 * 0)

LANES = 128
ROW_SUB = D_MODEL // LANES
NEG_BIG = -1e30
LOG2_E = math.log2(math.e)
ONES_ROWS = 16
VMEM_LIMIT = 56 * 1024 * 1024

ROW_TILE = 512
RET_ROWS = 512
ATT_BLOCK = 512
EXPERT_ROWS = 256
PERM_TILE = 256
COMBINE_TOKENS = 128
DMA_QUEUES = 2
SLOT_MAP_CALLS = 2


def _rms(x):
    return x * lax.rsqrt(jnp.mean(x * x, axis=-1, keepdims=True) + NORM_EPS)


def _store_row_tiles(ref, x):
    rows = x.shape[0]
    for s in range(ROW_SUB):
        ref[pl.ds(s, rows, stride=ROW_SUB), :] = x[:, s * LANES:(s + 1) * LANES]


def _load_row_tiles(ref, rows):
    return jnp.concatenate([ref[pl.ds(s, rows, stride=ROW_SUB), :] for s in range(ROW_SUB)], axis=1)


def _ada_kernel(c_ref, w_ref, b_ref, o_ref):
    c = c_ref[...]
    cond = c * jax.nn.sigmoid(c)
    o_ref[...] = jnp.dot(cond, w_ref[...], precision=lax.Precision.HIGHEST,
                         preferred_element_type=F32) + b_ref[...]


def _ada(c_pad, w_ada, b_ada):
    rows, d = c_pad.shape
    n = w_ada.shape[1]
    tn = 1024
    return pl.pallas_call(
        _ada_kernel,
        grid=(n // tn,),
        in_specs=[pl.BlockSpec((rows, d), lambda j: (0, 0)),
                  pl.BlockSpec((d, tn), lambda j: (0, j)),
                  pl.BlockSpec((1, tn), lambda j: (0, j))],
        out_specs=pl.BlockSpec((rows, tn), lambda j: (0, j)),
        out_shape=jax.ShapeDtypeStruct((rows, n), F32),
        name="ada",
    )(c_pad, w_ada, b_ada)


def _in_proj_kernel(x_ref, mod_ref, nw_ref, w_ref, *o_refs):
    x = x_ref[...]
    shift = mod_ref[0, 0:1, :]
    scale = mod_ref[0, 1:2, :]
    h = (_rms(x) * nw_ref[...]) * (1.0 + scale) + shift
    hb = h.astype(BF16)
    off = 0
    for o_ref, width in zip(o_refs, IN_SIZES):
        o_ref[...] = jnp.dot(hb, w_ref[:, off:off + width],
                             preferred_element_type=F32).astype(o_ref.dtype)
        off += width


def _in_proj(x2, mod, norm_w, w_in_bf16, seq):
    t, d = x2.shape
    tm = min(ROW_TILE, seq)
    per_batch = seq // tm
    in_w = w_in_bf16.shape[1]
    return pl.pallas_call(
        _in_proj_kernel,
        grid=(t // tm,),
        in_specs=[pl.BlockSpec((tm, d), lambda i: (i, 0)),
                  pl.BlockSpec((1, 6, d), lambda i: (i // per_batch, 0, 0)),
                  pl.BlockSpec((1, d), lambda i: (0, 0)),
                  pl.BlockSpec((d, in_w), lambda i: (0, 0))],
        out_specs=[pl.BlockSpec((tm, w), lambda i: (i, 0)) for w in IN_SIZES],
        out_shape=[jax.ShapeDtypeStruct((t, w), BF16) for w in IN_SIZES],
        compiler_params=pltpu.CompilerParams(vmem_limit_bytes=VMEM_LIMIT),
        name="in_proj",
    )(x2, mod, norm_w, w_in_bf16)


def _rotary(x, cos, sin_even, sin_odd):
    nxt = pltpu.roll(x, LANES - 1, 1)
    prv = pltpu.roll(x, 1, 1)
    return x * cos + nxt * sin_even + prv * sin_odd


def _ret_kernel(q_ref, k_ref, v_ref, g_ref, cos_ref, sine_ref, sino_ref,
                hmask_ref, xi_ref, zeta_ref, dmask_ref, gch_ref, o_ref, state_ref):
    @pl.when(pl.program_id(1) == 0)
    def _():
        state_ref[...] = jnp.zeros_like(state_ref)

    n_sub = q_ref.shape[0] // RET_CHUNK
    for c in range(n_sub):
        rows = slice(c * RET_CHUNK, (c + 1) * RET_CHUNK)
        for pair in range(RET_HEADS // 2):
            lanes = slice(pair * LANES, (pair + 1) * LANES)
            cos = cos_ref[rows, lanes]
            sine = sine_ref[rows, lanes]
            sino = sino_ref[rows, lanes]
            qr = _rotary(q_ref[rows, lanes].astype(F32), cos, sine, sino)
            kr = _rotary(k_ref[rows, lanes].astype(F32), cos, sine, sino) * (RET_KEY_DIM ** -0.5)
            qb = qr.astype(BF16)
            for hh in range(2):
                h = 2 * pair + hh
                vcols = slice(h * RET_VAL_DIM, (h + 1) * RET_VAL_DIM)
                v = v_ref[rows, vcols]
                km = (kr * hmask_ref[h]).astype(BF16)
                scores = lax.dot_general(qb, km, (((1,), (1,)), ((), ())),
                                         preferred_element_type=F32) * dmask_ref[h]
                inner = jnp.dot(scores.astype(BF16), v, preferred_element_type=F32)
                qx = (qr * xi_ref[h]).astype(BF16)
                state = state_ref[h]
                cross = jnp.dot(qx, state.astype(BF16), preferred_element_type=F32)
                kz = (kr * zeta_ref[h]).astype(BF16)
                kv = lax.dot_general(kz, v, (((0,), (0,)), ((), ())),
                                     preferred_element_type=F32)
                state_ref[h] = state * gch_ref[h] + kv
                y = _rms(inner + cross)
                g = g_ref[rows, vcols].astype(F32)
                o_ref[rows, vcols] = (g * jax.nn.sigmoid(g) * y).astype(o_ref.dtype)


def _retention_tables(seq):
    dk, c, nh = RET_KEY_DIM, RET_CHUNK, RET_HEADS
    pos = jnp.arange(seq, dtype=F32)
    inv_freq = 1.0 / (10000.0 ** jnp.linspace(0.0, 1.0, dk // 2, dtype=F32))
    ang = pos[:, None] * jnp.repeat(inv_freq, 2)[None, :]
    sin = jnp.tile(jnp.sin(ang), (1, nh))
    cos = jnp.tile(jnp.cos(ang), (1, nh))
    even = (jnp.arange(nh * dk) % 2 == 0)[None, :]
    sin_even = jnp.where(even, -sin, 0.0)
    sin_odd = jnp.where(even, 0.0, sin)
    log_g = jnp.log(1.0 - 2.0 ** (-5.0 - jnp.arange(nh, dtype=F32)))
    i = jnp.arange(c, dtype=F32)
    rel = i[:, None] - i[None, :]
    dmask = jnp.where(rel[None] >= 0,
                      jnp.exp(jnp.maximum(rel, 0.0)[None] * log_g[:, None, None]), 0.0)
    zeta = jnp.exp((c - 1.0 - i)[None, :] * log_g[:, None])
    xi = jnp.exp((i + 1.0)[None, :] * log_g[:, None])
    g_chunk = jnp.exp(c * log_g)
    lane = jnp.arange(LANES)
    hmask = jnp.stack([((lane // dk) == (h % 2)).astype(F32) for h in range(nh)])[:, None, :]
    xi_t = xi[:, :, None] * hmask
    zeta_t = zeta[:, :, None] * hmask
    gch = jnp.broadcast_to(g_chunk[:, None, None], (nh, 1, LANES))
    return cos, sin_even, sin_odd, hmask, xi_t, zeta_t, dmask, gch


def _retention(q, k, v, g, batch, seq):
    t = q.shape[0]
    rb = min(RET_ROWS, seq)
    per_batch = seq // rb
    cos, sin_even, sin_odd, hmask, xi_t, zeta_t, dmask, gch = _retention_tables(seq)
    row = lambda w: pl.BlockSpec((rb, w), lambda b, j: (b * per_batch + j, 0))
    tab = lambda w: pl.BlockSpec((rb, w), lambda b, j: (j, 0))
    full = lambda a: pl.BlockSpec(a.shape, lambda b, j: (0,) * a.ndim)
    return pl.pallas_call(
        _ret_kernel,
        grid=(batch, per_batch),
        in_specs=[row(RET_QK_W), row(RET_QK_W), row(RET_V_W), row(RET_V_W),
                  tab(RET_QK_W), tab(RET_QK_W), tab(RET_QK_W),
                  full(hmask), full(xi_t), full(zeta_t), full(dmask), full(gch)],
        out_specs=row(RET_V_W),
        out_shape=jax.ShapeDtypeStruct((t, RET_V_W), BF16),
        scratch_shapes=[pltpu.VMEM((RET_HEADS, LANES, RET_VAL_DIM), F32)],
        compiler_params=pltpu.CompilerParams(
            dimension_semantics=("arbitrary", "arbitrary"), vmem_limit_bytes=VMEM_LIMIT),
        name="retention",
    )(q, k, v, g, cos, sin_even, sin_odd, hmask, xi_t, zeta_t, dmask, gch)


def _t5_bucket(rel):
    n = jnp.maximum(rel, 0)
    max_exact = REL_BUCKETS // 2
    nf = jnp.maximum(n, 1).astype(F32)
    large = max_exact + (jnp.log(nf / max_exact) / math.log(REL_MAX_DIST / max_exact)
                         * (REL_BUCKETS - max_exact)).astype(jnp.int32)
    large = jnp.minimum(large, REL_BUCKETS - 1)
    return jnp.where(n < max_exact, n, large)


def _bias_tiles(rel_bias, blk):
    r = jnp.arange(blk, dtype=jnp.int32)
    far = rel_bias[REL_BUCKETS - 1]
    rel0 = r[None, :] - r[:, None]
    rel1 = rel0 + blk
    buckets = jnp.arange(REL_BUCKETS, dtype=jnp.int32)

    def tile(rel):
        hot = (_t5_bucket(rel)[:, :, None] == buckets).astype(F32)
        return jnp.einsum('krb,bh->hkr', hot, rel_bias, precision=lax.Precision.HIGHEST)

    b0 = jnp.where(rel0[None] >= 0, (tile(rel0) - far[:, None, None]) * LOG2_E, NEG_BIG)
    b1 = (tile(rel1) - far[:, None, None]) * LOG2_E
    return b0, b1


def _attn_kernel(q_ref, k_ref, vt_ref, b0_ref, b1_ref, lq1_ref, lk1_ref, lq2_ref, lk2_ref,
                 sw_ref, o_ref, m_ref, acc_ref):
    blk = q_ref.shape[0]
    i = pl.program_id(2)
    lane = lax.broadcasted_iota(jnp.int32, (1, LANES), 1)
    q = (q_ref[...].astype(F32) * (DIFF_HEAD_DIM ** -0.5 * LOG2_E)).astype(BF16)
    zero = jnp.zeros_like(q)
    qm = (jnp.where(lane < DIFF_HEAD_DIM, q, zero), jnp.where(lane >= DIFF_HEAD_DIM, q, zero))

    m_ref[...] = jnp.full_like(m_ref, NEG_BIG)
    acc_ref[...] = jnp.zeros_like(acc_ref)

    def step(j, bias):
        start = pl.multiple_of(j * blk, blk)
        kb = k_ref[pl.ds(start, blk), :]
        vt = vt_ref[0, 0, j]
        for mi in range(2):
            s = lax.dot_general(kb, qm[mi], (((1,), (1,)), ((), ())),
                                preferred_element_type=F32)
            if bias is not None:
                s = s + bias
            m_old = m_ref[mi]
            m_new = jnp.maximum(m_old, jnp.max(s, axis=0, keepdims=True))
            alpha = jnp.exp2(m_old - m_new)
            p = jnp.exp2(s - m_new).astype(BF16)
            acc_ref[mi] = alpha * acc_ref[mi] + jnp.dot(vt, p, preferred_element_type=F32)
            m_ref[mi] = m_new

    def far_body(j, carry):
        step(j, None)
        return carry

    lax.fori_loop(0, jnp.maximum(i - 1, 0), far_body, 0)

    @pl.when(i >= 1)
    def _():
        step(i - 1, b1_ref[0])

    step(i, b0_ref[0])

    lam = (jnp.exp(jnp.sum(lq1_ref[...] * lk1_ref[...], axis=-1, keepdims=True))
           - jnp.exp(jnp.sum(lq2_ref[...] * lk2_ref[...], axis=-1, keepdims=True))
           + LAMBDA_INIT)
    hw = 2 * DIFF_HEAD_DIM
    a = (acc_ref[0, :hw, :] / acc_ref[0, hw:hw + 1, :]
         - lam * (acc_ref[1, :hw, :] / acc_ref[1, hw:hw + 1, :]))
    a = a * lax.rsqrt(jnp.mean(a * a, axis=0, keepdims=True) + NORM_EPS)
    o_ref[...] = (a.T * sw_ref[...] * (1.0 - LAMBDA_INIT)).astype(o_ref.dtype)


def _diff_attention(q, k, v, rel_bias, lq1, lk1, lq2, lk2, subln_w, batch, seq):
    t = q.shape[0]
    blk = min(ATT_BLOCK, seq)
    nq = seq // blk
    hw = 2 * DIFF_HEAD_DIM
    b0, b1 = _bias_tiles(rel_bias, blk)
    vt = jnp.transpose(v.reshape(batch, nq, blk, DIFF_HEADS, hw), (0, 3, 1, 4, 2))
    ones = jnp.zeros((batch, DIFF_HEADS, nq, ONES_ROWS, blk), vt.dtype).at[:, :, :, 0, :].set(1.0)
    vt = jnp.concatenate([vt, ones], axis=3)
    small = lambda a: pl.BlockSpec(a.shape, lambda b, h, i: (0,) * a.ndim)
    return pl.pallas_call(
        _attn_kernel,
        grid=(batch, DIFF_HEADS, nq),
        in_specs=[pl.BlockSpec((blk, hw), lambda b, h, i: (b * nq + i, h)),
                  pl.BlockSpec((seq, hw), lambda b, h, i: (b, h)),
                  pl.BlockSpec((1, 1, nq, hw + ONES_ROWS, blk), lambda b, h, i: (b, h, 0, 0, 0)),
                  pl.BlockSpec((1, blk, blk), lambda b, h, i: (h, 0, 0)),
                  pl.BlockSpec((1, blk, blk), lambda b, h, i: (h, 0, 0)),
                  small(lq1), small(lk1), small(lq2), small(lk2), small(subln_w)],
        out_specs=pl.BlockSpec((blk, hw), lambda b, h, i: (b * nq + i, h)),
        out_shape=jax.ShapeDtypeStruct((t, DIFF_V_W), BF16),
        scratch_shapes=[pltpu.VMEM((2, 1, blk), F32), pltpu.VMEM((2, hw + ONES_ROWS, blk), F32)],
        compiler_params=pltpu.CompilerParams(
            dimension_semantics=("arbitrary", "arbitrary", "arbitrary"),
            vmem_limit_bytes=VMEM_LIMIT),
        name="diff_attn",
    )(q, k, vt, b0, b1, lq1, lk1, lq2, lk2, subln_w)


def _out_kernel(yr_ref, yd_ref, x_ref, mod_ref, nw_ref, wo_ref, wr_ref, br_ref,
                x1_ref, h2_ref, meta_ref, gate_ref, cnt_ref, run_ref):
    tm = x_ref.shape[0]

    @pl.when(pl.program_id(0) == 0)
    def _():
        run_ref[...] = jnp.zeros_like(run_ref)

    mixed = (jnp.dot(yr_ref[...], wo_ref[0:RET_V_W, :], preferred_element_type=F32)
             + jnp.dot(yd_ref[...], wo_ref[RET_V_W:, :], preferred_element_type=F32))
    gate1 = mod_ref[0, 2:3, :]
    shift2 = mod_ref[0, 3:4, :]
    scale2 = mod_ref[0, 4:5, :]
    x1 = x_ref[...] + gate1 * mixed
    x1_ref[...] = x1
    h2 = (_rms(x1) * nw_ref[...]) * (1.0 + scale2) + shift2
    _store_row_tiles(h2_ref, h2)

    logits = jnp.dot(h2, wr_ref[...], precision=lax.Precision.HIGHEST,
                     preferred_element_type=F32) + br_ref[...]
    lane = lax.broadcasted_iota(jnp.int32, logits.shape, 1)
    work = logits
    vals, idxs, hots = [], [], []
    for _ in range(TOP_K):
        mx = jnp.max(work, axis=-1, keepdims=True)
        idx = jnp.min(jnp.where(work == mx, lane, N_EXPERTS), axis=-1, keepdims=True)
        hot = lane == idx
        vals.append(mx)
        idxs.append(idx)
        hots.append(hot)
        work = jnp.where(hot, -jnp.inf, work)
    exps = [jnp.exp(v - vals[0]) for v in vals]
    denom = exps[0] + exps[1] + exps[2] + exps[3]

    sel = jnp.zeros(logits.shape, F32)
    for hot in hots:
        sel = sel + hot.astype(F32)
    r = lax.broadcasted_iota(jnp.int32, (tm, tm), 0)
    c = lax.broadcasted_iota(jnp.int32, (tm, tm), 1)
    before = (c < r).astype(BF16)
    prefix = jnp.dot(before, sel.astype(BF16), preferred_element_type=F32) + run_ref[...]
    ranks = [jnp.sum(jnp.where(hot, prefix, 0.0), axis=-1, keepdims=True) for hot in hots]
    run_ref[...] = run_ref[...] + jnp.sum(sel, axis=0, keepdims=True)
    cnt_ref[...] = run_ref[...].astype(jnp.int32)

    lane8 = lax.broadcasted_iota(jnp.int32, (tm, 2 * TOP_K), 1)
    meta = jnp.zeros((tm, 2 * TOP_K), jnp.int32)
    gates = jnp.zeros((tm, 2 * TOP_K), F32)
    for kk in range(TOP_K):
        meta = jnp.where(lane8 == kk, idxs[kk], meta)
        meta = jnp.where(lane8 == TOP_K + kk, ranks[kk].astype(jnp.int32), meta)
        gates = jnp.where(lane8 == kk, exps[kk] / denom, gates)
    meta_ref[...] = meta
    gate_ref[...] = gates


def _out_router(y_r, y_d, x2, mod, norm_w, w_out_bf16, w_router, b_router, seq):
    t, d = x2.shape
    tm = min(ROW_TILE, seq)
    per_batch = seq // tm
    ne = w_router.shape[1]
    row = lambda w: pl.BlockSpec((tm, w), lambda i: (i, 0))
    const = lambda a: pl.BlockSpec(a.shape, lambda i: (0,) * a.ndim)
    return pl.pallas_call(
        _out_kernel,
        grid=(t // tm,),
        in_specs=[row(RET_V_W), row(DIFF_V_W), row(d),
                  pl.BlockSpec((1, 6, d), lambda i: (i // per_batch, 0, 0)),
                  const(norm_w), const(w_out_bf16), const(w_router), const(b_router)],
        out_specs=[row(d), pl.BlockSpec((tm * ROW_SUB, LANES), lambda i: (i, 0)),
                   row(2 * TOP_K), row(2 * TOP_K),
                   pl.BlockSpec((1, ne), lambda i: (0, 0))],
        out_shape=[jax.ShapeDtypeStruct((t, d), F32),
                   jax.ShapeDtypeStruct((t * ROW_SUB, LANES), F32),
                   jax.ShapeDtypeStruct((t, 2 * TOP_K), jnp.int32),
                   jax.ShapeDtypeStruct((t, 2 * TOP_K), F32),
                   jax.ShapeDtypeStruct((1, ne), jnp.int32)],
        scratch_shapes=[pltpu.VMEM((1, ne), F32)],
        compiler_params=pltpu.CompilerParams(
            dimension_semantics=("arbitrary",), vmem_limit_bytes=VMEM_LIMIT),
        name="out_router",
    )(y_r, y_d, x2, mod, norm_w, w_out_bf16, w_router, b_router)


def _w1_prep_kernel(w_ref, p_ref, o_ref):
    for s in range(w_ref.shape[2] // PERM_TILE):
        cols = slice(s * PERM_TILE, (s + 1) * PERM_TILE)
        o_ref[0, :, cols] = jnp.dot(w_ref[0, :, cols].astype(BF16), p_ref[...],
                                    preferred_element_type=F32).astype(BF16)


def _pair_split_matrix():
    i = jnp.arange(PERM_TILE)[:, None]
    j = jnp.arange(PERM_TILE)[None, :]
    half = PERM_TILE // 2
    src = jnp.where(j < half, 2 * j, 2 * (j - half) + 1)
    return (i == src).astype(BF16)


def _w1_prep(w1):
    e, d, f2 = w1.shape
    tn = 1024
    return pl.pallas_call(
        _w1_prep_kernel,
        grid=(e, f2 // tn),
        in_specs=[pl.BlockSpec((1, d, tn), lambda i, j: (i, 0, j)),
                  pl.BlockSpec((PERM_TILE, PERM_TILE), lambda i, j: (0, 0))],
        out_specs=pl.BlockSpec((1, d, tn), lambda i, j: (i, 0, j)),
        out_shape=jax.ShapeDtypeStruct((e, d, f2), BF16),
        compiler_params=pltpu.CompilerParams(vmem_limit_bytes=VMEM_LIMIT),
        name="w1_prep",
    )(w1, _pair_split_matrix())


def _pair_split_bias(b1):
    e, f2 = b1.shape
    nt = f2 // PERM_TILE
    g = b1[:, 0::2].reshape(e, nt, 1, PERM_TILE // 2)
    l = b1[:, 1::2].reshape(e, nt, 1, PERM_TILE // 2)
    return jnp.concatenate([g, l], axis=2).reshape(e, 1, f2)


def _slot_tok_kernel(first_slot, dest_ref, init_ref, o_ref, sem):
    init_copy = pltpu.make_async_copy(init_ref, o_ref, sem)
    init_copy.start()
    init_copy.wait()
    n_slots = dest_ref.shape[0]

    def put(i, carry):
        o_ref[dest_ref[i]] = lax.shift_right_logical(first_slot + i, 2)
        return carry

    lax.fori_loop(0, n_slots, put, 0, unroll=8)


def _slot_tokens(dest, n_rows):
    assert TOP_K == 4
    flat = dest.reshape(-1)
    per_call = flat.shape[0] // SLOT_MAP_CALLS
    slot_tok = jnp.zeros((n_rows,), jnp.int32)
    for part in range(SLOT_MAP_CALLS):
        slot_tok = pl.pallas_call(
            functools.partial(_slot_tok_kernel, part * per_call),
            in_specs=[pl.BlockSpec(memory_space=pltpu.SMEM), pl.BlockSpec(memory_space=pl.ANY)],
            out_specs=pl.BlockSpec(memory_space=pltpu.SMEM),
            out_shape=jax.ShapeDtypeStruct((n_rows,), jnp.int32),
            scratch_shapes=[pltpu.SemaphoreType.DMA(())],
            name="slot_tokens",
        )(flat[part * per_call:(part + 1) * per_call], slot_tok)
    return slot_tok


def _expert_kernel(be_ref, nv_ref, tcur_ref, tnxt_ref, h_ref, w1_ref, b1_ref, w2_ref, b2_ref,
                   o_ref, xbuf, sems):
    b = pl.program_id(0)
    nb = pl.num_programs(0)
    nv = nv_ref[0]
    bm = xbuf.shape[1] // ROW_SUB
    slot = lax.rem(b, 2)

    def row_copy(tok, s, r):
        return pltpu.make_async_copy(h_ref.at[pl.ds(tok * ROW_SUB, ROW_SUB), :],
                                     xbuf.at[s, pl.ds(r * ROW_SUB, ROW_SUB), :], sems.at[s])

    def fetch(tok_ref, s, r0, r1):
        for r in range(r0, r1):
            row_copy(tok_ref[0, 0, r], s, r).start(priority=r % DMA_QUEUES)

    def wait_block(s):
        pltpu.make_async_copy(h_ref.at[pl.ds(0, bm * ROW_SUB), :], xbuf.at[s], sems.at[s]).wait()

    @pl.when(b == 0)
    def _():
        fetch(tcur_ref, 0, 0, bm)

    n_tiles = w1_ref.shape[2] // PERM_TILE

    @pl.when(b < nv)
    def _():
        wait_block(slot)
        fetch(tnxt_ref, 1 - slot, 0, bm)
        x = _load_row_tiles(xbuf.at[slot], bm).astype(BF16)
        half = PERM_TILE // 2
        acts = []
        for j in range(n_tiles):
            cols = slice(j * PERM_TILE, (j + 1) * PERM_TILE)
            h = jnp.dot(x, w1_ref[0, :, cols], preferred_element_type=F32) + b1_ref[0, :, cols]
            glu = jnp.minimum(h[:, :half], SWIGLU_LIMIT)
            lin = jnp.clip(h[:, half:], -SWIGLU_LIMIT, SWIGLU_LIMIT)
            acts.append((glu * jax.nn.sigmoid(SWIGLU_ALPHA * glu) * (lin + 1.0)).astype(BF16))
        act = jnp.concatenate(acts, axis=1)
        y = jnp.dot(act, w2_ref[0], preferred_element_type=F32) + b2_ref[0]
        _store_row_tiles(o_ref, y)

    @pl.when(b >= nv)
    def _():
        wait_block(slot)
        fetch(tnxt_ref, 1 - slot, 0, bm)
        o_ref[...] = jnp.zeros_like(o_ref)

    @pl.when(b == nb - 1)
    def _():
        wait_block(1 - slot)


def _experts(block_e, n_valid, slot_tok, h2, w1p, b1p, w2, b2):
    d = D_MODEL
    f2 = w1p.shape[2]
    f = w2.shape[1]
    bm = EXPERT_ROWS
    nb = slot_tok.shape[0] // bm
    tok3 = slot_tok.reshape(nb, 1, bm)
    exp = lambda b, be, nv: be[jnp.maximum(jnp.minimum(b, nv[0] - 1), 0)]
    grid_spec = pltpu.PrefetchScalarGridSpec(
        num_scalar_prefetch=2,
        grid=(nb,),
        in_specs=[pl.BlockSpec((1, 1, bm), lambda b, be, nv: (b, 0, 0), memory_space=pltpu.SMEM),
                  pl.BlockSpec((1, 1, bm), lambda b, be, nv: (jnp.minimum(b + 1, nb - 1), 0, 0),
                               memory_space=pltpu.SMEM),
                  pl.BlockSpec(memory_space=pl.ANY),
                  pl.BlockSpec((1, d, f2), lambda b, be, nv: (exp(b, be, nv), 0, 0)),
                  pl.BlockSpec((1, 1, f2), lambda b, be, nv: (exp(b, be, nv), 0, 0)),
                  pl.BlockSpec((1, f, d), lambda b, be, nv: (exp(b, be, nv), 0, 0)),
                  pl.BlockSpec((1, 1, d), lambda b, be, nv: (exp(b, be, nv), 0, 0))],
        out_specs=pl.BlockSpec((bm * ROW_SUB, LANES), lambda b, be, nv: (b, 0)),
        scratch_shapes=[pltpu.VMEM((2, bm * ROW_SUB, LANES), F32), pltpu.SemaphoreType.DMA((2,))],
    )
    return pl.pallas_call(
        _expert_kernel,
        grid_spec=grid_spec,
        out_shape=jax.ShapeDtypeStruct((nb * bm * ROW_SUB, LANES), F32),
        compiler_params=pltpu.CompilerParams(
            dimension_semantics=("arbitrary",), vmem_limit_bytes=VMEM_LIMIT),
        name="experts",
    )(block_e, n_valid, tok3, tok3, h2, w1p, b1p, w2, b2)


def _combine_kernel(dcur_ref, dnxt_ref, gate_ref, x1_ref, mod_ref, nf_ref, y_ref, o_ref, buf, sems):
    nt = x1_ref.shape[0]
    i = pl.program_id(0)
    n = pl.num_programs(0)
    slot = lax.rem(i, 2)

    def row_copy(src_row, s, kk, r):
        return pltpu.make_async_copy(y_ref.at[pl.ds(src_row * ROW_SUB, ROW_SUB), :],
                                     buf.at[s, kk, pl.ds(r * ROW_SUB, ROW_SUB), :], sems.at[s])

    def fetch(d_ref, s):
        for j in range(nt):
            for kk in range(TOP_K):
                row_copy(d_ref[0, 0, j * TOP_K + kk], s, kk, j).start(priority=kk % DMA_QUEUES)

    def wait_all(s):
        for kk in range(TOP_K):
            pltpu.make_async_copy(y_ref.at[pl.ds(0, nt * ROW_SUB), :], buf.at[s, kk], sems.at[s]).wait()

    @pl.when(i == 0)
    def _():
        fetch(dcur_ref, 0)

    fetch(dnxt_ref, 1 - slot)
    wait_all(slot)

    g = gate_ref[...]
    parts = []
    for s in range(ROW_SUB):
        acc = buf[slot, 0, pl.ds(s, nt, stride=ROW_SUB), :] * g[:, 0:1]
        for kk in range(1, TOP_K):
            acc = acc + buf[slot, kk, pl.ds(s, nt, stride=ROW_SUB), :] * g[:, kk:kk + 1]
        parts.append(acc)
    moe = jnp.concatenate(parts, axis=1)
    gate2 = mod_ref[0, 5:6, :]
    x2 = x1_ref[...] + gate2 * moe
    o_ref[...] = _rms(x2) * nf_ref[...]

    @pl.when(i == n - 1)
    def _():
        wait_all(1 - slot)


def _combine(dest, gates, x1, mod, normf_w, y, seq):
    t, d = x1.shape
    nt = min(COMBINE_TOKENS, seq)
    steps = t // nt
    per_batch = seq // nt
    dest2 = dest.reshape(steps, 1, nt * TOP_K)
    return pl.pallas_call(
        _combine_kernel,
        grid=(steps,),
        in_specs=[pl.BlockSpec((1, 1, nt * TOP_K), lambda i: (i, 0, 0), memory_space=pltpu.SMEM),
                  pl.BlockSpec((1, 1, nt * TOP_K), lambda i: (jnp.minimum(i + 1, steps - 1), 0, 0),
                               memory_space=pltpu.SMEM),
                  pl.BlockSpec((nt, 2 * TOP_K), lambda i: (i, 0)),
                  pl.BlockSpec((nt, d), lambda i: (i, 0)),
                  pl.BlockSpec((1, 6, d), lambda i: (i // per_batch, 0, 0)),
                  pl.BlockSpec((1, d), lambda i: (0, 0)),
                  pl.BlockSpec(memory_space=pl.ANY)],
        out_specs=pl.BlockSpec((nt, d), lambda i: (i, 0)),
        out_shape=jax.ShapeDtypeStruct((t, d), F32),
        scratch_shapes=[pltpu.VMEM((2, TOP_K, nt * ROW_SUB, LANES), F32),
                        pltpu.SemaphoreType.DMA((2,))],
        compiler_params=pltpu.CompilerParams(
            dimension_semantics=("arbitrary",), vmem_limit_bytes=VMEM_LIMIT),
        name="combine",
    )(dest2, dest2, gates, x1, mod, normf_w, y)


def kernel(x, c, w_ada, b_ada, norm1_w, w_in, lam_q1, lam_k1, lam_q2, lam_k2, subln_w, rel_bias,
           w_out, norm2_w, w_router, b_router, w1, b1, w2, b2, normf_w):
    batch, seq, d = x.shape
    t = batch * seq
    x2 = x.reshape(t, d)

    c_pad = jnp.zeros((8, d), F32).at[:batch].set(c)
    mod = _ada(c_pad, w_ada[0], b_ada[0][None, :])[:batch].reshape(batch, 6, d)

    q_r, k_r, v_r, g_r, q_d, k_d, v_d = _in_proj(x2, mod, norm1_w[0][None, :],
                                                 w_in[0].astype(BF16), seq)
    y_r = _retention(q_r, k_r, v_r, g_r, batch, seq)
    y_d = _diff_attention(q_d, k_d, v_d, rel_bias, lam_q1, lam_k1, lam_q2, lam_k2,
                          subln_w, batch, seq)

    x1, h2, meta, gates, counts = _out_router(y_r, y_d, x2, mod, norm2_w[0][None, :],
                                              w_out[0].astype(BF16), w_router[0],
                                              b_router[0][None, :], seq)

    bm = EXPERT_ROWS
    counts = counts[0]
    padded = (counts + bm - 1) // bm * bm
    pad_end = jnp.cumsum(padded)
    pad_start = pad_end - padded
    sel_e = meta[:, :TOP_K]
    hot_e = sel_e[:, :, None] == jnp.arange(N_EXPERTS, dtype=jnp.int32)[None, None, :]
    dest = jnp.sum(jnp.where(hot_e, pad_start[None, None, :], 0), axis=-1) + meta[:, TOP_K:]
    n_rows = (t * TOP_K // bm + N_EXPERTS) * bm
    nb = n_rows // bm
    block_start = jnp.arange(nb, dtype=jnp.int32) * bm
    block_e = jnp.minimum(jnp.sum((pad_end[None, :] <= block_start[:, None]).astype(jnp.int32), axis=1),
                          N_EXPERTS - 1)
    n_valid = (pad_end[-1:] // bm).astype(jnp.int32)
    slot_tok = _slot_tokens(dest, n_rows)

    ys = _experts(block_e, n_valid, slot_tok, h2, _w1_prep(w1[0]), _pair_split_bias(b1[0]),
                  w2[0].astype(BF16), b2[0][:, None, :])
    out = _combine(dest, gates, x1, mod, normf_w[None, :], ys, seq)
    return out.reshape(batch, seq, d)
```

```python
import functools
import math

import jax
import jax.numpy as jnp
from jax import lax
from jax.experimental import pallas as pl
from jax.experimental.pallas import tpu as pltpu

F32 = jnp.float32
BF16 = jnp.bfloat16

D_MODEL = 1024
RET_HEADS = 4
RET_KEY_DIM = 64
RET_VAL_DIM = 128
RET_QK_W = RET_HEADS * RET_KEY_DIM
RET_V_W = RET_HEADS * RET_VAL_DIM
RET_CHUNK = 128
DIFF_HEADS = 4
DIFF_HEAD_DIM = 64
DIFF_QK_W = DIFF_HEADS * 2 * DIFF_HEAD_DIM
DIFF_V_W = DIFF_HEADS * 2 * DIFF_HEAD_DIM
IN_SIZES = (RET_QK_W, RET_QK_W, RET_V_W, RET_V_W, DIFF_QK_W, DIFF_QK_W, DIFF_V_W)
REL_BUCKETS = 32
REL_MAX_DIST = 128
N_EXPERTS = 32
TOP_K = 4
SWIGLU_ALPHA = 1.702
SWIGLU_LIMIT = 7.0
NORM_EPS = 1e-6
LAMBDA_INIT = 0.8 - 0.6 * math.exp(-0.3 * 0)

LANES = 128
ROW_SUB = D_MODEL // LANES
NEG_BIG = -1e30
LOG2_E = math.log2(math.e)
ONES_ROWS = 16
VMEM_LIMIT = 56 * 1024 * 1024

ROW_TILE = 512
RET_ROWS = 512
ATT_BLOCK = 512
EXPERT_ROWS = 256
PERM_TILE = 256
COMBINE_TOKENS = 128
DMA_QUEUES = 2
SLOT_MAP_CALLS = 2


def _rms(x):
    return x * lax.rsqrt(jnp.mean(x * x, axis=-1, keepdims=True) + NORM_EPS)


def _store_row_tiles(ref, x):
    rows = x.shape[0]
    for s in range(ROW_SUB):
        ref[pl.ds(s, rows, stride=ROW_SUB), :] = x[:, s * LANES:(s + 1) * LANES]


def _load_row_tiles(ref, rows):
    return jnp.concatenate([ref[pl.ds(s, rows, stride=ROW_SUB), :] for s in range(ROW_SUB)], axis=1)


def _pack_halves(x):
    half = x.shape[1] // 2
    lo = lax.bitcast_convert_type(x[:, :half].astype(BF16).astype(F32), jnp.uint32)
    hi = lax.bitcast_convert_type(x[:, half:].astype(BF16).astype(F32), jnp.uint32)
    return (hi & jnp.uint32(0xFFFF0000)) | (lo >> 16)


def _unpack_halves(w):
    lo = lax.bitcast_convert_type(w << 16, F32).astype(BF16)
    hi = lax.bitcast_convert_type(w & jnp.uint32(0xFFFF0000), F32).astype(BF16)
    return jnp.concatenate([lo, hi], axis=1)


def _ada_kernel(c_ref, w_ref, b_ref, o_ref):
    c = c_ref[...]
    cond = c * jax.nn.sigmoid(c)
    o_ref[...] = jnp.dot(cond, w_ref[...], precision=lax.Precision.HIGHEST,
                         preferred_element_type=F32) + b_ref[...]


def _ada(c_pad, w_ada, b_ada):
    rows, d = c_pad.shape
    n = w_ada.shape[1]
    tn = 1024
    return pl.pallas_call(
        _ada_kernel,
        grid=(n // tn,),
        in_specs=[pl.BlockSpec((rows, d), lambda j: (0, 0)),
                  pl.BlockSpec((d, tn), lambda j: (0, j)),
                  pl.BlockSpec((1, tn), lambda j: (0, j))],
        out_specs=pl.BlockSpec((rows, tn), lambda j: (0, j)),
        out_shape=jax.ShapeDtypeStruct((rows, n), F32),
        name="ada",
    )(c_pad, w_ada, b_ada)


def _in_proj_kernel(x_ref, mod_ref, nw_ref, w_ref, *o_refs):
    x = x_ref[...]
    shift = mod_ref[0, 0:1, :]
    scale = mod_ref[0, 1:2, :]
    h = (_rms(x) * nw_ref[...]) * (1.0 + scale) + shift
    hb = h.astype(BF16)
    off = 0
    for o_ref, width in zip(o_refs, IN_SIZES):
        o_ref[...] = jnp.dot(hb, w_ref[:, off:off + width],
                             preferred_element_type=F32).astype(o_ref.dtype)
        off += width


def _in_proj(x2, mod, norm_w, w_in_bf16, seq):
    t, d = x2.shape
    tm = min(ROW_TILE, seq)
    per_batch = seq // tm
    in_w = w_in_bf16.shape[1]
    return pl.pallas_call(
        _in_proj_kernel,
        grid=(t // tm,),
        in_specs=[pl.BlockSpec((tm, d), lambda i: (i, 0)),
                  pl.BlockSpec((1, 6, d), lambda i: (i // per_batch, 0, 0)),
                  pl.BlockSpec((1, d), lambda i: (0, 0)),
                  pl.BlockSpec((d, in_w), lambda i: (0, 0))],
        out_specs=[pl.BlockSpec((tm, w), lambda i: (i, 0)) for w in IN_SIZES],
        out_shape=[jax.ShapeDtypeStruct((t, w), BF16) for w in IN_SIZES],
        compiler_params=pltpu.CompilerParams(vmem_limit_bytes=VMEM_LIMIT),
        name="in_proj",
    )(x2, mod, norm_w, w_in_bf16)


def _rotary(x, cos, sin_even, sin_odd):
    nxt = pltpu.roll(x, LANES - 1, 1)
    prv = pltpu.roll(x, 1, 1)
    return x * cos + nxt * sin_even + prv * sin_odd


def _ret_kernel(q_ref, k_ref, v_ref, g_ref, cos_ref, sine_ref, sino_ref,
                hmask_ref, xi_ref, zeta_ref, dmask_ref, gch_ref, o_ref, state_ref):
    @pl.when(pl.program_id(1) == 0)
    def _():
        state_ref[...] = jnp.zeros_like(state_ref)

    n_sub = q_ref.shape[0] // RET_CHUNK
    for c in range(n_sub):
        rows = slice(c * RET_CHUNK, (c + 1) * RET_CHUNK)
        for pair in range(RET_HEADS // 2):
            lanes = slice(pair * LANES, (pair + 1) * LANES)
            cos = cos_ref[rows, lanes]
            sine = sine_ref[rows, lanes]
            sino = sino_ref[rows, lanes]
            qr = _rotary(q_ref[rows, lanes].astype(F32), cos, sine, sino)
            kr = _rotary(k_ref[rows, lanes].astype(F32), cos, sine, sino) * (RET_KEY_DIM ** -0.5)
            qb = qr.astype(BF16)
            for hh in range(2):
                h = 2 * pair + hh
                vcols = slice(h * RET_VAL_DIM, (h + 1) * RET_VAL_DIM)
                v = v_ref[rows, vcols]
                km = (kr * hmask_ref[h]).astype(BF16)
                scores = lax.dot_general(qb, km, (((1,), (1,)), ((), ())),
                                         preferred_element_type=F32) * dmask_ref[h]
                inner = jnp.dot(scores.astype(BF16), v, preferred_element_type=F32)
                qx = (qr * xi_ref[h]).astype(BF16)
                state = state_ref[h]
                cross = jnp.dot(qx, state.astype(BF16), preferred_element_type=F32)
                kz = (kr * zeta_ref[h]).astype(BF16)
                kv = lax.dot_general(kz, v, (((0,), (0,)), ((), ())),
                                     preferred_element_type=F32)
                state_ref[h] = state * gch_ref[h] + kv
                y = _rms(inner + cross)
                g = g_ref[rows, vcols].astype(F32)
                o_ref[rows, vcols] = (g * jax.nn.sigmoid(g) * y).astype(o_ref.dtype)


def _retention_tables(seq):
    dk, c, nh = RET_KEY_DIM, RET_CHUNK, RET_HEADS
    pos = jnp.arange(seq, dtype=F32)
    inv_freq = 1.0 / (10000.0 ** jnp.linspace(0.0, 1.0, dk // 2, dtype=F32))
    ang = pos[:, None] * jnp.repeat(inv_freq, 2)[None, :]
    sin = jnp.tile(jnp.sin(ang), (1, nh))
    cos = jnp.tile(jnp.cos(ang), (1, nh))
    even = (jnp.arange(nh * dk) % 2 == 0)[None, :]
    sin_even = jnp.where(even, -sin, 0.0)
    sin_odd = jnp.where(even, 0.0, sin)
    log_g = jnp.log(1.0 - 2.0 ** (-5.0 - jnp.arange(nh, dtype=F32)))
    i = jnp.arange(c, dtype=F32)
    rel = i[:, None] - i[None, :]
    dmask = jnp.where(rel[None] >= 0,
                      jnp.exp(jnp.maximum(rel, 0.0)[None] * log_g[:, None, None]), 0.0)
    zeta = jnp.exp((c - 1.0 - i)[None, :] * log_g[:, None])
    xi = jnp.exp((i + 1.0)[None, :] * log_g[:, None])
    g_chunk = jnp.exp(c * log_g)
    lane = jnp.arange(LANES)
    hmask = jnp.stack([((lane // dk) == (h % 2)).astype(F32) for h in range(nh)])[:, None, :]
    xi_t = xi[:, :, None] * hmask
    zeta_t = zeta[:, :, None] * hmask
    gch = jnp.broadcast_to(g_chunk[:, None, None], (nh, 1, LANES))
    return cos, sin_even, sin_odd, hmask, xi_t, zeta_t, dmask, gch


def _retention(q, k, v, g, batch, seq):
    t = q.shape[0]
    rb = min(RET_ROWS, seq)
    per_batch = seq // rb
    cos, sin_even, sin_odd, hmask, xi_t, zeta_t, dmask, gch = _retention_tables(seq)
    row = lambda w: pl.BlockSpec((rb, w), lambda b, j: (b * per_batch + j, 0))
    tab = lambda w: pl.BlockSpec((rb, w), lambda b, j: (j, 0))
    full = lambda a: pl.BlockSpec(a.shape, lambda b, j: (0,) * a.ndim)
    return pl.pallas_call(
        _ret_kernel,
        grid=(batch, per_batch),
        in_specs=[row(RET_QK_W), row(RET_QK_W), row(RET_V_W), row(RET_V_W),
                  tab(RET_QK_W), tab(RET_QK_W), tab(RET_QK_W),
                  full(hmask), full(xi_t), full(zeta_t), full(dmask), full(gch)],
        out_specs=row(RET_V_W),
        out_shape=jax.ShapeDtypeStruct((t, RET_V_W), BF16),
        scratch_shapes=[pltpu.VMEM((RET_HEADS, LANES, RET_VAL_DIM), F32)],
        compiler_params=pltpu.CompilerParams(
            dimension_semantics=("arbitrary", "arbitrary"), vmem_limit_bytes=VMEM_LIMIT),
        name="retention",
    )(q, k, v, g, cos, sin_even, sin_odd, hmask, xi_t, zeta_t, dmask, gch)


def _t5_bucket(rel):
    n = jnp.maximum(rel, 0)
    max_exact = REL_BUCKETS // 2
    nf = jnp.maximum(n, 1).astype(F32)
    large = max_exact + (jnp.log(nf / max_exact) / math.log(REL_MAX_DIST / max_exact)
                         * (REL_BUCKETS - max_exact)).astype(jnp.int32)
    large = jnp.minimum(large, REL_BUCKETS - 1)
    return jnp.where(n < max_exact, n, large)


def _bias_tiles(rel_bias, blk):
    r = jnp.arange(blk, dtype=jnp.int32)
    far = rel_bias[REL_BUCKETS - 1]
    rel0 = r[None, :] - r[:, None]
    rel1 = rel0 + blk
    buckets = jnp.arange(REL_BUCKETS, dtype=jnp.int32)

    def tile(rel):
        hot = (_t5_bucket(rel)[:, :, None] == buckets).astype(F32)
        return jnp.einsum('krb,bh->hkr', hot, rel_bias, precision=lax.Precision.HIGHEST)

    b0 = jnp.where(rel0[None] >= 0, (tile(rel0) - far[:, None, None]) * LOG2_E, NEG_BIG)
    b1 = (tile(rel1) - far[:, None, None]) * LOG2_E
    return b0, b1


def _attn_kernel(q_ref, k_ref, vt_ref, b0_ref, b1_ref, lq1_ref, lk1_ref, lq2_ref, lk2_ref,
                 sw_ref, o_ref, m_ref, acc_ref):
    blk = q_ref.shape[0]
    i = pl.program_id(2)
    lane = lax.broadcasted_iota(jnp.int32, (1, LANES), 1)
    q = (q_ref[...].astype(F32) * (DIFF_HEAD_DIM ** -0.5 * LOG2_E)).astype(BF16)
    zero = jnp.zeros_like(q)
    qm = (jnp.where(lane < DIFF_HEAD_DIM, q, zero), jnp.where(lane >= DIFF_HEAD_DIM, q, zero))

    m_ref[...] = jnp.full_like(m_ref, NEG_BIG)
    acc_ref[...] = jnp.zeros_like(acc_ref)

    def step(j, bias):
        start = pl.multiple_of(j * blk, blk)
        kb = k_ref[pl.ds(start, blk), :]
        vt = vt_ref[0, 0, j]
        for mi in range(2):
            s = lax.dot_general(kb, qm[mi], (((1,), (1,)), ((), ())),
                                preferred_element_type=F32)
            if bias is not None:
                s = s + bias
            m_old = m_ref[mi]
            m_new = jnp.maximum(m_old, jnp.max(s, axis=0, keepdims=True))
            alpha = jnp.exp2(m_old - m_new)
            p = jnp.exp2(s - m_new).astype(BF16)
            acc_ref[mi] = alpha * acc_ref[mi] + jnp.dot(vt, p, preferred_element_type=F32)
            m_ref[mi] = m_new

    def far_body(j, carry):
        step(j, None)
        return carry

    lax.fori_loop(0, jnp.maximum(i - 1, 0), far_body, 0)

    @pl.when(i >= 1)
    def _():
        step(i - 1, b1_ref[0])

    step(i, b0_ref[0])

    lam = (jnp.exp(jnp.sum(lq1_ref[...] * lk1_ref[...], axis=-1, keepdims=True))
           - jnp.exp(jnp.sum(lq2_ref[...] * lk2_ref[...], axis=-1, keepdims=True))
           + LAMBDA_INIT)
    hw = 2 * DIFF_HEAD_DIM
    a = (acc_ref[0, :hw, :] / acc_ref[0, hw:hw + 1, :]
         - lam * (acc_ref[1, :hw, :] / acc_ref[1, hw:hw + 1, :]))
    a = a * lax.rsqrt(jnp.mean(a * a, axis=0, keepdims=True) + NORM_EPS)
    o_ref[...] = (a.T * sw_ref[...] * (1.0 - LAMBDA_INIT)).astype(o_ref.dtype)


def _diff_attention(q, k, v, rel_bias, lq1, lk1, lq2, lk2, subln_w, batch, seq):
    t = q.shape[0]
    blk = min(ATT_BLOCK, seq)
    nq = seq // blk
    hw = 2 * DIFF_HEAD_DIM
    b0, b1 = _bias_tiles(rel_bias, blk)
    vt = jnp.transpose(v.reshape(batch, nq, blk, DIFF_HEADS, hw), (0, 3, 1, 4, 2))
    ones = jnp.zeros((batch, DIFF_HEADS, nq, ONES_ROWS, blk), vt.dtype).at[:, :, :, 0, :].set(1.0)
    vt = jnp.concatenate([vt, ones], axis=3)
    small = lambda a: pl.BlockSpec(a.shape, lambda b, h, i: (0,) * a.ndim)
    return pl.pallas_call(
        _attn_kernel,
        grid=(batch, DIFF_HEADS, nq),
        in_specs=[pl.BlockSpec((blk, hw), lambda b, h, i: (b * nq + i, h)),
                  pl.BlockSpec((seq, hw), lambda b, h, i: (b, h)),
                  pl.BlockSpec((1, 1, nq, hw + ONES_ROWS, blk), lambda b, h, i: (b, h, 0, 0, 0)),
                  pl.BlockSpec((1, blk, blk), lambda b, h, i: (h, 0, 0)),
                  pl.BlockSpec((1, blk, blk), lambda b, h, i: (h, 0, 0)),
                  small(lq1), small(lk1), small(lq2), small(lk2), small(subln_w)],
        out_specs=pl.BlockSpec((blk, hw), lambda b, h, i: (b * nq + i, h)),
        out_shape=jax.ShapeDtypeStruct((t, DIFF_V_W), BF16),
        scratch_shapes=[pltpu.VMEM((2, 1, blk), F32), pltpu.VMEM((2, hw + ONES_ROWS, blk), F32)],
        compiler_params=pltpu.CompilerParams(
            dimension_semantics=("arbitrary", "arbitrary", "arbitrary"),
            vmem_limit_bytes=VMEM_LIMIT),
        name="diff_attn",
    )(q, k, vt, b0, b1, lq1, lk1, lq2, lk2, subln_w)


def _out_kernel(yr_ref, yd_ref, x_ref, mod_ref, nw_ref, wo_ref, wr_ref, br_ref,
                x1_ref, hp_ref, meta_ref, gate_ref, cnt_ref, run_ref):
    tm = x_ref.shape[0]

    @pl.when(pl.program_id(0) == 0)
    def _():
        run_ref[...] = jnp.zeros_like(run_ref)

    mixed = (jnp.dot(yr_ref[...], wo_ref[0:RET_V_W, :], preferred_element_type=F32)
             + jnp.dot(yd_ref[...], wo_ref[RET_V_W:, :], preferred_element_type=F32))
    gate1 = mod_ref[0, 2:3, :]
    shift2 = mod_ref[0, 3:4, :]
    scale2 = mod_ref[0, 4:5, :]
    x1 = x_ref[...] + gate1 * mixed
    x1_ref[...] = x1
    h2 = (_rms(x1) * nw_ref[...]) * (1.0 + scale2) + shift2
    hp_ref[...] = _pack_halves(h2)

    logits = jnp.dot(h2, wr_ref[...], precision=lax.Precision.HIGHEST,
                     preferred_element_type=F32) + br_ref[...]
    lane = lax.broadcasted_iota(jnp.int32, logits.shape, 1)
    work = logits
    vals, idxs, hots = [], [], []
    for _ in range(TOP_K):
        mx = jnp.max(work, axis=-1, keepdims=True)
        idx = jnp.min(jnp.where(work == mx, lane, N_EXPERTS), axis=-1, keepdims=True)
        hot = lane == idx
        vals.append(mx)
        idxs.append(idx)
        hots.append(hot)
        work = jnp.where(hot, -jnp.inf, work)
    exps = [jnp.exp(v - vals[0]) for v in vals]
    denom = exps[0] + exps[1] + exps[2] + exps[3]

    sel = jnp.zeros(logits.shape, F32)
    for hot in hots:
        sel = sel + hot.astype(F32)
    r = lax.broadcasted_iota(jnp.int32, (tm, tm), 0)
    c = lax.broadcasted_iota(jnp.int32, (tm, tm), 1)
    before = (c < r).astype(BF16)
    prefix = jnp.dot(before, sel.astype(BF16), preferred_element_type=F32) + run_ref[...]
    ranks = [jnp.sum(jnp.where(hot, prefix, 0.0), axis=-1, keepdims=True) for hot in hots]
    run_ref[...] = run_ref[...] + jnp.sum(sel, axis=0, keepdims=True)
    cnt_ref[...] = run_ref[...].astype(jnp.int32)

    lane8 = lax.broadcasted_iota(jnp.int32, (tm, 2 * TOP_K), 1)
    meta = jnp.zeros((tm, 2 * TOP_K), jnp.int32)
    gates = jnp.zeros((tm, 2 * TOP_K), F32)
    for kk in range(TOP_K):
        meta = jnp.where(lane8 == kk, idxs[kk], meta)
        meta = jnp.where(lane8 == TOP_K + kk, ranks[kk].astype(jnp.int32), meta)
        gates = jnp.where(lane8 == kk, exps[kk] / denom, gates)
    meta_ref[...] = meta
    gate_ref[...] = gates


def _out_router(y_r, y_d, x2, mod, norm_w, w_out_bf16, w_router, b_router, seq):
    t, d = x2.shape
    tm = min(ROW_TILE, seq)
    per_batch = seq // tm
    ne = w_router.shape[1]
    row = lambda w: pl.BlockSpec((tm, w), lambda i: (i, 0))
    const = lambda a: pl.BlockSpec(a.shape, lambda i: (0,) * a.ndim)
    return pl.pallas_call(
        _out_kernel,
        grid=(t // tm,),
        in_specs=[row(RET_V_W), row(DIFF_V_W), row(d),
                  pl.BlockSpec((1, 6, d), lambda i: (i // per_batch, 0, 0)),
                  const(norm_w), const(w_out_bf16), const(w_router), const(b_router)],
        out_specs=[row(d), row(d // 2), row(2 * TOP_K), row(2 * TOP_K),
                   pl.BlockSpec((1, ne), lambda i: (0, 0))],
        out_shape=[jax.ShapeDtypeStruct((t, d), F32),
                   jax.ShapeDtypeStruct((t, d // 2), jnp.uint32),
                   jax.ShapeDtypeStruct((t, 2 * TOP_K), jnp.int32),
                   jax.ShapeDtypeStruct((t, 2 * TOP_K), F32),
                   jax.ShapeDtypeStruct((1, ne), jnp.int32)],
        scratch_shapes=[pltpu.VMEM((1, ne), F32)],
        compiler_params=pltpu.CompilerParams(
            dimension_semantics=("arbitrary",), vmem_limit_bytes=VMEM_LIMIT),
        name="out_router",
    )(y_r, y_d, x2, mod, norm_w, w_out_bf16, w_router, b_router)


def _w1_prep_kernel(w_ref, p_ref, o_ref):
    for s in range(w_ref.shape[2] // PERM_TILE):
        cols = slice(s * PERM_TILE, (s + 1) * PERM_TILE)
        o_ref[0, :, cols] = jnp.dot(w_ref[0, :, cols].astype(BF16), p_ref[...],
                                    preferred_element_type=F32).astype(BF16)


def _pair_split_matrix():
    i = jnp.arange(PERM_TILE)[:, None]
    j = jnp.arange(PERM_TILE)[None, :]
    half = PERM_TILE // 2
    src = jnp.where(j < half, 2 * j, 2 * (j - half) + 1)
    return (i == src).astype(BF16)


def _w1_prep(w1):
    e, d, f2 = w1.shape
    tn = 1024
    return pl.pallas_call(
        _w1_prep_kernel,
        grid=(e, f2 // tn),
        in_specs=[pl.BlockSpec((1, d, tn), lambda i, j: (i, 0, j)),
                  pl.BlockSpec((PERM_TILE, PERM_TILE), lambda i, j: (0, 0))],
        out_specs=pl.BlockSpec((1, d, tn), lambda i, j: (i, 0, j)),
        out_shape=jax.ShapeDtypeStruct((e, d, f2), BF16),
        compiler_params=pltpu.CompilerParams(vmem_limit_bytes=VMEM_LIMIT),
        name="w1_prep",
    )(w1, _pair_split_matrix())


def _pair_split_bias(b1):
    e, f2 = b1.shape
    nt = f2 // PERM_TILE
    g = b1[:, 0::2].reshape(e, nt, 1, PERM_TILE // 2)
    l = b1[:, 1::2].reshape(e, nt, 1, PERM_TILE // 2)
    return jnp.concatenate([g, l], axis=2).reshape(e, 1, f2)


def _slot_tok_kernel(first_slot, dest_ref, init_ref, o_ref, sem):
    init_copy = pltpu.make_async_copy(init_ref, o_ref, sem)
    init_copy.start()
    init_copy.wait()
    n_slots = dest_ref.shape[0]

    def put(i, carry):
        o_ref[dest_ref[i]] = lax.shift_right_logical(first_slot + i, 2)
        return carry

    lax.fori_loop(0, n_slots, put, 0, unroll=8)


def _slot_tokens(dest, n_rows):
    assert TOP_K == 4
    flat = dest.reshape(-1)
    per_call = flat.shape[0] // SLOT_MAP_CALLS
    slot_tok = jnp.zeros((n_rows,), jnp.int32)
    for part in range(SLOT_MAP_CALLS):
        slot_tok = pl.pallas_call(
            functools.partial(_slot_tok_kernel, part * per_call),
            in_specs=[pl.BlockSpec(memory_space=pltpu.SMEM), pl.BlockSpec(memory_space=pl.ANY)],
            out_specs=pl.BlockSpec(memory_space=pltpu.SMEM),
            out_shape=jax.ShapeDtypeStruct((n_rows,), jnp.int32),
            scratch_shapes=[pltpu.SemaphoreType.DMA(())],
            name="slot_tokens",
        )(flat[part * per_call:(part + 1) * per_call], slot_tok)
    return slot_tok


def _expert_kernel(be_ref, nv_ref, tcur_ref, tnxt_ref, hp_ref, w1_ref, b1_ref, w2_ref, b2_ref,
                   o_ref, hbuf, xb0, xb1, sem):
    b = pl.program_id(0)
    nv = nv_ref[0]
    bm = xb0.shape[0]

    def gather(tok_ref, dst):
        for r in range(bm):
            dst[pl.ds(r, 1), :] = hbuf[pl.ds(tok_ref[0, 0, r], 1), :]

    @pl.when(b == 0)
    def _():
        load = pltpu.make_async_copy(hp_ref, hbuf, sem)
        load.start()
        load.wait()
        gather(tcur_ref, xb0)

    def run(cur, nxt):
        x = _unpack_halves(cur[...])
        half = PERM_TILE // 2
        acts = []
        for j in range(w1_ref.shape[2] // PERM_TILE):
            cols = slice(j * PERM_TILE, (j + 1) * PERM_TILE)
            h = jnp.dot(x, w1_ref[0, :, cols], preferred_element_type=F32) + b1_ref[0, :, cols]
            glu = jnp.minimum(h[:, :half], SWIGLU_LIMIT)
            lin = jnp.clip(h[:, half:], -SWIGLU_LIMIT, SWIGLU_LIMIT)
            acts.append((glu * jax.nn.sigmoid(SWIGLU_ALPHA * glu) * (lin + 1.0)).astype(BF16))
        act = jnp.concatenate(acts, axis=1)
        y = jnp.dot(act, w2_ref[0], preferred_element_type=F32) + b2_ref[0]
        _store_row_tiles(o_ref, y)
        gather(tnxt_ref, nxt)

    even = lax.rem(b, 2) == 0

    @pl.when(jnp.logical_and(b < nv, even))
    def _():
        run(xb0, xb1)

    @pl.when(jnp.logical_and(b < nv, jnp.logical_not(even)))
    def _():
        run(xb1, xb0)

    @pl.when(b >= nv)
    def _():
        o_ref[...] = jnp.zeros_like(o_ref)


def _experts(block_e, n_valid, slot_tok, hp, w1p, b1p, w2, b2):
    d = D_MODEL
    f2 = w1p.shape[2]
    f = w2.shape[1]
    bm = EXPERT_ROWS
    nb = slot_tok.shape[0] // bm
    tok3 = slot_tok.reshape(nb, 1, bm)
    exp = lambda b, be, nv: be[jnp.maximum(jnp.minimum(b, nv[0] - 1), 0)]
    grid_spec = pltpu.PrefetchScalarGridSpec(
        num_scalar_prefetch=2,
        grid=(nb,),
        in_specs=[pl.BlockSpec((1, 1, bm), lambda b, be, nv: (b, 0, 0), memory_space=pltpu.SMEM),
                  pl.BlockSpec((1, 1, bm), lambda b, be, nv: (jnp.minimum(b + 1, nb - 1), 0, 0),
                               memory_space=pltpu.SMEM),
                  pl.BlockSpec(memory_space=pl.ANY),
                  pl.BlockSpec((1, d, f2), lambda b, be, nv: (exp(b, be, nv), 0, 0)),
                  pl.BlockSpec((1, 1, f2), lambda b, be, nv: (exp(b, be, nv), 0, 0)),
                  pl.BlockSpec((1, f, d), lambda b, be, nv: (exp(b, be, nv), 0, 0)),
                  pl.BlockSpec((1, 1, d), lambda b, be, nv: (exp(b, be, nv), 0, 0))],
        out_specs=pl.BlockSpec((bm * ROW_SUB, LANES), lambda b, be, nv: (b, 0)),
        scratch_shapes=[pltpu.VMEM(hp.shape, hp.dtype), pltpu.VMEM((bm, hp.shape[1]), hp.dtype),
                        pltpu.VMEM((bm, hp.shape[1]), hp.dtype), pltpu.SemaphoreType.DMA(())],
    )
    return pl.pallas_call(
        _expert_kernel,
        grid_spec=grid_spec,
        out_shape=jax.ShapeDtypeStruct((nb * bm * ROW_SUB, LANES), F32),
        compiler_params=pltpu.CompilerParams(
            dimension_semantics=("arbitrary",), vmem_limit_bytes=VMEM_LIMIT),
        name="experts",
    )(block_e, n_valid, tok3, tok3, hp, w1p, b1p, w2, b2)


def _combine_kernel(dcur_ref, dnxt_ref, gate_ref, x1_ref, mod_ref, nf_ref, y_ref, o_ref, buf, sems):
    nt = x1_ref.shape[0]
    i = pl.program_id(0)
    n = pl.num_programs(0)
    slot = lax.rem(i, 2)

    def row_copy(src_row, s, kk, r):
        return pltpu.make_async_copy(y_ref.at[pl.ds(src_row * ROW_SUB, ROW_SUB), :],
                                     buf.at[s, kk, pl.ds(r * ROW_SUB, ROW_SUB), :], sems.at[s])

    def fetch(d_ref, s):
        for j in range(nt):
            for kk in range(TOP_K):
                row_copy(d_ref[0, 0, j * TOP_K + kk], s, kk, j).start(priority=kk % DMA_QUEUES)

    def wait_all(s):
        for kk in range(TOP_K):
            pltpu.make_async_copy(y_ref.at[pl.ds(0, nt * ROW_SUB), :], buf.at[s, kk], sems.at[s]).wait()

    @pl.when(i == 0)
    def _():
        fetch(dcur_ref, 0)

    fetch(dnxt_ref, 1 - slot)
    wait_all(slot)

    g = gate_ref[...]
    parts = []
    for s in range(ROW_SUB):
        acc = buf[slot, 0, pl.ds(s, nt, stride=ROW_SUB), :] * g[:, 0:1]
        for kk in range(1, TOP_K):
            acc = acc + buf[slot, kk, pl.ds(s, nt, stride=ROW_SUB), :] * g[:, kk:kk + 1]
        parts.append(acc)
    moe = jnp.concatenate(parts, axis=1)
    gate2 = mod_ref[0, 5:6, :]
    x2 = x1_ref[...] + gate2 * moe
    o_ref[...] = _rms(x2) * nf_ref[...]

    @pl.when(i == n - 1)
    def _():
        wait_all(1 - slot)


def _combine(dest, gates, x1, mod, normf_w, y, seq):
    t, d = x1.shape
    nt = min(COMBINE_TOKENS, seq)
    steps = t // nt
    per_batch = seq // nt
    dest2 = dest.reshape(steps, 1, nt * TOP_K)
    return pl.pallas_call(
        _combine_kernel,
        grid=(steps,),
        in_specs=[pl.BlockSpec((1, 1, nt * TOP_K), lambda i: (i, 0, 0), memory_space=pltpu.SMEM),
                  pl.BlockSpec((1, 1, nt * TOP_K), lambda i: (jnp.minimum(i + 1, steps - 1), 0, 0),
                               memory_space=pltpu.SMEM),
                  pl.BlockSpec((nt, 2 * TOP_K), lambda i: (i, 0)),
                  pl.BlockSpec((nt, d), lambda i: (i, 0)),
                  pl.BlockSpec((1, 6, d), lambda i: (i // per_batch, 0, 0)),
                  pl.BlockSpec((1, d), lambda i: (0, 0)),
                  pl.BlockSpec(memory_space=pl.ANY)],
        out_specs=pl.BlockSpec((nt, d), lambda i: (i, 0)),
        out_shape=jax.ShapeDtypeStruct((t, d), F32),
        scratch_shapes=[pltpu.VMEM((2, TOP_K, nt * ROW_SUB, LANES), F32),
                        pltpu.SemaphoreType.DMA((2,))],
        compiler_params=pltpu.CompilerParams(
            dimension_semantics=("arbitrary",), vmem_limit_bytes=VMEM_LIMIT),
        name="combine",
    )(dest2, dest2, gates, x1, mod, normf_w, y)


def kernel(x, c, w_ada, b_ada, norm1_w, w_in, lam_q1, lam_k1, lam_q2, lam_k2, subln_w, rel_bias,
           w_out, norm2_w, w_router, b_router, w1, b1, w2, b2, normf_w):
    batch, seq, d = x.shape
    t = batch * seq
    x2 = x.reshape(t, d)

    c_pad = jnp.zeros((8, d), F32).at[:batch].set(c)
    mod = _ada(c_pad, w_ada[0], b_ada[0][None, :])[:batch].reshape(batch, 6, d)

    q_r, k_r, v_r, g_r, q_d, k_d, v_d = _in_proj(x2, mod, norm1_w[0][None, :],
                                                 w_in[0].astype(BF16), seq)
    y_r = _retention(q_r, k_r, v_r, g_r, batch, seq)
    y_d = _diff_attention(q_d, k_d, v_d, rel_bias, lam_q1, lam_k1, lam_q2, lam_k2,
                          subln_w, batch, seq)

    x1, hp, meta, gates, counts = _out_router(y_r, y_d, x2, mod, norm2_w[0][None, :],
                                              w_out[0].astype(BF16), w_router[0],
                                              b_router[0][None, :], seq)

    bm = EXPERT_ROWS
    counts = counts[0]
    padded = (counts + bm - 1) // bm * bm
    pad_end = jnp.cumsum(padded)
    pad_start = pad_end - padded
    sel_e = meta[:, :TOP_K]
    hot_e = sel_e[:, :, None] == jnp.arange(N_EXPERTS, dtype=jnp.int32)[None, None, :]
    dest = jnp.sum(jnp.where(hot_e, pad_start[None, None, :], 0), axis=-1) + meta[:, TOP_K:]
    n_rows = (t * TOP_K // bm + N_EXPERTS) * bm
    nb = n_rows // bm
    block_start = jnp.arange(nb, dtype=jnp.int32) * bm
    block_e = jnp.minimum(jnp.sum((pad_end[None, :] <= block_start[:, None]).astype(jnp.int32), axis=1),
                          N_EXPERTS - 1)
    n_valid = (pad_end[-1:] // bm).astype(jnp.int32)
    slot_tok = _slot_tokens(dest, n_rows)

    ys = _experts(block_e, n_valid, slot_tok, hp, _w1_prep(w1[0]), _pair_split_bias(b1[0]),
                  w2[0].astype(BF16), b2[0][:, None, :])
    out = _combine(dest, gates, x1, mod, normf_w[None, :], ys, seq)
    return out.reshape(batch, seq, d)
```

```python
import functools
import math

import jax
import jax.numpy as jnp
from jax import lax
from jax.experimental import pallas as pl
from jax.experimental.pallas import tpu as pltpu

F32 = jnp.float32
BF16 = jnp.bfloat16

D_MODEL = 1024
RET_HEADS = 4
RET_KEY_DIM = 64
RET_VAL_DIM = 128
RET_QK_W = RET_HEADS * RET_KEY_DIM
RET_V_W = RET_HEADS * RET_VAL_DIM
RET_CHUNK = 128
DIFF_HEADS = 4
DIFF_HEAD_DIM = 64
DIFF_QK_W = DIFF_HEADS * 2 * DIFF_HEAD_DIM
DIFF_V_W = DIFF_HEADS * 2 * DIFF_HEAD_DIM
IN_SIZES = (RET_QK_W, RET_QK_W, RET_V_W, RET_V_W, DIFF_QK_W, DIFF_QK_W, DIFF_V_W)
REL_BUCKETS = 32
REL_MAX_DIST = 128
N_EXPERTS = 32
TOP_K = 4
SWIGLU_ALPHA = 1.702
SWIGLU_LIMIT = 7.0
NORM_EPS = 1e-6
LAMBDA_INIT = 0.8 - 0.6 * math.exp(-0.3 * 0)

LANES = 128
ROW_SUB = D_MODEL // LANES
NEG_BIG = -1e30
LOG2_E = math.log2(math.e)
ONES_ROWS = 16
VMEM_LIMIT = 56 * 1024 * 1024

ROW_TILE = 512
RET_ROWS = 512
ATT_BLOCK = 512
ATT_STRIP = 256
EXPERT_ROWS = 256
PERM_TILE = 256
COMBINE_TOKENS = 128
DMA_QUEUES = 2
SLOT_MAP_CALLS = 2


def _rms(x):
    return x * lax.rsqrt(jnp.mean(x * x, axis=-1, keepdims=True) + NORM_EPS)


def _store_row_tiles(ref, x):
    rows = x.shape[0]
    for s in range(ROW_SUB):
        ref[pl.ds(s, rows, stride=ROW_SUB), :] = x[:, s * LANES:(s + 1) * LANES]


def _load_row_tiles(ref, rows):
    return jnp.concatenate([ref[pl.ds(s, rows, stride=ROW_SUB), :] for s in range(ROW_SUB)], axis=1)


def _pack_halves(x):
    half = x.shape[1] // 2
    lo = lax.bitcast_convert_type(x[:, :half].astype(BF16).astype(F32), jnp.uint32)
    hi = lax.bitcast_convert_type(x[:, half:].astype(BF16).astype(F32), jnp.uint32)
    return (hi & jnp.uint32(0xFFFF0000)) | (lo >> 16)


def _unpack_halves(w):
    lo = lax.bitcast_convert_type(w << 16, F32).astype(BF16)
    hi = lax.bitcast_convert_type(w & jnp.uint32(0xFFFF0000), F32).astype(BF16)
    return jnp.concatenate([lo, hi], axis=1)


def _ada_kernel(c_ref, w_ref, b_ref, o_ref):
    c = c_ref[...]
    cond = c * jax.nn.sigmoid(c)
    o_ref[...] = jnp.dot(cond, w_ref[...], precision=lax.Precision.HIGHEST,
                         preferred_element_type=F32) + b_ref[...]


def _ada(c_pad, w_ada, b_ada):
    rows, d = c_pad.shape
    n = w_ada.shape[1]
    tn = 1024
    return pl.pallas_call(
        _ada_kernel,
        grid=(n // tn,),
        in_specs=[pl.BlockSpec((rows, d), lambda j: (0, 0)),
                  pl.BlockSpec((d, tn), lambda j: (0, j)),
                  pl.BlockSpec((1, tn), lambda j: (0, j))],
        out_specs=pl.BlockSpec((rows, tn), lambda j: (0, j)),
        out_shape=jax.ShapeDtypeStruct((rows, n), F32),
        name="ada",
    )(c_pad, w_ada, b_ada)


def _in_proj_kernel(x_ref, mod_ref, nw_ref, w_ref, *o_refs):
    x = x_ref[...]
    shift = mod_ref[0, 0:1, :]
    scale = mod_ref[0, 1:2, :]
    h = (_rms(x) * nw_ref[...]) * (1.0 + scale) + shift
    hb = h.astype(BF16)
    off = 0
    for o_ref, width in zip(o_refs, IN_SIZES):
        o_ref[...] = jnp.dot(hb, w_ref[:, off:off + width],
                             preferred_element_type=F32).astype(o_ref.dtype)
        off += width


def _in_proj(x2, mod, norm_w, w_in_bf16, seq):
    t, d = x2.shape
    tm = min(ROW_TILE, seq)
    per_batch = seq // tm
    in_w = w_in_bf16.shape[1]
    return pl.pallas_call(
        _in_proj_kernel,
        grid=(t // tm,),
        in_specs=[pl.BlockSpec((tm, d), lambda i: (i, 0)),
                  pl.BlockSpec((1, 6, d), lambda i: (i // per_batch, 0, 0)),
                  pl.BlockSpec((1, d), lambda i: (0, 0)),
                  pl.BlockSpec((d, in_w), lambda i: (0, 0))],
        out_specs=[pl.BlockSpec((tm, w), lambda i: (i, 0)) for w in IN_SIZES],
        out_shape=[jax.ShapeDtypeStruct((t, w), BF16) for w in IN_SIZES],
        compiler_params=pltpu.CompilerParams(vmem_limit_bytes=VMEM_LIMIT),
        name="in_proj",
    )(x2, mod, norm_w, w_in_bf16)


def _rotary(x, cos, sin_even, sin_odd):
    nxt = pltpu.roll(x, LANES - 1, 1)
    prv = pltpu.roll(x, 1, 1)
    return x * cos + nxt * sin_even + prv * sin_odd


def _ret_kernel(q_ref, k_ref, v_ref, g_ref, cos_ref, sine_ref, sino_ref,
                hmask_ref, xi_ref, zeta_ref, dmask_ref, gch_ref, o_ref, state_ref):
    @pl.when(pl.program_id(1) == 0)
    def _():
        state_ref[...] = jnp.zeros_like(state_ref)

    n_sub = q_ref.shape[0] // RET_CHUNK
    for c in range(n_sub):
        rows = slice(c * RET_CHUNK, (c + 1) * RET_CHUNK)
        for pair in range(RET_HEADS // 2):
            lanes = slice(pair * LANES, (pair + 1) * LANES)
            cos = cos_ref[rows, lanes]
            sine = sine_ref[rows, lanes]
            sino = sino_ref[rows, lanes]
            qr = _rotary(q_ref[rows, lanes].astype(F32), cos, sine, sino)
            kr = _rotary(k_ref[rows, lanes].astype(F32), cos, sine, sino) * (RET_KEY_DIM ** -0.5)
            qb = qr.astype(BF16)
            for hh in range(2):
                h = 2 * pair + hh
                vcols = slice(h * RET_VAL_DIM, (h + 1) * RET_VAL_DIM)
                v = v_ref[rows, vcols]
                km = (kr * hmask_ref[h]).astype(BF16)
                scores = lax.dot_general(qb, km, (((1,), (1,)), ((), ())),
                                         preferred_element_type=F32) * dmask_ref[h]
                inner = jnp.dot(scores.astype(BF16), v, preferred_element_type=F32)
                qx = (qr * xi_ref[h]).astype(BF16)
                state = state_ref[h]
                cross = jnp.dot(qx, state.astype(BF16), preferred_element_type=F32)
                kz = (kr * zeta_ref[h]).astype(BF16)
                kv = lax.dot_general(kz, v, (((0,), (0,)), ((), ())),
                                     preferred_element_type=F32)
                state_ref[h] = state * gch_ref[h] + kv
                y = _rms(inner + cross)
                g = g_ref[rows, vcols].astype(F32)
                o_ref[rows, vcols] = (g * jax.nn.sigmoid(g) * y).astype(o_ref.dtype)


def _retention_tables(seq):
    dk, c, nh = RET_KEY_DIM, RET_CHUNK, RET_HEADS
    pos = jnp.arange(seq, dtype=F32)
    inv_freq = 1.0 / (10000.0 ** jnp.linspace(0.0, 1.0, dk // 2, dtype=F32))
    ang = pos[:, None] * jnp.repeat(inv_freq, 2)[None, :]
    sin = jnp.tile(jnp.sin(ang), (1, nh))
    cos = jnp.tile(jnp.cos(ang), (1, nh))
    even = (jnp.arange(nh * dk) % 2 == 0)[None, :]
    sin_even = jnp.where(even, -sin, 0.0)
    sin_odd = jnp.where(even, 0.0, sin)
    log_g = jnp.log(1.0 - 2.0 ** (-5.0 - jnp.arange(nh, dtype=F32)))
    i = jnp.arange(c, dtype=F32)
    rel = i[:, None] - i[None, :]
    dmask = jnp.where(rel[None] >= 0,
                      jnp.exp(jnp.maximum(rel, 0.0)[None] * log_g[:, None, None]), 0.0)
    zeta = jnp.exp((c - 1.0 - i)[None, :] * log_g[:, None])
    xi = jnp.exp((i + 1.0)[None, :] * log_g[:, None])
    g_chunk = jnp.exp(c * log_g)
    lane = jnp.arange(LANES)
    hmask = jnp.stack([((lane // dk) == (h % 2)).astype(F32) for h in range(nh)])[:, None, :]
    xi_t = xi[:, :, None] * hmask
    zeta_t = zeta[:, :, None] * hmask
    gch = jnp.broadcast_to(g_chunk[:, None, None], (nh, 1, LANES))
    return cos, sin_even, sin_odd, hmask, xi_t, zeta_t, dmask, gch


def _retention(q, k, v, g, batch, seq):
    t = q.shape[0]
    rb = min(RET_ROWS, seq)
    per_batch = seq // rb
    cos, sin_even, sin_odd, hmask, xi_t, zeta_t, dmask, gch = _retention_tables(seq)
    row = lambda w: pl.BlockSpec((rb, w), lambda b, j: (b * per_batch + j, 0))
    tab = lambda w: pl.BlockSpec((rb, w), lambda b, j: (j, 0))
    full = lambda a: pl.BlockSpec(a.shape, lambda b, j: (0,) * a.ndim)
    return pl.pallas_call(
        _ret_kernel,
        grid=(batch, per_batch),
        in_specs=[row(RET_QK_W), row(RET_QK_W), row(RET_V_W), row(RET_V_W),
                  tab(RET_QK_W), tab(RET_QK_W), tab(RET_QK_W),
                  full(hmask), full(xi_t), full(zeta_t), full(dmask), full(gch)],
        out_specs=row(RET_V_W),
        out_shape=jax.ShapeDtypeStruct((t, RET_V_W), BF16),
        scratch_shapes=[pltpu.VMEM((RET_HEADS, LANES, RET_VAL_DIM), F32)],
        compiler_params=pltpu.CompilerParams(
            dimension_semantics=("arbitrary", "arbitrary"), vmem_limit_bytes=VMEM_LIMIT),
        name="retention",
    )(q, k, v, g, cos, sin_even, sin_odd, hmask, xi_t, zeta_t, dmask, gch)


def _t5_bucket(rel):
    n = jnp.maximum(rel, 0)
    max_exact = REL_BUCKETS // 2
    nf = jnp.maximum(n, 1).astype(F32)
    large = max_exact + (jnp.log(nf / max_exact) / math.log(REL_MAX_DIST / max_exact)
                         * (REL_BUCKETS - max_exact)).astype(jnp.int32)
    large = jnp.minimum(large, REL_BUCKETS - 1)
    return jnp.where(n < max_exact, n, large)


def _bias_tiles(rel_bias, blk):
    r = jnp.arange(blk, dtype=jnp.int32)
    far = rel_bias[REL_BUCKETS - 1]
    rel0 = r[None, :] - r[:, None]
    rel1 = rel0 + blk
    buckets = jnp.arange(REL_BUCKETS, dtype=jnp.int32)

    def tile(rel):
        hot = (_t5_bucket(rel)[:, :, None] == buckets).astype(F32)
        return jnp.einsum('krb,bh->hkr', hot, rel_bias, precision=lax.Precision.HIGHEST)

    b0 = jnp.where(rel0[None] >= 0, (tile(rel0) - far[:, None, None]) * LOG2_E, NEG_BIG)
    b1 = (tile(rel1) - far[:, None, None]) * LOG2_E
    return b0, b1


def _attn_kernel(q_ref, k_ref, vt_ref, b0_ref, b1_ref, lq1_ref, lk1_ref, lq2_ref, lk2_ref,
                 sw_ref, o_ref, m_ref, acc_ref):
    blk = q_ref.shape[0]
    i = pl.program_id(2)
    lane = lax.broadcasted_iota(jnp.int32, (1, LANES), 1)
    q = (q_ref[...].astype(F32) * (DIFF_HEAD_DIM ** -0.5 * LOG2_E)).astype(BF16)
    zero = jnp.zeros_like(q)
    qm = (jnp.where(lane < DIFF_HEAD_DIM, q, zero), jnp.where(lane >= DIFF_HEAD_DIM, q, zero))

    m_ref[...] = jnp.full_like(m_ref, NEG_BIG)
    acc_ref[...] = jnp.zeros_like(acc_ref)

    def step(blocks):
        kbs = [k_ref[pl.ds(pl.multiple_of(j * blk, blk), blk), :] for j, _ in blocks]
        vts = [vt_ref[0, 0, j] for j, _ in blocks]
        chains = [(mi, slice(qs * ATT_STRIP, (qs + 1) * ATT_STRIP))
                  for mi in range(2) for qs in range(blk // ATT_STRIP)]
        scores = []
        for mi, qc in chains:
            row = []
            for (_, bias), kb in zip(blocks, kbs):
                s = lax.dot_general(kb, qm[mi][qc, :], (((1,), (1,)), ((), ())),
                                    preferred_element_type=F32)
                row.append(s if bias is None else s + bias[:, qc])
            scores.append(row)
        stats = []
        for (mi, qc), row in zip(chains, scores):
            m_old = m_ref[mi, :, qc]
            m_new = m_old
            for s in row:
                m_new = jnp.maximum(m_new, jnp.max(s, axis=0, keepdims=True))
            stats.append((jnp.exp2(m_old - m_new), [jnp.exp2(s - m_new).astype(BF16) for s in row], m_new))
        for (mi, qc), (alpha, ps, m_new) in zip(chains, stats):
            pv = jnp.dot(vts[0], ps[0], preferred_element_type=F32)
            for vt, p in zip(vts[1:], ps[1:]):
                pv = pv + jnp.dot(vt, p, preferred_element_type=F32)
            acc_ref[mi, :, qc] = alpha * acc_ref[mi, :, qc] + pv
            m_ref[mi, :, qc] = m_new

    n_far = jnp.maximum(i - 1, 0)

    def far_pair(pair, carry):
        step([(2 * pair, None), (2 * pair + 1, None)])
        return carry

    lax.fori_loop(0, lax.shift_right_logical(n_far, 1), far_pair, 0)

    @pl.when(lax.rem(n_far, 2) == 1)
    def _():
        step([(n_far - 1, None)])

    @pl.when(i >= 1)
    def _():
        step([(i - 1, b1_ref[0]), (i, b0_ref[0])])

    @pl.when(i == 0)
    def _():
        step([(i, b0_ref[0])])

    lam = (jnp.exp(jnp.sum(lq1_ref[...] * lk1_ref[...], axis=-1, keepdims=True))
           - jnp.exp(jnp.sum(lq2_ref[...] * lk2_ref[...], axis=-1, keepdims=True))
           + LAMBDA_INIT)
    hw = 2 * DIFF_HEAD_DIM
    a = (acc_ref[0, :hw, :] / acc_ref[0, hw:hw + 1, :]
         - lam * (acc_ref[1, :hw, :] / acc_ref[1, hw:hw + 1, :]))
    a = a * lax.rsqrt(jnp.mean(a * a, axis=0, keepdims=True) + NORM_EPS)
    o_ref[...] = (a.T * sw_ref[...] * (1.0 - LAMBDA_INIT)).astype(o_ref.dtype)


def _diff_attention(q, k, v, rel_bias, lq1, lk1, lq2, lk2, subln_w, batch, seq):
    t = q.shape[0]
    blk = min(ATT_BLOCK, seq)
    nq = seq // blk
    hw = 2 * DIFF_HEAD_DIM
    b0, b1 = _bias_tiles(rel_bias, blk)
    vt = jnp.transpose(v.reshape(batch, nq, blk, DIFF_HEADS, hw), (0, 3, 1, 4, 2))
    ones = jnp.zeros((batch, DIFF_HEADS, nq, ONES_ROWS, blk), vt.dtype).at[:, :, :, 0, :].set(1.0)
    vt = jnp.concatenate([vt, ones], axis=3)
    small = lambda a: pl.BlockSpec(a.shape, lambda b, h, i: (0,) * a.ndim)
    return pl.pallas_call(
        _attn_kernel,
        grid=(batch, DIFF_HEADS, nq),
        in_specs=[pl.BlockSpec((blk, hw), lambda b, h, i: (b * nq + i, h)),
                  pl.BlockSpec((seq, hw), lambda b, h, i: (b, h)),
                  pl.BlockSpec((1, 1, nq, hw + ONES_ROWS, blk), lambda b, h, i: (b, h, 0, 0, 0)),
                  pl.BlockSpec((1, blk, blk), lambda b, h, i: (h, 0, 0)),
                  pl.BlockSpec((1, blk, blk), lambda b, h, i: (h, 0, 0)),
                  small(lq1), small(lk1), small(lq2), small(lk2), small(subln_w)],
        out_specs=pl.BlockSpec((blk, hw), lambda b, h, i: (b * nq + i, h)),
        out_shape=jax.ShapeDtypeStruct((t, DIFF_V_W), BF16),
        scratch_shapes=[pltpu.VMEM((2, 1, blk), F32), pltpu.VMEM((2, hw + ONES_ROWS, blk), F32)],
        compiler_params=pltpu.CompilerParams(
            dimension_semantics=("arbitrary", "arbitrary", "arbitrary"),
            vmem_limit_bytes=VMEM_LIMIT),
        name="diff_attn",
    )(q, k, vt, b0, b1, lq1, lk1, lq2, lk2, subln_w)


def _out_kernel(yr_ref, yd_ref, x_ref, mod_ref, nw_ref, wo_ref, wr_ref, br_ref,
                x1_ref, hp_ref, meta_ref, gate_ref, cnt_ref, run_ref):
    tm = x_ref.shape[0]

    @pl.when(pl.program_id(0) == 0)
    def _():
        run_ref[...] = jnp.zeros_like(run_ref)

    mixed = (jnp.dot(yr_ref[...], wo_ref[0:RET_V_W, :], preferred_element_type=F32)
             + jnp.dot(yd_ref[...], wo_ref[RET_V_W:, :], preferred_element_type=F32))
    gate1 = mod_ref[0, 2:3, :]
    shift2 = mod_ref[0, 3:4, :]
    scale2 = mod_ref[0, 4:5, :]
    x1 = x_ref[...] + gate1 * mixed
    x1_ref[...] = x1
    h2 = (_rms(x1) * nw_ref[...]) * (1.0 + scale2) + shift2
    hp_ref[...] = _pack_halves(h2)

    logits = jnp.dot(h2, wr_ref[...], precision=lax.Precision.HIGHEST,
                     preferred_element_type=F32) + br_ref[...]
    lane = lax.broadcasted_iota(jnp.int32, logits.shape, 1)
    work = logits
    vals, idxs, hots = [], [], []
    for _ in range(TOP_K):
        mx = jnp.max(work, axis=-1, keepdims=True)
        idx = jnp.min(jnp.where(work == mx, lane, N_EXPERTS), axis=-1, keepdims=True)
        hot = lane == idx
        vals.append(mx)
        idxs.append(idx)
        hots.append(hot)
        work = jnp.where(hot, -jnp.inf, work)
    exps = [jnp.exp(v - vals[0]) for v in vals]
    denom = exps[0] + exps[1] + exps[2] + exps[3]

    sel = jnp.zeros(logits.shape, F32)
    for hot in hots:
        sel = sel + hot.astype(F32)
    r = lax.broadcasted_iota(jnp.int32, (tm, tm), 0)
    c = lax.broadcasted_iota(jnp.int32, (tm, tm), 1)
    before = (c < r).astype(BF16)
    prefix = jnp.dot(before, sel.astype(BF16), preferred_element_type=F32) + run_ref[...]
    ranks = [jnp.sum(jnp.where(hot, prefix, 0.0), axis=-1, keepdims=True) for hot in hots]
    run_ref[...] = run_ref[...] + jnp.sum(sel, axis=0, keepdims=True)
    cnt_ref[...] = run_ref[...].astype(jnp.int32)

    lane8 = lax.broadcasted_iota(jnp.int32, (tm, 2 * TOP_K), 1)
    meta = jnp.zeros((tm, 2 * TOP_K), jnp.int32)
    gates = jnp.zeros((tm, 2 * TOP_K), F32)
    for kk in range(TOP_K):
        meta = jnp.where(lane8 == kk, idxs[kk], meta)
        meta = jnp.where(lane8 == TOP_K + kk, ranks[kk].astype(jnp.int32), meta)
        gates = jnp.where(lane8 == kk, exps[kk] / denom, gates)
    meta_ref[...] = meta
    gate_ref[...] = gates


def _out_router(y_r, y_d, x2, mod, norm_w, w_out_bf16, w_router, b_router, seq):
    t, d = x2.shape
    tm = min(ROW_TILE, seq)
    per_batch = seq // tm
    ne = w_router.shape[1]
    row = lambda w: pl.BlockSpec((tm, w), lambda i: (i, 0))
    const = lambda a: pl.BlockSpec(a.shape, lambda i: (0,) * a.ndim)
    return pl.pallas_call(
        _out_kernel,
        grid=(t // tm,),
        in_specs=[row(RET_V_W), row(DIFF_V_W), row(d),
                  pl.BlockSpec((1, 6, d), lambda i: (i // per_batch, 0, 0)),
                  const(norm_w), const(w_out_bf16), const(w_router), const(b_router)],
        out_specs=[row(d), row(d // 2), row(2 * TOP_K), row(2 * TOP_K),
                   pl.BlockSpec((1, ne), lambda i: (0, 0))],
        out_shape=[jax.ShapeDtypeStruct((t, d), F32),
                   jax.ShapeDtypeStruct((t, d // 2), jnp.uint32),
                   jax.ShapeDtypeStruct((t, 2 * TOP_K), jnp.int32),
                   jax.ShapeDtypeStruct((t, 2 * TOP_K), F32),
                   jax.ShapeDtypeStruct((1, ne), jnp.int32)],
        scratch_shapes=[pltpu.VMEM((1, ne), F32)],
        compiler_params=pltpu.CompilerParams(
            dimension_semantics=("arbitrary",), vmem_limit_bytes=VMEM_LIMIT),
        name="out_router",
    )(y_r, y_d, x2, mod, norm_w, w_out_bf16, w_router, b_router)


def _w1_prep_kernel(w_ref, p_ref, o_ref):
    for s in range(w_ref.shape[2] // PERM_TILE):
        cols = slice(s * PERM_TILE, (s + 1) * PERM_TILE)
        o_ref[0, :, cols] = jnp.dot(w_ref[0, :, cols].astype(BF16), p_ref[...],
                                    preferred_element_type=F32).astype(BF16)


def _pair_split_matrix():
    i = jnp.arange(PERM_TILE)[:, None]
    j = jnp.arange(PERM_TILE)[None, :]
    half = PERM_TILE // 2
    src = jnp.where(j < half, 2 * j, 2 * (j - half) + 1)
    return (i == src).astype(BF16)


def _w1_prep(w1):
    e, d, f2 = w1.shape
    tn = 1024
    return pl.pallas_call(
        _w1_prep_kernel,
        grid=(e, f2 // tn),
        in_specs=[pl.BlockSpec((1, d, tn), lambda i, j: (i, 0, j)),
                  pl.BlockSpec((PERM_TILE, PERM_TILE), lambda i, j: (0, 0))],
        out_specs=pl.BlockSpec((1, d, tn), lambda i, j: (i, 0, j)),
        out_shape=jax.ShapeDtypeStruct((e, d, f2), BF16),
        compiler_params=pltpu.CompilerParams(vmem_limit_bytes=VMEM_LIMIT),
        name="w1_prep",
    )(w1, _pair_split_matrix())


def _pair_split_bias(b1):
    e, f2 = b1.shape
    nt = f2 // PERM_TILE
    g = b1[:, 0::2].reshape(e, nt, 1, PERM_TILE // 2)
    l = b1[:, 1::2].reshape(e, nt, 1, PERM_TILE // 2)
    return jnp.concatenate([g, l], axis=2).reshape(e, 1, f2)


def _slot_tok_kernel(first_slot, dest_ref, init_ref, o_ref, sem):
    init_copy = pltpu.make_async_copy(init_ref, o_ref, sem)
    init_copy.start()
    init_copy.wait()
    n_slots = dest_ref.shape[0]

    def put(i, carry):
        o_ref[dest_ref[i]] = lax.shift_right_logical(first_slot + i, 2)
        return carry

    lax.fori_loop(0, n_slots, put, 0, unroll=8)


def _slot_tokens(dest, n_rows):
    assert TOP_K == 4
    flat = dest.reshape(-1)
    per_call = flat.shape[0] // SLOT_MAP_CALLS
    slot_tok = jnp.zeros((n_rows,), jnp.int32)
    for part in range(SLOT_MAP_CALLS):
        slot_tok = pl.pallas_call(
            functools.partial(_slot_tok_kernel, part * per_call),
            in_specs=[pl.BlockSpec(memory_space=pltpu.SMEM), pl.BlockSpec(memory_space=pl.ANY)],
            out_specs=pl.BlockSpec(memory_space=pltpu.SMEM),
            out_shape=jax.ShapeDtypeStruct((n_rows,), jnp.int32),
            scratch_shapes=[pltpu.SemaphoreType.DMA(())],
            name="slot_tokens",
        )(flat[part * per_call:(part + 1) * per_call], slot_tok)
    return slot_tok


def _expert_kernel(be_ref, nv_ref, tcur_ref, tnxt_ref, hp_ref, w1_ref, b1_ref, w2_ref, b2_ref,
                   o_ref, hbuf, xb0, xb1, sem):
    b = pl.program_id(0)
    nv = nv_ref[0]
    bm = xb0.shape[0]

    def gather(tok_ref, dst):
        for r in range(bm):
            dst[pl.ds(r, 1), :] = hbuf[pl.ds(tok_ref[0, 0, r], 1), :]

    @pl.when(b == 0)
    def _():
        load = pltpu.make_async_copy(hp_ref, hbuf, sem)
        load.start()
        load.wait()
        gather(tcur_ref, xb0)

    def run(cur, nxt):
        x = _unpack_halves(cur[...])
        half = PERM_TILE // 2
        acts = []
        for j in range(w1_ref.shape[2] // PERM_TILE):
            cols = slice(j * PERM_TILE, (j + 1) * PERM_TILE)
            h = jnp.dot(x, w1_ref[0, :, cols], preferred_element_type=F32) + b1_ref[0, :, cols]
            glu = jnp.minimum(h[:, :half], SWIGLU_LIMIT)
            lin = jnp.clip(h[:, half:], -SWIGLU_LIMIT, SWIGLU_LIMIT)
            acts.append((glu * jax.nn.sigmoid(SWIGLU_ALPHA * glu) * (lin + 1.0)).astype(BF16))
        act = jnp.concatenate(acts, axis=1)
        y = jnp.dot(act, w2_ref[0], preferred_element_type=F32) + b2_ref[0]
        _store_row_tiles(o_ref, y)
        gather(tnxt_ref, nxt)

    even = lax.rem(b, 2) == 0

    @pl.when(jnp.logical_and(b < nv, even))
    def _():
        run(xb0, xb1)

    @pl.when(jnp.logical_and(b < nv, jnp.logical_not(even)))
    def _():
        run(xb1, xb0)

    @pl.when(b >= nv)
    def _():
        o_ref[...] = jnp.zeros_like(o_ref)


def _experts(block_e, n_valid, slot_tok, hp, w1p, b1p, w2, b2):
    d = D_MODEL
    f2 = w1p.shape[2]
    f = w2.shape[1]
    bm = EXPERT_ROWS
    nb = slot_tok.shape[0] // bm
    tok3 = slot_tok.reshape(nb, 1, bm)
    exp = lambda b, be, nv: be[jnp.maximum(jnp.minimum(b, nv[0] - 1), 0)]
    grid_spec = pltpu.PrefetchScalarGridSpec(
        num_scalar_prefetch=2,
        grid=(nb,),
        in_specs=[pl.BlockSpec((1, 1, bm), lambda b, be, nv: (b, 0, 0), memory_space=pltpu.SMEM),
                  pl.BlockSpec((1, 1, bm), lambda b, be, nv: (jnp.minimum(b + 1, nb - 1), 0, 0),
                               memory_space=pltpu.SMEM),
                  pl.BlockSpec(memory_space=pl.ANY),
                  pl.BlockSpec((1, d, f2), lambda b, be, nv: (exp(b, be, nv), 0, 0)),
                  pl.BlockSpec((1, 1, f2), lambda b, be, nv: (exp(b, be, nv), 0, 0)),
                  pl.BlockSpec((1, f, d), lambda b, be, nv: (exp(b, be, nv), 0, 0)),
                  pl.BlockSpec((1, 1, d), lambda b, be, nv: (exp(b, be, nv), 0, 0))],
        out_specs=pl.BlockSpec((bm * ROW_SUB, LANES), lambda b, be, nv: (b, 0)),
        scratch_shapes=[pltpu.VMEM(hp.shape, hp.dtype), pltpu.VMEM((bm, hp.shape[1]), hp.dtype),
                        pltpu.VMEM((bm, hp.shape[1]), hp.dtype), pltpu.SemaphoreType.DMA(())],
    )
    return pl.pallas_call(
        _expert_kernel,
        grid_spec=grid_spec,
        out_shape=jax.ShapeDtypeStruct((nb * bm * ROW_SUB, LANES), F32),
        compiler_params=pltpu.CompilerParams(
            dimension_semantics=("arbitrary",), vmem_limit_bytes=VMEM_LIMIT),
        name="experts",
    )(block_e, n_valid, tok3, tok3, hp, w1p, b1p, w2, b2)


def _combine_kernel(dcur_ref, dnxt_ref, gate_ref, x1_ref, mod_ref, nf_ref, y_ref, o_ref, buf, sems):
    nt = x1_ref.shape[0]
    i = pl.program_id(0)
    n = pl.num_programs(0)
    slot = lax.rem(i, 2)

    def row_copy(src_row, s, kk, r):
        return pltpu.make_async_copy(y_ref.at[pl.ds(src_row * ROW_SUB, ROW_SUB), :],
                                     buf.at[s, kk, pl.ds(r * ROW_SUB, ROW_SUB), :], sems.at[s])

    def fetch(d_ref, s):
        for j in range(nt):
            for kk in range(TOP_K):
                row_copy(d_ref[0, 0, j * TOP_K + kk], s, kk, j).start(priority=kk % DMA_QUEUES)

    def wait_all(s):
        for kk in range(TOP_K):
            pltpu.make_async_copy(y_ref.at[pl.ds(0, nt * ROW_SUB), :], buf.at[s, kk], sems.at[s]).wait()

    @pl.when(i == 0)
    def _():
        fetch(dcur_ref, 0)

    fetch(dnxt_ref, 1 - slot)
    wait_all(slot)

    g = gate_ref[...]
    parts = []
    for s in range(ROW_SUB):
        acc = buf[slot, 0, pl.ds(s, nt, stride=ROW_SUB), :] * g[:, 0:1]
        for kk in range(1, TOP_K):
            acc = acc + buf[slot, kk, pl.ds(s, nt, stride=ROW_SUB), :] * g[:, kk:kk + 1]
        parts.append(acc)
    moe = jnp.concatenate(parts, axis=1)
    gate2 = mod_ref[0, 5:6, :]
    x2 = x1_ref[...] + gate2 * moe
    o_ref[...] = _rms(x2) * nf_ref[...]

    @pl.when(i == n - 1)
    def _():
        wait_all(1 - slot)


def _combine(dest, gates, x1, mod, normf_w, y, seq):
    t, d = x1.shape
    nt = min(COMBINE_TOKENS, seq)
    steps = t // nt
    per_batch = seq // nt
    dest2 = dest.reshape(steps, 1, nt * TOP_K)
    return pl.pallas_call(
        _combine_kernel,
        grid=(steps,),
        in_specs=[pl.BlockSpec((1, 1, nt * TOP_K), lambda i: (i, 0, 0), memory_space=pltpu.SMEM),
                  pl.BlockSpec((1, 1, nt * TOP_K), lambda i: (jnp.minimum(i + 1, steps - 1), 0, 0),
                               memory_space=pltpu.SMEM),
                  pl.BlockSpec((nt, 2 * TOP_K), lambda i: (i, 0)),
                  pl.BlockSpec((nt, d), lambda i: (i, 0)),
                  pl.BlockSpec((1, 6, d), lambda i: (i // per_batch, 0, 0)),
                  pl.BlockSpec((1, d), lambda i: (0, 0)),
                  pl.BlockSpec(memory_space=pl.ANY)],
        out_specs=pl.BlockSpec((nt, d), lambda i: (i, 0)),
        out_shape=jax.ShapeDtypeStruct((t, d), F32),
        scratch_shapes=[pltpu.VMEM((2, TOP_K, nt * ROW_SUB, LANES), F32),
                        pltpu.SemaphoreType.DMA((2,))],
        compiler_params=pltpu.CompilerParams(
            dimension_semantics=("arbitrary",), vmem_limit_bytes=VMEM_LIMIT),
        name="combine",
    )(dest2, dest2, gates, x1, mod, normf_w, y)


def kernel(x, c, w_ada, b_ada, norm1_w, w_in, lam_q1, lam_k1, lam_q2, lam_k2, subln_w, rel_bias,
           w_out, norm2_w, w_router, b_router, w1, b1, w2, b2, normf_w):
    batch, seq, d = x.shape
    t = batch * seq
    x2 = x.reshape(t, d)

    c_pad = jnp.zeros((8, d), F32).at[:batch].set(c)
    mod = _ada(c_pad, w_ada[0], b_ada[0][None, :])[:batch].reshape(batch, 6, d)

    q_r, k_r, v_r, g_r, q_d, k_d, v_d = _in_proj(x2, mod, norm1_w[0][None, :],
                                                 w_in[0].astype(BF16), seq)
    y_r = _retention(q_r, k_r, v_r, g_r, batch, seq)
    y_d = _diff_attention(q_d, k_d, v_d, rel_bias, lam_q1, lam_k1, lam_q2, lam_k2,
                          subln_w, batch, seq)

    x1, hp, meta, gates, counts = _out_router(y_r, y_d, x2, mod, norm2_w[0][None, :],
                                              w_out[0].astype(BF16), w_router[0],
                                              b_router[0][None, :], seq)

    bm = EXPERT_ROWS
    counts = counts[0]
    padded = (counts + bm - 1) // bm * bm
    pad_end = jnp.cumsum(padded)
    pad_start = pad_end - padded
    sel_e = meta[:, :TOP_K]
    hot_e = sel_e[:, :, None] == jnp.arange(N_EXPERTS, dtype=jnp.int32)[None, None, :]
    dest = jnp.sum(jnp.where(hot_e, pad_start[None, None, :], 0), axis=-1) + meta[:, TOP_K:]
    n_rows = (t * TOP_K // bm + N_EXPERTS) * bm
    nb = n_rows // bm
    block_start = jnp.arange(nb, dtype=jnp.int32) * bm
    block_e = jnp.minimum(jnp.sum((pad_end[None, :] <= block_start[:, None]).astype(jnp.int32), axis=1),
                          N_EXPERTS - 1)
    n_valid = (pad_end[-1:] // bm).astype(jnp.int32)
    slot_tok = _slot_tokens(dest, n_rows)

    ys = _experts(block_e, n_valid, slot_tok, hp, _w1_prep(w1[0]), _pair_split_bias(b1[0]),
                  w2[0].astype(BF16), b2[0][:, None, :])
    out = _combine(dest, gates, x1, mod, normf_w[None, :], ys, seq)
    return out.reshape(batch, seq, d)
```

```python
import functools
import math

import jax
import jax.numpy as jnp
from jax import lax
from jax.experimental import pallas as pl
from jax.experimental.pallas import tpu as pltpu

F32 = jnp.float32
BF16 = jnp.bfloat16

D_MODEL = 1024
RET_HEADS = 4
RET_KEY_DIM = 64
RET_VAL_DIM = 128
RET_QK_W = RET_HEADS * RET_KEY_DIM
RET_V_W = RET_HEADS * RET_VAL_DIM
RET_CHUNK = 128
DIFF_HEADS = 4
DIFF_HEAD_DIM = 64
DIFF_QK_W = DIFF_HEADS * 2 * DIFF_HEAD_DIM
DIFF_V_W = DIFF_HEADS * 2 * DIFF_HEAD_DIM
IN_SIZES = (RET_QK_W, RET_QK_W, RET_V_W, RET_V_W, DIFF_QK_W, DIFF_QK_W, DIFF_V_W)
REL_BUCKETS = 32
REL_MAX_DIST = 128
N_EXPERTS = 32
TOP_K = 4
SWIGLU_ALPHA = 1.702
SWIGLU_LIMIT = 7.0
NORM_EPS = 1e-6
LAMBDA_INIT = 0.8 - 0.6 * math.exp(-0.3 * 0)

LANES = 128
ROW_SUB = D_MODEL // LANES
PACK_SUB = ROW_SUB // 2
NEG_BIG = -1e30
LOG2_E = math.log2(math.e)
ONES_ROWS = 16
VMEM_LIMIT = 56 * 1024 * 1024

ROW_TILE = 512
RET_ROWS = 512
ATT_BLOCK = 512
ATT_STRIP = 256
EXPERT_ROWS = 256
PERM_TILE = 256
COMBINE_TOKENS = 128
DMA_QUEUES = 2
SLOT_MAP_CALLS = 2


def _rms(x):
    return x * lax.rsqrt(jnp.mean(x * x, axis=-1, keepdims=True) + NORM_EPS)


def _store_row_tiles(ref, x, sub=ROW_SUB):
    rows = x.shape[0]
    for s in range(sub):
        ref[pl.ds(s, rows, stride=sub), :] = x[:, s * LANES:(s + 1) * LANES]


def _load_row_tiles(ref, rows, sub=ROW_SUB):
    return jnp.concatenate([ref[pl.ds(s, rows, stride=sub), :] for s in range(sub)], axis=1)


def _pack_halves(x):
    half = x.shape[1] // 2
    lo = lax.bitcast_convert_type(x[:, :half].astype(BF16).astype(F32), jnp.uint32)
    hi = lax.bitcast_convert_type(x[:, half:].astype(BF16).astype(F32), jnp.uint32)
    return (hi & jnp.uint32(0xFFFF0000)) | (lo >> 16)


def _unpack_halves(w):
    lo = lax.bitcast_convert_type(w << 16, F32).astype(BF16)
    hi = lax.bitcast_convert_type(w & jnp.uint32(0xFFFF0000), F32).astype(BF16)
    return jnp.concatenate([lo, hi], axis=1)


def _ada_kernel(c_ref, w_ref, b_ref, o_ref):
    c = c_ref[...]
    cond = c * jax.nn.sigmoid(c)
    o_ref[...] = jnp.dot(cond, w_ref[...], precision=lax.Precision.HIGHEST,
                         preferred_element_type=F32) + b_ref[...]


def _ada(c_pad, w_ada, b_ada):
    rows, d = c_pad.shape
    n = w_ada.shape[1]
    tn = 1024
    return pl.pallas_call(
        _ada_kernel,
        grid=(n // tn,),
        in_specs=[pl.BlockSpec((rows, d), lambda j: (0, 0)),
                  pl.BlockSpec((d, tn), lambda j: (0, j)),
                  pl.BlockSpec((1, tn), lambda j: (0, j))],
        out_specs=pl.BlockSpec((rows, tn), lambda j: (0, j)),
        out_shape=jax.ShapeDtypeStruct((rows, n), F32),
        name="ada",
    )(c_pad, w_ada, b_ada)


def _in_proj_kernel(x_ref, mod_ref, nw_ref, w_ref, *o_refs):
    x = x_ref[...]
    shift = mod_ref[0, 0:1, :]
    scale = mod_ref[0, 1:2, :]
    h = (_rms(x) * nw_ref[...]) * (1.0 + scale) + shift
    hb = h.astype(BF16)
    off = 0
    for o_ref, width in zip(o_refs, IN_SIZES):
        o_ref[...] = jnp.dot(hb, w_ref[:, off:off + width],
                             preferred_element_type=F32).astype(o_ref.dtype)
        off += width


def _in_proj(x2, mod, norm_w, w_in_bf16, seq):
    t, d = x2.shape
    tm = min(ROW_TILE, seq)
    per_batch = seq // tm
    in_w = w_in_bf16.shape[1]
    return pl.pallas_call(
        _in_proj_kernel,
        grid=(t // tm,),
        in_specs=[pl.BlockSpec((tm, d), lambda i: (i, 0)),
                  pl.BlockSpec((1, 6, d), lambda i: (i // per_batch, 0, 0)),
                  pl.BlockSpec((1, d), lambda i: (0, 0)),
                  pl.BlockSpec((d, in_w), lambda i: (0, 0))],
        out_specs=[pl.BlockSpec((tm, w), lambda i: (i, 0)) for w in IN_SIZES],
        out_shape=[jax.ShapeDtypeStruct((t, w), BF16) for w in IN_SIZES],
        compiler_params=pltpu.CompilerParams(vmem_limit_bytes=VMEM_LIMIT),
        name="in_proj",
    )(x2, mod, norm_w, w_in_bf16)


def _rotary(x, cos, sin_even, sin_odd):
    nxt = pltpu.roll(x, LANES - 1, 1)
    prv = pltpu.roll(x, 1, 1)
    return x * cos + nxt * sin_even + prv * sin_odd


def _ret_kernel(q_ref, k_ref, v_ref, g_ref, cos_ref, sine_ref, sino_ref,
                hmask_ref, xi_ref, zeta_ref, dmask_ref, gch_ref, o_ref, state_ref):
    @pl.when(pl.program_id(1) == 0)
    def _():
        state_ref[...] = jnp.zeros_like(state_ref)

    n_sub = q_ref.shape[0] // RET_CHUNK
    for c in range(n_sub):
        rows = slice(c * RET_CHUNK, (c + 1) * RET_CHUNK)
        for pair in range(RET_HEADS // 2):
            lanes = slice(pair * LANES, (pair + 1) * LANES)
            cos = cos_ref[rows, lanes]
            sine = sine_ref[rows, lanes]
            sino = sino_ref[rows, lanes]
            qr = _rotary(q_ref[rows, lanes].astype(F32), cos, sine, sino)
            kr = _rotary(k_ref[rows, lanes].astype(F32), cos, sine, sino) * (RET_KEY_DIM ** -0.5)
            qb = qr.astype(BF16)
            for hh in range(2):
                h = 2 * pair + hh
                vcols = slice(h * RET_VAL_DIM, (h + 1) * RET_VAL_DIM)
                v = v_ref[rows, vcols]
                km = (kr * hmask_ref[h]).astype(BF16)
                scores = lax.dot_general(qb, km, (((1,), (1,)), ((), ())),
                                         preferred_element_type=F32) * dmask_ref[h]
                inner = jnp.dot(scores.astype(BF16), v, preferred_element_type=F32)
                qx = (qr * xi_ref[h]).astype(BF16)
                state = state_ref[h]
                cross = jnp.dot(qx, state.astype(BF16), preferred_element_type=F32)
                kz = (kr * zeta_ref[h]).astype(BF16)
                kv = lax.dot_general(kz, v, (((0,), (0,)), ((), ())),
                                     preferred_element_type=F32)
                state_ref[h] = state * gch_ref[h] + kv
                y = _rms(inner + cross)
                g = g_ref[rows, vcols].astype(F32)
                o_ref[rows, vcols] = (g * jax.nn.sigmoid(g) * y).astype(o_ref.dtype)


def _retention_tables(seq):
    dk, c, nh = RET_KEY_DIM, RET_CHUNK, RET_HEADS
    pos = jnp.arange(seq, dtype=F32)
    inv_freq = 1.0 / (10000.0 ** jnp.linspace(0.0, 1.0, dk // 2, dtype=F32))
    ang = pos[:, None] * jnp.repeat(inv_freq, 2)[None, :]
    sin = jnp.tile(jnp.sin(ang), (1, nh))
    cos = jnp.tile(jnp.cos(ang), (1, nh))
    even = (jnp.arange(nh * dk) % 2 == 0)[None, :]
    sin_even = jnp.where(even, -sin, 0.0)
    sin_odd = jnp.where(even, 0.0, sin)
    log_g = jnp.log(1.0 - 2.0 ** (-5.0 - jnp.arange(nh, dtype=F32)))
    i = jnp.arange(c, dtype=F32)
    rel = i[:, None] - i[None, :]
    dmask = jnp.where(rel[None] >= 0,
                      jnp.exp(jnp.maximum(rel, 0.0)[None] * log_g[:, None, None]), 0.0)
    zeta = jnp.exp((c - 1.0 - i)[None, :] * log_g[:, None])
    xi = jnp.exp((i + 1.0)[None, :] * log_g[:, None])
    g_chunk = jnp.exp(c * log_g)
    lane = jnp.arange(LANES)
    hmask = jnp.stack([((lane // dk) == (h % 2)).astype(F32) for h in range(nh)])[:, None, :]
    xi_t = xi[:, :, None] * hmask
    zeta_t = zeta[:, :, None] * hmask
    gch = jnp.broadcast_to(g_chunk[:, None, None], (nh, 1, LANES))
    return cos, sin_even, sin_odd, hmask, xi_t, zeta_t, dmask, gch


def _retention(q, k, v, g, batch, seq):
    t = q.shape[0]
    rb = min(RET_ROWS, seq)
    per_batch = seq // rb
    cos, sin_even, sin_odd, hmask, xi_t, zeta_t, dmask, gch = _retention_tables(seq)
    row = lambda w: pl.BlockSpec((rb, w), lambda b, j: (b * per_batch + j, 0))
    tab = lambda w: pl.BlockSpec((rb, w), lambda b, j: (j, 0))
    full = lambda a: pl.BlockSpec(a.shape, lambda b, j: (0,) * a.ndim)
    return pl.pallas_call(
        _ret_kernel,
        grid=(batch, per_batch),
        in_specs=[row(RET_QK_W), row(RET_QK_W), row(RET_V_W), row(RET_V_W),
                  tab(RET_QK_W), tab(RET_QK_W), tab(RET_QK_W),
                  full(hmask), full(xi_t), full(zeta_t), full(dmask), full(gch)],
        out_specs=row(RET_V_W),
        out_shape=jax.ShapeDtypeStruct((t, RET_V_W), BF16),
        scratch_shapes=[pltpu.VMEM((RET_HEADS, LANES, RET_VAL_DIM), F32)],
        compiler_params=pltpu.CompilerParams(
            dimension_semantics=("arbitrary", "arbitrary"), vmem_limit_bytes=VMEM_LIMIT),
        name="retention",
    )(q, k, v, g, cos, sin_even, sin_odd, hmask, xi_t, zeta_t, dmask, gch)


def _t5_bucket(rel):
    n = jnp.maximum(rel, 0)
    max_exact = REL_BUCKETS // 2
    nf = jnp.maximum(n, 1).astype(F32)
    large = max_exact + (jnp.log(nf / max_exact) / math.log(REL_MAX_DIST / max_exact)
                         * (REL_BUCKETS - max_exact)).astype(jnp.int32)
    large = jnp.minimum(large, REL_BUCKETS - 1)
    return jnp.where(n < max_exact, n, large)


def _bias_tiles(rel_bias, blk):
    r = jnp.arange(blk, dtype=jnp.int32)
    far = rel_bias[REL_BUCKETS - 1]
    rel0 = r[None, :] - r[:, None]
    rel1 = rel0 + blk
    buckets = jnp.arange(REL_BUCKETS, dtype=jnp.int32)

    def tile(rel):
        hot = (_t5_bucket(rel)[:, :, None] == buckets).astype(F32)
        return jnp.einsum('krb,bh->hkr', hot, rel_bias, precision=lax.Precision.HIGHEST)

    b0 = jnp.where(rel0[None] >= 0, (tile(rel0) - far[:, None, None]) * LOG2_E, NEG_BIG)
    b1 = (tile(rel1) - far[:, None, None]) * LOG2_E
    return b0, b1


def _attn_kernel(q_ref, k_ref, vt_ref, b0_ref, b1_ref, lq1_ref, lk1_ref, lq2_ref, lk2_ref,
                 sw_ref, o_ref, m_ref, acc_ref):
    blk = q_ref.shape[0]
    i = pl.program_id(2)
    lane = lax.broadcasted_iota(jnp.int32, (1, LANES), 1)
    q = (q_ref[...].astype(F32) * (DIFF_HEAD_DIM ** -0.5 * LOG2_E)).astype(BF16)
    zero = jnp.zeros_like(q)
    qm = (jnp.where(lane < DIFF_HEAD_DIM, q, zero), jnp.where(lane >= DIFF_HEAD_DIM, q, zero))

    m_ref[...] = jnp.full_like(m_ref, NEG_BIG)
    acc_ref[...] = jnp.zeros_like(acc_ref)

    def step(blocks):
        kbs = [k_ref[pl.ds(pl.multiple_of(j * blk, blk), blk), :] for j, _ in blocks]
        vts = [vt_ref[0, 0, j] for j, _ in blocks]
        chains = [(mi, slice(qs * ATT_STRIP, (qs + 1) * ATT_STRIP))
                  for mi in range(2) for qs in range(blk // ATT_STRIP)]
        scores = []
        for mi, qc in chains:
            row = []
            for (_, bias), kb in zip(blocks, kbs):
                s = lax.dot_general(kb, qm[mi][qc, :], (((1,), (1,)), ((), ())),
                                    preferred_element_type=F32)
                row.append(s if bias is None else s + bias[:, qc])
            scores.append(row)
        stats = []
        for (mi, qc), row in zip(chains, scores):
            m_old = m_ref[mi, :, qc]
            m_new = m_old
            for s in row:
                m_new = jnp.maximum(m_new, jnp.max(s, axis=0, keepdims=True))
            stats.append((jnp.exp2(m_old - m_new), [jnp.exp2(s - m_new).astype(BF16) for s in row], m_new))
        for (mi, qc), (alpha, ps, m_new) in zip(chains, stats):
            pv = jnp.dot(vts[0], ps[0], preferred_element_type=F32)
            for vt, p in zip(vts[1:], ps[1:]):
                pv = pv + jnp.dot(vt, p, preferred_element_type=F32)
            acc_ref[mi, :, qc] = alpha * acc_ref[mi, :, qc] + pv
            m_ref[mi, :, qc] = m_new

    n_far = jnp.maximum(i - 1, 0)

    def far_pair(pair, carry):
        step([(2 * pair, None), (2 * pair + 1, None)])
        return carry

    lax.fori_loop(0, lax.shift_right_logical(n_far, 1), far_pair, 0)

    @pl.when(lax.rem(n_far, 2) == 1)
    def _():
        step([(n_far - 1, None)])

    @pl.when(i >= 1)
    def _():
        step([(i - 1, b1_ref[0]), (i, b0_ref[0])])

    @pl.when(i == 0)
    def _():
        step([(i, b0_ref[0])])

    lam = (jnp.exp(jnp.sum(lq1_ref[...] * lk1_ref[...], axis=-1, keepdims=True))
           - jnp.exp(jnp.sum(lq2_ref[...] * lk2_ref[...], axis=-1, keepdims=True))
           + LAMBDA_INIT)
    hw = 2 * DIFF_HEAD_DIM
    a = (acc_ref[0, :hw, :] / acc_ref[0, hw:hw + 1, :]
         - lam * (acc_ref[1, :hw, :] / acc_ref[1, hw:hw + 1, :]))
    a = a * lax.rsqrt(jnp.mean(a * a, axis=0, keepdims=True) + NORM_EPS)
    o_ref[...] = (a.T * sw_ref[...] * (1.0 - LAMBDA_INIT)).astype(o_ref.dtype)


def _diff_attention(q, k, v, rel_bias, lq1, lk1, lq2, lk2, subln_w, batch, seq):
    t = q.shape[0]
    blk = min(ATT_BLOCK, seq)
    nq = seq // blk
    hw = 2 * DIFF_HEAD_DIM
    b0, b1 = _bias_tiles(rel_bias, blk)
    vt = jnp.transpose(v.reshape(batch, nq, blk, DIFF_HEADS, hw), (0, 3, 1, 4, 2))
    ones = jnp.zeros((batch, DIFF_HEADS, nq, ONES_ROWS, blk), vt.dtype).at[:, :, :, 0, :].set(1.0)
    vt = jnp.concatenate([vt, ones], axis=3)
    small = lambda a: pl.BlockSpec(a.shape, lambda b, h, i: (0,) * a.ndim)
    return pl.pallas_call(
        _attn_kernel,
        grid=(batch, DIFF_HEADS, nq),
        in_specs=[pl.BlockSpec((blk, hw), lambda b, h, i: (b * nq + i, h)),
                  pl.BlockSpec((seq, hw), lambda b, h, i: (b, h)),
                  pl.BlockSpec((1, 1, nq, hw + ONES_ROWS, blk), lambda b, h, i: (b, h, 0, 0, 0)),
                  pl.BlockSpec((1, blk, blk), lambda b, h, i: (h, 0, 0)),
                  pl.BlockSpec((1, blk, blk), lambda b, h, i: (h, 0, 0)),
                  small(lq1), small(lk1), small(lq2), small(lk2), small(subln_w)],
        out_specs=pl.BlockSpec((blk, hw), lambda b, h, i: (b * nq + i, h)),
        out_shape=jax.ShapeDtypeStruct((t, DIFF_V_W), BF16),
        scratch_shapes=[pltpu.VMEM((2, 1, blk), F32), pltpu.VMEM((2, hw + ONES_ROWS, blk), F32)],
        compiler_params=pltpu.CompilerParams(
            dimension_semantics=("arbitrary", "arbitrary", "arbitrary"),
            vmem_limit_bytes=VMEM_LIMIT),
        name="diff_attn",
    )(q, k, vt, b0, b1, lq1, lk1, lq2, lk2, subln_w)


def _out_kernel(yr_ref, yd_ref, x_ref, mod_ref, nw_ref, wo_ref, wr_ref, br_ref, upper_ref,
                x1_ref, hp_ref, meta_ref, gate_ref, cnt_ref, run_ref):
    tm = x_ref.shape[0]
    ne = run_ref.shape[0]

    @pl.when(pl.program_id(0) == 0)
    def _():
        run_ref[...] = jnp.zeros_like(run_ref)

    mixed = (jnp.dot(yr_ref[...], wo_ref[0:RET_V_W, :], preferred_element_type=F32)
             + jnp.dot(yd_ref[...], wo_ref[RET_V_W:, :], preferred_element_type=F32))
    gate1 = mod_ref[0, 2:3, :]
    shift2 = mod_ref[0, 3:4, :]
    scale2 = mod_ref[0, 4:5, :]
    x1 = x_ref[...] + gate1 * mixed
    x1_ref[...] = x1
    h2 = (_rms(x1) * nw_ref[...]) * (1.0 + scale2) + shift2
    _store_row_tiles(hp_ref, _pack_halves(h2), PACK_SUB)

    h_hi = h2.astype(BF16)
    h_lo = (h2 - h_hi.astype(F32)).astype(BF16)
    nt_dims = (((1,), (1,)), ((), ()))
    both = lax.dot_general(wr_ref[...], h_hi, nt_dims, preferred_element_type=F32)
    low = lax.dot_general(wr_ref[0:ne, :], h_lo, nt_dims, preferred_element_type=F32)
    logits = both[0:ne, :] + both[ne:, :] + low + br_ref[...]

    row = lax.broadcasted_iota(jnp.int32, logits.shape, 0)
    work = logits
    vals, idxs, hots = [], [], []
    for _ in range(TOP_K):
        mx = jnp.max(work, axis=0, keepdims=True)
        idx = jnp.min(jnp.where(work == mx, row, ne), axis=0, keepdims=True)
        hot = row == idx
        vals.append(mx)
        idxs.append(idx)
        hots.append(hot)
        work = jnp.where(hot, -jnp.inf, work)
    exps = [jnp.exp(v - vals[0]) for v in vals]
    denom = exps[0] + exps[1] + exps[2] + exps[3]

    sel = jnp.zeros(logits.shape, F32)
    for hot in hots:
        sel = sel + hot.astype(F32)
    prefix = jnp.dot(sel.astype(BF16), upper_ref[...], preferred_element_type=F32) + run_ref[...]
    ranks = [jnp.sum(jnp.where(hot, prefix, 0.0), axis=0, keepdims=True) for hot in hots]
    run_ref[...] = run_ref[...] + jnp.sum(sel, axis=1, keepdims=True)
    cnt_ref[...] = jnp.broadcast_to(run_ref[...], cnt_ref.shape).astype(jnp.int32)

    meta_ref[...] = jnp.concatenate(idxs + [r.astype(jnp.int32) for r in ranks], axis=0)
    gate_ref[...] = jnp.concatenate([e / denom for e in exps] + [jnp.zeros_like(denom)] * TOP_K, axis=0)


def _out_router(y_r, y_d, x2, mod, norm_w, w_out_bf16, w_router, b_router, seq):
    t, d = x2.shape
    tm = min(ROW_TILE, seq)
    per_batch = seq // tm
    ne = w_router.shape[1]
    w_hi = w_router.astype(BF16)
    w_lo = (w_router - w_hi.astype(F32)).astype(BF16)
    wr_t = jnp.concatenate([w_hi, w_lo], axis=1).T
    idx = jnp.arange(tm, dtype=jnp.int32)
    upper = (idx[:, None] < idx[None, :]).astype(BF16)
    row = lambda w: pl.BlockSpec((tm, w), lambda i: (i, 0))
    col = lambda h: pl.BlockSpec((h, tm), lambda i: (0, i))
    const = lambda a: pl.BlockSpec(a.shape, lambda i: (0,) * a.ndim)
    x1, hp, meta_t, gates_t, counts = pl.pallas_call(
        _out_kernel,
        grid=(t // tm,),
        in_specs=[row(RET_V_W), row(DIFF_V_W), row(d),
                  pl.BlockSpec((1, 6, d), lambda i: (i // per_batch, 0, 0)),
                  const(norm_w), const(w_out_bf16), const(wr_t), const(b_router), const(upper)],
        out_specs=[row(d), pl.BlockSpec((tm * PACK_SUB, LANES), lambda i: (i, 0)),
                   col(2 * TOP_K), col(2 * TOP_K),
                   pl.BlockSpec((ne, LANES), lambda i: (0, 0))],
        out_shape=[jax.ShapeDtypeStruct((t, d), F32),
                   jax.ShapeDtypeStruct((t * PACK_SUB, LANES), jnp.uint32),
                   jax.ShapeDtypeStruct((2 * TOP_K, t), jnp.int32),
                   jax.ShapeDtypeStruct((2 * TOP_K, t), F32),
                   jax.ShapeDtypeStruct((ne, LANES), jnp.int32)],
        scratch_shapes=[pltpu.VMEM((ne, 1), F32)],
        compiler_params=pltpu.CompilerParams(
            dimension_semantics=("arbitrary",), vmem_limit_bytes=VMEM_LIMIT),
        name="out_router",
    )(y_r, y_d, x2, mod, norm_w, w_out_bf16, wr_t, b_router, upper)
    return x1, hp, meta_t.T, gates_t.T, counts[:, 0]


def _w1_prep_kernel(w_ref, p_ref, o_ref):
    for s in range(w_ref.shape[2] // PERM_TILE):
        cols = slice(s * PERM_TILE, (s + 1) * PERM_TILE)
        o_ref[0, :, cols] = jnp.dot(w_ref[0, :, cols].astype(BF16), p_ref[...],
                                    preferred_element_type=F32).astype(BF16)


def _pair_split_matrix():
    i = jnp.arange(PERM_TILE)[:, None]
    j = jnp.arange(PERM_TILE)[None, :]
    half = PERM_TILE // 2
    src = jnp.where(j < half, 2 * j, 2 * (j - half) + 1)
    return (i == src).astype(BF16)


def _w1_prep(w1):
    e, d, f2 = w1.shape
    tn = 1024
    return pl.pallas_call(
        _w1_prep_kernel,
        grid=(e, f2 // tn),
        in_specs=[pl.BlockSpec((1, d, tn), lambda i, j: (i, 0, j)),
                  pl.BlockSpec((PERM_TILE, PERM_TILE), lambda i, j: (0, 0))],
        out_specs=pl.BlockSpec((1, d, tn), lambda i, j: (i, 0, j)),
        out_shape=jax.ShapeDtypeStruct((e, d, f2), BF16),
        compiler_params=pltpu.CompilerParams(vmem_limit_bytes=VMEM_LIMIT),
        name="w1_prep",
    )(w1, _pair_split_matrix())


def _pair_split_bias(b1):
    e, f2 = b1.shape
    nt = f2 // PERM_TILE
    g = b1[:, 0::2].reshape(e, nt, 1, PERM_TILE // 2)
    l = b1[:, 1::2].reshape(e, nt, 1, PERM_TILE // 2)
    return jnp.concatenate([g, l], axis=2).reshape(e, 1, f2)


def _slot_tok_kernel(first_slot, dest_ref, init_ref, o_ref, sem):
    init_copy = pltpu.make_async_copy(init_ref, o_ref, sem)
    init_copy.start()
    init_copy.wait()
    n_slots = dest_ref.shape[0]

    def put(i, carry):
        o_ref[dest_ref[i]] = lax.shift_right_logical(first_slot + i, 2)
        return carry

    lax.fori_loop(0, n_slots, put, 0, unroll=8)


def _slot_tokens(dest, n_rows):
    assert TOP_K == 4
    flat = dest.reshape(-1)
    per_call = flat.shape[0] // SLOT_MAP_CALLS
    slot_tok = jnp.zeros((n_rows,), jnp.int32)
    for part in range(SLOT_MAP_CALLS):
        slot_tok = pl.pallas_call(
            functools.partial(_slot_tok_kernel, part * per_call),
            in_specs=[pl.BlockSpec(memory_space=pltpu.SMEM), pl.BlockSpec(memory_space=pl.ANY)],
            out_specs=pl.BlockSpec(memory_space=pltpu.SMEM),
            out_shape=jax.ShapeDtypeStruct((n_rows,), jnp.int32),
            scratch_shapes=[pltpu.SemaphoreType.DMA(())],
            name="slot_tokens",
        )(flat[part * per_call:(part + 1) * per_call], slot_tok)
    return slot_tok


def _expert_kernel(be_ref, nv_ref, tcur_ref, tnxt_ref, hp_ref, w1_ref, b1_ref, w2_ref, b2_ref,
                   o_ref, hbuf, xb0, xb1, sem):
    b = pl.program_id(0)
    nv = nv_ref[0]
    bm = xb0.shape[0] // PACK_SUB

    def gather(tok_ref, dst):
        for r in range(bm):
            src = pl.multiple_of(tok_ref[0, 0, r] * PACK_SUB, PACK_SUB)
            dst[pl.ds(r * PACK_SUB, PACK_SUB), :] = hbuf[pl.ds(src, PACK_SUB), :]

    @pl.when(b == 0)
    def _():
        load = pltpu.make_async_copy(hp_ref, hbuf, sem)
        load.start()
        load.wait()
        gather(tcur_ref, xb0)

    def run(cur, nxt):
        x = _unpack_halves(_load_row_tiles(cur, bm, PACK_SUB))
        half = PERM_TILE // 2
        acts = []
        for j in range(w1_ref.shape[2] // PERM_TILE):
            cols = slice(j * PERM_TILE, (j + 1) * PERM_TILE)
            h = jnp.dot(x, w1_ref[0, :, cols], preferred_element_type=F32) + b1_ref[0, :, cols]
            glu = jnp.minimum(h[:, :half], SWIGLU_LIMIT)
            lin = jnp.clip(h[:, half:], -SWIGLU_LIMIT, SWIGLU_LIMIT)
            acts.append((glu * jax.nn.sigmoid(SWIGLU_ALPHA * glu) * (lin + 1.0)).astype(BF16))
        act = jnp.concatenate(acts, axis=1)
        y = jnp.dot(act, w2_ref[0], preferred_element_type=F32) + b2_ref[0]
        _store_row_tiles(o_ref, y)
        gather(tnxt_ref, nxt)

    even = lax.rem(b, 2) == 0

    @pl.when(jnp.logical_and(b < nv, even))
    def _():
        run(xb0, xb1)

    @pl.when(jnp.logical_and(b < nv, jnp.logical_not(even)))
    def _():
        run(xb1, xb0)

    @pl.when(b >= nv)
    def _():
        o_ref[...] = jnp.zeros_like(o_ref)


def _experts(block_e, n_valid, slot_tok, hp, w1p, b1p, w2, b2):
    d = D_MODEL
    f2 = w1p.shape[2]
    f = w2.shape[1]
    bm = EXPERT_ROWS
    nb = slot_tok.shape[0] // bm
    tok3 = slot_tok.reshape(nb, 1, bm)
    exp = lambda b, be, nv: be[jnp.maximum(jnp.minimum(b, nv[0] - 1), 0)]
    grid_spec = pltpu.PrefetchScalarGridSpec(
        num_scalar_prefetch=2,
        grid=(nb,),
        in_specs=[pl.BlockSpec((1, 1, bm), lambda b, be, nv: (b, 0, 0), memory_space=pltpu.SMEM),
                  pl.BlockSpec((1, 1, bm), lambda b, be, nv: (jnp.minimum(b + 1, nb - 1), 0, 0),
                               memory_space=pltpu.SMEM),
                  pl.BlockSpec(memory_space=pl.ANY),
                  pl.BlockSpec((1, d, f2), lambda b, be, nv: (exp(b, be, nv), 0, 0)),
                  pl.BlockSpec((1, 1, f2), lambda b, be, nv: (exp(b, be, nv), 0, 0)),
                  pl.BlockSpec((1, f, d), lambda b, be, nv: (exp(b, be, nv), 0, 0)),
                  pl.BlockSpec((1, 1, d), lambda b, be, nv: (exp(b, be, nv), 0, 0))],
        out_specs=pl.BlockSpec((bm * ROW_SUB, LANES), lambda b, be, nv: (b, 0)),
        scratch_shapes=[pltpu.VMEM(hp.shape, hp.dtype), pltpu.VMEM((bm * PACK_SUB, LANES), hp.dtype),
                        pltpu.VMEM((bm * PACK_SUB, LANES), hp.dtype), pltpu.SemaphoreType.DMA(())],
    )
    return pl.pallas_call(
        _expert_kernel,
        grid_spec=grid_spec,
        out_shape=jax.ShapeDtypeStruct((nb * bm * ROW_SUB, LANES), F32),
        compiler_params=pltpu.CompilerParams(
            dimension_semantics=("arbitrary",), vmem_limit_bytes=VMEM_LIMIT),
        name="experts",
    )(block_e, n_valid, tok3, tok3, hp, w1p, b1p, w2, b2)


def _combine_kernel(dcur_ref, dnxt_ref, gate_ref, x1_ref, mod_ref, nf_ref, y_ref, o_ref, buf, sems):
    nt = x1_ref.shape[0]
    i = pl.program_id(0)
    n = pl.num_programs(0)
    slot = lax.rem(i, 2)

    def row_copy(src_row, s, kk, r):
        return pltpu.make_async_copy(y_ref.at[pl.ds(src_row * ROW_SUB, ROW_SUB), :],
                                     buf.at[s, kk, pl.ds(r * ROW_SUB, ROW_SUB), :], sems.at[s])

    def fetch(d_ref, s):
        for j in range(nt):
            for kk in range(TOP_K):
                row_copy(d_ref[0, 0, j * TOP_K + kk], s, kk, j).start(priority=kk % DMA_QUEUES)

    def wait_all(s):
        for kk in range(TOP_K):
            pltpu.make_async_copy(y_ref.at[pl.ds(0, nt * ROW_SUB), :], buf.at[s, kk], sems.at[s]).wait()

    @pl.when(i == 0)
    def _():
        fetch(dcur_ref, 0)

    fetch(dnxt_ref, 1 - slot)
    wait_all(slot)

    g = gate_ref[...]
    parts = []
    for s in range(ROW_SUB):
        acc = buf[slot, 0, pl.ds(s, nt, stride=ROW_SUB), :] * g[:, 0:1]
        for kk in range(1, TOP_K):
            acc = acc + buf[slot, kk, pl.ds(s, nt, stride=ROW_SUB), :] * g[:, kk:kk + 1]
        parts.append(acc)
    moe = jnp.concatenate(parts, axis=1)
    gate2 = mod_ref[0, 5:6, :]
    x2 = x1_ref[...] + gate2 * moe
    o_ref[...] = _rms(x2) * nf_ref[...]

    @pl.when(i == n - 1)
    def _():
        wait_all(1 - slot)


def _combine(dest, gates, x1, mod, normf_w, y, seq):
    t, d = x1.shape
    nt = min(COMBINE_TOKENS, seq)
    steps = t // nt
    per_batch = seq // nt
    dest2 = dest.reshape(steps, 1, nt * TOP_K)
    return pl.pallas_call(
        _combine_kernel,
        grid=(steps,),
        in_specs=[pl.BlockSpec((1, 1, nt * TOP_K), lambda i: (i, 0, 0), memory_space=pltpu.SMEM),
                  pl.BlockSpec((1, 1, nt * TOP_K), lambda i: (jnp.minimum(i + 1, steps - 1), 0, 0),
                               memory_space=pltpu.SMEM),
                  pl.BlockSpec((nt, 2 * TOP_K), lambda i: (i, 0)),
                  pl.BlockSpec((nt, d), lambda i: (i, 0)),
                  pl.BlockSpec((1, 6, d), lambda i: (i // per_batch, 0, 0)),
                  pl.BlockSpec((1, d), lambda i: (0, 0)),
                  pl.BlockSpec(memory_space=pl.ANY)],
        out_specs=pl.BlockSpec((nt, d), lambda i: (i, 0)),
        out_shape=jax.ShapeDtypeStruct((t, d), F32),
        scratch_shapes=[pltpu.VMEM((2, TOP_K, nt * ROW_SUB, LANES), F32),
                        pltpu.SemaphoreType.DMA((2,))],
        compiler_params=pltpu.CompilerParams(
            dimension_semantics=("arbitrary",), vmem_limit_bytes=VMEM_LIMIT),
        name="combine",
    )(dest2, dest2, gates, x1, mod, normf_w, y)


def kernel(x, c, w_ada, b_ada, norm1_w, w_in, lam_q1, lam_k1, lam_q2, lam_k2, subln_w, rel_bias,
           w_out, norm2_w, w_router, b_router, w1, b1, w2, b2, normf_w):
    batch, seq, d = x.shape
    t = batch * seq
    x2 = x.reshape(t, d)

    c_pad = jnp.zeros((8, d), F32).at[:batch].set(c)
    mod = _ada(c_pad, w_ada[0], b_ada[0][None, :])[:batch].reshape(batch, 6, d)

    q_r, k_r, v_r, g_r, q_d, k_d, v_d = _in_proj(x2, mod, norm1_w[0][None, :],
                                                 w_in[0].astype(BF16), seq)
    y_r = _retention(q_r, k_r, v_r, g_r, batch, seq)
    y_d = _diff_attention(q_d, k_d, v_d, rel_bias, lam_q1, lam_k1, lam_q2, lam_k2,
                          subln_w, batch, seq)

    x1, hp, meta, gates, counts = _out_router(y_r, y_d, x2, mod, norm2_w[0][None, :],
                                              w_out[0].astype(BF16), w_router[0],
                                              b_router[0][:, None], seq)

    bm = EXPERT_ROWS
    padded = (counts + bm - 1) // bm * bm
    pad_end = jnp.cumsum(padded)
    pad_start = pad_end - padded
    sel_e = meta[:, :TOP_K]
    hot_e = sel_e[:, :, None] == jnp.arange(N_EXPERTS, dtype=jnp.int32)[None, None, :]
    dest = jnp.sum(jnp.where(hot_e, pad_start[None, None, :], 0), axis=-1) + meta[:, TOP_K:]
    n_rows = (t * TOP_K // bm + N_EXPERTS) * bm
    nb = n_rows // bm
    block_start = jnp.arange(nb, dtype=jnp.int32) * bm
    block_e = jnp.minimum(jnp.sum((pad_end[None, :] <= block_start[:, None]).astype(jnp.int32), axis=1),
                          N_EXPERTS - 1)
    n_valid = (pad_end[-1:] // bm).astype(jnp.int32)
    slot_tok = _slot_tokens(dest, n_rows)

    ys = _experts(block_e, n_valid, slot_tok, hp, _w1_prep(w1[0]), _pair_split_bias(b1[0]),
                  w2[0].astype(BF16), b2[0][:, None, :])
    out = _combine(dest, gates, x1, mod, normf_w[None, :], ys, seq)
    return out.reshape(batch, seq, d)
```

```python
import functools
import math

import jax
import jax.numpy as jnp
from jax import lax
from jax.experimental import pallas as pl
from jax.experimental.pallas import tpu as pltpu

F32 = jnp.float32
BF16 = jnp.bfloat16

D_MODEL = 1024
RET_HEADS = 4
RET_KEY_DIM = 64
RET_VAL_DIM = 128
RET_QK_W = RET_HEADS * RET_KEY_DIM
RET_V_W = RET_HEADS * RET_VAL_DIM
RET_CHUNK = 128
DIFF_HEADS = 4
DIFF_HEAD_DIM = 64
DIFF_QK_W = DIFF_HEADS * 2 * DIFF_HEAD_DIM
DIFF_V_W = DIFF_HEADS * 2 * DIFF_HEAD_DIM
IN_SIZES = (RET_QK_W, RET_QK_W, RET_V_W, RET_V_W, DIFF_QK_W, DIFF_QK_W, DIFF_V_W)
REL_BUCKETS = 32
REL_MAX_DIST = 128
N_EXPERTS = 32
TOP_K = 4
SWIGLU_ALPHA = 1.702
SWIGLU_LIMIT = 7.0
NORM_EPS = 1e-6
LAMBDA_INIT = 0.8 - 0.6 * math.exp(-0.3 * 0)

LANES = 128
ROW_SUB = D_MODEL // LANES
NEG_BIG = -1e30
LOG2_E = math.log2(math.e)
ONES_ROWS = 16
VMEM_LIMIT = 56 * 1024 * 1024

ROW_TILE = 512
RET_ROWS = 512
ATT_BLOCK = 512
ATT_STRIP = 256
EXPERT_ROWS = 256
PERM_TILE = 256
COMBINE_TOKENS = 128
DMA_QUEUES = 2
SLOT_MAP_CALLS = 2


def _rms(x):
    return x * lax.rsqrt(jnp.mean(x * x, axis=-1, keepdims=True) + NORM_EPS)


def _store_row_tiles(ref, x):
    rows = x.shape[0]
    for s in range(ROW_SUB):
        ref[pl.ds(s, rows, stride=ROW_SUB), :] = x[:, s * LANES:(s + 1) * LANES]


def _pack_halves(x):
    half = x.shape[1] // 2
    lo = lax.bitcast_convert_type(x[:, :half].astype(BF16).astype(F32), jnp.uint32)
    hi = lax.bitcast_convert_type(x[:, half:].astype(BF16).astype(F32), jnp.uint32)
    return (hi & jnp.uint32(0xFFFF0000)) | (lo >> 16)


def _unpack_halves(w):
    lo = lax.bitcast_convert_type(w << 16, F32).astype(BF16)
    hi = lax.bitcast_convert_type(w & jnp.uint32(0xFFFF0000), F32).astype(BF16)
    return jnp.concatenate([lo, hi], axis=1)


def _ada_kernel(c_ref, w_ref, b_ref, o_ref):
    c = c_ref[...]
    cond = c * jax.nn.sigmoid(c)
    o_ref[...] = jnp.dot(cond, w_ref[...], precision=lax.Precision.HIGHEST,
                         preferred_element_type=F32) + b_ref[...]


def _ada(c_pad, w_ada, b_ada):
    rows, d = c_pad.shape
    n = w_ada.shape[1]
    tn = 1024
    return pl.pallas_call(
        _ada_kernel,
        grid=(n // tn,),
        in_specs=[pl.BlockSpec((rows, d), lambda j: (0, 0)),
                  pl.BlockSpec((d, tn), lambda j: (0, j)),
                  pl.BlockSpec((1, tn), lambda j: (0, j))],
        out_specs=pl.BlockSpec((rows, tn), lambda j: (0, j)),
        out_shape=jax.ShapeDtypeStruct((rows, n), F32),
        name="ada",
    )(c_pad, w_ada, b_ada)


def _in_proj_kernel(x_ref, mod_ref, nw_ref, w_ref, *o_refs):
    x = x_ref[...]
    shift = mod_ref[0, 0:1, :]
    scale = mod_ref[0, 1:2, :]
    h = (_rms(x) * nw_ref[...]) * (1.0 + scale) + shift
    hb = h.astype(BF16)
    off = 0
    for o_ref, width in zip(o_refs[:-1], IN_SIZES[:-1]):
        o_ref[...] = jnp.dot(hb, w_ref[:, off:off + width],
                             preferred_element_type=F32).astype(o_ref.dtype)
        off += width
    vt_ref = o_refs[-1]
    v = jnp.dot(hb, w_ref[:, off:], preferred_element_type=F32)
    hw = 2 * DIFF_HEAD_DIM
    tail = (lax.broadcasted_iota(jnp.int32, (ONES_ROWS, v.shape[0]), 0) == 0).astype(vt_ref.dtype)
    for hd in range(DIFF_HEADS):
        vt_ref[0, hd, 0, 0:hw, :] = v[:, hd * hw:(hd + 1) * hw].T.astype(vt_ref.dtype)
        vt_ref[0, hd, 0, hw:, :] = tail


def _in_proj(x2, mod, norm_w, w_in_bf16, seq):
    t, d = x2.shape
    tm = min(ROW_TILE, seq)
    assert tm == min(ATT_BLOCK, seq)
    per_batch = seq // tm
    in_w = w_in_bf16.shape[1]
    hw = 2 * DIFF_HEAD_DIM
    vt_shape = (t // seq, DIFF_HEADS, per_batch, hw + ONES_ROWS, tm)
    return pl.pallas_call(
        _in_proj_kernel,
        grid=(t // tm,),
        in_specs=[pl.BlockSpec((tm, d), lambda i: (i, 0)),
                  pl.BlockSpec((1, 6, d), lambda i: (i // per_batch, 0, 0)),
                  pl.BlockSpec((1, d), lambda i: (0, 0)),
                  pl.BlockSpec((d, in_w), lambda i: (0, 0))],
        out_specs=([pl.BlockSpec((tm, w), lambda i: (i, 0)) for w in IN_SIZES[:-1]]
                   + [pl.BlockSpec((1,) + vt_shape[1:2] + (1,) + vt_shape[3:],
                                   lambda i: (i // per_batch, 0, i % per_batch, 0, 0))]),
        out_shape=([jax.ShapeDtypeStruct((t, w), BF16) for w in IN_SIZES[:-1]]
                   + [jax.ShapeDtypeStruct(vt_shape, BF16)]),
        compiler_params=pltpu.CompilerParams(vmem_limit_bytes=VMEM_LIMIT),
        name="in_proj",
    )(x2, mod, norm_w, w_in_bf16)


def _rotary(x, cos, sin_even, sin_odd):
    nxt = pltpu.roll(x, LANES - 1, 1)
    prv = pltpu.roll(x, 1, 1)
    return x * cos + nxt * sin_even + prv * sin_odd


def _ret_kernel(q_ref, k_ref, v_ref, g_ref, cos_ref, sine_ref, sino_ref,
                hmask_ref, xi_ref, zeta_ref, dmask_ref, gch_ref, o_ref, state_ref):
    @pl.when(pl.program_id(1) == 0)
    def _():
        state_ref[...] = jnp.zeros_like(state_ref)

    n_sub = q_ref.shape[0] // RET_CHUNK
    for c in range(n_sub):
        rows = slice(c * RET_CHUNK, (c + 1) * RET_CHUNK)
        for pair in range(RET_HEADS // 2):
            lanes = slice(pair * LANES, (pair + 1) * LANES)
            cos = cos_ref[rows, lanes]
            sine = sine_ref[rows, lanes]
            sino = sino_ref[rows, lanes]
            qr = _rotary(q_ref[rows, lanes].astype(F32), cos, sine, sino)
            kr = _rotary(k_ref[rows, lanes].astype(F32), cos, sine, sino) * (RET_KEY_DIM ** -0.5)
            qb = qr.astype(BF16)
            for hh in range(2):
                h = 2 * pair + hh
                vcols = slice(h * RET_VAL_DIM, (h + 1) * RET_VAL_DIM)
                v = v_ref[rows, vcols]
                km = (kr * hmask_ref[h]).astype(BF16)
                scores = lax.dot_general(qb, km, (((1,), (1,)), ((), ())),
                                         preferred_element_type=F32) * dmask_ref[h]
                inner = jnp.dot(scores.astype(BF16), v, preferred_element_type=F32)
                qx = (qr * xi_ref[h]).astype(BF16)
                state = state_ref[h]
                cross = jnp.dot(qx, state.astype(BF16), preferred_element_type=F32)
                kz = (kr * zeta_ref[h]).astype(BF16)
                kv = lax.dot_general(kz, v, (((0,), (0,)), ((), ())),
                                     preferred_element_type=F32)
                state_ref[h] = state * gch_ref[h] + kv
                y = _rms(inner + cross)
                g = g_ref[rows, vcols].astype(F32)
                o_ref[rows, vcols] = (g * jax.nn.sigmoid(g) * y).astype(o_ref.dtype)


def _retention_tables(seq):
    dk, c, nh = RET_KEY_DIM, RET_CHUNK, RET_HEADS
    pos = jnp.arange(seq, dtype=F32)
    inv_freq = 1.0 / (10000.0 ** jnp.linspace(0.0, 1.0, dk // 2, dtype=F32))
    ang = pos[:, None] * jnp.repeat(inv_freq, 2)[None, :]
    sin = jnp.tile(jnp.sin(ang), (1, nh))
    cos = jnp.tile(jnp.cos(ang), (1, nh))
    even = (jnp.arange(nh * dk) % 2 == 0)[None, :]
    sin_even = jnp.where(even, -sin, 0.0)
    sin_odd = jnp.where(even, 0.0, sin)
    log_g = jnp.log(1.0 - 2.0 ** (-5.0 - jnp.arange(nh, dtype=F32)))
    i = jnp.arange(c, dtype=F32)
    rel = i[:, None] - i[None, :]
    dmask = jnp.where(rel[None] >= 0,
                      jnp.exp(jnp.maximum(rel, 0.0)[None] * log_g[:, None, None]), 0.0)
    zeta = jnp.exp((c - 1.0 - i)[None, :] * log_g[:, None])
    xi = jnp.exp((i + 1.0)[None, :] * log_g[:, None])
    g_chunk = jnp.exp(c * log_g)
    lane = jnp.arange(LANES)
    hmask = jnp.stack([((lane // dk) == (h % 2)).astype(F32) for h in range(nh)])[:, None, :]
    xi_t = xi[:, :, None] * hmask
    zeta_t = zeta[:, :, None] * hmask
    gch = jnp.broadcast_to(g_chunk[:, None, None], (nh, 1, LANES))
    return cos, sin_even, sin_odd, hmask, xi_t, zeta_t, dmask, gch


def _retention(q, k, v, g, batch, seq):
    t = q.shape[0]
    rb = min(RET_ROWS, seq)
    per_batch = seq // rb
    cos, sin_even, sin_odd, hmask, xi_t, zeta_t, dmask, gch = _retention_tables(seq)
    row = lambda w: pl.BlockSpec((rb, w), lambda b, j: (b * per_batch + j, 0))
    tab = lambda w: pl.BlockSpec((rb, w), lambda b, j: (j, 0))
    full = lambda a: pl.BlockSpec(a.shape, lambda b, j: (0,) * a.ndim)
    return pl.pallas_call(
        _ret_kernel,
        grid=(batch, per_batch),
        in_specs=[row(RET_QK_W), row(RET_QK_W), row(RET_V_W), row(RET_V_W),
                  tab(RET_QK_W), tab(RET_QK_W), tab(RET_QK_W),
                  full(hmask), full(xi_t), full(zeta_t), full(dmask), full(gch)],
        out_specs=row(RET_V_W),
        out_shape=jax.ShapeDtypeStruct((t, RET_V_W), BF16),
        scratch_shapes=[pltpu.VMEM((RET_HEADS, LANES, RET_VAL_DIM), F32)],
        compiler_params=pltpu.CompilerParams(
            dimension_semantics=("arbitrary", "arbitrary"), vmem_limit_bytes=VMEM_LIMIT),
        name="retention",
    )(q, k, v, g, cos, sin_even, sin_odd, hmask, xi_t, zeta_t, dmask, gch)


def _t5_bucket(rel):
    n = jnp.maximum(rel, 0)
    max_exact = REL_BUCKETS // 2
    nf = jnp.maximum(n, 1).astype(F32)
    large = max_exact + (jnp.log(nf / max_exact) / math.log(REL_MAX_DIST / max_exact)
                         * (REL_BUCKETS - max_exact)).astype(jnp.int32)
    large = jnp.minimum(large, REL_BUCKETS - 1)
    return jnp.where(n < max_exact, n, large)


def _bias_tiles(rel_bias, blk):
    r = jnp.arange(blk, dtype=jnp.int32)
    far = rel_bias[REL_BUCKETS - 1]
    rel0 = r[None, :] - r[:, None]
    rel1 = rel0 + blk
    buckets = jnp.arange(REL_BUCKETS, dtype=jnp.int32)

    def tile(rel):
        hot = (_t5_bucket(rel)[:, :, None] == buckets).astype(F32)
        return jnp.einsum('krb,bh->hkr', hot, rel_bias, precision=lax.Precision.HIGHEST)

    b0 = jnp.where(rel0[None] >= 0, (tile(rel0) - far[:, None, None]) * LOG2_E, NEG_BIG)
    b1 = (tile(rel1) - far[:, None, None]) * LOG2_E
    return b0, b1


def _attn_kernel(q_ref, k_ref, vt_ref, b0_ref, b1_ref, lq1_ref, lk1_ref, lq2_ref, lk2_ref,
                 sw_ref, o_ref, m_ref, acc_ref):
    blk = q_ref.shape[0]
    i = pl.program_id(2)
    lane = lax.broadcasted_iota(jnp.int32, (1, LANES), 1)
    q = (q_ref[...].astype(F32) * (DIFF_HEAD_DIM ** -0.5 * LOG2_E)).astype(BF16)
    zero = jnp.zeros_like(q)
    qm = (jnp.where(lane < DIFF_HEAD_DIM, q, zero), jnp.where(lane >= DIFF_HEAD_DIM, q, zero))

    m_ref[...] = jnp.full_like(m_ref, NEG_BIG)
    acc_ref[...] = jnp.zeros_like(acc_ref)

    def step(blocks):
        kbs = [k_ref[pl.ds(pl.multiple_of(j * blk, blk), blk), :] for j, _ in blocks]
        vts = [vt_ref[0, 0, j] for j, _ in blocks]
        chains = [(mi, slice(qs * ATT_STRIP, (qs + 1) * ATT_STRIP))
                  for mi in range(2) for qs in range(blk // ATT_STRIP)]
        scores = []
        for mi, qc in chains:
            row = []
            for (_, bias), kb in zip(blocks, kbs):
                s = lax.dot_general(kb, qm[mi][qc, :], (((1,), (1,)), ((), ())),
                                    preferred_element_type=F32)
                row.append(s if bias is None else s + bias[:, qc])
            scores.append(row)
        stats = []
        for (mi, qc), row in zip(chains, scores):
            m_old = m_ref[mi, :, qc]
            m_new = m_old
            for s in row:
                m_new = jnp.maximum(m_new, jnp.max(s, axis=0, keepdims=True))
            stats.append((jnp.exp2(m_old - m_new), [jnp.exp2(s - m_new).astype(BF16) for s in row], m_new))
        for (mi, qc), (alpha, ps, m_new) in zip(chains, stats):
            pv = jnp.dot(vts[0], ps[0], preferred_element_type=F32)
            for vt, p in zip(vts[1:], ps[1:]):
                pv = pv + jnp.dot(vt, p, preferred_element_type=F32)
            acc_ref[mi, :, qc] = alpha * acc_ref[mi, :, qc] + pv
            m_ref[mi, :, qc] = m_new

    n_far = jnp.maximum(i - 1, 0)

    def far_pair(pair, carry):
        step([(2 * pair, None), (2 * pair + 1, None)])
        return carry

    lax.fori_loop(0, lax.shift_right_logical(n_far, 1), far_pair, 0)

    @pl.when(lax.rem(n_far, 2) == 1)
    def _():
        step([(n_far - 1, None)])

    @pl.when(i >= 1)
    def _():
        step([(i - 1, b1_ref[0]), (i, b0_ref[0])])

    @pl.when(i == 0)
    def _():
        step([(i, b0_ref[0])])

    lam = (jnp.exp(jnp.sum(lq1_ref[...] * lk1_ref[...], axis=-1, keepdims=True))
           - jnp.exp(jnp.sum(lq2_ref[...] * lk2_ref[...], axis=-1, keepdims=True))
           + LAMBDA_INIT)
    hw = 2 * DIFF_HEAD_DIM
    a = (acc_ref[0, :hw, :] / acc_ref[0, hw:hw + 1, :]
         - lam * (acc_ref[1, :hw, :] / acc_ref[1, hw:hw + 1, :]))
    a = a * lax.rsqrt(jnp.mean(a * a, axis=0, keepdims=True) + NORM_EPS)
    o_ref[...] = (a.T * sw_ref[...] * (1.0 - LAMBDA_INIT)).astype(o_ref.dtype)


def _diff_attention(q, k, vt, rel_bias, lq1, lk1, lq2, lk2, subln_w, batch, seq):
    t = q.shape[0]
    blk = min(ATT_BLOCK, seq)
    nq = seq // blk
    hw = 2 * DIFF_HEAD_DIM
    b0, b1 = _bias_tiles(rel_bias, blk)
    small = lambda a: pl.BlockSpec(a.shape, lambda b, h, i: (0,) * a.ndim)
    return pl.pallas_call(
        _attn_kernel,
        grid=(batch, DIFF_HEADS, nq),
        in_specs=[pl.BlockSpec((blk, hw), lambda b, h, i: (b * nq + i, h)),
                  pl.BlockSpec((seq, hw), lambda b, h, i: (b, h)),
                  pl.BlockSpec((1, 1, nq, hw + ONES_ROWS, blk), lambda b, h, i: (b, h, 0, 0, 0)),
                  pl.BlockSpec((1, blk, blk), lambda b, h, i: (h, 0, 0)),
                  pl.BlockSpec((1, blk, blk), lambda b, h, i: (h, 0, 0)),
                  small(lq1), small(lk1), small(lq2), small(lk2), small(subln_w)],
        out_specs=pl.BlockSpec((blk, hw), lambda b, h, i: (b * nq + i, h)),
        out_shape=jax.ShapeDtypeStruct((t, DIFF_V_W), BF16),
        scratch_shapes=[pltpu.VMEM((2, 1, blk), F32), pltpu.VMEM((2, hw + ONES_ROWS, blk), F32)],
        compiler_params=pltpu.CompilerParams(
            dimension_semantics=("arbitrary", "arbitrary", "arbitrary"),
            vmem_limit_bytes=VMEM_LIMIT),
        name="diff_attn",
    )(q, k, vt, b0, b1, lq1, lk1, lq2, lk2, subln_w)


def _out_kernel(yr_ref, yd_ref, x_ref, mod_ref, nw_ref, wo_ref, wr_ref, br_ref, upper_ref,
                x1_ref, hp_ref, meta_ref, gate_ref, cnt_ref, run_ref):
    tm = x_ref.shape[0]
    ne = run_ref.shape[0]

    @pl.when(pl.program_id(0) == 0)
    def _():
        run_ref[...] = jnp.zeros_like(run_ref)

    mixed = (jnp.dot(yr_ref[...], wo_ref[0:RET_V_W, :], preferred_element_type=F32)
             + jnp.dot(yd_ref[...], wo_ref[RET_V_W:, :], preferred_element_type=F32))
    gate1 = mod_ref[0, 2:3, :]
    shift2 = mod_ref[0, 3:4, :]
    scale2 = mod_ref[0, 4:5, :]
    x1 = x_ref[...] + gate1 * mixed
    x1_ref[...] = x1
    h2 = (_rms(x1) * nw_ref[...]) * (1.0 + scale2) + shift2
    hp_ref[...] = _pack_halves(h2)

    h_hi = h2.astype(BF16)
    h_lo = (h2 - h_hi.astype(F32)).astype(BF16)
    nt_dims = (((1,), (1,)), ((), ()))
    both = lax.dot_general(wr_ref[...], h_hi, nt_dims, preferred_element_type=F32)
    low = lax.dot_general(wr_ref[0:ne, :], h_lo, nt_dims, preferred_element_type=F32)
    logits = both[0:ne, :] + both[ne:, :] + low + br_ref[...]

    row = lax.broadcasted_iota(jnp.int32, logits.shape, 0)
    work = logits
    vals, idxs, hots = [], [], []
    for _ in range(TOP_K):
        mx = jnp.max(work, axis=0, keepdims=True)
        idx = jnp.min(jnp.where(work == mx, row, ne), axis=0, keepdims=True)
        hot = row == idx
        vals.append(mx)
        idxs.append(idx)
        hots.append(hot)
        work = jnp.where(hot, -jnp.inf, work)
    exps = [jnp.exp(v - vals[0]) for v in vals]
    denom = exps[0] + exps[1] + exps[2] + exps[3]

    sel = jnp.zeros(logits.shape, F32)
    for hot in hots:
        sel = sel + hot.astype(F32)
    prefix = jnp.dot(sel.astype(BF16), upper_ref[...], preferred_element_type=F32) + run_ref[...]
    ranks = [jnp.sum(jnp.where(hot, prefix, 0.0), axis=0, keepdims=True) for hot in hots]
    run_ref[...] = run_ref[...] + jnp.sum(sel, axis=1, keepdims=True)
    cnt_ref[...] = jnp.broadcast_to(run_ref[...], cnt_ref.shape).astype(jnp.int32)

    meta_ref[...] = jnp.concatenate(idxs + [r.astype(jnp.int32) for r in ranks], axis=0)
    gate_ref[...] = jnp.concatenate([e / denom for e in exps] + [jnp.zeros_like(denom)] * TOP_K, axis=0)


def _out_router(y_r, y_d, x2, mod, norm_w, w_out_bf16, w_router, b_router, seq):
    t, d = x2.shape
    tm = min(ROW_TILE, seq)
    per_batch = seq // tm
    ne = w_router.shape[1]
    w_hi = w_router.astype(BF16)
    w_lo = (w_router - w_hi.astype(F32)).astype(BF16)
    wr_t = jnp.concatenate([w_hi, w_lo], axis=1).T
    idx = jnp.arange(tm, dtype=jnp.int32)
    upper = (idx[:, None] < idx[None, :]).astype(BF16)
    row = lambda w: pl.BlockSpec((tm, w), lambda i: (i, 0))
    col = lambda h: pl.BlockSpec((h, tm), lambda i: (0, i))
    const = lambda a: pl.BlockSpec(a.shape, lambda i: (0,) * a.ndim)
    x1, hp, meta_t, gates_t, counts = pl.pallas_call(
        _out_kernel,
        grid=(t // tm,),
        in_specs=[row(RET_V_W), row(DIFF_V_W), row(d),
                  pl.BlockSpec((1, 6, d), lambda i: (i // per_batch, 0, 0)),
                  const(norm_w), const(w_out_bf16), const(wr_t), const(b_router), const(upper)],
        out_specs=[row(d), row(d // 2), col(2 * TOP_K), col(2 * TOP_K),
                   pl.BlockSpec((ne, LANES), lambda i: (0, 0))],
        out_shape=[jax.ShapeDtypeStruct((t, d), F32),
                   jax.ShapeDtypeStruct((t, d // 2), jnp.uint32),
                   jax.ShapeDtypeStruct((2 * TOP_K, t), jnp.int32),
                   jax.ShapeDtypeStruct((2 * TOP_K, t), F32),
                   jax.ShapeDtypeStruct((ne, LANES), jnp.int32)],
        scratch_shapes=[pltpu.VMEM((ne, 1), F32)],
        compiler_params=pltpu.CompilerParams(
            dimension_semantics=("arbitrary",), vmem_limit_bytes=VMEM_LIMIT),
        name="out_router",
    )(y_r, y_d, x2, mod, norm_w, w_out_bf16, wr_t, b_router, upper)
    return x1, hp, meta_t.T, gates_t.T, counts[:, 0]


def _w1_prep_kernel(w_ref, p_ref, o_ref):
    for s in range(w_ref.shape[2] // PERM_TILE):
        cols = slice(s * PERM_TILE, (s + 1) * PERM_TILE)
        o_ref[0, :, cols] = jnp.dot(w_ref[0, :, cols].astype(BF16), p_ref[...],
                                    preferred_element_type=F32).astype(BF16)


def _pair_split_matrix():
    i = jnp.arange(PERM_TILE)[:, None]
    j = jnp.arange(PERM_TILE)[None, :]
    half = PERM_TILE // 2
    src = jnp.where(j < half, 2 * j, 2 * (j - half) + 1)
    return (i == src).astype(BF16)


def _w1_prep(w1):
    e, d, f2 = w1.shape
    tn = 1024
    return pl.pallas_call(
        _w1_prep_kernel,
        grid=(e, f2 // tn),
        in_specs=[pl.BlockSpec((1, d, tn), lambda i, j: (i, 0, j)),
                  pl.BlockSpec((PERM_TILE, PERM_TILE), lambda i, j: (0, 0))],
        out_specs=pl.BlockSpec((1, d, tn), lambda i, j: (i, 0, j)),
        out_shape=jax.ShapeDtypeStruct((e, d, f2), BF16),
        compiler_params=pltpu.CompilerParams(vmem_limit_bytes=VMEM_LIMIT),
        name="w1_prep",
    )(w1, _pair_split_matrix())


def _pair_split_bias(b1):
    e, f2 = b1.shape
    nt = f2 // PERM_TILE
    g = b1[:, 0::2].reshape(e, nt, 1, PERM_TILE // 2)
    l = b1[:, 1::2].reshape(e, nt, 1, PERM_TILE // 2)
    return jnp.concatenate([g, l], axis=2).reshape(e, 1, f2)


def _slot_tok_kernel(first_slot, dest_ref, init_ref, o_ref, sem):
    init_copy = pltpu.make_async_copy(init_ref, o_ref, sem)
    init_copy.start()
    init_copy.wait()
    n_slots = dest_ref.shape[0]

    def put(i, carry):
        o_ref[dest_ref[i]] = lax.shift_right_logical(first_slot + i, 2)
        return carry

    lax.fori_loop(0, n_slots, put, 0, unroll=8)


def _slot_tokens(dest, n_rows):
    assert TOP_K == 4
    flat = dest.reshape(-1)
    per_call = flat.shape[0] // SLOT_MAP_CALLS
    slot_tok = jnp.zeros((n_rows,), jnp.int32)
    for part in range(SLOT_MAP_CALLS):
        slot_tok = pl.pallas_call(
            functools.partial(_slot_tok_kernel, part * per_call),
            in_specs=[pl.BlockSpec(memory_space=pltpu.SMEM), pl.BlockSpec(memory_space=pl.ANY)],
            out_specs=pl.BlockSpec(memory_space=pltpu.SMEM),
            out_shape=jax.ShapeDtypeStruct((n_rows,), jnp.int32),
            scratch_shapes=[pltpu.SemaphoreType.DMA(())],
            name="slot_tokens",
        )(flat[part * per_call:(part + 1) * per_call], slot_tok)
    return slot_tok


def _expert_kernel(be_ref, nv_ref, tcur_ref, tnxt_ref, hp_ref, w1_ref, b1_ref, w2_ref, b2_ref,
                   o_ref, hbuf, xb0, xb1, w2b, sem):
    b = pl.program_id(0)
    nv = nv_ref[0]
    bm = xb0.shape[0]

    def gather(tok_ref, dst):
        for r in range(bm):
            dst[pl.ds(r, 1), :] = hbuf[pl.ds(tok_ref[0, 0, r], 1), :]

    @pl.when(b == 0)
    def _():
        load = pltpu.make_async_copy(hp_ref, hbuf, sem)
        load.start()
        load.wait()
        gather(tcur_ref, xb0)

    last = jnp.maximum(b - 1, 0)

    @pl.when(jnp.logical_and(b < nv, jnp.logical_or(b == 0, be_ref[b] != be_ref[last])))
    def _():
        w2b[...] = w2_ref[0].astype(BF16)

    def run(cur, nxt):
        x = _unpack_halves(cur[...])
        half = PERM_TILE // 2
        acts = []
        for j in range(w1_ref.shape[2] // PERM_TILE):
            cols = slice(j * PERM_TILE, (j + 1) * PERM_TILE)
            h = jnp.dot(x, w1_ref[0, :, cols], preferred_element_type=F32) + b1_ref[0, :, cols]
            glu = jnp.minimum(h[:, :half], SWIGLU_LIMIT)
            lin = jnp.clip(h[:, half:], -SWIGLU_LIMIT, SWIGLU_LIMIT)
            acts.append((glu * jax.nn.sigmoid(SWIGLU_ALPHA * glu) * (lin + 1.0)).astype(BF16))
        act = jnp.concatenate(acts, axis=1)
        y = jnp.dot(act, w2b[...], preferred_element_type=F32) + b2_ref[0]
        _store_row_tiles(o_ref, y)
        gather(tnxt_ref, nxt)

    even = lax.rem(b, 2) == 0

    @pl.when(jnp.logical_and(b < nv, even))
    def _():
        run(xb0, xb1)

    @pl.when(jnp.logical_and(b < nv, jnp.logical_not(even)))
    def _():
        run(xb1, xb0)

    @pl.when(b >= nv)
    def _():
        o_ref[...] = jnp.zeros_like(o_ref)


def _experts(block_e, n_valid, slot_tok, hp, w1p, b1p, w2, b2):
    d = D_MODEL
    f2 = w1p.shape[2]
    f = w2.shape[1]
    bm = EXPERT_ROWS
    nb = slot_tok.shape[0] // bm
    tok3 = slot_tok.reshape(nb, 1, bm)
    exp = lambda b, be, nv: be[jnp.maximum(jnp.minimum(b, nv[0] - 1), 0)]
    grid_spec = pltpu.PrefetchScalarGridSpec(
        num_scalar_prefetch=2,
        grid=(nb,),
        in_specs=[pl.BlockSpec((1, 1, bm), lambda b, be, nv: (b, 0, 0), memory_space=pltpu.SMEM),
                  pl.BlockSpec((1, 1, bm), lambda b, be, nv: (jnp.minimum(b + 1, nb - 1), 0, 0),
                               memory_space=pltpu.SMEM),
                  pl.BlockSpec(memory_space=pl.ANY),
                  pl.BlockSpec((1, d, f2), lambda b, be, nv: (exp(b, be, nv), 0, 0)),
                  pl.BlockSpec((1, 1, f2), lambda b, be, nv: (exp(b, be, nv), 0, 0)),
                  pl.BlockSpec((1, f, d), lambda b, be, nv: (exp(b, be, nv), 0, 0)),
                  pl.BlockSpec((1, 1, d), lambda b, be, nv: (exp(b, be, nv), 0, 0))],
        out_specs=pl.BlockSpec((bm * ROW_SUB, LANES), lambda b, be, nv: (b, 0)),
        scratch_shapes=[pltpu.VMEM(hp.shape, hp.dtype), pltpu.VMEM((bm, hp.shape[1]), hp.dtype),
                        pltpu.VMEM((bm, hp.shape[1]), hp.dtype), pltpu.VMEM(w2.shape[1:], BF16),
                        pltpu.SemaphoreType.DMA(())],
    )
    return pl.pallas_call(
        _expert_kernel,
        grid_spec=grid_spec,
        out_shape=jax.ShapeDtypeStruct((nb * bm * ROW_SUB, LANES), F32),
        compiler_params=pltpu.CompilerParams(
            dimension_semantics=("arbitrary",), vmem_limit_bytes=VMEM_LIMIT),
        name="experts",
    )(block_e, n_valid, tok3, tok3, hp, w1p, b1p, w2, b2)


def _combine_kernel(dcur_ref, dnxt_ref, gate_ref, x1_ref, mod_ref, nf_ref, y_ref, o_ref, buf, sems):
    nt = x1_ref.shape[0]
    i = pl.program_id(0)
    n = pl.num_programs(0)
    slot = lax.rem(i, 2)

    def row_copy(src_row, s, kk, r):
        return pltpu.make_async_copy(y_ref.at[pl.ds(src_row * ROW_SUB, ROW_SUB), :],
                                     buf.at[s, kk, pl.ds(r * ROW_SUB, ROW_SUB), :], sems.at[s])

    def fetch(d_ref, s):
        for j in range(nt):
            for kk in range(TOP_K):
                row_copy(d_ref[0, 0, j * TOP_K + kk], s, kk, j).start(priority=kk % DMA_QUEUES)

    def wait_all(s):
        for kk in range(TOP_K):
            pltpu.make_async_copy(y_ref.at[pl.ds(0, nt * ROW_SUB), :], buf.at[s, kk], sems.at[s]).wait()

    @pl.when(i == 0)
    def _():
        fetch(dcur_ref, 0)

    fetch(dnxt_ref, 1 - slot)
    wait_all(slot)

    g = gate_ref[...]
    parts = []
    for s in range(ROW_SUB):
        acc = buf[slot, 0, pl.ds(s, nt, stride=ROW_SUB), :] * g[:, 0:1]
        for kk in range(1, TOP_K):
            acc = acc + buf[slot, kk, pl.ds(s, nt, stride=ROW_SUB), :] * g[:, kk:kk + 1]
        parts.append(acc)
    moe = jnp.concatenate(parts, axis=1)
    gate2 = mod_ref[0, 5:6, :]
    x2 = x1_ref[...] + gate2 * moe
    o_ref[...] = _rms(x2) * nf_ref[...]

    @pl.when(i == n - 1)
    def _():
        wait_all(1 - slot)


def _combine(dest, gates, x1, mod, normf_w, y, seq):
    t, d = x1.shape
    nt = min(COMBINE_TOKENS, seq)
    steps = t // nt
    per_batch = seq // nt
    dest2 = dest.reshape(steps, 1, nt * TOP_K)
    return pl.pallas_call(
        _combine_kernel,
        grid=(steps,),
        in_specs=[pl.BlockSpec((1, 1, nt * TOP_K), lambda i: (i, 0, 0), memory_space=pltpu.SMEM),
                  pl.BlockSpec((1, 1, nt * TOP_K), lambda i: (jnp.minimum(i + 1, steps - 1), 0, 0),
                               memory_space=pltpu.SMEM),
                  pl.BlockSpec((nt, 2 * TOP_K), lambda i: (i, 0)),
                  pl.BlockSpec((nt, d), lambda i: (i, 0)),
                  pl.BlockSpec((1, 6, d), lambda i: (i // per_batch, 0, 0)),
                  pl.BlockSpec((1, d), lambda i: (0, 0)),
                  pl.BlockSpec(memory_space=pl.ANY)],
        out_specs=pl.BlockSpec((nt, d), lambda i: (i, 0)),
        out_shape=jax.ShapeDtypeStruct((t, d), F32),
        scratch_shapes=[pltpu.VMEM((2, TOP_K, nt * ROW_SUB, LANES), F32),
                        pltpu.SemaphoreType.DMA((2,))],
        compiler_params=pltpu.CompilerParams(
            dimension_semantics=("arbitrary",), vmem_limit_bytes=VMEM_LIMIT),
        name="combine",
    )(dest2, dest2, gates, x1, mod, normf_w, y)


def kernel(x, c, w_ada, b_ada, norm1_w, w_in, lam_q1, lam_k1, lam_q2, lam_k2, subln_w, rel_bias,
           w_out, norm2_w, w_router, b_router, w1, b1, w2, b2, normf_w):
    batch, seq, d = x.shape
    t = batch * seq
    x2 = x.reshape(t, d)

    c_pad = jnp.zeros((8, d), F32).at[:batch].set(c)
    mod = _ada(c_pad, w_ada[0], b_ada[0][None, :])[:batch].reshape(batch, 6, d)

    q_r, k_r, v_r, g_r, q_d, k_d, vt_d = _in_proj(x2, mod, norm1_w[0][None, :],
                                                 w_in[0].astype(BF16), seq)
    y_r = _retention(q_r, k_r, v_r, g_r, batch, seq)
    y_d = _diff_attention(q_d, k_d, vt_d, rel_bias, lam_q1, lam_k1, lam_q2, lam_k2,
                          subln_w, batch, seq)

    x1, hp, meta, gates, counts = _out_router(y_r, y_d, x2, mod, norm2_w[0][None, :],
                                              w_out[0].astype(BF16), w_router[0],
                                              b_router[0][:, None], seq)

    bm = EXPERT_ROWS
    padded = (counts + bm - 1) // bm * bm
    pad_end = jnp.cumsum(padded)
    pad_start = pad_end - padded
    sel_e = meta[:, :TOP_K]
    hot_e = sel_e[:, :, None] == jnp.arange(N_EXPERTS, dtype=jnp.int32)[None, None, :]
    dest = jnp.sum(jnp.where(hot_e, pad_start[None, None, :], 0), axis=-1) + meta[:, TOP_K:]
    n_rows = (t * TOP_K // bm + N_EXPERTS) * bm
    nb = n_rows // bm
    block_start = jnp.arange(nb, dtype=jnp.int32) * bm
    block_e = jnp.minimum(jnp.sum((pad_end[None, :] <= block_start[:, None]).astype(jnp.int32), axis=1),
                          N_EXPERTS - 1)
    n_valid = (pad_end[-1:] // bm).astype(jnp.int32)
    slot_tok = _slot_tokens(dest, n_rows)

    ys = _experts(block_e, n_valid, slot_tok, hp, _w1_prep(w1[0]), _pair_split_bias(b1[0]),
                  w2[0], b2[0][:, None, :])
    out = _combine(dest, gates, x1, mod, normf_w[None, :], ys, seq)
    return out.reshape(batch, seq, d)
```

```python
import math

import jax
import jax.numpy as jnp
from jax import lax
from jax.experimental import pallas as pl
from jax.experimental.pallas import tpu as pltpu
from jax.experimental.pallas import tpu_sc as plsc

F32 = jnp.float32
BF16 = jnp.bfloat16

D_MODEL = 1024
RET_HEADS = 4
RET_KEY_DIM = 64
RET_VAL_DIM = 128
RET_QK_W = RET_HEADS * RET_KEY_DIM
RET_V_W = RET_HEADS * RET_VAL_DIM
RET_CHUNK = 128
DIFF_HEADS = 4
DIFF_HEAD_DIM = 64
DIFF_QK_W = DIFF_HEADS * 2 * DIFF_HEAD_DIM
DIFF_V_W = DIFF_HEADS * 2 * DIFF_HEAD_DIM
IN_SIZES = (RET_QK_W, RET_QK_W, RET_V_W, RET_V_W, DIFF_QK_W, DIFF_QK_W, DIFF_V_W)
REL_BUCKETS = 32
REL_MAX_DIST = 128
N_EXPERTS = 32
TOP_K = 4
SWIGLU_ALPHA = 1.702
SWIGLU_LIMIT = 7.0
NORM_EPS = 1e-6
LAMBDA_INIT = 0.8 - 0.6 * math.exp(-0.3 * 0)

LANES = 128
ROW_SUB = D_MODEL // LANES
NEG_BIG = -1e30
LOG2_E = math.log2(math.e)
ONES_ROWS = 16
VMEM_LIMIT = 56 * 1024 * 1024

ROW_TILE = 512
RET_ROWS = 512
ATT_BLOCK = 512
ATT_STRIP = 256
EXPERT_ROWS = 256
PERM_TILE = 256
COMBINE_TOKENS = 128
DMA_QUEUES = 2
SC_CORES = 2
SC_SUBCORES = 16
SC_LANES = 16
SC_SLOT_CHUNK = 4096


def _rms(x):
    return x * lax.rsqrt(jnp.mean(x * x, axis=-1, keepdims=True) + NORM_EPS)


def _store_row_tiles(ref, x):
    rows = x.shape[0]
    for s in range(ROW_SUB):
        ref[pl.ds(s, rows, stride=ROW_SUB), :] = x[:, s * LANES:(s + 1) * LANES]


def _pack_halves(x):
    half = x.shape[1] // 2
    lo = lax.bitcast_convert_type(x[:, :half].astype(BF16).astype(F32), jnp.uint32)
    hi = lax.bitcast_convert_type(x[:, half:].astype(BF16).astype(F32), jnp.uint32)
    return (hi & jnp.uint32(0xFFFF0000)) | (lo >> 16)


def _unpack_halves(w):
    lo = lax.bitcast_convert_type(w << 16, F32).astype(BF16)
    hi = lax.bitcast_convert_type(w & jnp.uint32(0xFFFF0000), F32).astype(BF16)
    return jnp.concatenate([lo, hi], axis=1)


def _ada_kernel(c_ref, w_ref, b_ref, o_ref):
    c = c_ref[...]
    cond = c * jax.nn.sigmoid(c)
    o_ref[...] = jnp.dot(cond, w_ref[...], precision=lax.Precision.HIGHEST,
                         preferred_element_type=F32) + b_ref[...]


def _ada(c_pad, w_ada, b_ada):
    rows, d = c_pad.shape
    n = w_ada.shape[1]
    tn = 1024
    return pl.pallas_call(
        _ada_kernel,
        grid=(n // tn,),
        in_specs=[pl.BlockSpec((rows, d), lambda j: (0, 0)),
                  pl.BlockSpec((d, tn), lambda j: (0, j)),
                  pl.BlockSpec((1, tn), lambda j: (0, j))],
        out_specs=pl.BlockSpec((rows, tn), lambda j: (0, j)),
        out_shape=jax.ShapeDtypeStruct((rows, n), F32),
        name="ada",
    )(c_pad, w_ada, b_ada)


def _in_proj_kernel(x_ref, mod_ref, nw_ref, w_ref, *o_refs):
    x = x_ref[...]
    shift = mod_ref[0, 0:1, :]
    scale = mod_ref[0, 1:2, :]
    h = (_rms(x) * nw_ref[...]) * (1.0 + scale) + shift
    hb = h.astype(BF16)
    off = 0
    for o_ref, width in zip(o_refs[:-1], IN_SIZES[:-1]):
        o_ref[...] = jnp.dot(hb, w_ref[:, off:off + width],
                             preferred_element_type=F32).astype(o_ref.dtype)
        off += width
    vt_ref = o_refs[-1]
    v = jnp.dot(hb, w_ref[:, off:], preferred_element_type=F32)
    hw = 2 * DIFF_HEAD_DIM
    tail = (lax.broadcasted_iota(jnp.int32, (ONES_ROWS, v.shape[0]), 0) == 0).astype(vt_ref.dtype)
    for hd in range(DIFF_HEADS):
        vt_ref[0, hd, 0, 0:hw, :] = v[:, hd * hw:(hd + 1) * hw].T.astype(vt_ref.dtype)
        vt_ref[0, hd, 0, hw:, :] = tail


def _in_proj(x2, mod, norm_w, w_in_bf16, seq):
    t, d = x2.shape
    tm = min(ROW_TILE, seq)
    assert tm == min(ATT_BLOCK, seq)
    per_batch = seq // tm
    in_w = w_in_bf16.shape[1]
    hw = 2 * DIFF_HEAD_DIM
    vt_shape = (t // seq, DIFF_HEADS, per_batch, hw + ONES_ROWS, tm)
    return pl.pallas_call(
        _in_proj_kernel,
        grid=(t // tm,),
        in_specs=[pl.BlockSpec((tm, d), lambda i: (i, 0)),
                  pl.BlockSpec((1, 6, d), lambda i: (i // per_batch, 0, 0)),
                  pl.BlockSpec((1, d), lambda i: (0, 0)),
                  pl.BlockSpec((d, in_w), lambda i: (0, 0))],
        out_specs=([pl.BlockSpec((tm, w), lambda i: (i, 0)) for w in IN_SIZES[:-1]]
                   + [pl.BlockSpec((1,) + vt_shape[1:2] + (1,) + vt_shape[3:],
                                   lambda i: (i // per_batch, 0, i % per_batch, 0, 0))]),
        out_shape=([jax.ShapeDtypeStruct((t, w), BF16) for w in IN_SIZES[:-1]]
                   + [jax.ShapeDtypeStruct(vt_shape, BF16)]),
        compiler_params=pltpu.CompilerParams(vmem_limit_bytes=VMEM_LIMIT),
        name="in_proj",
    )(x2, mod, norm_w, w_in_bf16)


def _rotary(x, cos, sin_even, sin_odd):
    nxt = pltpu.roll(x, LANES - 1, 1)
    prv = pltpu.roll(x, 1, 1)
    return x * cos + nxt * sin_even + prv * sin_odd


def _ret_kernel(q_ref, k_ref, v_ref, g_ref, cos_ref, sine_ref, sino_ref,
                hmask_ref, xi_ref, zeta_ref, dmask_ref, gch_ref, o_ref, state_ref):
    @pl.when(pl.program_id(1) == 0)
    def _():
        state_ref[...] = jnp.zeros_like(state_ref)

    n_sub = q_ref.shape[0] // RET_CHUNK
    for c in range(n_sub):
        rows = slice(c * RET_CHUNK, (c + 1) * RET_CHUNK)
        for pair in range(RET_HEADS // 2):
            lanes = slice(pair * LANES, (pair + 1) * LANES)
            cos = cos_ref[rows, lanes]
            sine = sine_ref[rows, lanes]
            sino = sino_ref[rows, lanes]
            qr = _rotary(q_ref[rows, lanes].astype(F32), cos, sine, sino)
            kr = _rotary(k_ref[rows, lanes].astype(F32), cos, sine, sino) * (RET_KEY_DIM ** -0.5)
            qb = qr.astype(BF16)
            for hh in range(2):
                h = 2 * pair + hh
                vcols = slice(h * RET_VAL_DIM, (h + 1) * RET_VAL_DIM)
                v = v_ref[rows, vcols]
                km = (kr * hmask_ref[h]).astype(BF16)
                scores = lax.dot_general(qb, km, (((1,), (1,)), ((), ())),
                                         preferred_element_type=F32) * dmask_ref[h]
                inner = jnp.dot(scores.astype(BF16), v, preferred_element_type=F32)
                qx = (qr * xi_ref[h]).astype(BF16)
                state = state_ref[h]
                cross = jnp.dot(qx, state.astype(BF16), preferred_element_type=F32)
                kz = (kr * zeta_ref[h]).astype(BF16)
                kv = lax.dot_general(kz, v, (((0,), (0,)), ((), ())),
                                     preferred_element_type=F32)
                state_ref[h] = state * gch_ref[h] + kv
                y = _rms(inner + cross)
                g = g_ref[rows, vcols].astype(F32)
                o_ref[rows, vcols] = (g * jax.nn.sigmoid(g) * y).astype(o_ref.dtype)


def _retention_tables(seq):
    dk, c, nh = RET_KEY_DIM, RET_CHUNK, RET_HEADS
    pos = jnp.arange(seq, dtype=F32)
    inv_freq = 1.0 / (10000.0 ** jnp.linspace(0.0, 1.0, dk // 2, dtype=F32))
    ang = pos[:, None] * jnp.repeat(inv_freq, 2)[None, :]
    sin = jnp.tile(jnp.sin(ang), (1, nh))
    cos = jnp.tile(jnp.cos(ang), (1, nh))
    even = (jnp.arange(nh * dk) % 2 == 0)[None, :]
    sin_even = jnp.where(even, -sin, 0.0)
    sin_odd = jnp.where(even, 0.0, sin)
    log_g = jnp.log(1.0 - 2.0 ** (-5.0 - jnp.arange(nh, dtype=F32)))
    i = jnp.arange(c, dtype=F32)
    rel = i[:, None] - i[None, :]
    dmask = jnp.where(rel[None] >= 0,
                      jnp.exp(jnp.maximum(rel, 0.0)[None] * log_g[:, None, None]), 0.0)
    zeta = jnp.exp((c - 1.0 - i)[None, :] * log_g[:, None])
    xi = jnp.exp((i + 1.0)[None, :] * log_g[:, None])
    g_chunk = jnp.exp(c * log_g)
    lane = jnp.arange(LANES)
    hmask = jnp.stack([((lane // dk) == (h % 2)).astype(F32) for h in range(nh)])[:, None, :]
    xi_t = xi[:, :, None] * hmask
    zeta_t = zeta[:, :, None] * hmask
    gch = jnp.broadcast_to(g_chunk[:, None, None], (nh, 1, LANES))
    return cos, sin_even, sin_odd, hmask, xi_t, zeta_t, dmask, gch


def _retention(q, k, v, g, batch, seq):
    t = q.shape[0]
    rb = min(RET_ROWS, seq)
    per_batch = seq // rb
    cos, sin_even, sin_odd, hmask, xi_t, zeta_t, dmask, gch = _retention_tables(seq)
    row = lambda w: pl.BlockSpec((rb, w), lambda b, j: (b * per_batch + j, 0))
    tab = lambda w: pl.BlockSpec((rb, w), lambda b, j: (j, 0))
    full = lambda a: pl.BlockSpec(a.shape, lambda b, j: (0,) * a.ndim)
    return pl.pallas_call(
        _ret_kernel,
        grid=(batch, per_batch),
        in_specs=[row(RET_QK_W), row(RET_QK_W), row(RET_V_W), row(RET_V_W),
                  tab(RET_QK_W), tab(RET_QK_W), tab(RET_QK_W),
                  full(hmask), full(xi_t), full(zeta_t), full(dmask), full(gch)],
        out_specs=row(RET_V_W),
        out_shape=jax.ShapeDtypeStruct((t, RET_V_W), BF16),
        scratch_shapes=[pltpu.VMEM((RET_HEADS, LANES, RET_VAL_DIM), F32)],
        compiler_params=pltpu.CompilerParams(
            dimension_semantics=("arbitrary", "arbitrary"), vmem_limit_bytes=VMEM_LIMIT),
        name="retention",
    )(q, k, v, g, cos, sin_even, sin_odd, hmask, xi_t, zeta_t, dmask, gch)


def _t5_bucket(rel):
    n = jnp.maximum(rel, 0)
    max_exact = REL_BUCKETS // 2
    nf = jnp.maximum(n, 1).astype(F32)
    large = max_exact + (jnp.log(nf / max_exact) / math.log(REL_MAX_DIST / max_exact)
                         * (REL_BUCKETS - max_exact)).astype(jnp.int32)
    large = jnp.minimum(large, REL_BUCKETS - 1)
    return jnp.where(n < max_exact, n, large)


def _bias_tiles(rel_bias, blk):
    r = jnp.arange(blk, dtype=jnp.int32)
    far = rel_bias[REL_BUCKETS - 1]
    rel0 = r[None, :] - r[:, None]
    rel1 = rel0 + blk
    buckets = jnp.arange(REL_BUCKETS, dtype=jnp.int32)

    def tile(rel):
        hot = (_t5_bucket(rel)[:, :, None] == buckets).astype(F32)
        return jnp.einsum('krb,bh->hkr', hot, rel_bias, precision=lax.Precision.HIGHEST)

    b0 = jnp.where(rel0[None] >= 0, (tile(rel0) - far[:, None, None]) * LOG2_E, NEG_BIG)
    b1 = (tile(rel1) - far[:, None, None]) * LOG2_E
    return b0, b1


def _attn_kernel(q_ref, k_ref, vt_ref, b0_ref, b1_ref, lq1_ref, lk1_ref, lq2_ref, lk2_ref,
                 sw_ref, o_ref, m_ref, acc_ref):
    blk = q_ref.shape[0]
    i = pl.program_id(2)
    lane = lax.broadcasted_iota(jnp.int32, (1, LANES), 1)
    q = (q_ref[...].astype(F32) * (DIFF_HEAD_DIM ** -0.5 * LOG2_E)).astype(BF16)
    zero = jnp.zeros_like(q)
    qm = (jnp.where(lane < DIFF_HEAD_DIM, q, zero), jnp.where(lane >= DIFF_HEAD_DIM, q, zero))

    m_ref[...] = jnp.full_like(m_ref, NEG_BIG)
    acc_ref[...] = jnp.zeros_like(acc_ref)

    def step(blocks):
        kbs = [k_ref[pl.ds(pl.multiple_of(j * blk, blk), blk), :] for j, _, _ in blocks]
        vts = [vt_ref[0, 0, j] for j, _, _ in blocks]
        chains = [(mi, qs) for mi in range(2) for qs in range(blk // ATT_STRIP)]
        scores = []
        for mi, qs in chains:
            qc = slice(qs * ATT_STRIP, (qs + 1) * ATT_STRIP)
            row = []
            for (_, bias, diagonal), kb in zip(blocks, kbs):
                keys = (qs + 1) * ATT_STRIP if diagonal else blk
                s = lax.dot_general(kb[:keys], qm[mi][qc, :], (((1,), (1,)), ((), ())),
                                    preferred_element_type=F32)
                row.append(s if bias is None else s + bias[:keys, qc])
            scores.append(row)
        stats = []
        for (mi, qs), row in zip(chains, scores):
            qc = slice(qs * ATT_STRIP, (qs + 1) * ATT_STRIP)
            m_old = m_ref[mi, :, qc]
            m_new = m_old
            for s in row:
                m_new = jnp.maximum(m_new, jnp.max(s, axis=0, keepdims=True))
            stats.append((jnp.exp2(m_old - m_new), [jnp.exp2(s - m_new).astype(BF16) for s in row], m_new))
        for (mi, qs), (alpha, ps, m_new) in zip(chains, stats):
            qc = slice(qs * ATT_STRIP, (qs + 1) * ATT_STRIP)
            pv = None
            for vt, p in zip(vts, ps):
                part = jnp.dot(vt[:, :p.shape[0]], p, preferred_element_type=F32)
                pv = part if pv is None else pv + part
            acc_ref[mi, :, qc] = alpha * acc_ref[mi, :, qc] + pv
            m_ref[mi, :, qc] = m_new

    n_far = jnp.maximum(i - 1, 0)

    def far_pair(pair, carry):
        step([(2 * pair, None, False), (2 * pair + 1, None, False)])
        return carry

    lax.fori_loop(0, lax.shift_right_logical(n_far, 1), far_pair, 0)

    @pl.when(lax.rem(n_far, 2) == 1)
    def _():
        step([(n_far - 1, None, False)])

    @pl.when(i >= 1)
    def _():
        step([(i - 1, b1_ref[0], False), (i, b0_ref[0], True)])

    @pl.when(i == 0)
    def _():
        step([(i, b0_ref[0], True)])

    lam = (jnp.exp(jnp.sum(lq1_ref[...] * lk1_ref[...], axis=-1, keepdims=True))
           - jnp.exp(jnp.sum(lq2_ref[...] * lk2_ref[...], axis=-1, keepdims=True))
           + LAMBDA_INIT)
    hw = 2 * DIFF_HEAD_DIM
    a = (acc_ref[0, :hw, :] / acc_ref[0, hw:hw + 1, :]
         - lam * (acc_ref[1, :hw, :] / acc_ref[1, hw:hw + 1, :]))
    a = a * lax.rsqrt(jnp.mean(a * a, axis=0, keepdims=True) + NORM_EPS)
    o_ref[...] = (a.T * sw_ref[...] * (1.0 - LAMBDA_INIT)).astype(o_ref.dtype)


def _diff_attention(q, k, vt, rel_bias, lq1, lk1, lq2, lk2, subln_w, batch, seq):
    t = q.shape[0]
    blk = min(ATT_BLOCK, seq)
    nq = seq // blk
    hw = 2 * DIFF_HEAD_DIM
    b0, b1 = _bias_tiles(rel_bias, blk)
    small = lambda a: pl.BlockSpec(a.shape, lambda b, h, i: (0,) * a.ndim)
    return pl.pallas_call(
        _attn_kernel,
        grid=(batch, DIFF_HEADS, nq),
        in_specs=[pl.BlockSpec((blk, hw), lambda b, h, i: (b * nq + i, h)),
                  pl.BlockSpec((seq, hw), lambda b, h, i: (b, h)),
                  pl.BlockSpec((1, 1, nq, hw + ONES_ROWS, blk), lambda b, h, i: (b, h, 0, 0, 0)),
                  pl.BlockSpec((1, blk, blk), lambda b, h, i: (h, 0, 0)),
                  pl.BlockSpec((1, blk, blk), lambda b, h, i: (h, 0, 0)),
                  small(lq1), small(lk1), small(lq2), small(lk2), small(subln_w)],
        out_specs=pl.BlockSpec((blk, hw), lambda b, h, i: (b * nq + i, h)),
        out_shape=jax.ShapeDtypeStruct((t, DIFF_V_W), BF16),
        scratch_shapes=[pltpu.VMEM((2, 1, blk), F32), pltpu.VMEM((2, hw + ONES_ROWS, blk), F32)],
        compiler_params=pltpu.CompilerParams(
            dimension_semantics=("arbitrary", "arbitrary", "arbitrary"),
            vmem_limit_bytes=VMEM_LIMIT),
        name="diff_attn",
    )(q, k, vt, b0, b1, lq1, lk1, lq2, lk2, subln_w)


def _out_kernel(yr_ref, yd_ref, x_ref, mod_ref, nw_ref, wo_ref, wr_ref, br_ref, upper_ref,
                x1_ref, hp_ref, meta_ref, gate_ref, cnt_ref, run_ref):
    tm = x_ref.shape[0]
    ne = run_ref.shape[0]

    @pl.when(pl.program_id(0) == 0)
    def _():
        run_ref[...] = jnp.zeros_like(run_ref)

    mixed = (jnp.dot(yr_ref[...], wo_ref[0:RET_V_W, :], preferred_element_type=F32)
             + jnp.dot(yd_ref[...], wo_ref[RET_V_W:, :], preferred_element_type=F32))
    gate1 = mod_ref[0, 2:3, :]
    shift2 = mod_ref[0, 3:4, :]
    scale2 = mod_ref[0, 4:5, :]
    x1 = x_ref[...] + gate1 * mixed
    x1_ref[...] = x1
    h2 = (_rms(x1) * nw_ref[...]) * (1.0 + scale2) + shift2
    hp_ref[...] = _pack_halves(h2)

    h_hi = h2.astype(BF16)
    h_lo = (h2 - h_hi.astype(F32)).astype(BF16)
    nt_dims = (((1,), (1,)), ((), ()))
    both = lax.dot_general(wr_ref[...], h_hi, nt_dims, preferred_element_type=F32)
    low = lax.dot_general(wr_ref[0:ne, :], h_lo, nt_dims, preferred_element_type=F32)
    logits = both[0:ne, :] + both[ne:, :] + low + br_ref[...]

    row = lax.broadcasted_iota(jnp.int32, logits.shape, 0)
    work = logits
    vals, idxs, hots = [], [], []
    for _ in range(TOP_K):
        mx = jnp.max(work, axis=0, keepdims=True)
        idx = jnp.min(jnp.where(work == mx, row, ne), axis=0, keepdims=True)
        hot = row == idx
        vals.append(mx)
        idxs.append(idx)
        hots.append(hot)
        work = jnp.where(hot, -jnp.inf, work)
    exps = [jnp.exp(v - vals[0]) for v in vals]
    denom = exps[0] + exps[1] + exps[2] + exps[3]

    sel = jnp.zeros(logits.shape, F32)
    for hot in hots:
        sel = sel + hot.astype(F32)
    prefix = jnp.dot(sel.astype(BF16), upper_ref[...], preferred_element_type=F32) + run_ref[...]
    ranks = [jnp.sum(jnp.where(hot, prefix, 0.0), axis=0, keepdims=True) for hot in hots]
    run_ref[...] = run_ref[...] + jnp.sum(sel, axis=1, keepdims=True)
    cnt_ref[...] = jnp.broadcast_to(run_ref[...], cnt_ref.shape).astype(jnp.int32)

    meta_ref[...] = jnp.concatenate(idxs + [r.astype(jnp.int32) for r in ranks], axis=0)
    gate_ref[...] = jnp.concatenate([e / denom for e in exps] + [jnp.zeros_like(denom)] * TOP_K, axis=0)


def _out_router(y_r, y_d, x2, mod, norm_w, w_out_bf16, w_router, b_router, seq):
    t, d = x2.shape
    tm = min(ROW_TILE, seq)
    per_batch = seq // tm
    ne = w_router.shape[1]
    w_hi = w_router.astype(BF16)
    w_lo = (w_router - w_hi.astype(F32)).astype(BF16)
    wr_t = jnp.concatenate([w_hi, w_lo], axis=1).T
    idx = jnp.arange(tm, dtype=jnp.int32)
    upper = (idx[:, None] < idx[None, :]).astype(BF16)
    row = lambda w: pl.BlockSpec((tm, w), lambda i: (i, 0))
    col = lambda h: pl.BlockSpec((h, tm), lambda i: (0, i))
    const = lambda a: pl.BlockSpec(a.shape, lambda i: (0,) * a.ndim)
    x1, hp, meta_t, gates_t, counts = pl.pallas_call(
        _out_kernel,
        grid=(t // tm,),
        in_specs=[row(RET_V_W), row(DIFF_V_W), row(d),
                  pl.BlockSpec((1, 6, d), lambda i: (i // per_batch, 0, 0)),
                  const(norm_w), const(w_out_bf16), const(wr_t), const(b_router), const(upper)],
        out_specs=[row(d), row(d // 2), col(2 * TOP_K), col(2 * TOP_K),
                   pl.BlockSpec((ne, LANES), lambda i: (0, 0))],
        out_shape=[jax.ShapeDtypeStruct((t, d), F32),
                   jax.ShapeDtypeStruct((t, d // 2), jnp.uint32),
                   jax.ShapeDtypeStruct((2 * TOP_K, t), jnp.int32),
                   jax.ShapeDtypeStruct((2 * TOP_K, t), F32),
                   jax.ShapeDtypeStruct((ne, LANES), jnp.int32)],
        scratch_shapes=[pltpu.VMEM((ne, 1), F32)],
        compiler_params=pltpu.CompilerParams(
            dimension_semantics=("arbitrary",), vmem_limit_bytes=VMEM_LIMIT),
        name="out_router",
    )(y_r, y_d, x2, mod, norm_w, w_out_bf16, wr_t, b_router, upper)
    return x1, hp, meta_t.T, gates_t.T, counts[:, 0]


def _w1_prep_kernel(w_ref, p_ref, o_ref):
    for s in range(w_ref.shape[2] // PERM_TILE):
        cols = slice(s * PERM_TILE, (s + 1) * PERM_TILE)
        o_ref[0, :, cols] = jnp.dot(w_ref[0, :, cols].astype(BF16), p_ref[...],
                                    preferred_element_type=F32).astype(BF16)


def _pair_split_matrix():
    i = jnp.arange(PERM_TILE)[:, None]
    j = jnp.arange(PERM_TILE)[None, :]
    half = PERM_TILE // 2
    src = jnp.where(j < half, 2 * j, 2 * (j - half) + 1)
    return (i == src).astype(BF16)


def _w1_prep(w1):
    e, d, f2 = w1.shape
    tn = 1024
    return pl.pallas_call(
        _w1_prep_kernel,
        grid=(e, f2 // tn),
        in_specs=[pl.BlockSpec((1, d, tn), lambda i, j: (i, 0, j)),
                  pl.BlockSpec((PERM_TILE, PERM_TILE), lambda i, j: (0, 0))],
        out_specs=pl.BlockSpec((1, d, tn), lambda i, j: (i, 0, j)),
        out_shape=jax.ShapeDtypeStruct((e, d, f2), BF16),
        compiler_params=pltpu.CompilerParams(vmem_limit_bytes=VMEM_LIMIT),
        name="w1_prep",
    )(w1, _pair_split_matrix())


def _pair_split_bias(b1):
    e, f2 = b1.shape
    nt = f2 // PERM_TILE
    g = b1[:, 0::2].reshape(e, nt, 1, PERM_TILE // 2)
    l = b1[:, 1::2].reshape(e, nt, 1, PERM_TILE // 2)
    return jnp.concatenate([g, l], axis=2).reshape(e, 1, f2)


def _slot_tokens(dest, n_rows):
    assert TOP_K == 4
    flat = dest.reshape(-1)
    n_slots = flat.shape[0]
    chunk = min(SC_SLOT_CHUNK, n_slots)
    mesh = plsc.VectorSubcoreMesh(core_axis_name="c", subcore_axis_name="s",
                                  num_cores=SC_CORES, num_subcores=SC_SUBCORES)

    def body(dest_hbm, out_hbm, out_v, dest_v):
        first = jnp.logical_and(lax.axis_index("c") == 0, lax.axis_index("s") == 0)

        @pl.when(first)
        def _():
            zeros = jnp.zeros((SC_LANES,), jnp.int32)

            @pl.loop(0, n_rows // SC_LANES)
            def _(g):
                out_v[pl.ds(g * SC_LANES, SC_LANES)] = zeros

            lane = lax.iota(jnp.int32, SC_LANES)

            @pl.loop(0, n_slots // chunk)
            def _(c):
                pltpu.sync_copy(dest_hbm.at[pl.ds(c * chunk, chunk)], dest_v)

                @pl.loop(0, chunk // SC_LANES)
                def _(g):
                    rows = dest_v[pl.ds(g * SC_LANES, SC_LANES)]
                    slot = lane + (c * chunk + g * SC_LANES)
                    plsc.store_scatter(out_v, [rows], lax.shift_right_logical(slot, 2))

            pltpu.sync_copy(out_v, out_hbm)

    return pl.kernel(
        body,
        out_type=jax.ShapeDtypeStruct((n_rows,), jnp.int32),
        mesh=mesh,
        scratch_types=[pltpu.VMEM((n_rows,), jnp.int32), pltpu.VMEM((chunk,), jnp.int32)],
        compiler_params=pltpu.CompilerParams(needs_layout_passes=False),
        name="slot_tokens",
    )(flat)


def _expert_kernel(be_ref, nv_ref, tcur_ref, tnxt_ref, hp_ref, w1_ref, b1_ref, w2_ref, b2_ref,
                   o_ref, hbuf, xb0, xb1, w2b, sem):
    b = pl.program_id(0)
    nv = nv_ref[0]
    bm = xb0.shape[0]

    def gather(tok_ref, dst):
        for r in range(bm):
            dst[pl.ds(r, 1), :] = hbuf[pl.ds(tok_ref[0, 0, r], 1), :]

    @pl.when(b == 0)
    def _():
        load = pltpu.make_async_copy(hp_ref, hbuf, sem)
        load.start()
        load.wait()
        gather(tcur_ref, xb0)

    last = jnp.maximum(b - 1, 0)

    @pl.when(jnp.logical_and(b < nv, jnp.logical_or(b == 0, be_ref[b] != be_ref[last])))
    def _():
        w2b[...] = w2_ref[0].astype(BF16)

    def run(cur, nxt):
        x = _unpack_halves(cur[...])
        half = PERM_TILE // 2
        acts = []
        for j in range(w1_ref.shape[2] // PERM_TILE):
            cols = slice(j * PERM_TILE, (j + 1) * PERM_TILE)
            h = jnp.dot(x, w1_ref[0, :, cols], preferred_element_type=F32) + b1_ref[0, :, cols]
            glu = jnp.minimum(h[:, :half], SWIGLU_LIMIT)
            lin = jnp.clip(h[:, half:], -SWIGLU_LIMIT, SWIGLU_LIMIT)
            acts.append((glu * jax.nn.sigmoid(SWIGLU_ALPHA * glu) * (lin + 1.0)).astype(BF16))
        act = jnp.concatenate(acts, axis=1)
        y = jnp.dot(act, w2b[...], preferred_element_type=F32) + b2_ref[0]
        _store_row_tiles(o_ref, y)
        gather(tnxt_ref, nxt)

    even = lax.rem(b, 2) == 0

    @pl.when(jnp.logical_and(b < nv, even))
    def _():
        run(xb0, xb1)

    @pl.when(jnp.logical_and(b < nv, jnp.logical_not(even)))
    def _():
        run(xb1, xb0)

    @pl.when(b >= nv)
    def _():
        o_ref[...] = jnp.zeros_like(o_ref)


def _experts(block_e, n_valid, slot_tok, hp, w1p, b1p, w2, b2):
    d = D_MODEL
    f2 = w1p.shape[2]
    f = w2.shape[1]
    bm = EXPERT_ROWS
    nb = slot_tok.shape[0] // bm
    tok3 = slot_tok.reshape(nb, 1, bm)
    exp = lambda b, be, nv: be[jnp.maximum(jnp.minimum(b, nv[0] - 1), 0)]
    grid_spec = pltpu.PrefetchScalarGridSpec(
        num_scalar_prefetch=2,
        grid=(nb,),
        in_specs=[pl.BlockSpec((1, 1, bm), lambda b, be, nv: (b, 0, 0), memory_space=pltpu.SMEM),
                  pl.BlockSpec((1, 1, bm), lambda b, be, nv: (jnp.minimum(b + 1, nb - 1), 0, 0),
                               memory_space=pltpu.SMEM),
                  pl.BlockSpec(memory_space=pl.ANY),
                  pl.BlockSpec((1, d, f2), lambda b, be, nv: (exp(b, be, nv), 0, 0)),
                  pl.BlockSpec((1, 1, f2), lambda b, be, nv: (exp(b, be, nv), 0, 0)),
                  pl.BlockSpec((1, f, d), lambda b, be, nv: (exp(b, be, nv), 0, 0)),
                  pl.BlockSpec((1, 1, d), lambda b, be, nv: (exp(b, be, nv), 0, 0))],
        out_specs=pl.BlockSpec((bm * ROW_SUB, LANES), lambda b, be, nv: (b, 0)),
        scratch_shapes=[pltpu.VMEM(hp.shape, hp.dtype), pltpu.VMEM((bm, hp.shape[1]), hp.dtype),
                        pltpu.VMEM((bm, hp.shape[1]), hp.dtype), pltpu.VMEM(w2.shape[1:], BF16),
                        pltpu.SemaphoreType.DMA(())],
    )
    return pl.pallas_call(
        _expert_kernel,
        grid_spec=grid_spec,
        out_shape=jax.ShapeDtypeStruct((nb * bm * ROW_SUB, LANES), F32),
        compiler_params=pltpu.CompilerParams(
            dimension_semantics=("arbitrary",), vmem_limit_bytes=VMEM_LIMIT),
        name="experts",
    )(block_e, n_valid, tok3, tok3, hp, w1p, b1p, w2, b2)


def _combine_kernel(dcur_ref, dnxt_ref, gate_ref, x1_ref, mod_ref, nf_ref, y_ref, o_ref, buf, sems):
    nt = x1_ref.shape[0]
    i = pl.program_id(0)
    n = pl.num_programs(0)
    slot = lax.rem(i, 2)

    def row_copy(src_row, s, kk, r):
        return pltpu.make_async_copy(y_ref.at[pl.ds(src_row * ROW_SUB, ROW_SUB), :],
                                     buf.at[s, kk, pl.ds(r * ROW_SUB, ROW_SUB), :], sems.at[s])

    def fetch(d_ref, s):
        for j in range(nt):
            for kk in range(TOP_K):
                row_copy(d_ref[0, 0, j * TOP_K + kk], s, kk, j).start(priority=kk % DMA_QUEUES)

    def wait_all(s):
        for kk in range(TOP_K):
            pltpu.make_async_copy(y_ref.at[pl.ds(0, nt * ROW_SUB), :], buf.at[s, kk], sems.at[s]).wait()

    @pl.when(i == 0)
    def _():
        fetch(dcur_ref, 0)

    fetch(dnxt_ref, 1 - slot)
    wait_all(slot)

    g = gate_ref[...]
    parts = []
    for s in range(ROW_SUB):
        acc = buf[slot, 0, pl.ds(s, nt, stride=ROW_SUB), :] * g[:, 0:1]
        for kk in range(1, TOP_K):
            acc = acc + buf[slot, kk, pl.ds(s, nt, stride=ROW_SUB), :] * g[:, kk:kk + 1]
        parts.append(acc)
    moe = jnp.concatenate(parts, axis=1)
    gate2 = mod_ref[0, 5:6, :]
    x2 = x1_ref[...] + gate2 * moe
    o_ref[...] = _rms(x2) * nf_ref[...]

    @pl.when(i == n - 1)
    def _():
        wait_all(1 - slot)


def _combine(dest, gates, x1, mod, normf_w, y, seq):
    t, d = x1.shape
    nt = min(COMBINE_TOKENS, seq)
    steps = t // nt
    per_batch = seq // nt
    dest2 = dest.reshape(steps, 1, nt * TOP_K)
    return pl.pallas_call(
        _combine_kernel,
        grid=(steps,),
        in_specs=[pl.BlockSpec((1, 1, nt * TOP_K), lambda i: (i, 0, 0), memory_space=pltpu.SMEM),
                  pl.BlockSpec((1, 1, nt * TOP_K), lambda i: (jnp.minimum(i + 1, steps - 1), 0, 0),
                               memory_space=pltpu.SMEM),
                  pl.BlockSpec((nt, 2 * TOP_K), lambda i: (i, 0)),
                  pl.BlockSpec((nt, d), lambda i: (i, 0)),
                  pl.BlockSpec((1, 6, d), lambda i: (i // per_batch, 0, 0)),
                  pl.BlockSpec((1, d), lambda i: (0, 0)),
                  pl.BlockSpec(memory_space=pl.ANY)],
        out_specs=pl.BlockSpec((nt, d), lambda i: (i, 0)),
        out_shape=jax.ShapeDtypeStruct((t, d), F32),
        scratch_shapes=[pltpu.VMEM((2, TOP_K, nt * ROW_SUB, LANES), F32),
                        pltpu.SemaphoreType.DMA((2,))],
        compiler_params=pltpu.CompilerParams(
            dimension_semantics=("arbitrary",), vmem_limit_bytes=VMEM_LIMIT),
        name="combine",
    )(dest2, dest2, gates, x1, mod, normf_w, y)


def kernel(x, c, w_ada, b_ada, norm1_w, w_in, lam_q1, lam_k1, lam_q2, lam_k2, subln_w, rel_bias,
           w_out, norm2_w, w_router, b_router, w1, b1, w2, b2, normf_w):
    batch, seq, d = x.shape
    t = batch * seq
    x2 = x.reshape(t, d)

    c_pad = jnp.zeros((8, d), F32).at[:batch].set(c)
    mod = _ada(c_pad, w_ada[0], b_ada[0][None, :])[:batch].reshape(batch, 6, d)

    q_r, k_r, v_r, g_r, q_d, k_d, vt_d = _in_proj(x2, mod, norm1_w[0][None, :],
                                                 w_in[0].astype(BF16), seq)
    y_r = _retention(q_r, k_r, v_r, g_r, batch, seq)
    y_d = _diff_attention(q_d, k_d, vt_d, rel_bias, lam_q1, lam_k1, lam_q2, lam_k2,
                          subln_w, batch, seq)

    x1, hp, meta, gates, counts = _out_router(y_r, y_d, x2, mod, norm2_w[0][None, :],
                                              w_out[0].astype(BF16), w_router[0],
                                              b_router[0][:, None], seq)

    bm = EXPERT_ROWS
    padded = (counts + bm - 1) // bm * bm
    pad_end = jnp.cumsum(padded)
    pad_start = pad_end - padded
    sel_e = meta[:, :TOP_K]
    hot_e = sel_e[:, :, None] == jnp.arange(N_EXPERTS, dtype=jnp.int32)[None, None, :]
    dest = jnp.sum(jnp.where(hot_e, pad_start[None, None, :], 0), axis=-1) + meta[:, TOP_K:]
    n_rows = (t * TOP_K // bm + N_EXPERTS) * bm
    nb = n_rows // bm
    block_start = jnp.arange(nb, dtype=jnp.int32) * bm
    block_e = jnp.minimum(jnp.sum((pad_end[None, :] <= block_start[:, None]).astype(jnp.int32), axis=1),
                          N_EXPERTS - 1)
    n_valid = (pad_end[-1:] // bm).astype(jnp.int32)
    slot_tok = _slot_tokens(dest, n_rows)

    ys = _experts(block_e, n_valid, slot_tok, hp, _w1_prep(w1[0]), _pair_split_bias(b1[0]),
                  w2[0], b2[0][:, None, :])
    out = _combine(dest, gates, x1, mod, normf_w[None, :], ys, seq)
    return out.reshape(batch, seq, d)
```

```python
import math

import jax
import jax.numpy as jnp
from jax import lax
from jax.experimental import pallas as pl
from jax.experimental.pallas import tpu as pltpu
from jax.experimental.pallas import tpu_sc as plsc

F32 = jnp.float32
BF16 = jnp.bfloat16

D_MODEL = 1024
RET_HEADS = 4
RET_KEY_DIM = 64
RET_VAL_DIM = 128
RET_QK_W = RET_HEADS * RET_KEY_DIM
RET_V_W = RET_HEADS * RET_VAL_DIM
RET_CHUNK = 128
DIFF_HEADS = 4
DIFF_HEAD_DIM = 64
DIFF_QK_W = DIFF_HEADS * 2 * DIFF_HEAD_DIM
DIFF_V_W = DIFF_HEADS * 2 * DIFF_HEAD_DIM
IN_SIZES = (RET_QK_W, RET_QK_W, RET_V_W, RET_V_W, DIFF_QK_W, DIFF_QK_W, DIFF_V_W)
REL_BUCKETS = 32
REL_MAX_DIST = 128
N_EXPERTS = 32
TOP_K = 4
SWIGLU_ALPHA = 1.702
SWIGLU_LIMIT = 7.0
NORM_EPS = 1e-6
LAMBDA_INIT = 0.8 - 0.6 * math.exp(-0.3 * 0)

LANES = 128
ROW_SUB = D_MODEL // LANES
NEG_BIG = -1e30
LOG2_E = math.log2(math.e)
ONES_ROWS = 16
VMEM_LIMIT = 56 * 1024 * 1024

ROW_TILE = 512
RET_ROWS = 512
ATT_BLOCK = 512
ATT_STRIP = 256
EXPERT_ROWS = 256
PERM_TILE = 256
COMBINE_TOKENS = 256
DMA_QUEUES = 2
SC_CORES = 2
SC_SUBCORES = 16
SC_LANES = 16
SC_SLOT_CHUNK = 4096


def _rms(x):
    return x * lax.rsqrt(jnp.mean(x * x, axis=-1, keepdims=True) + NORM_EPS)


def _store_row_tiles(ref, x):
    rows = x.shape[0]
    for s in range(ROW_SUB):
        ref[pl.ds(s, rows, stride=ROW_SUB), :] = x[:, s * LANES:(s + 1) * LANES]


def _pack_halves(x):
    half = x.shape[1] // 2
    lo = lax.bitcast_convert_type(x[:, :half].astype(BF16).astype(F32), jnp.uint32)
    hi = lax.bitcast_convert_type(x[:, half:].astype(BF16).astype(F32), jnp.uint32)
    return (hi & jnp.uint32(0xFFFF0000)) | (lo >> 16)


def _unpack_halves(w):
    lo = lax.bitcast_convert_type(w << 16, F32).astype(BF16)
    hi = lax.bitcast_convert_type(w & jnp.uint32(0xFFFF0000), F32).astype(BF16)
    return jnp.concatenate([lo, hi], axis=1)


def _ada_kernel(c_ref, w_ref, b_ref, o_ref):
    c = c_ref[...]
    cond = c * jax.nn.sigmoid(c)
    o_ref[...] = jnp.dot(cond, w_ref[...], precision=lax.Precision.HIGHEST,
                         preferred_element_type=F32) + b_ref[...]


def _ada(c_pad, w_ada, b_ada):
    rows, d = c_pad.shape
    n = w_ada.shape[1]
    tn = 1024
    return pl.pallas_call(
        _ada_kernel,
        grid=(n // tn,),
        in_specs=[pl.BlockSpec((rows, d), lambda j: (0, 0)),
                  pl.BlockSpec((d, tn), lambda j: (0, j)),
                  pl.BlockSpec((1, tn), lambda j: (0, j))],
        out_specs=pl.BlockSpec((rows, tn), lambda j: (0, j)),
        out_shape=jax.ShapeDtypeStruct((rows, n), F32),
        name="ada",
    )(c_pad, w_ada, b_ada)


def _in_proj_kernel(x_ref, mod_ref, nw_ref, w_ref, *o_refs):
    x = x_ref[...]
    shift = mod_ref[0, 0:1, :]
    scale = mod_ref[0, 1:2, :]
    h = (_rms(x) * nw_ref[...]) * (1.0 + scale) + shift
    hb = h.astype(BF16)
    off = 0
    for o_ref, width in zip(o_refs[:-1], IN_SIZES[:-1]):
        o_ref[...] = jnp.dot(hb, w_ref[:, off:off + width],
                             preferred_element_type=F32).astype(o_ref.dtype)
        off += width
    vt_ref = o_refs[-1]
    v = jnp.dot(hb, w_ref[:, off:], preferred_element_type=F32)
    hw = 2 * DIFF_HEAD_DIM
    tail = (lax.broadcasted_iota(jnp.int32, (ONES_ROWS, v.shape[0]), 0) == 0).astype(vt_ref.dtype)
    for hd in range(DIFF_HEADS):
        vt_ref[0, hd, 0, 0:hw, :] = v[:, hd * hw:(hd + 1) * hw].T.astype(vt_ref.dtype)
        vt_ref[0, hd, 0, hw:, :] = tail


def _in_proj(x2, mod, norm_w, w_in_bf16, seq):
    t, d = x2.shape
    tm = min(ROW_TILE, seq)
    assert tm == min(ATT_BLOCK, seq)
    per_batch = seq // tm
    in_w = w_in_bf16.shape[1]
    hw = 2 * DIFF_HEAD_DIM
    vt_shape = (t // seq, DIFF_HEADS, per_batch, hw + ONES_ROWS, tm)
    return pl.pallas_call(
        _in_proj_kernel,
        grid=(t // tm,),
        in_specs=[pl.BlockSpec((tm, d), lambda i: (i, 0)),
                  pl.BlockSpec((1, 6, d), lambda i: (i // per_batch, 0, 0)),
                  pl.BlockSpec((1, d), lambda i: (0, 0)),
                  pl.BlockSpec((d, in_w), lambda i: (0, 0))],
        out_specs=([pl.BlockSpec((tm, w), lambda i: (i, 0)) for w in IN_SIZES[:-1]]
                   + [pl.BlockSpec((1,) + vt_shape[1:2] + (1,) + vt_shape[3:],
                                   lambda i: (i // per_batch, 0, i % per_batch, 0, 0))]),
        out_shape=([jax.ShapeDtypeStruct((t, w), BF16) for w in IN_SIZES[:-1]]
                   + [jax.ShapeDtypeStruct(vt_shape, BF16)]),
        compiler_params=pltpu.CompilerParams(vmem_limit_bytes=VMEM_LIMIT),
        name="in_proj",
    )(x2, mod, norm_w, w_in_bf16)


def _rotary(x, cos, sin_even, sin_odd):
    nxt = pltpu.roll(x, LANES - 1, 1)
    prv = pltpu.roll(x, 1, 1)
    return x * cos + nxt * sin_even + prv * sin_odd


def _ret_kernel(q_ref, k_ref, v_ref, g_ref, cos_ref, sine_ref, sino_ref,
                hmask_ref, xi_ref, zeta_ref, dmask_ref, gch_ref, o_ref, state_ref):
    @pl.when(pl.program_id(1) == 0)
    def _():
        state_ref[...] = jnp.zeros_like(state_ref)

    n_sub = q_ref.shape[0] // RET_CHUNK
    for c in range(n_sub):
        rows = slice(c * RET_CHUNK, (c + 1) * RET_CHUNK)
        for pair in range(RET_HEADS // 2):
            lanes = slice(pair * LANES, (pair + 1) * LANES)
            cos = cos_ref[rows, :]
            sine = sine_ref[rows, :]
            sino = sino_ref[rows, :]
            qr = _rotary(q_ref[rows, lanes].astype(F32), cos, sine, sino)
            kr = _rotary(k_ref[rows, lanes].astype(F32), cos, sine, sino) * (RET_KEY_DIM ** -0.5)
            qb = qr.astype(BF16)
            for hh in range(2):
                h = 2 * pair + hh
                vcols = slice(h * RET_VAL_DIM, (h + 1) * RET_VAL_DIM)
                v = v_ref[rows, vcols]
                km = (kr * hmask_ref[h]).astype(BF16)
                scores = lax.dot_general(qb, km, (((1,), (1,)), ((), ())),
                                         preferred_element_type=F32) * dmask_ref[h]
                inner = jnp.dot(scores.astype(BF16), v, preferred_element_type=F32)
                qx = (qr * xi_ref[h]).astype(BF16)
                state = state_ref[h]
                cross = jnp.dot(qx, state.astype(BF16), preferred_element_type=F32)
                kz = (kr * zeta_ref[h]).astype(BF16)
                kv = lax.dot_general(kz, v, (((0,), (0,)), ((), ())),
                                     preferred_element_type=F32)
                state_ref[h] = state * gch_ref[h] + kv
                y = _rms(inner + cross)
                g = g_ref[rows, vcols].astype(F32)
                o_ref[rows, vcols] = (g * jax.nn.sigmoid(g) * y).astype(o_ref.dtype)


def _retention_tables(seq):
    dk, c, nh = RET_KEY_DIM, RET_CHUNK, RET_HEADS
    pos = jnp.arange(seq, dtype=F32)
    inv_freq = 1.0 / (10000.0 ** jnp.linspace(0.0, 1.0, dk // 2, dtype=F32))
    ang = pos[:, None] * jnp.repeat(inv_freq, 2)[None, :]
    sin = jnp.tile(jnp.sin(ang), (1, LANES // dk))
    cos = jnp.tile(jnp.cos(ang), (1, LANES // dk))
    even = (jnp.arange(LANES) % 2 == 0)[None, :]
    sin_even = jnp.where(even, -sin, 0.0)
    sin_odd = jnp.where(even, 0.0, sin)
    log_g = jnp.log(1.0 - 2.0 ** (-5.0 - jnp.arange(nh, dtype=F32)))
    i = jnp.arange(c, dtype=F32)
    rel = i[:, None] - i[None, :]
    dmask = jnp.where(rel[None] >= 0,
                      jnp.exp(jnp.maximum(rel, 0.0)[None] * log_g[:, None, None]), 0.0)
    zeta = jnp.exp((c - 1.0 - i)[None, :] * log_g[:, None])
    xi = jnp.exp((i + 1.0)[None, :] * log_g[:, None])
    g_chunk = jnp.exp(c * log_g)
    lane = jnp.arange(LANES)
    hmask = jnp.stack([((lane // dk) == (h % 2)).astype(F32) for h in range(nh)])[:, None, :]
    xi_t = xi[:, :, None] * hmask
    zeta_t = zeta[:, :, None] * hmask
    gch = jnp.broadcast_to(g_chunk[:, None, None], (nh, 1, LANES))
    return cos, sin_even, sin_odd, hmask, xi_t, zeta_t, dmask, gch


def _retention(q, k, v, g, batch, seq):
    t = q.shape[0]
    rb = min(RET_ROWS, seq)
    per_batch = seq // rb
    cos, sin_even, sin_odd, hmask, xi_t, zeta_t, dmask, gch = _retention_tables(seq)
    row = lambda w: pl.BlockSpec((rb, w), lambda b, j: (b * per_batch + j, 0))
    tab = lambda w: pl.BlockSpec((rb, w), lambda b, j: (j, 0))
    full = lambda a: pl.BlockSpec(a.shape, lambda b, j: (0,) * a.ndim)
    return pl.pallas_call(
        _ret_kernel,
        grid=(batch, per_batch),
        in_specs=[row(RET_QK_W), row(RET_QK_W), row(RET_V_W), row(RET_V_W),
                  tab(LANES), tab(LANES), tab(LANES),
                  full(hmask), full(xi_t), full(zeta_t), full(dmask), full(gch)],
        out_specs=row(RET_V_W),
        out_shape=jax.ShapeDtypeStruct((t, RET_V_W), BF16),
        scratch_shapes=[pltpu.VMEM((RET_HEADS, LANES, RET_VAL_DIM), F32)],
        compiler_params=pltpu.CompilerParams(
            dimension_semantics=("arbitrary", "arbitrary"), vmem_limit_bytes=VMEM_LIMIT),
        name="retention",
    )(q, k, v, g, cos, sin_even, sin_odd, hmask, xi_t, zeta_t, dmask, gch)


def _t5_bucket(rel):
    n = jnp.maximum(rel, 0)
    max_exact = REL_BUCKETS // 2
    nf = jnp.maximum(n, 1).astype(F32)
    large = max_exact + (jnp.log(nf / max_exact) / math.log(REL_MAX_DIST / max_exact)
                         * (REL_BUCKETS - max_exact)).astype(jnp.int32)
    large = jnp.minimum(large, REL_BUCKETS - 1)
    return jnp.where(n < max_exact, n, large)


def _bias_tiles(rel_bias, blk):
    r = jnp.arange(blk, dtype=jnp.int32)
    far = rel_bias[REL_BUCKETS - 1]
    rel0 = r[None, :] - r[:, None]
    rel1 = rel0 + blk
    buckets = jnp.arange(REL_BUCKETS, dtype=jnp.int32)

    def tile(rel):
        hot = (_t5_bucket(rel)[:, :, None] == buckets).astype(F32)
        return jnp.einsum('krb,bh->hkr', hot, rel_bias, precision=lax.Precision.HIGHEST)

    b0 = jnp.where(rel0[None] >= 0, (tile(rel0) - far[:, None, None]) * LOG2_E, NEG_BIG)
    b1 = (tile(rel1) - far[:, None, None]) * LOG2_E
    return b0, b1


def _attn_kernel(q_ref, k_ref, vt_ref, b0_ref, b1_ref, lq1_ref, lk1_ref, lq2_ref, lk2_ref,
                 sw_ref, o_ref, m_ref, acc_ref):
    blk = q_ref.shape[0]
    i = pl.program_id(2)
    lane = lax.broadcasted_iota(jnp.int32, (1, LANES), 1)
    q = (q_ref[...].astype(F32) * (DIFF_HEAD_DIM ** -0.5 * LOG2_E)).astype(BF16)
    zero = jnp.zeros_like(q)
    qm = (jnp.where(lane < DIFF_HEAD_DIM, q, zero), jnp.where(lane >= DIFF_HEAD_DIM, q, zero))

    m_ref[...] = jnp.full_like(m_ref, NEG_BIG)
    acc_ref[...] = jnp.zeros_like(acc_ref)

    def step(blocks):
        kbs = [k_ref[pl.ds(pl.multiple_of(j * blk, blk), blk), :] for j, _, _ in blocks]
        vts = [vt_ref[0, 0, j] for j, _, _ in blocks]
        chains = [(mi, qs) for mi in range(2) for qs in range(blk // ATT_STRIP)]
        scores = []
        for mi, qs in chains:
            qc = slice(qs * ATT_STRIP, (qs + 1) * ATT_STRIP)
            row = []
            for (_, bias, diagonal), kb in zip(blocks, kbs):
                keys = (qs + 1) * ATT_STRIP if diagonal else blk
                s = lax.dot_general(kb[:keys], qm[mi][qc, :], (((1,), (1,)), ((), ())),
                                    preferred_element_type=F32)
                row.append(s if bias is None else s + bias[:keys, qc])
            scores.append(row)
        stats = []
        for (mi, qs), row in zip(chains, scores):
            qc = slice(qs * ATT_STRIP, (qs + 1) * ATT_STRIP)
            m_old = m_ref[mi, :, qc]
            m_new = m_old
            for s in row:
                m_new = jnp.maximum(m_new, jnp.max(s, axis=0, keepdims=True))
            stats.append((jnp.exp2(m_old - m_new), [jnp.exp2(s - m_new).astype(BF16) for s in row], m_new))
        for (mi, qs), (alpha, ps, m_new) in zip(chains, stats):
            qc = slice(qs * ATT_STRIP, (qs + 1) * ATT_STRIP)
            pv = None
            for vt, p in zip(vts, ps):
                part = jnp.dot(vt[:, :p.shape[0]], p, preferred_element_type=F32)
                pv = part if pv is None else pv + part
            acc_ref[mi, :, qc] = alpha * acc_ref[mi, :, qc] + pv
            m_ref[mi, :, qc] = m_new

    n_far = jnp.maximum(i - 1, 0)

    def far_pair(pair, carry):
        step([(2 * pair, None, False), (2 * pair + 1, None, False)])
        return carry

    lax.fori_loop(0, lax.shift_right_logical(n_far, 1), far_pair, 0)

    @pl.when(lax.rem(n_far, 2) == 1)
    def _():
        step([(n_far - 1, None, False)])

    @pl.when(i >= 1)
    def _():
        step([(i - 1, b1_ref[0], False), (i, b0_ref[0], True)])

    @pl.when(i == 0)
    def _():
        step([(i, b0_ref[0], True)])

    lam = (jnp.exp(jnp.sum(lq1_ref[...] * lk1_ref[...], axis=-1, keepdims=True))
           - jnp.exp(jnp.sum(lq2_ref[...] * lk2_ref[...], axis=-1, keepdims=True))
           + LAMBDA_INIT)
    hw = 2 * DIFF_HEAD_DIM
    a = (acc_ref[0, :hw, :] / acc_ref[0, hw:hw + 1, :]
         - lam * (acc_ref[1, :hw, :] / acc_ref[1, hw:hw + 1, :]))
    a = a * lax.rsqrt(jnp.mean(a * a, axis=0, keepdims=True) + NORM_EPS)
    o_ref[...] = (a.T * sw_ref[...] * (1.0 - LAMBDA_INIT)).astype(o_ref.dtype)


def _diff_attention(q, k, vt, rel_bias, lq1, lk1, lq2, lk2, subln_w, batch, seq):
    t = q.shape[0]
    blk = min(ATT_BLOCK, seq)
    nq = seq // blk
    hw = 2 * DIFF_HEAD_DIM
    b0, b1 = _bias_tiles(rel_bias, blk)
    small = lambda a: pl.BlockSpec(a.shape, lambda b, h, i: (0,) * a.ndim)
    return pl.pallas_call(
        _attn_kernel,
        grid=(batch, DIFF_HEADS, nq),
        in_specs=[pl.BlockSpec((blk, hw), lambda b, h, i: (b * nq + i, h)),
                  pl.BlockSpec((seq, hw), lambda b, h, i: (b, h)),
                  pl.BlockSpec((1, 1, nq, hw + ONES_ROWS, blk), lambda b, h, i: (b, h, 0, 0, 0)),
                  pl.BlockSpec((1, blk, blk), lambda b, h, i: (h, 0, 0)),
                  pl.BlockSpec((1, blk, blk), lambda b, h, i: (h, 0, 0)),
                  small(lq1), small(lk1), small(lq2), small(lk2), small(subln_w)],
        out_specs=pl.BlockSpec((blk, hw), lambda b, h, i: (b * nq + i, h)),
        out_shape=jax.ShapeDtypeStruct((t, DIFF_V_W), BF16),
        scratch_shapes=[pltpu.VMEM((2, 1, blk), F32), pltpu.VMEM((2, hw + ONES_ROWS, blk), F32)],
        compiler_params=pltpu.CompilerParams(
            dimension_semantics=("arbitrary", "arbitrary", "arbitrary"),
            vmem_limit_bytes=VMEM_LIMIT),
        name="diff_attn",
    )(q, k, vt, b0, b1, lq1, lk1, lq2, lk2, subln_w)


def _out_kernel(yr_ref, yd_ref, x_ref, mod_ref, nw_ref, wo_ref, wr_ref, br_ref, upper_ref,
                x1_ref, hp_ref, meta_ref, gate_ref, cnt_ref, run_ref):
    tm = x_ref.shape[0]
    ne = run_ref.shape[0]

    @pl.when(pl.program_id(0) == 0)
    def _():
        run_ref[...] = jnp.zeros_like(run_ref)

    mixed = (jnp.dot(yr_ref[...], wo_ref[0:RET_V_W, :], preferred_element_type=F32)
             + jnp.dot(yd_ref[...], wo_ref[RET_V_W:, :], preferred_element_type=F32))
    gate1 = mod_ref[0, 2:3, :]
    shift2 = mod_ref[0, 3:4, :]
    scale2 = mod_ref[0, 4:5, :]
    x1 = x_ref[...] + gate1 * mixed
    x1_ref[...] = x1
    h2 = (_rms(x1) * nw_ref[...]) * (1.0 + scale2) + shift2
    hp_ref[...] = _pack_halves(h2)

    h_hi = h2.astype(BF16)
    h_lo = (h2 - h_hi.astype(F32)).astype(BF16)
    nt_dims = (((1,), (1,)), ((), ()))
    both = lax.dot_general(wr_ref[...], h_hi, nt_dims, preferred_element_type=F32)
    low = lax.dot_general(wr_ref[0:ne, :], h_lo, nt_dims, preferred_element_type=F32)
    logits = both[0:ne, :] + both[ne:, :] + low + br_ref[...]

    row = lax.broadcasted_iota(jnp.int32, logits.shape, 0)
    work = logits
    vals, idxs, hots = [], [], []
    for _ in range(TOP_K):
        mx = jnp.max(work, axis=0, keepdims=True)
        idx = jnp.min(jnp.where(work == mx, row, ne), axis=0, keepdims=True)
        hot = row == idx
        vals.append(mx)
        idxs.append(idx)
        hots.append(hot)
        work = jnp.where(hot, -jnp.inf, work)
    exps = [jnp.exp(v - vals[0]) for v in vals]
    denom = exps[0] + exps[1] + exps[2] + exps[3]

    sel = jnp.zeros(logits.shape, F32)
    for hot in hots:
        sel = sel + hot.astype(F32)
    prefix = jnp.dot(sel.astype(BF16), upper_ref[...], preferred_element_type=F32) + run_ref[...]
    ranks = [jnp.sum(jnp.where(hot, prefix, 0.0), axis=0, keepdims=True) for hot in hots]
    run_ref[...] = run_ref[...] + jnp.sum(sel, axis=1, keepdims=True)
    cnt_ref[...] = jnp.broadcast_to(run_ref[...], cnt_ref.shape).astype(jnp.int32)

    meta_ref[...] = jnp.concatenate(idxs + [r.astype(jnp.int32) for r in ranks], axis=0)
    gate_ref[...] = jnp.concatenate([e / denom for e in exps] + [jnp.zeros_like(denom)] * TOP_K, axis=0)


def _out_router(y_r, y_d, x2, mod, norm_w, w_out_bf16, w_router, b_router, seq):
    t, d = x2.shape
    tm = min(ROW_TILE, seq)
    per_batch = seq // tm
    ne = w_router.shape[1]
    w_hi = w_router.astype(BF16)
    w_lo = (w_router - w_hi.astype(F32)).astype(BF16)
    wr_t = jnp.concatenate([w_hi, w_lo], axis=1).T
    idx = jnp.arange(tm, dtype=jnp.int32)
    upper = (idx[:, None] < idx[None, :]).astype(BF16)
    row = lambda w: pl.BlockSpec((tm, w), lambda i: (i, 0))
    col = lambda h: pl.BlockSpec((h, tm), lambda i: (0, i))
    const = lambda a: pl.BlockSpec(a.shape, lambda i: (0,) * a.ndim)
    x1, hp, meta_t, gates_t, counts = pl.pallas_call(
        _out_kernel,
        grid=(t // tm,),
        in_specs=[row(RET_V_W), row(DIFF_V_W), row(d),
                  pl.BlockSpec((1, 6, d), lambda i: (i // per_batch, 0, 0)),
                  const(norm_w), const(w_out_bf16), const(wr_t), const(b_router), const(upper)],
        out_specs=[row(d), row(d // 2), col(2 * TOP_K), col(2 * TOP_K),
                   pl.BlockSpec((ne, LANES), lambda i: (0, 0))],
        out_shape=[jax.ShapeDtypeStruct((t, d), F32),
                   jax.ShapeDtypeStruct((t, d // 2), jnp.uint32),
                   jax.ShapeDtypeStruct((2 * TOP_K, t), jnp.int32),
                   jax.ShapeDtypeStruct((2 * TOP_K, t), F32),
                   jax.ShapeDtypeStruct((ne, LANES), jnp.int32)],
        scratch_shapes=[pltpu.VMEM((ne, 1), F32)],
        compiler_params=pltpu.CompilerParams(
            dimension_semantics=("arbitrary",), vmem_limit_bytes=VMEM_LIMIT),
        name="out_router",
    )(y_r, y_d, x2, mod, norm_w, w_out_bf16, wr_t, b_router, upper)
    return x1, hp, meta_t.T, gates_t.T, counts[:, 0]


def _w1_prep_kernel(w_ref, p_ref, o_ref):
    for s in range(w_ref.shape[2] // PERM_TILE):
        cols = slice(s * PERM_TILE, (s + 1) * PERM_TILE)
        o_ref[0, :, cols] = jnp.dot(w_ref[0, :, cols].astype(BF16), p_ref[...],
                                    preferred_element_type=F32).astype(BF16)


def _pair_split_matrix():
    i = jnp.arange(PERM_TILE)[:, None]
    j = jnp.arange(PERM_TILE)[None, :]
    half = PERM_TILE // 2
    src = jnp.where(j < half, 2 * j, 2 * (j - half) + 1)
    return (i == src).astype(BF16)


def _w1_prep(w1):
    e, d, f2 = w1.shape
    tn = 1024
    return pl.pallas_call(
        _w1_prep_kernel,
        grid=(e, f2 // tn),
        in_specs=[pl.BlockSpec((1, d, tn), lambda i, j: (i, 0, j)),
                  pl.BlockSpec((PERM_TILE, PERM_TILE), lambda i, j: (0, 0))],
        out_specs=pl.BlockSpec((1, d, tn), lambda i, j: (i, 0, j)),
        out_shape=jax.ShapeDtypeStruct((e, d, f2), BF16),
        compiler_params=pltpu.CompilerParams(vmem_limit_bytes=VMEM_LIMIT),
        name="w1_prep",
    )(w1, _pair_split_matrix())


def _pair_split_bias(b1):
    e, f2 = b1.shape
    nt = f2 // PERM_TILE
    g = b1[:, 0::2].reshape(e, nt, 1, PERM_TILE // 2)
    l = b1[:, 1::2].reshape(e, nt, 1, PERM_TILE // 2)
    return jnp.concatenate([g, l], axis=2).reshape(e, 1, f2)


def _slot_tokens(dest, n_rows):
    assert TOP_K == 4
    flat = dest.reshape(-1)
    n_slots = flat.shape[0]
    chunk = min(SC_SLOT_CHUNK, n_slots)
    mesh = plsc.VectorSubcoreMesh(core_axis_name="c", subcore_axis_name="s",
                                  num_cores=SC_CORES, num_subcores=SC_SUBCORES)

    def body(dest_hbm, out_hbm, out_v, dest_v):
        first = jnp.logical_and(lax.axis_index("c") == 0, lax.axis_index("s") == 0)

        @pl.when(first)
        def _():
            zeros = jnp.zeros((SC_LANES,), jnp.int32)

            @pl.loop(0, n_rows // SC_LANES)
            def _(g):
                out_v[pl.ds(g * SC_LANES, SC_LANES)] = zeros

            lane = lax.iota(jnp.int32, SC_LANES)

            @pl.loop(0, n_slots // chunk)
            def _(c):
                pltpu.sync_copy(dest_hbm.at[pl.ds(c * chunk, chunk)], dest_v)

                @pl.loop(0, chunk // SC_LANES)
                def _(g):
                    rows = dest_v[pl.ds(g * SC_LANES, SC_LANES)]
                    slot = lane + (c * chunk + g * SC_LANES)
                    plsc.store_scatter(out_v, [rows], lax.shift_right_logical(slot, 2))

            pltpu.sync_copy(out_v, out_hbm)

    return pl.kernel(
        body,
        out_type=jax.ShapeDtypeStruct((n_rows,), jnp.int32),
        mesh=mesh,
        scratch_types=[pltpu.VMEM((n_rows,), jnp.int32), pltpu.VMEM((chunk,), jnp.int32)],
        compiler_params=pltpu.CompilerParams(needs_layout_passes=False),
        name="slot_tokens",
    )(flat)


def _expert_kernel(be_ref, nv_ref, tcur_ref, tnxt_ref, hp_ref, w1_ref, b1_ref, w2_ref, b2_ref,
                   o_ref, hbuf, xb0, xb1, w2b, sem):
    b = pl.program_id(0)
    nv = nv_ref[0]
    bm = xb0.shape[0]

    def gather(tok_ref, dst):
        for r in range(bm):
            dst[pl.ds(r, 1), :] = hbuf[pl.ds(tok_ref[0, 0, r], 1), :]

    @pl.when(b == 0)
    def _():
        load = pltpu.make_async_copy(hp_ref, hbuf, sem)
        load.start()
        load.wait()
        gather(tcur_ref, xb0)

    last = jnp.maximum(b - 1, 0)

    @pl.when(jnp.logical_and(b < nv, jnp.logical_or(b == 0, be_ref[b] != be_ref[last])))
    def _():
        w2b[...] = w2_ref[0].astype(BF16)

    def run(cur, nxt):
        x = _unpack_halves(cur[...])
        half = PERM_TILE // 2
        acts = []
        for j in range(w1_ref.shape[2] // PERM_TILE):
            cols = slice(j * PERM_TILE, (j + 1) * PERM_TILE)
            h = jnp.dot(x, w1_ref[0, :, cols], preferred_element_type=F32) + b1_ref[0, :, cols]
            glu = jnp.minimum(h[:, :half], SWIGLU_LIMIT)
            lin = jnp.clip(h[:, half:], -SWIGLU_LIMIT, SWIGLU_LIMIT)
            acts.append((glu * jax.nn.sigmoid(SWIGLU_ALPHA * glu) * (lin + 1.0)).astype(BF16))
        act = jnp.concatenate(acts, axis=1)
        y = jnp.dot(act, w2b[...], preferred_element_type=F32) + b2_ref[0]
        _store_row_tiles(o_ref, y)
        gather(tnxt_ref, nxt)

    even = lax.rem(b, 2) == 0

    @pl.when(jnp.logical_and(b < nv, even))
    def _():
        run(xb0, xb1)

    @pl.when(jnp.logical_and(b < nv, jnp.logical_not(even)))
    def _():
        run(xb1, xb0)

    @pl.when(b >= nv)
    def _():
        o_ref[...] = jnp.zeros_like(o_ref)


def _experts(block_e, n_valid, slot_tok, hp, w1p, b1p, w2, b2):
    d = D_MODEL
    f2 = w1p.shape[2]
    f = w2.shape[1]
    bm = EXPERT_ROWS
    nb = slot_tok.shape[0] // bm
    tok3 = slot_tok.reshape(nb, 1, bm)
    exp = lambda b, be, nv: be[jnp.maximum(jnp.minimum(b, nv[0] - 1), 0)]
    grid_spec = pltpu.PrefetchScalarGridSpec(
        num_scalar_prefetch=2,
        grid=(nb,),
        in_specs=[pl.BlockSpec((1, 1, bm), lambda b, be, nv: (b, 0, 0), memory_space=pltpu.SMEM),
                  pl.BlockSpec((1, 1, bm), lambda b, be, nv: (jnp.minimum(b + 1, nb - 1), 0, 0),
                               memory_space=pltpu.SMEM),
                  pl.BlockSpec(memory_space=pl.ANY),
                  pl.BlockSpec((1, d, f2), lambda b, be, nv: (exp(b, be, nv), 0, 0)),
                  pl.BlockSpec((1, 1, f2), lambda b, be, nv: (exp(b, be, nv), 0, 0)),
                  pl.BlockSpec((1, f, d), lambda b, be, nv: (exp(b, be, nv), 0, 0)),
                  pl.BlockSpec((1, 1, d), lambda b, be, nv: (exp(b, be, nv), 0, 0))],
        out_specs=pl.BlockSpec((bm * ROW_SUB, LANES), lambda b, be, nv: (b, 0)),
        scratch_shapes=[pltpu.VMEM(hp.shape, hp.dtype), pltpu.VMEM((bm, hp.shape[1]), hp.dtype),
                        pltpu.VMEM((bm, hp.shape[1]), hp.dtype), pltpu.VMEM(w2.shape[1:], BF16),
                        pltpu.SemaphoreType.DMA(())],
    )
    return pl.pallas_call(
        _expert_kernel,
        grid_spec=grid_spec,
        out_shape=jax.ShapeDtypeStruct((nb * bm * ROW_SUB, LANES), F32),
        compiler_params=pltpu.CompilerParams(
            dimension_semantics=("arbitrary",), vmem_limit_bytes=VMEM_LIMIT),
        name="experts",
    )(block_e, n_valid, tok3, tok3, hp, w1p, b1p, w2, b2)


def _combine_kernel(dcur_ref, dnxt_ref, gate_ref, x1_ref, mod_ref, nf_ref, y_ref, o_ref, buf, sems):
    nt = x1_ref.shape[0]
    i = pl.program_id(0)
    n = pl.num_programs(0)
    slot = lax.rem(i, 2)

    def row_copy(src_row, s, kk, r):
        return pltpu.make_async_copy(y_ref.at[pl.ds(src_row * ROW_SUB, ROW_SUB), :],
                                     buf.at[s, kk, pl.ds(r * ROW_SUB, ROW_SUB), :], sems.at[s])

    def fetch(d_ref, s):
        for j in range(nt):
            for kk in range(TOP_K):
                row_copy(d_ref[0, 0, j * TOP_K + kk], s, kk, j).start(priority=kk % DMA_QUEUES)

    def wait_all(s):
        for kk in range(TOP_K):
            pltpu.make_async_copy(y_ref.at[pl.ds(0, nt * ROW_SUB), :], buf.at[s, kk], sems.at[s]).wait()

    @pl.when(i == 0)
    def _():
        fetch(dcur_ref, 0)

    fetch(dnxt_ref, 1 - slot)
    wait_all(slot)

    g = gate_ref[...]
    parts = []
    for s in range(ROW_SUB):
        acc = buf[slot, 0, pl.ds(s, nt, stride=ROW_SUB), :] * g[:, 0:1]
        for kk in range(1, TOP_K):
            acc = acc + buf[slot, kk, pl.ds(s, nt, stride=ROW_SUB), :] * g[:, kk:kk + 1]
        parts.append(acc)
    moe = jnp.concatenate(parts, axis=1)
    gate2 = mod_ref[0, 5:6, :]
    x2 = x1_ref[...] + gate2 * moe
    o_ref[...] = _rms(x2) * nf_ref[...]

    @pl.when(i == n - 1)
    def _():
        wait_all(1 - slot)


def _combine(dest, gates, x1, mod, normf_w, y, seq):
    t, d = x1.shape
    nt = min(COMBINE_TOKENS, seq)
    steps = t // nt
    per_batch = seq // nt
    dest2 = dest.reshape(steps, 1, nt * TOP_K)
    return pl.pallas_call(
        _combine_kernel,
        grid=(steps,),
        in_specs=[pl.BlockSpec((1, 1, nt * TOP_K), lambda i: (i, 0, 0), memory_space=pltpu.SMEM),
                  pl.BlockSpec((1, 1, nt * TOP_K), lambda i: (jnp.minimum(i + 1, steps - 1), 0, 0),
                               memory_space=pltpu.SMEM),
                  pl.BlockSpec((nt, 2 * TOP_K), lambda i: (i, 0)),
                  pl.BlockSpec((nt, d), lambda i: (i, 0)),
                  pl.BlockSpec((1, 6, d), lambda i: (i // per_batch, 0, 0)),
                  pl.BlockSpec((1, d), lambda i: (0, 0)),
                  pl.BlockSpec(memory_space=pl.ANY)],
        out_specs=pl.BlockSpec((nt, d), lambda i: (i, 0)),
        out_shape=jax.ShapeDtypeStruct((t, d), F32),
        scratch_shapes=[pltpu.VMEM((2, TOP_K, nt * ROW_SUB, LANES), F32),
                        pltpu.SemaphoreType.DMA((2,))],
        compiler_params=pltpu.CompilerParams(
            dimension_semantics=("arbitrary",), vmem_limit_bytes=VMEM_LIMIT),
        name="combine",
    )(dest2, dest2, gates, x1, mod, normf_w, y)


def kernel(x, c, w_ada, b_ada, norm1_w, w_in, lam_q1, lam_k1, lam_q2, lam_k2, subln_w, rel_bias,
           w_out, norm2_w, w_router, b_router, w1, b1, w2, b2, normf_w):
    batch, seq, d = x.shape
    t = batch * seq
    x2 = x.reshape(t, d)

    c_pad = jnp.zeros((8, d), F32).at[:batch].set(c)
    mod = _ada(c_pad, w_ada[0], b_ada[0][None, :])[:batch].reshape(batch, 6, d)

    q_r, k_r, v_r, g_r, q_d, k_d, vt_d = _in_proj(x2, mod, norm1_w[0][None, :],
                                                 w_in[0].astype(BF16), seq)
    y_r = _retention(q_r, k_r, v_r, g_r, batch, seq)
    y_d = _diff_attention(q_d, k_d, vt_d, rel_bias, lam_q1, lam_k1, lam_q2, lam_k2,
                          subln_w, batch, seq)

    x1, hp, meta, gates, counts = _out_router(y_r, y_d, x2, mod, norm2_w[0][None, :],
                                              w_out[0].astype(BF16), w_router[0],
                                              b_router[0][:, None], seq)

    bm = EXPERT_ROWS
    padded = (counts + bm - 1) // bm * bm
    pad_end = jnp.cumsum(padded)
    pad_start = pad_end - padded
    sel_e = meta[:, :TOP_K]
    hot_e = sel_e[:, :, None] == jnp.arange(N_EXPERTS, dtype=jnp.int32)[None, None, :]
    dest = jnp.sum(jnp.where(hot_e, pad_start[None, None, :], 0), axis=-1) + meta[:, TOP_K:]
    n_rows = (t * TOP_K // bm + N_EXPERTS) * bm
    nb = n_rows // bm
    block_start = jnp.arange(nb, dtype=jnp.int32) * bm
    block_e = jnp.minimum(jnp.sum((pad_end[None, :] <= block_start[:, None]).astype(jnp.int32), axis=1),
                          N_EXPERTS - 1)
    n_valid = (pad_end[-1:] // bm).astype(jnp.int32)
    slot_tok = _slot_tokens(dest, n_rows)

    ys = _experts(block_e, n_valid, slot_tok, hp, _w1_prep(w1[0]), _pair_split_bias(b1[0]),
                  w2[0], b2[0][:, None, :])
    out = _combine(dest, gates, x1, mod, normf_w[None, :], ys, seq)
    return out.reshape(batch, seq, d)
```

```python
import math

import jax
import jax.numpy as jnp
from jax import lax
from jax.experimental import pallas as pl
from jax.experimental.pallas import tpu as pltpu
from jax.experimental.pallas import tpu_sc as plsc

F32 = jnp.float32
BF16 = jnp.bfloat16

D_MODEL = 1024
RET_HEADS = 4
RET_KEY_DIM = 64
RET_VAL_DIM = 128
RET_QK_W = RET_HEADS * RET_KEY_DIM
RET_V_W = RET_HEADS * RET_VAL_DIM
RET_CHUNK = 128
DIFF_HEADS = 4
DIFF_HEAD_DIM = 64
DIFF_QK_W = DIFF_HEADS * 2 * DIFF_HEAD_DIM
DIFF_V_W = DIFF_HEADS * 2 * DIFF_HEAD_DIM
IN_SIZES = (RET_QK_W, RET_QK_W, RET_V_W, RET_V_W, DIFF_QK_W, DIFF_QK_W, DIFF_V_W)
REL_BUCKETS = 32
REL_MAX_DIST = 128
N_EXPERTS = 32
TOP_K = 4
SWIGLU_ALPHA = 1.702
SWIGLU_LIMIT = 7.0
NORM_EPS = 1e-6
LAMBDA_INIT = 0.8 - 0.6 * math.exp(-0.3 * 0)

LANES = 128
ROW_SUB = D_MODEL // LANES
NEG_BIG = -1e30
LOG2_E = math.log2(math.e)
ONES_ROWS = 16
VMEM_LIMIT = 56 * 1024 * 1024

ROW_TILE = 512
RET_ROWS = 512
ATT_BLOCK = 512
ATT_STRIP = 256
EXPERT_ROWS = 256
PERM_TILE = 256
COMBINE_TOKENS = 256
DMA_QUEUES = 2
SC_CORES = 2
SC_SUBCORES = 16
SC_LANES = 16
SC_SLOT_CHUNK = 4096


def _rms(x):
    return x * lax.rsqrt(jnp.mean(x * x, axis=-1, keepdims=True) + NORM_EPS)


def _store_row_tiles(ref, x):
    rows = x.shape[0]
    for s in range(ROW_SUB):
        ref[pl.ds(s, rows, stride=ROW_SUB), :] = x[:, s * LANES:(s + 1) * LANES]


def _pack_halves(x):
    half = x.shape[1] // 2
    lo = lax.bitcast_convert_type(x[:, :half].astype(BF16).astype(F32), jnp.uint32)
    hi = lax.bitcast_convert_type(x[:, half:].astype(BF16).astype(F32), jnp.uint32)
    return (hi & jnp.uint32(0xFFFF0000)) | (lo >> 16)


def _unpack_halves(w):
    lo = lax.bitcast_convert_type(w << 16, F32).astype(BF16)
    hi = lax.bitcast_convert_type(w & jnp.uint32(0xFFFF0000), F32).astype(BF16)
    return jnp.concatenate([lo, hi], axis=1)


def _ada_kernel(c_ref, w_ref, b_ref, o_ref):
    c = c_ref[...]
    cond = c * jax.nn.sigmoid(c)
    o_ref[...] = jnp.dot(cond, w_ref[...], precision=lax.Precision.HIGHEST,
                         preferred_element_type=F32) + b_ref[...]


def _ada(c_pad, w_ada, b_ada):
    rows, d = c_pad.shape
    n = w_ada.shape[1]
    tn = 1024
    return pl.pallas_call(
        _ada_kernel,
        grid=(n // tn,),
        in_specs=[pl.BlockSpec((rows, d), lambda j: (0, 0)),
                  pl.BlockSpec((d, tn), lambda j: (0, j)),
                  pl.BlockSpec((1, tn), lambda j: (0, j))],
        out_specs=pl.BlockSpec((rows, tn), lambda j: (0, j)),
        out_shape=jax.ShapeDtypeStruct((rows, n), F32),
        name="ada",
    )(c_pad, w_ada, b_ada)


def _in_proj_kernel(x_ref, mod_ref, nw_ref, w_ref, *o_refs):
    x = x_ref[...]
    shift = mod_ref[0, 0:1, :]
    scale = mod_ref[0, 1:2, :]
    h = (_rms(x) * nw_ref[...]) * (1.0 + scale) + shift
    hb = h.astype(BF16)
    off = 0
    for o_ref, width in zip(o_refs[:-1], IN_SIZES[:-1]):
        o_ref[...] = jnp.dot(hb, w_ref[:, off:off + width],
                             preferred_element_type=F32).astype(o_ref.dtype)
        off += width
    vt_ref = o_refs[-1]
    v = jnp.dot(hb, w_ref[:, off:], preferred_element_type=F32)
    hw = 2 * DIFF_HEAD_DIM
    tail = (lax.broadcasted_iota(jnp.int32, (ONES_ROWS, v.shape[0]), 0) == 0).astype(vt_ref.dtype)
    for hd in range(DIFF_HEADS):
        vt_ref[0, hd, 0, 0:hw, :] = v[:, hd * hw:(hd + 1) * hw].T.astype(vt_ref.dtype)
        vt_ref[0, hd, 0, hw:, :] = tail


def _in_proj(x2, mod, norm_w, w_in_bf16, seq):
    t, d = x2.shape
    tm = min(ROW_TILE, seq)
    assert tm == min(ATT_BLOCK, seq)
    per_batch = seq // tm
    in_w = w_in_bf16.shape[1]
    hw = 2 * DIFF_HEAD_DIM
    vt_shape = (t // seq, DIFF_HEADS, per_batch, hw + ONES_ROWS, tm)
    return pl.pallas_call(
        _in_proj_kernel,
        grid=(t // tm,),
        in_specs=[pl.BlockSpec((tm, d), lambda i: (i, 0)),
                  pl.BlockSpec((1, 6, d), lambda i: (i // per_batch, 0, 0)),
                  pl.BlockSpec((1, d), lambda i: (0, 0)),
                  pl.BlockSpec((d, in_w), lambda i: (0, 0))],
        out_specs=([pl.BlockSpec((tm, w), lambda i: (i, 0)) for w in IN_SIZES[:-1]]
                   + [pl.BlockSpec((1,) + vt_shape[1:2] + (1,) + vt_shape[3:],
                                   lambda i: (i // per_batch, 0, i % per_batch, 0, 0))]),
        out_shape=([jax.ShapeDtypeStruct((t, w), BF16) for w in IN_SIZES[:-1]]
                   + [jax.ShapeDtypeStruct(vt_shape, BF16)]),
        compiler_params=pltpu.CompilerParams(vmem_limit_bytes=VMEM_LIMIT),
        name="in_proj",
    )(x2, mod, norm_w, w_in_bf16)


def _rotary(x, cos, sin_even, sin_odd):
    nxt = pltpu.roll(x, LANES - 1, 1)
    prv = pltpu.roll(x, 1, 1)
    return x * cos + nxt * sin_even + prv * sin_odd


def _ret_kernel(q_ref, k_ref, v_ref, g_ref, cos_ref, sine_ref, sino_ref,
                hmask_ref, xi_ref, zeta_ref, dmask_ref, gch_ref, o_ref, state_ref):
    @pl.when(pl.program_id(1) == 0)
    def _():
        state_ref[...] = jnp.zeros_like(state_ref)

    n_sub = q_ref.shape[0] // RET_CHUNK
    for c in range(n_sub):
        rows = slice(c * RET_CHUNK, (c + 1) * RET_CHUNK)
        for pair in range(RET_HEADS // 2):
            lanes = slice(pair * LANES, (pair + 1) * LANES)
            cos = cos_ref[rows, :]
            sine = sine_ref[rows, :]
            sino = sino_ref[rows, :]
            qr = _rotary(q_ref[rows, lanes].astype(F32), cos, sine, sino)
            kr = _rotary(k_ref[rows, lanes].astype(F32), cos, sine, sino) * (RET_KEY_DIM ** -0.5)
            qb = qr.astype(BF16)
            for hh in range(2):
                h = 2 * pair + hh
                vcols = slice(h * RET_VAL_DIM, (h + 1) * RET_VAL_DIM)
                v = v_ref[rows, vcols]
                km = (kr * hmask_ref[h]).astype(BF16)
                scores = lax.dot_general(qb, km, (((1,), (1,)), ((), ())),
                                         preferred_element_type=F32) * dmask_ref[h]
                inner = jnp.dot(scores.astype(BF16), v, preferred_element_type=F32)
                qx = (qr * xi_ref[h]).astype(BF16)
                state = state_ref[h]
                cross = jnp.dot(qx, state.astype(BF16), preferred_element_type=F32)
                kz = (kr * zeta_ref[h]).astype(BF16)
                kv = lax.dot_general(kz, v, (((0,), (0,)), ((), ())),
                                     preferred_element_type=F32)
                state_ref[h] = state * gch_ref[h] + kv
                y = _rms(inner + cross)
                g = g_ref[rows, vcols].astype(F32)
                o_ref[rows, vcols] = (g * jax.nn.sigmoid(g) * y).astype(o_ref.dtype)


def _retention_tables(seq):
    dk, c, nh = RET_KEY_DIM, RET_CHUNK, RET_HEADS
    pos = jnp.arange(seq, dtype=F32)
    inv_freq = 1.0 / (10000.0 ** jnp.linspace(0.0, 1.0, dk // 2, dtype=F32))
    ang = pos[:, None] * jnp.repeat(inv_freq, 2)[None, :]
    sin = jnp.tile(jnp.sin(ang), (1, LANES // dk))
    cos = jnp.tile(jnp.cos(ang), (1, LANES // dk))
    even = (jnp.arange(LANES) % 2 == 0)[None, :]
    sin_even = jnp.where(even, -sin, 0.0)
    sin_odd = jnp.where(even, 0.0, sin)
    log_g = jnp.log(1.0 - 2.0 ** (-5.0 - jnp.arange(nh, dtype=F32)))
    i = jnp.arange(c, dtype=F32)
    rel = i[:, None] - i[None, :]
    dmask = jnp.where(rel[None] >= 0,
                      jnp.exp(jnp.maximum(rel, 0.0)[None] * log_g[:, None, None]), 0.0)
    zeta = jnp.exp((c - 1.0 - i)[None, :] * log_g[:, None])
    xi = jnp.exp((i + 1.0)[None, :] * log_g[:, None])
    g_chunk = jnp.exp(c * log_g)
    lane = jnp.arange(LANES)
    hmask = jnp.stack([((lane // dk) == (h % 2)).astype(F32) for h in range(nh)])[:, None, :]
    xi_t = xi[:, :, None] * hmask
    zeta_t = zeta[:, :, None] * hmask
    gch = jnp.broadcast_to(g_chunk[:, None, None], (nh, 1, LANES))
    return cos, sin_even, sin_odd, hmask, xi_t, zeta_t, dmask, gch


def _retention(q, k, v, g, batch, seq):
    t = q.shape[0]
    rb = min(RET_ROWS, seq)
    per_batch = seq // rb
    cos, sin_even, sin_odd, hmask, xi_t, zeta_t, dmask, gch = _retention_tables(seq)
    row = lambda w: pl.BlockSpec((rb, w), lambda b, j: (b * per_batch + j, 0))
    tab = lambda w: pl.BlockSpec((rb, w), lambda b, j: (j, 0))
    full = lambda a: pl.BlockSpec(a.shape, lambda b, j: (0,) * a.ndim)
    return pl.pallas_call(
        _ret_kernel,
        grid=(batch, per_batch),
        in_specs=[row(RET_QK_W), row(RET_QK_W), row(RET_V_W), row(RET_V_W),
                  tab(LANES), tab(LANES), tab(LANES),
                  full(hmask), full(xi_t), full(zeta_t), full(dmask), full(gch)],
        out_specs=row(RET_V_W),
        out_shape=jax.ShapeDtypeStruct((t, RET_V_W), BF16),
        scratch_shapes=[pltpu.VMEM((RET_HEADS, LANES, RET_VAL_DIM), F32)],
        compiler_params=pltpu.CompilerParams(
            dimension_semantics=("arbitrary", "arbitrary"), vmem_limit_bytes=VMEM_LIMIT),
        name="retention",
    )(q, k, v, g, cos, sin_even, sin_odd, hmask, xi_t, zeta_t, dmask, gch)


def _t5_bucket(rel):
    n = jnp.maximum(rel, 0)
    max_exact = REL_BUCKETS // 2
    nf = jnp.maximum(n, 1).astype(F32)
    large = max_exact + (jnp.log(nf / max_exact) / math.log(REL_MAX_DIST / max_exact)
                         * (REL_BUCKETS - max_exact)).astype(jnp.int32)
    large = jnp.minimum(large, REL_BUCKETS - 1)
    return jnp.where(n < max_exact, n, large)


def _bias_tiles(rel_bias, blk):
    r = jnp.arange(blk, dtype=jnp.int32)
    far = rel_bias[REL_BUCKETS - 1]
    rel0 = r[None, :] - r[:, None]
    rel1 = rel0 + blk
    buckets = jnp.arange(REL_BUCKETS, dtype=jnp.int32)

    def tile(rel):
        hot = (_t5_bucket(rel)[:, :, None] == buckets).astype(F32)
        return jnp.einsum('krb,bh->hkr', hot, rel_bias, precision=lax.Precision.HIGHEST)

    b0 = jnp.where(rel0[None] >= 0, (tile(rel0) - far[:, None, None]) * LOG2_E, NEG_BIG)
    b1 = (tile(rel1) - far[:, None, None]) * LOG2_E
    return b0, b1


def _attn_kernel(q_ref, k_ref, vt_ref, b0_ref, b1_ref, lq1_ref, lk1_ref, lq2_ref, lk2_ref,
                 sw_ref, o_ref, m_ref, acc_ref):
    blk = q_ref.shape[0]
    i = pl.program_id(2)
    lane = lax.broadcasted_iota(jnp.int32, (1, LANES), 1)
    q = (q_ref[...].astype(F32) * (DIFF_HEAD_DIM ** -0.5 * LOG2_E)).astype(BF16)
    zero = jnp.zeros_like(q)
    qm = (jnp.where(lane < DIFF_HEAD_DIM, q, zero), jnp.where(lane >= DIFF_HEAD_DIM, q, zero))

    m_ref[...] = jnp.full_like(m_ref, NEG_BIG)
    acc_ref[...] = jnp.zeros_like(acc_ref)

    def step(blocks):
        kbs = [k_ref[pl.ds(pl.multiple_of(j * blk, blk), blk), :] for j, _, _ in blocks]
        vts = [vt_ref[0, 0, j] for j, _, _ in blocks]
        chains = [(mi, qs) for mi in range(2) for qs in range(blk // ATT_STRIP)]
        scores = []
        for mi, qs in chains:
            qc = slice(qs * ATT_STRIP, (qs + 1) * ATT_STRIP)
            row = []
            for (_, bias, diagonal), kb in zip(blocks, kbs):
                keys = (qs + 1) * ATT_STRIP if diagonal else blk
                s = lax.dot_general(kb[:keys], qm[mi][qc, :], (((1,), (1,)), ((), ())),
                                    preferred_element_type=F32)
                row.append(s if bias is None else s + bias[:keys, qc])
            scores.append(row)
        stats = []
        for (mi, qs), row in zip(chains, scores):
            qc = slice(qs * ATT_STRIP, (qs + 1) * ATT_STRIP)
            m_old = m_ref[mi, :, qc]
            m_new = m_old
            for s in row:
                m_new = jnp.maximum(m_new, jnp.max(s, axis=0, keepdims=True))
            stats.append((jnp.exp2(m_old - m_new), [jnp.exp2(s - m_new).astype(BF16) for s in row], m_new))
        for (mi, qs), (alpha, ps, m_new) in zip(chains, stats):
            qc = slice(qs * ATT_STRIP, (qs + 1) * ATT_STRIP)
            pv = None
            for vt, p in zip(vts, ps):
                part = jnp.dot(vt[:, :p.shape[0]], p, preferred_element_type=F32)
                pv = part if pv is None else pv + part
            acc_ref[mi, :, qc] = alpha * acc_ref[mi, :, qc] + pv
            m_ref[mi, :, qc] = m_new

    n_far = jnp.maximum(i - 1, 0)

    def far_pair(pair, carry):
        step([(2 * pair, None, False), (2 * pair + 1, None, False)])
        return carry

    lax.fori_loop(0, lax.shift_right_logical(n_far, 1), far_pair, 0)

    @pl.when(lax.rem(n_far, 2) == 1)
    def _():
        step([(n_far - 1, None, False)])

    @pl.when(i >= 1)
    def _():
        step([(i - 1, b1_ref[0], False), (i, b0_ref[0], True)])

    @pl.when(i == 0)
    def _():
        step([(i, b0_ref[0], True)])

    lam = (jnp.exp(jnp.sum(lq1_ref[...] * lk1_ref[...], axis=-1, keepdims=True))
           - jnp.exp(jnp.sum(lq2_ref[...] * lk2_ref[...], axis=-1, keepdims=True))
           + LAMBDA_INIT)
    hw = 2 * DIFF_HEAD_DIM
    a = (acc_ref[0, :hw, :] / acc_ref[0, hw:hw + 1, :]
         - lam * (acc_ref[1, :hw, :] / acc_ref[1, hw:hw + 1, :]))
    a = a * lax.rsqrt(jnp.mean(a * a, axis=0, keepdims=True) + NORM_EPS)
    o_ref[...] = (a.T * sw_ref[...] * (1.0 - LAMBDA_INIT)).astype(o_ref.dtype)


def _diff_attention(q, k, vt, rel_bias, lq1, lk1, lq2, lk2, subln_w, batch, seq):
    t = q.shape[0]
    blk = min(ATT_BLOCK, seq)
    nq = seq // blk
    hw = 2 * DIFF_HEAD_DIM
    b0, b1 = _bias_tiles(rel_bias, blk)
    small = lambda a: pl.BlockSpec(a.shape, lambda b, h, i: (0,) * a.ndim)
    return pl.pallas_call(
        _attn_kernel,
        grid=(batch, DIFF_HEADS, nq),
        in_specs=[pl.BlockSpec((blk, hw), lambda b, h, i: (b * nq + i, h)),
                  pl.BlockSpec((seq, hw), lambda b, h, i: (b, h)),
                  pl.BlockSpec((1, 1, nq, hw + ONES_ROWS, blk), lambda b, h, i: (b, h, 0, 0, 0)),
                  pl.BlockSpec((1, blk, blk), lambda b, h, i: (h, 0, 0)),
                  pl.BlockSpec((1, blk, blk), lambda b, h, i: (h, 0, 0)),
                  small(lq1), small(lk1), small(lq2), small(lk2), small(subln_w)],
        out_specs=pl.BlockSpec((blk, hw), lambda b, h, i: (b * nq + i, h)),
        out_shape=jax.ShapeDtypeStruct((t, DIFF_V_W), BF16),
        scratch_shapes=[pltpu.VMEM((2, 1, blk), F32), pltpu.VMEM((2, hw + ONES_ROWS, blk), F32)],
        compiler_params=pltpu.CompilerParams(
            dimension_semantics=("arbitrary", "arbitrary", "arbitrary"),
            vmem_limit_bytes=VMEM_LIMIT),
        name="diff_attn",
    )(q, k, vt, b0, b1, lq1, lk1, lq2, lk2, subln_w)


def _out_kernel(yr_ref, yd_ref, x_ref, mod_ref, nw_ref, wo_ref, wr_ref, br_ref, upper_ref,
                x1_ref, hp_ref, meta_ref, gate_ref, cnt_ref, run_ref):
    tm = x_ref.shape[0]
    ne = run_ref.shape[0]

    @pl.when(pl.program_id(0) == 0)
    def _():
        run_ref[...] = jnp.zeros_like(run_ref)

    mixed = (jnp.dot(yr_ref[...], wo_ref[0:RET_V_W, :], preferred_element_type=F32)
             + jnp.dot(yd_ref[...], wo_ref[RET_V_W:, :], preferred_element_type=F32))
    gate1 = mod_ref[0, 2:3, :]
    shift2 = mod_ref[0, 3:4, :]
    scale2 = mod_ref[0, 4:5, :]
    x1 = x_ref[...] + gate1 * mixed
    x1_ref[...] = x1
    h2 = (_rms(x1) * nw_ref[...]) * (1.0 + scale2) + shift2
    hp_ref[...] = _pack_halves(h2)

    h_hi = h2.astype(BF16)
    h_lo = (h2 - h_hi.astype(F32)).astype(BF16)
    nt_dims = (((1,), (1,)), ((), ()))
    both = lax.dot_general(wr_ref[...], h_hi, nt_dims, preferred_element_type=F32)
    low = lax.dot_general(wr_ref[0:ne, :], h_lo, nt_dims, preferred_element_type=F32)
    logits = both[0:ne, :] + both[ne:, :] + low + br_ref[...]

    row = lax.broadcasted_iota(jnp.int32, logits.shape, 0)
    work = logits
    vals, idxs, hots = [], [], []
    for _ in range(TOP_K):
        mx = jnp.max(work, axis=0, keepdims=True)
        idx = jnp.min(jnp.where(work == mx, row, ne), axis=0, keepdims=True)
        hot = row == idx
        vals.append(mx)
        idxs.append(idx)
        hots.append(hot)
        work = jnp.where(hot, -jnp.inf, work)
    exps = [jnp.exp(v - vals[0]) for v in vals]
    denom = exps[0] + exps[1] + exps[2] + exps[3]

    sel = jnp.zeros(logits.shape, F32)
    for hot in hots:
        sel = sel + hot.astype(F32)
    prefix = jnp.dot(sel.astype(BF16), upper_ref[...], preferred_element_type=F32) + run_ref[...]
    ranks = [jnp.sum(jnp.where(hot, prefix, 0.0), axis=0, keepdims=True) for hot in hots]
    run_ref[...] = run_ref[...] + jnp.sum(sel, axis=1, keepdims=True)
    cnt_ref[...] = jnp.broadcast_to(run_ref[...], cnt_ref.shape).astype(jnp.int32)

    meta_ref[...] = jnp.concatenate(idxs + [r.astype(jnp.int32) for r in ranks], axis=0)
    gate_ref[...] = jnp.concatenate([e / denom for e in exps] + [jnp.zeros_like(denom)] * TOP_K, axis=0)


def _out_router(y_r, y_d, x2, mod, norm_w, w_out_bf16, w_router, b_router, seq):
    t, d = x2.shape
    tm = min(ROW_TILE, seq)
    per_batch = seq // tm
    ne = w_router.shape[1]
    w_hi = w_router.astype(BF16)
    w_lo = (w_router - w_hi.astype(F32)).astype(BF16)
    wr_t = jnp.concatenate([w_hi, w_lo], axis=1).T
    idx = jnp.arange(tm, dtype=jnp.int32)
    upper = (idx[:, None] < idx[None, :]).astype(BF16)
    row = lambda w: pl.BlockSpec((tm, w), lambda i: (i, 0))
    col = lambda h: pl.BlockSpec((h, tm), lambda i: (0, i))
    const = lambda a: pl.BlockSpec(a.shape, lambda i: (0,) * a.ndim)
    x1, hp, meta_t, gates_t, counts = pl.pallas_call(
        _out_kernel,
        grid=(t // tm,),
        in_specs=[row(RET_V_W), row(DIFF_V_W), row(d),
                  pl.BlockSpec((1, 6, d), lambda i: (i // per_batch, 0, 0)),
                  const(norm_w), const(w_out_bf16), const(wr_t), const(b_router), const(upper)],
        out_specs=[row(d), row(d // 2), col(2 * TOP_K), col(2 * TOP_K),
                   pl.BlockSpec((ne, LANES), lambda i: (0, 0))],
        out_shape=[jax.ShapeDtypeStruct((t, d), F32),
                   jax.ShapeDtypeStruct((t, d // 2), jnp.uint32),
                   jax.ShapeDtypeStruct((2 * TOP_K, t), jnp.int32),
                   jax.ShapeDtypeStruct((2 * TOP_K, t), F32),
                   jax.ShapeDtypeStruct((ne, LANES), jnp.int32)],
        scratch_shapes=[pltpu.VMEM((ne, 1), F32)],
        compiler_params=pltpu.CompilerParams(
            dimension_semantics=("arbitrary",), vmem_limit_bytes=VMEM_LIMIT),
        name="out_router",
    )(y_r, y_d, x2, mod, norm_w, w_out_bf16, wr_t, b_router, upper)
    return x1, hp, meta_t.T, gates_t.T, counts[:, 0]


def _w1_prep_kernel(w_ref, p_ref, o_ref):
    for s in range(w_ref.shape[2] // PERM_TILE):
        cols = slice(s * PERM_TILE, (s + 1) * PERM_TILE)
        o_ref[0, :, cols] = jnp.dot(w_ref[0, :, cols].astype(BF16), p_ref[...],
                                    preferred_element_type=F32).astype(BF16)


def _pair_split_matrix():
    i = jnp.arange(PERM_TILE)[:, None]
    j = jnp.arange(PERM_TILE)[None, :]
    half = PERM_TILE // 2
    src = jnp.where(j < half, 2 * j, 2 * (j - half) + 1)
    return (i == src).astype(BF16)


def _w1_prep(w1):
    e, d, f2 = w1.shape
    tn = 1024
    return pl.pallas_call(
        _w1_prep_kernel,
        grid=(e, f2 // tn),
        in_specs=[pl.BlockSpec((1, d, tn), lambda i, j: (i, 0, j)),
                  pl.BlockSpec((PERM_TILE, PERM_TILE), lambda i, j: (0, 0))],
        out_specs=pl.BlockSpec((1, d, tn), lambda i, j: (i, 0, j)),
        out_shape=jax.ShapeDtypeStruct((e, d, f2), BF16),
        compiler_params=pltpu.CompilerParams(vmem_limit_bytes=VMEM_LIMIT),
        name="w1_prep",
    )(w1, _pair_split_matrix())


def _pair_split_bias(b1):
    e, f2 = b1.shape
    nt = f2 // PERM_TILE
    g = b1[:, 0::2].reshape(e, nt, 1, PERM_TILE // 2)
    l = b1[:, 1::2].reshape(e, nt, 1, PERM_TILE // 2)
    return jnp.concatenate([g, l], axis=2).reshape(e, 1, f2)


def _slot_tokens(dest, n_rows):
    assert TOP_K == 4
    flat = dest.reshape(-1)
    n_slots = flat.shape[0]
    chunk = min(SC_SLOT_CHUNK, n_slots)
    mesh = plsc.VectorSubcoreMesh(core_axis_name="c", subcore_axis_name="s",
                                  num_cores=SC_CORES, num_subcores=SC_SUBCORES)

    def body(dest_hbm, out_hbm, out_v, dest_v):
        first = jnp.logical_and(lax.axis_index("c") == 0, lax.axis_index("s") == 0)

        @pl.when(first)
        def _():
            zeros = jnp.zeros((SC_LANES,), jnp.int32)

            @pl.loop(0, n_rows // SC_LANES)
            def _(g):
                out_v[pl.ds(g * SC_LANES, SC_LANES)] = zeros

            lane = lax.iota(jnp.int32, SC_LANES)

            @pl.loop(0, n_slots // chunk)
            def _(c):
                pltpu.sync_copy(dest_hbm.at[pl.ds(c * chunk, chunk)], dest_v)

                @pl.loop(0, chunk // SC_LANES)
                def _(g):
                    rows = dest_v[pl.ds(g * SC_LANES, SC_LANES)]
                    slot = lane + (c * chunk + g * SC_LANES)
                    plsc.store_scatter(out_v, [rows], lax.shift_right_logical(slot, 2))

            pltpu.sync_copy(out_v, out_hbm)

    return pl.kernel(
        body,
        out_type=jax.ShapeDtypeStruct((n_rows,), jnp.int32),
        mesh=mesh,
        scratch_types=[pltpu.VMEM((n_rows,), jnp.int32), pltpu.VMEM((chunk,), jnp.int32)],
        compiler_params=pltpu.CompilerParams(needs_layout_passes=False),
        name="slot_tokens",
    )(flat)


def _expert_kernel(be_ref, nv_ref, nexte_ref, tcur_ref, tnxt_ref, hp_ref, w1_ref, b1_ref, w2_ref, b2_ref,
                   o_ref, hbuf, xb0, xb1, w1buf, w2buf, w2b, sem, wsems):
    b = pl.program_id(0)
    nv = nv_ref[0]
    bm = xb0.shape[0]

    def gather(tok_ref, dst):
        for r in range(bm):
            dst[pl.ds(r, 1), :] = hbuf[pl.ds(tok_ref[0, 0, r], 1), :]

    def weight_copies(e, slot):
        return (pltpu.make_async_copy(w1_ref.at[e], w1buf.at[slot], wsems.at[0, slot]),
                pltpu.make_async_copy(w2_ref.at[e], w2buf.at[slot], wsems.at[1, slot]))

    e = be_ref[b]
    prev_e = be_ref[jnp.maximum(b - 1, 0)]
    first_of_expert = jnp.logical_and(b < nv, jnp.logical_or(b == 0, e != prev_e))
    slot = lax.rem(nexte_ref[N_EXPERTS + e], 2)

    @pl.when(b == 0)
    def _():
        for cp in weight_copies(e, slot):
            cp.start()
        load = pltpu.make_async_copy(hp_ref, hbuf, sem)
        load.start()
        load.wait()
        gather(tcur_ref, xb0)

    @pl.when(first_of_expert)
    def _():
        for cp in weight_copies(e, slot):
            cp.wait()
        nxt = nexte_ref[e]

        @pl.when(nxt >= 0)
        def _():
            for cp in weight_copies(nxt, 1 - slot):
                cp.start()

        w2b[...] = w2buf[slot].astype(BF16)

    def run(cur, nxt_rows):
        x = _unpack_halves(cur[...])
        half = PERM_TILE // 2
        acts = []
        for j in range(w1buf.shape[2] // PERM_TILE):
            cols = slice(j * PERM_TILE, (j + 1) * PERM_TILE)
            h = jnp.dot(x, w1buf[slot, :, cols], preferred_element_type=F32) + b1_ref[0, :, cols]
            glu = jnp.minimum(h[:, :half], SWIGLU_LIMIT)
            lin = jnp.clip(h[:, half:], -SWIGLU_LIMIT, SWIGLU_LIMIT)
            acts.append((glu * jax.nn.sigmoid(SWIGLU_ALPHA * glu) * (lin + 1.0)).astype(BF16))
        act = jnp.concatenate(acts, axis=1)
        y = jnp.dot(act, w2b[...], preferred_element_type=F32) + b2_ref[0]
        _store_row_tiles(o_ref, y)
        gather(tnxt_ref, nxt_rows)

    even = lax.rem(b, 2) == 0

    @pl.when(jnp.logical_and(b < nv, even))
    def _():
        run(xb0, xb1)

    @pl.when(jnp.logical_and(b < nv, jnp.logical_not(even)))
    def _():
        run(xb1, xb0)

    @pl.when(b >= nv)
    def _():
        o_ref[...] = jnp.zeros_like(o_ref)


def _experts(block_e, n_valid, next_e, slot_tok, hp, w1p, b1p, w2, b2):
    d = D_MODEL
    f2 = w1p.shape[2]
    f = w2.shape[1]
    bm = EXPERT_ROWS
    nb = slot_tok.shape[0] // bm
    tok3 = slot_tok.reshape(nb, 1, bm)
    exp = lambda b, be, nv, ne: be[jnp.maximum(jnp.minimum(b, nv[0] - 1), 0)]
    grid_spec = pltpu.PrefetchScalarGridSpec(
        num_scalar_prefetch=3,
        grid=(nb,),
        in_specs=[pl.BlockSpec((1, 1, bm), lambda b, be, nv, ne: (b, 0, 0), memory_space=pltpu.SMEM),
                  pl.BlockSpec((1, 1, bm), lambda b, be, nv, ne: (jnp.minimum(b + 1, nb - 1), 0, 0),
                               memory_space=pltpu.SMEM),
                  pl.BlockSpec(memory_space=pl.ANY),
                  pl.BlockSpec(memory_space=pl.ANY),
                  pl.BlockSpec((1, 1, f2), lambda b, be, nv, ne: (exp(b, be, nv, ne), 0, 0)),
                  pl.BlockSpec(memory_space=pl.ANY),
                  pl.BlockSpec((1, 1, d), lambda b, be, nv, ne: (exp(b, be, nv, ne), 0, 0))],
        out_specs=pl.BlockSpec((bm * ROW_SUB, LANES), lambda b, be, nv, ne: (b, 0)),
        scratch_shapes=[pltpu.VMEM(hp.shape, hp.dtype), pltpu.VMEM((bm, hp.shape[1]), hp.dtype),
                        pltpu.VMEM((bm, hp.shape[1]), hp.dtype),
                        pltpu.VMEM((2,) + w1p.shape[1:], w1p.dtype), pltpu.VMEM((2,) + w2.shape[1:], w2.dtype),
                        pltpu.VMEM(w2.shape[1:], BF16),
                        pltpu.SemaphoreType.DMA(()), pltpu.SemaphoreType.DMA((2, 2))],
    )
    return pl.pallas_call(
        _expert_kernel,
        grid_spec=grid_spec,
        out_shape=jax.ShapeDtypeStruct((nb * bm * ROW_SUB, LANES), F32),
        compiler_params=pltpu.CompilerParams(
            dimension_semantics=("arbitrary",), vmem_limit_bytes=VMEM_LIMIT),
        name="experts",
    )(block_e, n_valid, next_e, tok3, tok3, hp, w1p, b1p, w2, b2)


def _combine_kernel(dcur_ref, dnxt_ref, gate_ref, x1_ref, mod_ref, nf_ref, y_ref, o_ref, buf, sems):
    nt = x1_ref.shape[0]
    i = pl.program_id(0)
    n = pl.num_programs(0)
    slot = lax.rem(i, 2)

    def row_copy(src_row, s, kk, r):
        return pltpu.make_async_copy(y_ref.at[pl.ds(src_row * ROW_SUB, ROW_SUB), :],
                                     buf.at[s, kk, pl.ds(r * ROW_SUB, ROW_SUB), :], sems.at[s])

    def fetch(d_ref, s):
        for j in range(nt):
            for kk in range(TOP_K):
                row_copy(d_ref[0, 0, j * TOP_K + kk], s, kk, j).start(priority=kk % DMA_QUEUES)

    def wait_all(s):
        for kk in range(TOP_K):
            pltpu.make_async_copy(y_ref.at[pl.ds(0, nt * ROW_SUB), :], buf.at[s, kk], sems.at[s]).wait()

    @pl.when(i == 0)
    def _():
        fetch(dcur_ref, 0)

    fetch(dnxt_ref, 1 - slot)
    wait_all(slot)

    g = gate_ref[...]
    parts = []
    for s in range(ROW_SUB):
        acc = buf[slot, 0, pl.ds(s, nt, stride=ROW_SUB), :] * g[:, 0:1]
        for kk in range(1, TOP_K):
            acc = acc + buf[slot, kk, pl.ds(s, nt, stride=ROW_SUB), :] * g[:, kk:kk + 1]
        parts.append(acc)
    moe = jnp.concatenate(parts, axis=1)
    gate2 = mod_ref[0, 5:6, :]
    x2 = x1_ref[...] + gate2 * moe
    o_ref[...] = _rms(x2) * nf_ref[...]

    @pl.when(i == n - 1)
    def _():
        wait_all(1 - slot)


def _combine(dest, gates, x1, mod, normf_w, y, seq):
    t, d = x1.shape
    nt = min(COMBINE_TOKENS, seq)
    steps = t // nt
    per_batch = seq // nt
    dest2 = dest.reshape(steps, 1, nt * TOP_K)
    return pl.pallas_call(
        _combine_kernel,
        grid=(steps,),
        in_specs=[pl.BlockSpec((1, 1, nt * TOP_K), lambda i: (i, 0, 0), memory_space=pltpu.SMEM),
                  pl.BlockSpec((1, 1, nt * TOP_K), lambda i: (jnp.minimum(i + 1, steps - 1), 0, 0),
                               memory_space=pltpu.SMEM),
                  pl.BlockSpec((nt, 2 * TOP_K), lambda i: (i, 0)),
                  pl.BlockSpec((nt, d), lambda i: (i, 0)),
                  pl.BlockSpec((1, 6, d), lambda i: (i // per_batch, 0, 0)),
                  pl.BlockSpec((1, d), lambda i: (0, 0)),
                  pl.BlockSpec(memory_space=pl.ANY)],
        out_specs=pl.BlockSpec((nt, d), lambda i: (i, 0)),
        out_shape=jax.ShapeDtypeStruct((t, d), F32),
        scratch_shapes=[pltpu.VMEM((2, TOP_K, nt * ROW_SUB, LANES), F32),
                        pltpu.SemaphoreType.DMA((2,))],
        compiler_params=pltpu.CompilerParams(
            dimension_semantics=("arbitrary",), vmem_limit_bytes=VMEM_LIMIT),
        name="combine",
    )(dest2, dest2, gates, x1, mod, normf_w, y)


def kernel(x, c, w_ada, b_ada, norm1_w, w_in, lam_q1, lam_k1, lam_q2, lam_k2, subln_w, rel_bias,
           w_out, norm2_w, w_router, b_router, w1, b1, w2, b2, normf_w):
    batch, seq, d = x.shape
    t = batch * seq
    x2 = x.reshape(t, d)

    c_pad = jnp.zeros((8, d), F32).at[:batch].set(c)
    mod = _ada(c_pad, w_ada[0], b_ada[0][None, :])[:batch].reshape(batch, 6, d)

    q_r, k_r, v_r, g_r, q_d, k_d, vt_d = _in_proj(x2, mod, norm1_w[0][None, :],
                                                 w_in[0].astype(BF16), seq)
    y_r = _retention(q_r, k_r, v_r, g_r, batch, seq)
    y_d = _diff_attention(q_d, k_d, vt_d, rel_bias, lam_q1, lam_k1, lam_q2, lam_k2,
                          subln_w, batch, seq)

    x1, hp, meta, gates, counts = _out_router(y_r, y_d, x2, mod, norm2_w[0][None, :],
                                              w_out[0].astype(BF16), w_router[0],
                                              b_router[0][:, None], seq)

    bm = EXPERT_ROWS
    padded = (counts + bm - 1) // bm * bm
    pad_end = jnp.cumsum(padded)
    pad_start = pad_end - padded
    sel_e = meta[:, :TOP_K]
    hot_e = sel_e[:, :, None] == jnp.arange(N_EXPERTS, dtype=jnp.int32)[None, None, :]
    dest = jnp.sum(jnp.where(hot_e, pad_start[None, None, :], 0), axis=-1) + meta[:, TOP_K:]
    n_rows = (t * TOP_K // bm + N_EXPERTS) * bm
    nb = n_rows // bm
    block_start = jnp.arange(nb, dtype=jnp.int32) * bm
    block_e = jnp.minimum(jnp.sum((pad_end[None, :] <= block_start[:, None]).astype(jnp.int32), axis=1),
                          N_EXPERTS - 1)
    n_valid = (pad_end[-1:] // bm).astype(jnp.int32)
    slot_tok = _slot_tokens(dest, n_rows)

    owns = padded > 0
    ids = jnp.arange(N_EXPERTS, dtype=jnp.int32)
    later = jnp.logical_and(owns[None, :], ids[None, :] > ids[:, None])
    next_owner = jnp.min(jnp.where(later, ids[None, :], N_EXPERTS), axis=1)
    next_owner = jnp.where(next_owner == N_EXPERTS, -1, next_owner).astype(jnp.int32)
    ordinal = (jnp.cumsum(owns.astype(jnp.int32)) - 1).astype(jnp.int32)
    next_e = jnp.concatenate([next_owner, jnp.maximum(ordinal, 0)])

    ys = _experts(block_e, n_valid, next_e, slot_tok, hp, _w1_prep(w1[0]), _pair_split_bias(b1[0]),
                  w2[0], b2[0][:, None, :])
    out = _combine(dest, gates, x1, mod, normf_w[None, :], ys, seq)
    return out.reshape(batch, seq, d)
```

```python
import math

import jax
import jax.numpy as jnp
from jax import lax
from jax.experimental import pallas as pl
from jax.experimental.pallas import tpu as pltpu
from jax.experimental.pallas import tpu_sc as plsc

F32 = jnp.float32
BF16 = jnp.bfloat16

D_MODEL = 1024
RET_HEADS = 4
RET_KEY_DIM = 64
RET_VAL_DIM = 128
RET_QK_W = RET_HEADS * RET_KEY_DIM
RET_V_W = RET_HEADS * RET_VAL_DIM
RET_CHUNK = 128
DIFF_HEADS = 4
DIFF_HEAD_DIM = 64
DIFF_QK_W = DIFF_HEADS * 2 * DIFF_HEAD_DIM
DIFF_V_W = DIFF_HEADS * 2 * DIFF_HEAD_DIM
IN_SIZES = (RET_QK_W, RET_QK_W, RET_V_W, RET_V_W, DIFF_QK_W, DIFF_QK_W, DIFF_V_W)
REL_BUCKETS = 32
REL_MAX_DIST = 128
N_EXPERTS = 32
TOP_K = 4
SWIGLU_ALPHA = 1.702
SWIGLU_LIMIT = 7.0
NORM_EPS = 1e-6
LAMBDA_INIT = 0.8 - 0.6 * math.exp(-0.3 * 0)

LANES = 128
ROW_SUB = D_MODEL // LANES
NEG_BIG = -1e30
LOG2_E = math.log2(math.e)
ONES_ROWS = 16
VMEM_LIMIT = 56 * 1024 * 1024

ROW_TILE = 512
RET_ROWS = 512
ATT_BLOCK = 512
ATT_STRIP = 256
ATT_HEADS = 4
EXPERT_ROWS = 256
PERM_TILE = 256
COMBINE_TOKENS = 256
DMA_QUEUES = 2
SC_CORES = 2
SC_SUBCORES = 16
SC_LANES = 16
SC_SLOT_CHUNK = 4096


def _rms(x):
    return x * lax.rsqrt(jnp.mean(x * x, axis=-1, keepdims=True) + NORM_EPS)


def _store_row_tiles(ref, x):
    rows = x.shape[0]
    for s in range(ROW_SUB):
        ref[pl.ds(s, rows, stride=ROW_SUB), :] = x[:, s * LANES:(s + 1) * LANES]


def _pack_halves(x):
    half = x.shape[1] // 2
    lo = lax.bitcast_convert_type(x[:, :half].astype(BF16).astype(F32), jnp.uint32)
    hi = lax.bitcast_convert_type(x[:, half:].astype(BF16).astype(F32), jnp.uint32)
    return (hi & jnp.uint32(0xFFFF0000)) | (lo >> 16)


def _unpack_halves(w):
    lo = lax.bitcast_convert_type(w << 16, F32).astype(BF16)
    hi = lax.bitcast_convert_type(w & jnp.uint32(0xFFFF0000), F32).astype(BF16)
    return jnp.concatenate([lo, hi], axis=1)


def _ada_kernel(c_ref, w_ref, b_ref, o_ref):
    c = c_ref[...]
    cond = c * jax.nn.sigmoid(c)
    o_ref[...] = jnp.dot(cond, w_ref[...], precision=lax.Precision.HIGHEST,
                         preferred_element_type=F32) + b_ref[...]


def _ada(c_pad, w_ada, b_ada):
    rows, d = c_pad.shape
    n = w_ada.shape[1]
    tn = 1024
    return pl.pallas_call(
        _ada_kernel,
        grid=(n // tn,),
        in_specs=[pl.BlockSpec((rows, d), lambda j: (0, 0)),
                  pl.BlockSpec((d, tn), lambda j: (0, j)),
                  pl.BlockSpec((1, tn), lambda j: (0, j))],
        out_specs=pl.BlockSpec((rows, tn), lambda j: (0, j)),
        out_shape=jax.ShapeDtypeStruct((rows, n), F32),
        name="ada",
    )(c_pad, w_ada, b_ada)


def _in_proj_kernel(x_ref, mod_ref, nw_ref, w_ref, *o_refs):
    x = x_ref[...]
    shift = mod_ref[0, 0:1, :]
    scale = mod_ref[0, 1:2, :]
    h = (_rms(x) * nw_ref[...]) * (1.0 + scale) + shift
    hb = h.astype(BF16)
    off = 0
    for o_ref, width in zip(o_refs[:-1], IN_SIZES[:-1]):
        o_ref[...] = jnp.dot(hb, w_ref[:, off:off + width],
                             preferred_element_type=F32).astype(o_ref.dtype)
        off += width
    vt_ref = o_refs[-1]
    v = jnp.dot(hb, w_ref[:, off:], preferred_element_type=F32)
    hw = 2 * DIFF_HEAD_DIM
    tail = (lax.broadcasted_iota(jnp.int32, (ONES_ROWS, v.shape[0]), 0) == 0).astype(vt_ref.dtype)
    for hd in range(DIFF_HEADS):
        vt_ref[0, hd, 0, 0:hw, :] = v[:, hd * hw:(hd + 1) * hw].T.astype(vt_ref.dtype)
        vt_ref[0, hd, 0, hw:, :] = tail


def _in_proj(x2, mod, norm_w, w_in_bf16, seq):
    t, d = x2.shape
    tm = min(ROW_TILE, seq)
    assert tm == min(ATT_BLOCK, seq)
    per_batch = seq // tm
    in_w = w_in_bf16.shape[1]
    hw = 2 * DIFF_HEAD_DIM
    vt_shape = (t // seq, DIFF_HEADS, per_batch, hw + ONES_ROWS, tm)
    return pl.pallas_call(
        _in_proj_kernel,
        grid=(t // tm,),
        in_specs=[pl.BlockSpec((tm, d), lambda i: (i, 0)),
                  pl.BlockSpec((1, 6, d), lambda i: (i // per_batch, 0, 0)),
                  pl.BlockSpec((1, d), lambda i: (0, 0)),
                  pl.BlockSpec((d, in_w), lambda i: (0, 0))],
        out_specs=([pl.BlockSpec((tm, w), lambda i: (i, 0)) for w in IN_SIZES[:-1]]
                   + [pl.BlockSpec((1,) + vt_shape[1:2] + (1,) + vt_shape[3:],
                                   lambda i: (i // per_batch, 0, i % per_batch, 0, 0))]),
        out_shape=([jax.ShapeDtypeStruct((t, w), BF16) for w in IN_SIZES[:-1]]
                   + [jax.ShapeDtypeStruct(vt_shape, BF16)]),
        compiler_params=pltpu.CompilerParams(vmem_limit_bytes=VMEM_LIMIT),
        name="in_proj",
    )(x2, mod, norm_w, w_in_bf16)


def _rotary(x, cos, sin_even, sin_odd):
    nxt = pltpu.roll(x, LANES - 1, 1)
    prv = pltpu.roll(x, 1, 1)
    return x * cos + nxt * sin_even + prv * sin_odd


def _ret_kernel(q_ref, k_ref, v_ref, g_ref, cos_ref, sine_ref, sino_ref,
                hmask_ref, xi_ref, zeta_ref, dmask_ref, gch_ref, o_ref, state_ref):
    @pl.when(pl.program_id(1) == 0)
    def _():
        state_ref[...] = jnp.zeros_like(state_ref)

    n_sub = q_ref.shape[0] // RET_CHUNK
    for c in range(n_sub):
        rows = slice(c * RET_CHUNK, (c + 1) * RET_CHUNK)
        for pair in range(RET_HEADS // 2):
            lanes = slice(pair * LANES, (pair + 1) * LANES)
            cos = cos_ref[rows, :]
            sine = sine_ref[rows, :]
            sino = sino_ref[rows, :]
            qr = _rotary(q_ref[rows, lanes].astype(F32), cos, sine, sino)
            kr = _rotary(k_ref[rows, lanes].astype(F32), cos, sine, sino) * (RET_KEY_DIM ** -0.5)
            qb = qr.astype(BF16)
            for hh in range(2):
                h = 2 * pair + hh
                vcols = slice(h * RET_VAL_DIM, (h + 1) * RET_VAL_DIM)
                v = v_ref[rows, vcols]
                km = (kr * hmask_ref[h]).astype(BF16)
                scores = lax.dot_general(qb, km, (((1,), (1,)), ((), ())),
                                         preferred_element_type=F32) * dmask_ref[h]
                inner = jnp.dot(scores.astype(BF16), v, preferred_element_type=F32)
                qx = (qr * xi_ref[h]).astype(BF16)
                state = state_ref[h]
                cross = jnp.dot(qx, state.astype(BF16), preferred_element_type=F32)
                kz = (kr * zeta_ref[h]).astype(BF16)
                kv = lax.dot_general(kz, v, (((0,), (0,)), ((), ())),
                                     preferred_element_type=F32)
                state_ref[h] = state * gch_ref[h] + kv
                y = _rms(inner + cross)
                g = g_ref[rows, vcols].astype(F32)
                o_ref[rows, vcols] = (g * jax.nn.sigmoid(g) * y).astype(o_ref.dtype)


def _retention_tables(seq):
    dk, c, nh = RET_KEY_DIM, RET_CHUNK, RET_HEADS
    pos = jnp.arange(seq, dtype=F32)
    inv_freq = 1.0 / (10000.0 ** jnp.linspace(0.0, 1.0, dk // 2, dtype=F32))
    ang = pos[:, None] * jnp.repeat(inv_freq, 2)[None, :]
    sin = jnp.tile(jnp.sin(ang), (1, LANES // dk))
    cos = jnp.tile(jnp.cos(ang), (1, LANES // dk))
    even = (jnp.arange(LANES) % 2 == 0)[None, :]
    sin_even = jnp.where(even, -sin, 0.0)
    sin_odd = jnp.where(even, 0.0, sin)
    log_g = jnp.log(1.0 - 2.0 ** (-5.0 - jnp.arange(nh, dtype=F32)))
    i = jnp.arange(c, dtype=F32)
    rel = i[:, None] - i[None, :]
    dmask = jnp.where(rel[None] >= 0,
                      jnp.exp(jnp.maximum(rel, 0.0)[None] * log_g[:, None, None]), 0.0)
    zeta = jnp.exp((c - 1.0 - i)[None, :] * log_g[:, None])
    xi = jnp.exp((i + 1.0)[None, :] * log_g[:, None])
    g_chunk = jnp.exp(c * log_g)
    lane = jnp.arange(LANES)
    hmask = jnp.stack([((lane // dk) == (h % 2)).astype(F32) for h in range(nh)])[:, None, :]
    xi_t = xi[:, :, None] * hmask
    zeta_t = zeta[:, :, None] * hmask
    gch = jnp.broadcast_to(g_chunk[:, None, None], (nh, 1, LANES))
    return cos, sin_even, sin_odd, hmask, xi_t, zeta_t, dmask, gch


def _retention(q, k, v, g, batch, seq):
    t = q.shape[0]
    rb = min(RET_ROWS, seq)
    per_batch = seq // rb
    cos, sin_even, sin_odd, hmask, xi_t, zeta_t, dmask, gch = _retention_tables(seq)
    row = lambda w: pl.BlockSpec((rb, w), lambda b, j: (b * per_batch + j, 0))
    tab = lambda w: pl.BlockSpec((rb, w), lambda b, j: (j, 0))
    full = lambda a: pl.BlockSpec(a.shape, lambda b, j: (0,) * a.ndim)
    return pl.pallas_call(
        _ret_kernel,
        grid=(batch, per_batch),
        in_specs=[row(RET_QK_W), row(RET_QK_W), row(RET_V_W), row(RET_V_W),
                  tab(LANES), tab(LANES), tab(LANES),
                  full(hmask), full(xi_t), full(zeta_t), full(dmask), full(gch)],
        out_specs=row(RET_V_W),
        out_shape=jax.ShapeDtypeStruct((t, RET_V_W), BF16),
        scratch_shapes=[pltpu.VMEM((RET_HEADS, LANES, RET_VAL_DIM), F32)],
        compiler_params=pltpu.CompilerParams(
            dimension_semantics=("arbitrary", "arbitrary"), vmem_limit_bytes=VMEM_LIMIT),
        name="retention",
    )(q, k, v, g, cos, sin_even, sin_odd, hmask, xi_t, zeta_t, dmask, gch)


def _t5_bucket(rel):
    n = jnp.maximum(rel, 0)
    max_exact = REL_BUCKETS // 2
    nf = jnp.maximum(n, 1).astype(F32)
    large = max_exact + (jnp.log(nf / max_exact) / math.log(REL_MAX_DIST / max_exact)
                         * (REL_BUCKETS - max_exact)).astype(jnp.int32)
    large = jnp.minimum(large, REL_BUCKETS - 1)
    return jnp.where(n < max_exact, n, large)


def _bias_tiles(rel_bias, blk):
    r = jnp.arange(blk, dtype=jnp.int32)
    far = rel_bias[REL_BUCKETS - 1]
    rel0 = r[None, :] - r[:, None]
    rel1 = rel0 + blk
    buckets = jnp.arange(REL_BUCKETS, dtype=jnp.int32)

    def tile(rel):
        hot = (_t5_bucket(rel)[:, :, None] == buckets).astype(F32)
        return jnp.einsum('krb,bh->hkr', hot, rel_bias, precision=lax.Precision.HIGHEST)

    b0 = jnp.where(rel0[None] >= 0, (tile(rel0) - far[:, None, None]) * LOG2_E, NEG_BIG)
    b1 = (tile(rel1) - far[:, None, None]) * LOG2_E
    return b0, b1


def _attn_kernel(q_ref, k_ref, vt_ref, b0_ref, b1_ref, lq1_ref, lk1_ref, lq2_ref, lk2_ref,
                 sw_ref, o_ref, m_ref, acc_ref):
    blk = q_ref.shape[0]
    hw = 2 * DIFF_HEAD_DIM
    i = pl.program_id(2)
    lane = lax.broadcasted_iota(jnp.int32, (1, LANES), 1)
    qm = []
    for hd in range(ATT_HEADS):
        q = (q_ref[:, hd * hw:(hd + 1) * hw].astype(F32) * (DIFF_HEAD_DIM ** -0.5 * LOG2_E)).astype(BF16)
        zero = jnp.zeros_like(q)
        qm.append((jnp.where(lane < DIFF_HEAD_DIM, q, zero), jnp.where(lane >= DIFF_HEAD_DIM, q, zero)))

    m_ref[...] = jnp.full_like(m_ref, NEG_BIG)
    acc_ref[...] = jnp.zeros_like(acc_ref)

    def step(blocks):
        kbs = [k_ref[pl.ds(pl.multiple_of(j * blk, blk), blk), :] for j, _, _ in blocks]
        chains = [(hd, mi, qs) for hd in range(ATT_HEADS) for mi in range(2)
                  for qs in range(blk // ATT_STRIP)]
        scores = []
        for hd, mi, qs in chains:
            qc = slice(qs * ATT_STRIP, (qs + 1) * ATT_STRIP)
            row = []
            for (_, bias, diagonal), kb in zip(blocks, kbs):
                keys = (qs + 1) * ATT_STRIP if diagonal else blk
                s = lax.dot_general(kb[:keys, hd * hw:(hd + 1) * hw], qm[hd][mi][qc, :],
                                    (((1,), (1,)), ((), ())), preferred_element_type=F32)
                row.append(s if bias is None else s + bias[hd, :keys, qc])
            scores.append(row)
        stats = []
        for (hd, mi, qs), row in zip(chains, scores):
            qc = slice(qs * ATT_STRIP, (qs + 1) * ATT_STRIP)
            m_old = m_ref[hd, mi, :, qc]
            m_new = m_old
            for s in row:
                m_new = jnp.maximum(m_new, jnp.max(s, axis=0, keepdims=True))
            stats.append((jnp.exp2(m_old - m_new), [jnp.exp2(s - m_new).astype(BF16) for s in row], m_new))
        for (hd, mi, qs), (alpha, ps, m_new) in zip(chains, stats):
            qc = slice(qs * ATT_STRIP, (qs + 1) * ATT_STRIP)
            pv = None
            for (j, _, _), p in zip(blocks, ps):
                part = jnp.dot(vt_ref[0, hd, j, :, 0:p.shape[0]], p, preferred_element_type=F32)
                pv = part if pv is None else pv + part
            acc_ref[hd, mi, :, qc] = alpha * acc_ref[hd, mi, :, qc] + pv
            m_ref[hd, mi, :, qc] = m_new

    n_far = jnp.maximum(i - 1, 0)

    def far_pair(pair, carry):
        step([(2 * pair, None, False), (2 * pair + 1, None, False)])
        return carry

    lax.fori_loop(0, lax.shift_right_logical(n_far, 1), far_pair, 0)

    @pl.when(lax.rem(n_far, 2) == 1)
    def _():
        step([(n_far - 1, None, False)])

    @pl.when(i >= 1)
    def _():
        step([(i - 1, b1_ref, False), (i, b0_ref, True)])

    @pl.when(i == 0)
    def _():
        step([(i, b0_ref, True)])

    lam = (jnp.exp(jnp.sum(lq1_ref[...] * lk1_ref[...], axis=-1, keepdims=True))
           - jnp.exp(jnp.sum(lq2_ref[...] * lk2_ref[...], axis=-1, keepdims=True))
           + LAMBDA_INIT)
    for hd in range(ATT_HEADS):
        a = (acc_ref[hd, 0, :hw, :] / acc_ref[hd, 0, hw:hw + 1, :]
             - lam * (acc_ref[hd, 1, :hw, :] / acc_ref[hd, 1, hw:hw + 1, :]))
        a = a * lax.rsqrt(jnp.mean(a * a, axis=0, keepdims=True) + NORM_EPS)
        o_ref[:, hd * hw:(hd + 1) * hw] = (a.T * sw_ref[...] * (1.0 - LAMBDA_INIT)).astype(o_ref.dtype)


def _diff_attention(q, k, vt, rel_bias, lq1, lk1, lq2, lk2, subln_w, batch, seq):
    t = q.shape[0]
    blk = min(ATT_BLOCK, seq)
    nq = seq // blk
    hw = 2 * DIFF_HEAD_DIM
    nh = ATT_HEADS
    b0, b1 = _bias_tiles(rel_bias, blk)
    small = lambda a: pl.BlockSpec(a.shape, lambda b, h, i: (0,) * a.ndim)
    return pl.pallas_call(
        _attn_kernel,
        grid=(batch, DIFF_HEADS // nh, nq),
        in_specs=[pl.BlockSpec((blk, nh * hw), lambda b, h, i: (b * nq + i, h)),
                  pl.BlockSpec((seq, nh * hw), lambda b, h, i: (b, h)),
                  pl.BlockSpec((1, nh, nq, hw + ONES_ROWS, blk), lambda b, h, i: (b, h, 0, 0, 0)),
                  pl.BlockSpec((nh, blk, blk), lambda b, h, i: (h, 0, 0)),
                  pl.BlockSpec((nh, blk, blk), lambda b, h, i: (h, 0, 0)),
                  small(lq1), small(lk1), small(lq2), small(lk2), small(subln_w)],
        out_specs=pl.BlockSpec((blk, nh * hw), lambda b, h, i: (b * nq + i, h)),
        out_shape=jax.ShapeDtypeStruct((t, DIFF_V_W), BF16),
        scratch_shapes=[pltpu.VMEM((nh, 2, 1, blk), F32), pltpu.VMEM((nh, 2, hw + ONES_ROWS, blk), F32)],
        compiler_params=pltpu.CompilerParams(
            dimension_semantics=("arbitrary", "arbitrary", "arbitrary"),
            vmem_limit_bytes=VMEM_LIMIT),
        name="diff_attn",
    )(q, k, vt, b0, b1, lq1, lk1, lq2, lk2, subln_w)


def _out_kernel(yr_ref, yd_ref, x_ref, mod_ref, nw_ref, wo_ref, wr_ref, br_ref, upper_ref,
                x1_ref, hp_ref, meta_ref, gate_ref, cnt_ref, run_ref):
    tm = x_ref.shape[0]
    ne = run_ref.shape[0]

    @pl.when(pl.program_id(0) == 0)
    def _():
        run_ref[...] = jnp.zeros_like(run_ref)

    mixed = (jnp.dot(yr_ref[...], wo_ref[0:RET_V_W, :], preferred_element_type=F32)
             + jnp.dot(yd_ref[...], wo_ref[RET_V_W:, :], preferred_element_type=F32))
    gate1 = mod_ref[0, 2:3, :]
    shift2 = mod_ref[0, 3:4, :]
    scale2 = mod_ref[0, 4:5, :]
    x1 = x_ref[...] + gate1 * mixed
    x1_ref[...] = x1
    h2 = (_rms(x1) * nw_ref[...]) * (1.0 + scale2) + shift2
    hp_ref[...] = _pack_halves(h2)

    h_hi = h2.astype(BF16)
    h_lo = (h2 - h_hi.astype(F32)).astype(BF16)
    nt_dims = (((1,), (1,)), ((), ()))
    both = lax.dot_general(wr_ref[...], h_hi, nt_dims, preferred_element_type=F32)
    low = lax.dot_general(wr_ref[0:ne, :], h_lo, nt_dims, preferred_element_type=F32)
    logits = both[0:ne, :] + both[ne:, :] + low + br_ref[...]

    row = lax.broadcasted_iota(jnp.int32, logits.shape, 0)
    work = logits
    vals, idxs, hots = [], [], []
    for _ in range(TOP_K):
        mx = jnp.max(work, axis=0, keepdims=True)
        idx = jnp.min(jnp.where(work == mx, row, ne), axis=0, keepdims=True)
        hot = row == idx
        vals.append(mx)
        idxs.append(idx)
        hots.append(hot)
        work = jnp.where(hot, -jnp.inf, work)
    exps = [jnp.exp(v - vals[0]) for v in vals]
    denom = exps[0] + exps[1] + exps[2] + exps[3]

    sel = jnp.zeros(logits.shape, F32)
    for hot in hots:
        sel = sel + hot.astype(F32)
    prefix = jnp.dot(sel.astype(BF16), upper_ref[...], preferred_element_type=F32) + run_ref[...]
    ranks = [jnp.sum(jnp.where(hot, prefix, 0.0), axis=0, keepdims=True) for hot in hots]
    run_ref[...] = run_ref[...] + jnp.sum(sel, axis=1, keepdims=True)
    cnt_ref[...] = jnp.broadcast_to(run_ref[...], cnt_ref.shape).astype(jnp.int32)

    meta_ref[...] = jnp.concatenate(idxs + [r.astype(jnp.int32) for r in ranks], axis=0)
    gate_ref[...] = jnp.concatenate([e / denom for e in exps] + [jnp.zeros_like(denom)] * TOP_K, axis=0)


def _out_router(y_r, y_d, x2, mod, norm_w, w_out_bf16, w_router, b_router, seq):
    t, d = x2.shape
    tm = min(ROW_TILE, seq)
    per_batch = seq // tm
    ne = w_router.shape[1]
    w_hi = w_router.astype(BF16)
    w_lo = (w_router - w_hi.astype(F32)).astype(BF16)
    wr_t = jnp.concatenate([w_hi, w_lo], axis=1).T
    idx = jnp.arange(tm, dtype=jnp.int32)
    upper = (idx[:, None] < idx[None, :]).astype(BF16)
    row = lambda w: pl.BlockSpec((tm, w), lambda i: (i, 0))
    col = lambda h: pl.BlockSpec((h, tm), lambda i: (0, i))
    const = lambda a: pl.BlockSpec(a.shape, lambda i: (0,) * a.ndim)
    x1, hp, meta_t, gates_t, counts = pl.pallas_call(
        _out_kernel,
        grid=(t // tm,),
        in_specs=[row(RET_V_W), row(DIFF_V_W), row(d),
                  pl.BlockSpec((1, 6, d), lambda i: (i // per_batch, 0, 0)),
                  const(norm_w), const(w_out_bf16), const(wr_t), const(b_router), const(upper)],
        out_specs=[row(d), row(d // 2), col(2 * TOP_K), col(2 * TOP_K),
                   pl.BlockSpec((ne, LANES), lambda i: (0, 0))],
        out_shape=[jax.ShapeDtypeStruct((t, d), F32),
                   jax.ShapeDtypeStruct((t, d // 2), jnp.uint32),
                   jax.ShapeDtypeStruct((2 * TOP_K, t), jnp.int32),
                   jax.ShapeDtypeStruct((2 * TOP_K, t), F32),
                   jax.ShapeDtypeStruct((ne, LANES), jnp.int32)],
        scratch_shapes=[pltpu.VMEM((ne, 1), F32)],
        compiler_params=pltpu.CompilerParams(
            dimension_semantics=("arbitrary",), vmem_limit_bytes=VMEM_LIMIT),
        name="out_router",
    )(y_r, y_d, x2, mod, norm_w, w_out_bf16, wr_t, b_router, upper)
    return x1, hp, meta_t.T, gates_t.T, counts[:, 0]


def _w1_prep_kernel(w_ref, p_ref, o_ref):
    for s in range(w_ref.shape[2] // PERM_TILE):
        cols = slice(s * PERM_TILE, (s + 1) * PERM_TILE)
        o_ref[0, :, cols] = jnp.dot(w_ref[0, :, cols].astype(BF16), p_ref[...],
                                    preferred_element_type=F32).astype(BF16)


def _pair_split_matrix():
    i = jnp.arange(PERM_TILE)[:, None]
    j = jnp.arange(PERM_TILE)[None, :]
    half = PERM_TILE // 2
    src = jnp.where(j < half, 2 * j, 2 * (j - half) + 1)
    return (i == src).astype(BF16)


def _w1_prep(w1):
    e, d, f2 = w1.shape
    tn = 1024
    return pl.pallas_call(
        _w1_prep_kernel,
        grid=(e, f2 // tn),
        in_specs=[pl.BlockSpec((1, d, tn), lambda i, j: (i, 0, j)),
                  pl.BlockSpec((PERM_TILE, PERM_TILE), lambda i, j: (0, 0))],
        out_specs=pl.BlockSpec((1, d, tn), lambda i, j: (i, 0, j)),
        out_shape=jax.ShapeDtypeStruct((e, d, f2), BF16),
        compiler_params=pltpu.CompilerParams(vmem_limit_bytes=VMEM_LIMIT),
        name="w1_prep",
    )(w1, _pair_split_matrix())


def _pair_split_bias(b1):
    e, f2 = b1.shape
    nt = f2 // PERM_TILE
    g = b1[:, 0::2].reshape(e, nt, 1, PERM_TILE // 2)
    l = b1[:, 1::2].reshape(e, nt, 1, PERM_TILE // 2)
    return jnp.concatenate([g, l], axis=2).reshape(e, 1, f2)


def _slot_tokens(dest, n_rows):
    assert TOP_K == 4
    flat = dest.reshape(-1)
    n_slots = flat.shape[0]
    chunk = min(SC_SLOT_CHUNK, n_slots)
    mesh = plsc.VectorSubcoreMesh(core_axis_name="c", subcore_axis_name="s",
                                  num_cores=SC_CORES, num_subcores=SC_SUBCORES)

    def body(dest_hbm, out_hbm, out_v, dest_v):
        first = jnp.logical_and(lax.axis_index("c") == 0, lax.axis_index("s") == 0)

        @pl.when(first)
        def _():
            zeros = jnp.zeros((SC_LANES,), jnp.int32)

            @pl.loop(0, n_rows // SC_LANES)
            def _(g):
                out_v[pl.ds(g * SC_LANES, SC_LANES)] = zeros

            lane = lax.iota(jnp.int32, SC_LANES)

            @pl.loop(0, n_slots // chunk)
            def _(c):
                pltpu.sync_copy(dest_hbm.at[pl.ds(c * chunk, chunk)], dest_v)

                @pl.loop(0, chunk // SC_LANES)
                def _(g):
                    rows = dest_v[pl.ds(g * SC_LANES, SC_LANES)]
                    slot = lane + (c * chunk + g * SC_LANES)
                    plsc.store_scatter(out_v, [rows], lax.shift_right_logical(slot, 2))

            pltpu.sync_copy(out_v, out_hbm)

    return pl.kernel(
        body,
        out_type=jax.ShapeDtypeStruct((n_rows,), jnp.int32),
        mesh=mesh,
        scratch_types=[pltpu.VMEM((n_rows,), jnp.int32), pltpu.VMEM((chunk,), jnp.int32)],
        compiler_params=pltpu.CompilerParams(needs_layout_passes=False),
        name="slot_tokens",
    )(flat)


def _expert_kernel(be_ref, nv_ref, nexte_ref, tcur_ref, tnxt_ref, hp_ref, w1_ref, b1_ref, w2_ref, b2_ref,
                   o_ref, hbuf, xb0, xb1, w1buf, w2buf, w2b, sem, wsems):
    b = pl.program_id(0)
    nv = nv_ref[0]
    bm = xb0.shape[0]

    def gather(tok_ref, dst):
        for r in range(bm):
            dst[pl.ds(r, 1), :] = hbuf[pl.ds(tok_ref[0, 0, r], 1), :]

    def weight_copies(e, slot):
        return (pltpu.make_async_copy(w1_ref.at[e], w1buf.at[slot], wsems.at[0, slot]),
                pltpu.make_async_copy(w2_ref.at[e], w2buf.at[slot], wsems.at[1, slot]))

    e = be_ref[b]
    prev_e = be_ref[jnp.maximum(b - 1, 0)]
    first_of_expert = jnp.logical_and(b < nv, jnp.logical_or(b == 0, e != prev_e))
    slot = lax.rem(nexte_ref[N_EXPERTS + e], 2)

    @pl.when(b == 0)
    def _():
        for cp in weight_copies(e, slot):
            cp.start()
        load = pltpu.make_async_copy(hp_ref, hbuf, sem)
        load.start()
        load.wait()
        gather(tcur_ref, xb0)

    @pl.when(first_of_expert)
    def _():
        for cp in weight_copies(e, slot):
            cp.wait()
        nxt = nexte_ref[e]

        @pl.when(nxt >= 0)
        def _():
            for cp in weight_copies(nxt, 1 - slot):
                cp.start()

        w2b[...] = w2buf[slot].astype(BF16)

    def run(cur, nxt_rows):
        x = _unpack_halves(cur[...])
        half = PERM_TILE // 2
        acts = []
        for j in range(w1buf.shape[2] // PERM_TILE):
            cols = slice(j * PERM_TILE, (j + 1) * PERM_TILE)
            h = jnp.dot(x, w1buf[slot, :, cols], preferred_element_type=F32) + b1_ref[0, :, cols]
            glu = jnp.minimum(h[:, :half], SWIGLU_LIMIT)
            lin = jnp.clip(h[:, half:], -SWIGLU_LIMIT, SWIGLU_LIMIT)
            acts.append((glu * jax.nn.sigmoid(SWIGLU_ALPHA * glu) * (lin + 1.0)).astype(BF16))
        act = jnp.concatenate(acts, axis=1)
        y = jnp.dot(act, w2b[...], preferred_element_type=F32) + b2_ref[0]
        _store_row_tiles(o_ref, y)
        gather(tnxt_ref, nxt_rows)

    even = lax.rem(b, 2) == 0

    @pl.when(jnp.logical_and(b < nv, even))
    def _():
        run(xb0, xb1)

    @pl.when(jnp.logical_and(b < nv, jnp.logical_not(even)))
    def _():
        run(xb1, xb0)

    @pl.when(b >= nv)
    def _():
        o_ref[...] = jnp.zeros_like(o_ref)


def _experts(block_e, n_valid, next_e, slot_tok, hp, w1p, b1p, w2, b2):
    d = D_MODEL
    f2 = w1p.shape[2]
    f = w2.shape[1]
    bm = EXPERT_ROWS
    nb = slot_tok.shape[0] // bm
    tok3 = slot_tok.reshape(nb, 1, bm)
    exp = lambda b, be, nv, ne: be[jnp.maximum(jnp.minimum(b, nv[0] - 1), 0)]
    grid_spec = pltpu.PrefetchScalarGridSpec(
        num_scalar_prefetch=3,
        grid=(nb,),
        in_specs=[pl.BlockSpec((1, 1, bm), lambda b, be, nv, ne: (b, 0, 0), memory_space=pltpu.SMEM),
                  pl.BlockSpec((1, 1, bm), lambda b, be, nv, ne: (jnp.minimum(b + 1, nb - 1), 0, 0),
                               memory_space=pltpu.SMEM),
                  pl.BlockSpec(memory_space=pl.ANY),
                  pl.BlockSpec(memory_space=pl.ANY),
                  pl.BlockSpec((1, 1, f2), lambda b, be, nv, ne: (exp(b, be, nv, ne), 0, 0)),
                  pl.BlockSpec(memory_space=pl.ANY),
                  pl.BlockSpec((1, 1, d), lambda b, be, nv, ne: (exp(b, be, nv, ne), 0, 0))],
        out_specs=pl.BlockSpec((bm * ROW_SUB, LANES), lambda b, be, nv, ne: (b, 0)),
        scratch_shapes=[pltpu.VMEM(hp.shape, hp.dtype), pltpu.VMEM((bm, hp.shape[1]), hp.dtype),
                        pltpu.VMEM((bm, hp.shape[1]), hp.dtype),
                        pltpu.VMEM((2,) + w1p.shape[1:], w1p.dtype), pltpu.VMEM((2,) + w2.shape[1:], w2.dtype),
                        pltpu.VMEM(w2.shape[1:], BF16),
                        pltpu.SemaphoreType.DMA(()), pltpu.SemaphoreType.DMA((2, 2))],
    )
    return pl.pallas_call(
        _expert_kernel,
        grid_spec=grid_spec,
        out_shape=jax.ShapeDtypeStruct((nb * bm * ROW_SUB, LANES), F32),
        compiler_params=pltpu.CompilerParams(
            dimension_semantics=("arbitrary",), vmem_limit_bytes=VMEM_LIMIT),
        name="experts",
    )(block_e, n_valid, next_e, tok3, tok3, hp, w1p, b1p, w2, b2)


def _combine_kernel(dcur_ref, dnxt_ref, gate_ref, x1_ref, mod_ref, nf_ref, y_ref, o_ref, buf, sems):
    nt = x1_ref.shape[0]
    i = pl.program_id(0)
    n = pl.num_programs(0)
    slot = lax.rem(i, 2)

    def row_copy(src_row, s, kk, r):
        return pltpu.make_async_copy(y_ref.at[pl.ds(src_row * ROW_SUB, ROW_SUB), :],
                                     buf.at[s, kk, pl.ds(r * ROW_SUB, ROW_SUB), :], sems.at[s])

    def fetch(d_ref, s):
        for j in range(nt):
            for kk in range(TOP_K):
                row_copy(d_ref[0, 0, j * TOP_K + kk], s, kk, j).start(priority=kk % DMA_QUEUES)

    def wait_all(s):
        for kk in range(TOP_K):
            pltpu.make_async_copy(y_ref.at[pl.ds(0, nt * ROW_SUB), :], buf.at[s, kk], sems.at[s]).wait()

    @pl.when(i == 0)
    def _():
        fetch(dcur_ref, 0)

    fetch(dnxt_ref, 1 - slot)
    wait_all(slot)

    g = gate_ref[...]
    parts = []
    for s in range(ROW_SUB):
        acc = buf[slot, 0, pl.ds(s, nt, stride=ROW_SUB), :] * g[:, 0:1]
        for kk in range(1, TOP_K):
            acc = acc + buf[slot, kk, pl.ds(s, nt, stride=ROW_SUB), :] * g[:, kk:kk + 1]
        parts.append(acc)
    moe = jnp.concatenate(parts, axis=1)
    gate2 = mod_ref[0, 5:6, :]
    x2 = x1_ref[...] + gate2 * moe
    o_ref[...] = _rms(x2) * nf_ref[...]

    @pl.when(i == n - 1)
    def _():
        wait_all(1 - slot)


def _combine(dest, gates, x1, mod, normf_w, y, seq):
    t, d = x1.shape
    nt = min(COMBINE_TOKENS, seq)
    steps = t // nt
    per_batch = seq // nt
    dest2 = dest.reshape(steps, 1, nt * TOP_K)
    return pl.pallas_call(
        _combine_kernel,
        grid=(steps,),
        in_specs=[pl.BlockSpec((1, 1, nt * TOP_K), lambda i: (i, 0, 0), memory_space=pltpu.SMEM),
                  pl.BlockSpec((1, 1, nt * TOP_K), lambda i: (jnp.minimum(i + 1, steps - 1), 0, 0),
                               memory_space=pltpu.SMEM),
                  pl.BlockSpec((nt, 2 * TOP_K), lambda i: (i, 0)),
                  pl.BlockSpec((nt, d), lambda i: (i, 0)),
                  pl.BlockSpec((1, 6, d), lambda i: (i // per_batch, 0, 0)),
                  pl.BlockSpec((1, d), lambda i: (0, 0)),
                  pl.BlockSpec(memory_space=pl.ANY)],
        out_specs=pl.BlockSpec((nt, d), lambda i: (i, 0)),
        out_shape=jax.ShapeDtypeStruct((t, d), F32),
        scratch_shapes=[pltpu.VMEM((2, TOP_K, nt * ROW_SUB, LANES), F32),
                        pltpu.SemaphoreType.DMA((2,))],
        compiler_params=pltpu.CompilerParams(
            dimension_semantics=("arbitrary",), vmem_limit_bytes=VMEM_LIMIT),
        name="combine",
    )(dest2, dest2, gates, x1, mod, normf_w, y)


def kernel(x, c, w_ada, b_ada, norm1_w, w_in, lam_q1, lam_k1, lam_q2, lam_k2, subln_w, rel_bias,
           w_out, norm2_w, w_router, b_router, w1, b1, w2, b2, normf_w):
    batch, seq, d = x.shape
    t = batch * seq
    x2 = x.reshape(t, d)

    c_pad = jnp.zeros((8, d), F32).at[:batch].set(c)
    mod = _ada(c_pad, w_ada[0], b_ada[0][None, :])[:batch].reshape(batch, 6, d)

    q_r, k_r, v_r, g_r, q_d, k_d, vt_d = _in_proj(x2, mod, norm1_w[0][None, :],
                                                 w_in[0].astype(BF16), seq)
    y_r = _retention(q_r, k_r, v_r, g_r, batch, seq)
    y_d = _diff_attention(q_d, k_d, vt_d, rel_bias, lam_q1, lam_k1, lam_q2, lam_k2,
                          subln_w, batch, seq)

    x1, hp, meta, gates, counts = _out_router(y_r, y_d, x2, mod, norm2_w[0][None, :],
                                              w_out[0].astype(BF16), w_router[0],
                                              b_router[0][:, None], seq)

    bm = EXPERT_ROWS
    padded = (counts + bm - 1) // bm * bm
    pad_end = jnp.cumsum(padded)
    pad_start = pad_end - padded
    sel_e = meta[:, :TOP_K]
    hot_e = sel_e[:, :, None] == jnp.arange(N_EXPERTS, dtype=jnp.int32)[None, None, :]
    dest = jnp.sum(jnp.where(hot_e, pad_start[None, None, :], 0), axis=-1) + meta[:, TOP_K:]
    n_rows = (t * TOP_K // bm + N_EXPERTS) * bm
    nb = n_rows // bm
    block_start = jnp.arange(nb, dtype=jnp.int32) * bm
    block_e = jnp.minimum(jnp.sum((pad_end[None, :] <= block_start[:, None]).astype(jnp.int32), axis=1),
                          N_EXPERTS - 1)
    n_valid = (pad_end[-1:] // bm).astype(jnp.int32)
    slot_tok = _slot_tokens(dest, n_rows)

    owns = padded > 0
    ids = jnp.arange(N_EXPERTS, dtype=jnp.int32)
    later = jnp.logical_and(owns[None, :], ids[None, :] > ids[:, None])
    next_owner = jnp.min(jnp.where(later, ids[None, :], N_EXPERTS), axis=1)
    next_owner = jnp.where(next_owner == N_EXPERTS, -1, next_owner).astype(jnp.int32)
    ordinal = (jnp.cumsum(owns.astype(jnp.int32)) - 1).astype(jnp.int32)
    next_e = jnp.concatenate([next_owner, jnp.maximum(ordinal, 0)])

    ys = _experts(block_e, n_valid, next_e, slot_tok, hp, _w1_prep(w1[0]), _pair_split_bias(b1[0]),
                  w2[0], b2[0][:, None, :])
    out = _combine(dest, gates, x1, mod, normf_w[None, :], ys, seq)
    return out.reshape(batch, seq, d)
```

```python
import math

import jax
import jax.numpy as jnp
from jax import lax
from jax.experimental import pallas as pl
from jax.experimental.pallas import tpu as pltpu
from jax.experimental.pallas import tpu_sc as plsc

F32 = jnp.float32
BF16 = jnp.bfloat16

D_MODEL = 1024
RET_HEADS = 4
RET_KEY_DIM = 64
RET_VAL_DIM = 128
RET_QK_W = RET_HEADS * RET_KEY_DIM
RET_V_W = RET_HEADS * RET_VAL_DIM
RET_CHUNK = 128
DIFF_HEADS = 4
DIFF_HEAD_DIM = 64
DIFF_QK_W = DIFF_HEADS * 2 * DIFF_HEAD_DIM
DIFF_V_W = DIFF_HEADS * 2 * DIFF_HEAD_DIM
IN_SIZES = (RET_QK_W, RET_QK_W, RET_V_W, RET_V_W, DIFF_QK_W, DIFF_QK_W, DIFF_V_W)
REL_BUCKETS = 32
REL_MAX_DIST = 128
N_EXPERTS = 32
TOP_K = 4
SWIGLU_ALPHA = 1.702
SWIGLU_LIMIT = 7.0
NORM_EPS = 1e-6
LAMBDA_INIT = 0.8 - 0.6 * math.exp(-0.3 * 0)

LANES = 128
ROW_SUB = D_MODEL // LANES
NEG_BIG = -1e30
LOG2_E = math.log2(math.e)
ONES_ROWS = 16
VMEM_LIMIT = 56 * 1024 * 1024

ROW_TILE = 512
RET_ROWS = 512
ATT_BLOCK = 512
ATT_STRIP = 256
ATT_HEADS = 4
EXPERT_ROWS = 256
PERM_TILE = 256
COMBINE_TOKENS = 256
DMA_QUEUES = 2
SC_CORES = 2
SC_SUBCORES = 16
SC_LANES = 16
SC_SLOT_CHUNK = 4096


def _rms(x):
    return x * lax.rsqrt(jnp.mean(x * x, axis=-1, keepdims=True) + NORM_EPS)


def _store_row_tiles(ref, x):
    rows = x.shape[0]
    for s in range(ROW_SUB):
        ref[pl.ds(s, rows, stride=ROW_SUB), :] = x[:, s * LANES:(s + 1) * LANES]


def _pack_halves(x):
    half = x.shape[1] // 2
    lo = lax.bitcast_convert_type(x[:, :half].astype(BF16).astype(F32), jnp.uint32)
    hi = lax.bitcast_convert_type(x[:, half:].astype(BF16).astype(F32), jnp.uint32)
    return (hi & jnp.uint32(0xFFFF0000)) | (lo >> 16)


def _unpack_halves(w):
    lo = lax.bitcast_convert_type(w << 16, F32).astype(BF16)
    hi = lax.bitcast_convert_type(w & jnp.uint32(0xFFFF0000), F32).astype(BF16)
    return jnp.concatenate([lo, hi], axis=1)


def _ada_kernel(c_ref, w_ref, b_ref, o_ref):
    c = c_ref[...]
    cond = c * jax.nn.sigmoid(c)
    o_ref[...] = jnp.dot(cond, w_ref[...], precision=lax.Precision.HIGHEST,
                         preferred_element_type=F32) + b_ref[...]


def _ada(c_pad, w_ada, b_ada):
    rows, d = c_pad.shape
    n = w_ada.shape[1]
    tn = 1024
    return pl.pallas_call(
        _ada_kernel,
        grid=(n // tn,),
        in_specs=[pl.BlockSpec((rows, d), lambda j: (0, 0)),
                  pl.BlockSpec((d, tn), lambda j: (0, j)),
                  pl.BlockSpec((1, tn), lambda j: (0, j))],
        out_specs=pl.BlockSpec((rows, tn), lambda j: (0, j)),
        out_shape=jax.ShapeDtypeStruct((rows, n), F32),
        name="ada",
    )(c_pad, w_ada, b_ada)


def _in_proj_kernel(x_ref, mod_ref, nw_ref, w_ref, *o_refs):
    x = x_ref[...]
    shift = mod_ref[0, 0:1, :]
    scale = mod_ref[0, 1:2, :]
    h = (_rms(x) * nw_ref[...]) * (1.0 + scale) + shift
    hb = h.astype(BF16)
    off = 0
    for o_ref, width in zip(o_refs[:-1], IN_SIZES[:-1]):
        o_ref[...] = jnp.dot(hb, w_ref[:, off:off + width],
                             preferred_element_type=F32).astype(o_ref.dtype)
        off += width
    vt_ref = o_refs[-1]
    v = jnp.dot(hb, w_ref[:, off:], preferred_element_type=F32)
    hw = 2 * DIFF_HEAD_DIM
    tail = (lax.broadcasted_iota(jnp.int32, (ONES_ROWS, v.shape[0]), 0) == 0).astype(vt_ref.dtype)
    for hd in range(DIFF_HEADS):
        vt_ref[0, hd, 0, 0:hw, :] = v[:, hd * hw:(hd + 1) * hw].T.astype(vt_ref.dtype)
        vt_ref[0, hd, 0, hw:, :] = tail


def _in_proj(x2, mod, norm_w, w_in_bf16, seq):
    t, d = x2.shape
    tm = min(ROW_TILE, seq)
    assert tm == min(ATT_BLOCK, seq)
    per_batch = seq // tm
    in_w = w_in_bf16.shape[1]
    hw = 2 * DIFF_HEAD_DIM
    vt_shape = (t // seq, DIFF_HEADS, per_batch, hw + ONES_ROWS, tm)
    return pl.pallas_call(
        _in_proj_kernel,
        grid=(t // tm,),
        in_specs=[pl.BlockSpec((tm, d), lambda i: (i, 0)),
                  pl.BlockSpec((1, 6, d), lambda i: (i // per_batch, 0, 0)),
                  pl.BlockSpec((1, d), lambda i: (0, 0)),
                  pl.BlockSpec((d, in_w), lambda i: (0, 0))],
        out_specs=([pl.BlockSpec((tm, w), lambda i: (i, 0)) for w in IN_SIZES[:-1]]
                   + [pl.BlockSpec((1,) + vt_shape[1:2] + (1,) + vt_shape[3:],
                                   lambda i: (i // per_batch, 0, i % per_batch, 0, 0))]),
        out_shape=([jax.ShapeDtypeStruct((t, w), BF16) for w in IN_SIZES[:-1]]
                   + [jax.ShapeDtypeStruct(vt_shape, BF16)]),
        compiler_params=pltpu.CompilerParams(vmem_limit_bytes=VMEM_LIMIT),
        name="in_proj",
    )(x2, mod, norm_w, w_in_bf16)


def _rotary(x, cos, sin_even, sin_odd):
    nxt = pltpu.roll(x, LANES - 1, 1)
    prv = pltpu.roll(x, 1, 1)
    return x * cos + nxt * sin_even + prv * sin_odd


def _ret_kernel(q_ref, k_ref, v_ref, g_ref, cos_ref, sine_ref, sino_ref,
                hmask_ref, xi_ref, zeta_ref, dmask_ref, gch_ref, o_ref, state_ref):
    @pl.when(pl.program_id(1) == 0)
    def _():
        state_ref[...] = jnp.zeros_like(state_ref)

    n_sub = q_ref.shape[0] // RET_CHUNK
    for c in range(n_sub):
        rows = slice(c * RET_CHUNK, (c + 1) * RET_CHUNK)
        for pair in range(RET_HEADS // 2):
            lanes = slice(pair * LANES, (pair + 1) * LANES)
            cos = cos_ref[rows, :]
            sine = sine_ref[rows, :]
            sino = sino_ref[rows, :]
            qr = _rotary(q_ref[rows, lanes].astype(F32), cos, sine, sino)
            kr = _rotary(k_ref[rows, lanes].astype(F32), cos, sine, sino) * (RET_KEY_DIM ** -0.5)
            qb = qr.astype(BF16)
            for hh in range(2):
                h = 2 * pair + hh
                vcols = slice(h * RET_VAL_DIM, (h + 1) * RET_VAL_DIM)
                v = v_ref[rows, vcols]
                km = (kr * hmask_ref[h]).astype(BF16)
                scores = lax.dot_general(qb, km, (((1,), (1,)), ((), ())),
                                         preferred_element_type=F32) * dmask_ref[h]
                inner = jnp.dot(scores.astype(BF16), v, preferred_element_type=F32)
                qx = (qr * xi_ref[h]).astype(BF16)
                state = state_ref[h]
                cross = jnp.dot(qx, state.astype(BF16), preferred_element_type=F32)
                kz = (kr * zeta_ref[h]).astype(BF16)
                kv = lax.dot_general(kz, v, (((0,), (0,)), ((), ())),
                                     preferred_element_type=F32)
                state_ref[h] = state * gch_ref[h] + kv
                y = _rms(inner + cross)
                g = g_ref[rows, vcols].astype(F32)
                o_ref[rows, vcols] = (g * jax.nn.sigmoid(g) * y).astype(o_ref.dtype)


def _retention_tables(seq):
    dk, c, nh = RET_KEY_DIM, RET_CHUNK, RET_HEADS
    pos = jnp.arange(seq, dtype=F32)
    inv_freq = 1.0 / (10000.0 ** jnp.linspace(0.0, 1.0, dk // 2, dtype=F32))
    ang = pos[:, None] * jnp.repeat(inv_freq, 2)[None, :]
    sin = jnp.tile(jnp.sin(ang), (1, LANES // dk))
    cos = jnp.tile(jnp.cos(ang), (1, LANES // dk))
    even = (jnp.arange(LANES) % 2 == 0)[None, :]
    sin_even = jnp.where(even, -sin, 0.0)
    sin_odd = jnp.where(even, 0.0, sin)
    log_g = jnp.log(1.0 - 2.0 ** (-5.0 - jnp.arange(nh, dtype=F32)))
    i = jnp.arange(c, dtype=F32)
    rel = i[:, None] - i[None, :]
    dmask = jnp.where(rel[None] >= 0,
                      jnp.exp(jnp.maximum(rel, 0.0)[None] * log_g[:, None, None]), 0.0)
    zeta = jnp.exp((c - 1.0 - i)[None, :] * log_g[:, None])
    xi = jnp.exp((i + 1.0)[None, :] * log_g[:, None])
    g_chunk = jnp.exp(c * log_g)
    lane = jnp.arange(LANES)
    hmask = jnp.stack([((lane // dk) == (h % 2)).astype(F32) for h in range(nh)])[:, None, :]
    xi_t = xi[:, :, None] * hmask
    zeta_t = zeta[:, :, None] * hmask
    gch = jnp.broadcast_to(g_chunk[:, None, None], (nh, 1, LANES))
    return cos, sin_even, sin_odd, hmask, xi_t, zeta_t, dmask, gch


def _retention(q, k, v, g, batch, seq):
    t = q.shape[0]
    rb = min(RET_ROWS, seq)
    per_batch = seq // rb
    cos, sin_even, sin_odd, hmask, xi_t, zeta_t, dmask, gch = _retention_tables(seq)
    row = lambda w: pl.BlockSpec((rb, w), lambda b, j: (b * per_batch + j, 0))
    tab = lambda w: pl.BlockSpec((rb, w), lambda b, j: (j, 0))
    full = lambda a: pl.BlockSpec(a.shape, lambda b, j: (0,) * a.ndim)
    return pl.pallas_call(
        _ret_kernel,
        grid=(batch, per_batch),
        in_specs=[row(RET_QK_W), row(RET_QK_W), row(RET_V_W), row(RET_V_W),
                  tab(LANES), tab(LANES), tab(LANES),
                  full(hmask), full(xi_t), full(zeta_t), full(dmask), full(gch)],
        out_specs=row(RET_V_W),
        out_shape=jax.ShapeDtypeStruct((t, RET_V_W), BF16),
        scratch_shapes=[pltpu.VMEM((RET_HEADS, LANES, RET_VAL_DIM), F32)],
        compiler_params=pltpu.CompilerParams(
            dimension_semantics=("arbitrary", "arbitrary"), vmem_limit_bytes=VMEM_LIMIT),
        name="retention",
    )(q, k, v, g, cos, sin_even, sin_odd, hmask, xi_t, zeta_t, dmask, gch)


def _t5_bucket(rel):
    n = jnp.maximum(rel, 0)
    max_exact = REL_BUCKETS // 2
    nf = jnp.maximum(n, 1).astype(F32)
    large = max_exact + (jnp.log(nf / max_exact) / math.log(REL_MAX_DIST / max_exact)
                         * (REL_BUCKETS - max_exact)).astype(jnp.int32)
    large = jnp.minimum(large, REL_BUCKETS - 1)
    return jnp.where(n < max_exact, n, large)


def _bias_tiles(rel_bias, blk):
    r = jnp.arange(blk, dtype=jnp.int32)
    far = rel_bias[REL_BUCKETS - 1]
    rel0 = r[None, :] - r[:, None]
    rel1 = rel0 + blk
    buckets = jnp.arange(REL_BUCKETS, dtype=jnp.int32)

    def tile(rel):
        hot = (_t5_bucket(rel)[:, :, None] == buckets).astype(F32)
        return jnp.einsum('krb,bh->hkr', hot, rel_bias, precision=lax.Precision.HIGHEST)

    b0 = jnp.where(rel0[None] >= 0, (tile(rel0) - far[:, None, None]) * LOG2_E, NEG_BIG)
    b1 = (tile(rel1) - far[:, None, None]) * LOG2_E
    return b0, b1


def _attn_kernel(q_ref, k_ref, vt_ref, b0_ref, b1_ref, lq1_ref, lk1_ref, lq2_ref, lk2_ref,
                 sw_ref, o_ref, m_ref, acc_ref):
    blk = q_ref.shape[0]
    hw = 2 * DIFF_HEAD_DIM
    i = pl.program_id(2)
    lane = lax.broadcasted_iota(jnp.int32, (1, LANES), 1)
    qm = []
    for hd in range(ATT_HEADS):
        q = (q_ref[:, hd * hw:(hd + 1) * hw].astype(F32) * (DIFF_HEAD_DIM ** -0.5 * LOG2_E)).astype(BF16)
        zero = jnp.zeros_like(q)
        qm.append((jnp.where(lane < DIFF_HEAD_DIM, q, zero), jnp.where(lane >= DIFF_HEAD_DIM, q, zero)))

    m_ref[...] = jnp.full_like(m_ref, NEG_BIG)
    acc_ref[...] = jnp.zeros_like(acc_ref)

    def step(blocks):
        kbs = [k_ref[pl.ds(pl.multiple_of(j * blk, blk), blk), :] for j, _, _ in blocks]
        chains = [(hd, mi, qs) for hd in range(ATT_HEADS) for mi in range(2)
                  for qs in range(blk // ATT_STRIP)]
        scores = []
        for hd, mi, qs in chains:
            qc = slice(qs * ATT_STRIP, (qs + 1) * ATT_STRIP)
            row = []
            for (_, bias, diagonal), kb in zip(blocks, kbs):
                keys = (qs + 1) * ATT_STRIP if diagonal else blk
                s = lax.dot_general(kb[:keys, hd * hw:(hd + 1) * hw], qm[hd][mi][qc, :],
                                    (((1,), (1,)), ((), ())), preferred_element_type=F32)
                row.append(s if bias is None else s + bias[hd, :keys, qc])
            scores.append(row)
        stats = []
        for (hd, mi, qs), row in zip(chains, scores):
            qc = slice(qs * ATT_STRIP, (qs + 1) * ATT_STRIP)
            m_old = m_ref[hd, mi, :, qc]
            m_new = m_old
            for s in row:
                m_new = jnp.maximum(m_new, jnp.max(s, axis=0, keepdims=True))
            stats.append((jnp.exp2(m_old - m_new), [jnp.exp2(s - m_new).astype(BF16) for s in row], m_new))
        for (hd, mi, qs), (alpha, ps, m_new) in zip(chains, stats):
            qc = slice(qs * ATT_STRIP, (qs + 1) * ATT_STRIP)
            pv = None
            for (j, _, _), p in zip(blocks, ps):
                part = jnp.dot(vt_ref[0, hd, j, :, 0:p.shape[0]], p, preferred_element_type=F32)
                pv = part if pv is None else pv + part
            acc_ref[hd, mi, :, qc] = alpha * acc_ref[hd, mi, :, qc] + pv
            m_ref[hd, mi, :, qc] = m_new

    n_far = jnp.maximum(i - 1, 0)

    def far_pair(pair, carry):
        step([(2 * pair, None, False), (2 * pair + 1, None, False)])
        return carry

    lax.fori_loop(0, lax.shift_right_logical(n_far, 1), far_pair, 0)

    @pl.when(lax.rem(n_far, 2) == 1)
    def _():
        step([(n_far - 1, None, False)])

    @pl.when(i >= 1)
    def _():
        step([(i - 1, b1_ref, False), (i, b0_ref, True)])

    @pl.when(i == 0)
    def _():
        step([(i, b0_ref, True)])

    lam = (jnp.exp(jnp.sum(lq1_ref[...] * lk1_ref[...], axis=-1, keepdims=True))
           - jnp.exp(jnp.sum(lq2_ref[...] * lk2_ref[...], axis=-1, keepdims=True))
           + LAMBDA_INIT)
    for hd in range(ATT_HEADS):
        a = (acc_ref[hd, 0, :hw, :] / acc_ref[hd, 0, hw:hw + 1, :]
             - lam * (acc_ref[hd, 1, :hw, :] / acc_ref[hd, 1, hw:hw + 1, :]))
        a = a * lax.rsqrt(jnp.mean(a * a, axis=0, keepdims=True) + NORM_EPS)
        o_ref[:, hd * hw:(hd + 1) * hw] = (a.T * sw_ref[...] * (1.0 - LAMBDA_INIT)).astype(o_ref.dtype)


def _diff_attention(q, k, vt, rel_bias, lq1, lk1, lq2, lk2, subln_w, batch, seq):
    t = q.shape[0]
    blk = min(ATT_BLOCK, seq)
    nq = seq // blk
    hw = 2 * DIFF_HEAD_DIM
    nh = ATT_HEADS
    b0, b1 = _bias_tiles(rel_bias, blk)
    small = lambda a: pl.BlockSpec(a.shape, lambda b, h, i: (0,) * a.ndim)
    return pl.pallas_call(
        _attn_kernel,
        grid=(batch, DIFF_HEADS // nh, nq),
        in_specs=[pl.BlockSpec((blk, nh * hw), lambda b, h, i: (b * nq + i, h)),
                  pl.BlockSpec((seq, nh * hw), lambda b, h, i: (b, h)),
                  pl.BlockSpec((1, nh, nq, hw + ONES_ROWS, blk), lambda b, h, i: (b, h, 0, 0, 0)),
                  pl.BlockSpec((nh, blk, blk), lambda b, h, i: (h, 0, 0)),
                  pl.BlockSpec((nh, blk, blk), lambda b, h, i: (h, 0, 0)),
                  small(lq1), small(lk1), small(lq2), small(lk2), small(subln_w)],
        out_specs=pl.BlockSpec((blk, nh * hw), lambda b, h, i: (b * nq + i, h)),
        out_shape=jax.ShapeDtypeStruct((t, DIFF_V_W), BF16),
        scratch_shapes=[pltpu.VMEM((nh, 2, 1, blk), F32), pltpu.VMEM((nh, 2, hw + ONES_ROWS, blk), F32)],
        compiler_params=pltpu.CompilerParams(
            dimension_semantics=("arbitrary", "arbitrary", "arbitrary"),
            vmem_limit_bytes=VMEM_LIMIT),
        name="diff_attn",
    )(q, k, vt, b0, b1, lq1, lk1, lq2, lk2, subln_w)


def _out_kernel(yr_ref, yd_ref, x_ref, mod_ref, nw_ref, wo_ref, wr_ref, br_ref, upper_ref,
                x1_ref, hp_ref, meta_ref, gate_ref, cnt_ref, run_ref):
    tm = x_ref.shape[0]
    ne = run_ref.shape[0]

    @pl.when(pl.program_id(0) == 0)
    def _():
        run_ref[...] = jnp.zeros_like(run_ref)

    mixed = (jnp.dot(yr_ref[...], wo_ref[0:RET_V_W, :], preferred_element_type=F32)
             + jnp.dot(yd_ref[...], wo_ref[RET_V_W:, :], preferred_element_type=F32))
    gate1 = mod_ref[0, 2:3, :]
    shift2 = mod_ref[0, 3:4, :]
    scale2 = mod_ref[0, 4:5, :]
    x1 = x_ref[...] + gate1 * mixed
    x1_ref[...] = x1
    h2 = (_rms(x1) * nw_ref[...]) * (1.0 + scale2) + shift2
    hp_ref[...] = _pack_halves(h2)

    h_hi = h2.astype(BF16)
    h_lo = (h2 - h_hi.astype(F32)).astype(BF16)
    nt_dims = (((1,), (1,)), ((), ()))
    both = lax.dot_general(wr_ref[...], h_hi, nt_dims, preferred_element_type=F32)
    low = lax.dot_general(wr_ref[0:ne, :], h_lo, nt_dims, preferred_element_type=F32)
    logits = both[0:ne, :] + both[ne:, :] + low + br_ref[...]

    row = lax.broadcasted_iota(jnp.int32, logits.shape, 0)
    work = logits
    vals, idxs, hots = [], [], []
    for _ in range(TOP_K):
        mx = jnp.max(work, axis=0, keepdims=True)
        idx = jnp.min(jnp.where(work == mx, row, ne), axis=0, keepdims=True)
        hot = row == idx
        vals.append(mx)
        idxs.append(idx)
        hots.append(hot)
        work = jnp.where(hot, -jnp.inf, work)
    exps = [jnp.exp(v - vals[0]) for v in vals]
    denom = exps[0] + exps[1] + exps[2] + exps[3]

    sel = jnp.zeros(logits.shape, F32)
    for hot in hots:
        sel = sel + hot.astype(F32)
    prefix = jnp.dot(sel.astype(BF16), upper_ref[...], preferred_element_type=F32) + run_ref[...]
    ranks = [jnp.sum(jnp.where(hot, prefix, 0.0), axis=0, keepdims=True) for hot in hots]
    run_ref[...] = run_ref[...] + jnp.sum(sel, axis=1, keepdims=True)
    cnt_ref[...] = jnp.broadcast_to(run_ref[...], cnt_ref.shape).astype(jnp.int32)

    meta_ref[...] = jnp.concatenate(idxs + [r.astype(jnp.int32) for r in ranks], axis=0)
    gate_ref[...] = jnp.concatenate([e / denom for e in exps] + [jnp.zeros_like(denom)] * TOP_K, axis=0)


def _out_router(y_r, y_d, x2, mod, norm_w, w_out_bf16, w_router, b_router, seq):
    t, d = x2.shape
    tm = min(ROW_TILE, seq)
    per_batch = seq // tm
    ne = w_router.shape[1]
    w_hi = w_router.astype(BF16)
    w_lo = (w_router - w_hi.astype(F32)).astype(BF16)
    wr_t = jnp.concatenate([w_hi, w_lo], axis=1).T
    idx = jnp.arange(tm, dtype=jnp.int32)
    upper = (idx[:, None] < idx[None, :]).astype(BF16)
    row = lambda w: pl.BlockSpec((tm, w), lambda i: (i, 0))
    col = lambda h: pl.BlockSpec((h, tm), lambda i: (0, i))
    const = lambda a: pl.BlockSpec(a.shape, lambda i: (0,) * a.ndim)
    x1, hp, meta_t, gates_t, counts = pl.pallas_call(
        _out_kernel,
        grid=(t // tm,),
        in_specs=[row(RET_V_W), row(DIFF_V_W), row(d),
                  pl.BlockSpec((1, 6, d), lambda i: (i // per_batch, 0, 0)),
                  const(norm_w), const(w_out_bf16), const(wr_t), const(b_router), const(upper)],
        out_specs=[row(d), row(d // 2), col(2 * TOP_K), col(2 * TOP_K),
                   pl.BlockSpec((ne, LANES), lambda i: (0, 0))],
        out_shape=[jax.ShapeDtypeStruct((t, d), F32),
                   jax.ShapeDtypeStruct((t, d // 2), jnp.uint32),
                   jax.ShapeDtypeStruct((2 * TOP_K, t), jnp.int32),
                   jax.ShapeDtypeStruct((2 * TOP_K, t), F32),
                   jax.ShapeDtypeStruct((ne, LANES), jnp.int32)],
        scratch_shapes=[pltpu.VMEM((ne, 1), F32)],
        compiler_params=pltpu.CompilerParams(
            dimension_semantics=("arbitrary",), vmem_limit_bytes=VMEM_LIMIT),
        name="out_router",
    )(y_r, y_d, x2, mod, norm_w, w_out_bf16, wr_t, b_router, upper)
    return x1, hp, meta_t, gates_t.T, counts[:, 0]


def _w1_prep_kernel(w_ref, p_ref, o_ref):
    for s in range(w_ref.shape[2] // PERM_TILE):
        cols = slice(s * PERM_TILE, (s + 1) * PERM_TILE)
        o_ref[0, :, cols] = jnp.dot(w_ref[0, :, cols].astype(BF16), p_ref[...],
                                    preferred_element_type=F32).astype(BF16)


def _pair_split_matrix():
    i = jnp.arange(PERM_TILE)[:, None]
    j = jnp.arange(PERM_TILE)[None, :]
    half = PERM_TILE // 2
    src = jnp.where(j < half, 2 * j, 2 * (j - half) + 1)
    return (i == src).astype(BF16)


def _w1_prep(w1):
    e, d, f2 = w1.shape
    tn = 1024
    return pl.pallas_call(
        _w1_prep_kernel,
        grid=(e, f2 // tn),
        in_specs=[pl.BlockSpec((1, d, tn), lambda i, j: (i, 0, j)),
                  pl.BlockSpec((PERM_TILE, PERM_TILE), lambda i, j: (0, 0))],
        out_specs=pl.BlockSpec((1, d, tn), lambda i, j: (i, 0, j)),
        out_shape=jax.ShapeDtypeStruct((e, d, f2), BF16),
        compiler_params=pltpu.CompilerParams(vmem_limit_bytes=VMEM_LIMIT),
        name="w1_prep",
    )(w1, _pair_split_matrix())


def _pair_split_bias(b1):
    e, f2 = b1.shape
    nt = f2 // PERM_TILE
    g = b1[:, 0::2].reshape(e, nt, 1, PERM_TILE // 2)
    l = b1[:, 1::2].reshape(e, nt, 1, PERM_TILE // 2)
    return jnp.concatenate([g, l], axis=2).reshape(e, 1, f2)


def _slot_tokens(dest_t, n_rows):
    n_tok = dest_t.shape[1]
    assert n_tok & (n_tok - 1) == 0
    flat = dest_t.reshape(-1)
    n_slots = flat.shape[0]
    chunk = min(SC_SLOT_CHUNK, n_slots)
    mesh = plsc.VectorSubcoreMesh(core_axis_name="c", subcore_axis_name="s",
                                  num_cores=SC_CORES, num_subcores=SC_SUBCORES)

    def body(dest_hbm, out_hbm, out_v, dest_v):
        first = jnp.logical_and(lax.axis_index("c") == 0, lax.axis_index("s") == 0)

        @pl.when(first)
        def _():
            zeros = jnp.zeros((SC_LANES,), jnp.int32)

            @pl.loop(0, n_rows // SC_LANES)
            def _(g):
                out_v[pl.ds(g * SC_LANES, SC_LANES)] = zeros

            lane = lax.iota(jnp.int32, SC_LANES)

            @pl.loop(0, n_slots // chunk)
            def _(c):
                pltpu.sync_copy(dest_hbm.at[pl.ds(c * chunk, chunk)], dest_v)

                @pl.loop(0, chunk // SC_LANES)
                def _(g):
                    rows = dest_v[pl.ds(g * SC_LANES, SC_LANES)]
                    slot = lane + (c * chunk + g * SC_LANES)
                    plsc.store_scatter(out_v, [rows], slot & (n_tok - 1))

            pltpu.sync_copy(out_v, out_hbm)

    return pl.kernel(
        body,
        out_type=jax.ShapeDtypeStruct((n_rows,), jnp.int32),
        mesh=mesh,
        scratch_types=[pltpu.VMEM((n_rows,), jnp.int32), pltpu.VMEM((chunk,), jnp.int32)],
        compiler_params=pltpu.CompilerParams(needs_layout_passes=False),
        name="slot_tokens",
    )(flat)


def _expert_kernel(be_ref, nv_ref, nexte_ref, tcur_ref, tnxt_ref, hp_ref, w1_ref, b1_ref, w2_ref, b2_ref,
                   o_ref, hbuf, xb0, xb1, w1buf, w2buf, w2b, sem, wsems):
    b = pl.program_id(0)
    nv = nv_ref[0]
    bm = xb0.shape[0]

    def gather(tok_ref, dst):
        for r in range(bm):
            dst[pl.ds(r, 1), :] = hbuf[pl.ds(tok_ref[0, 0, r], 1), :]

    def weight_copies(e, slot):
        return (pltpu.make_async_copy(w1_ref.at[e], w1buf.at[slot], wsems.at[0, slot]),
                pltpu.make_async_copy(w2_ref.at[e], w2buf.at[slot], wsems.at[1, slot]))

    e = be_ref[b]
    prev_e = be_ref[jnp.maximum(b - 1, 0)]
    first_of_expert = jnp.logical_and(b < nv, jnp.logical_or(b == 0, e != prev_e))
    slot = lax.rem(nexte_ref[N_EXPERTS + e], 2)

    @pl.when(b == 0)
    def _():
        for cp in weight_copies(e, slot):
            cp.start()
        load = pltpu.make_async_copy(hp_ref, hbuf, sem)
        load.start()
        load.wait()
        gather(tcur_ref, xb0)

    @pl.when(first_of_expert)
    def _():
        for cp in weight_copies(e, slot):
            cp.wait()
        nxt = nexte_ref[e]

        @pl.when(nxt >= 0)
        def _():
            for cp in weight_copies(nxt, 1 - slot):
                cp.start()

        w2b[...] = w2buf[slot].astype(BF16)

    def run(cur, nxt_rows):
        x = _unpack_halves(cur[...])
        half = PERM_TILE // 2
        acts = []
        for j in range(w1buf.shape[2] // PERM_TILE):
            cols = slice(j * PERM_TILE, (j + 1) * PERM_TILE)
            h = jnp.dot(x, w1buf[slot, :, cols], preferred_element_type=F32) + b1_ref[0, :, cols]
            glu = jnp.minimum(h[:, :half], SWIGLU_LIMIT)
            lin = jnp.clip(h[:, half:], -SWIGLU_LIMIT, SWIGLU_LIMIT)
            acts.append((glu * jax.nn.sigmoid(SWIGLU_ALPHA * glu) * (lin + 1.0)).astype(BF16))
        act = jnp.concatenate(acts, axis=1)
        y = jnp.dot(act, w2b[...], preferred_element_type=F32) + b2_ref[0]
        _store_row_tiles(o_ref, y)
        gather(tnxt_ref, nxt_rows)

    even = lax.rem(b, 2) == 0

    @pl.when(jnp.logical_and(b < nv, even))
    def _():
        run(xb0, xb1)

    @pl.when(jnp.logical_and(b < nv, jnp.logical_not(even)))
    def _():
        run(xb1, xb0)

    @pl.when(b >= nv)
    def _():
        o_ref[...] = jnp.zeros_like(o_ref)


def _experts(block_e, n_valid, next_e, slot_tok, hp, w1p, b1p, w2, b2):
    d = D_MODEL
    f2 = w1p.shape[2]
    f = w2.shape[1]
    bm = EXPERT_ROWS
    nb = slot_tok.shape[0] // bm
    tok3 = slot_tok.reshape(nb, 1, bm)
    exp = lambda b, be, nv, ne: be[jnp.maximum(jnp.minimum(b, nv[0] - 1), 0)]
    grid_spec = pltpu.PrefetchScalarGridSpec(
        num_scalar_prefetch=3,
        grid=(nb,),
        in_specs=[pl.BlockSpec((1, 1, bm), lambda b, be, nv, ne: (b, 0, 0), memory_space=pltpu.SMEM),
                  pl.BlockSpec((1, 1, bm), lambda b, be, nv, ne: (jnp.minimum(b + 1, nb - 1), 0, 0),
                               memory_space=pltpu.SMEM),
                  pl.BlockSpec(memory_space=pl.ANY),
                  pl.BlockSpec(memory_space=pl.ANY),
                  pl.BlockSpec((1, 1, f2), lambda b, be, nv, ne: (exp(b, be, nv, ne), 0, 0)),
                  pl.BlockSpec(memory_space=pl.ANY),
                  pl.BlockSpec((1, 1, d), lambda b, be, nv, ne: (exp(b, be, nv, ne), 0, 0))],
        out_specs=pl.BlockSpec((bm * ROW_SUB, LANES), lambda b, be, nv, ne: (b, 0)),
        scratch_shapes=[pltpu.VMEM(hp.shape, hp.dtype), pltpu.VMEM((bm, hp.shape[1]), hp.dtype),
                        pltpu.VMEM((bm, hp.shape[1]), hp.dtype),
                        pltpu.VMEM((2,) + w1p.shape[1:], w1p.dtype), pltpu.VMEM((2,) + w2.shape[1:], w2.dtype),
                        pltpu.VMEM(w2.shape[1:], BF16),
                        pltpu.SemaphoreType.DMA(()), pltpu.SemaphoreType.DMA((2, 2))],
    )
    return pl.pallas_call(
        _expert_kernel,
        grid_spec=grid_spec,
        out_shape=jax.ShapeDtypeStruct((nb * bm * ROW_SUB, LANES), F32),
        compiler_params=pltpu.CompilerParams(
            dimension_semantics=("arbitrary",), vmem_limit_bytes=VMEM_LIMIT),
        name="experts",
    )(block_e, n_valid, next_e, tok3, tok3, hp, w1p, b1p, w2, b2)


def _combine_kernel(dcur_ref, dnxt_ref, gate_ref, x1_ref, mod_ref, nf_ref, y_ref, o_ref, buf, sems):
    nt = x1_ref.shape[0]
    i = pl.program_id(0)
    n = pl.num_programs(0)
    slot = lax.rem(i, 2)

    def row_copy(src_row, s, kk, r):
        return pltpu.make_async_copy(y_ref.at[pl.ds(src_row * ROW_SUB, ROW_SUB), :],
                                     buf.at[s, kk, pl.ds(r * ROW_SUB, ROW_SUB), :], sems.at[s])

    def fetch(d_ref, s):
        for j in range(nt):
            for kk in range(TOP_K):
                row_copy(d_ref[0, 0, kk * nt + j], s, kk, j).start(priority=kk % DMA_QUEUES)

    def wait_all(s):
        for kk in range(TOP_K):
            pltpu.make_async_copy(y_ref.at[pl.ds(0, nt * ROW_SUB), :], buf.at[s, kk], sems.at[s]).wait()

    @pl.when(i == 0)
    def _():
        fetch(dcur_ref, 0)

    fetch(dnxt_ref, 1 - slot)
    wait_all(slot)

    g = gate_ref[...]
    parts = []
    for s in range(ROW_SUB):
        acc = buf[slot, 0, pl.ds(s, nt, stride=ROW_SUB), :] * g[:, 0:1]
        for kk in range(1, TOP_K):
            acc = acc + buf[slot, kk, pl.ds(s, nt, stride=ROW_SUB), :] * g[:, kk:kk + 1]
        parts.append(acc)
    moe = jnp.concatenate(parts, axis=1)
    gate2 = mod_ref[0, 5:6, :]
    x2 = x1_ref[...] + gate2 * moe
    o_ref[...] = _rms(x2) * nf_ref[...]

    @pl.when(i == n - 1)
    def _():
        wait_all(1 - slot)


def _combine(dest_t, gates, x1, mod, normf_w, y, seq):
    t, d = x1.shape
    nt = min(COMBINE_TOKENS, seq)
    steps = t // nt
    per_batch = seq // nt
    dest2 = jnp.transpose(dest_t.reshape(TOP_K, steps, nt), (1, 0, 2)).reshape(steps, 1, nt * TOP_K)
    return pl.pallas_call(
        _combine_kernel,
        grid=(steps,),
        in_specs=[pl.BlockSpec((1, 1, nt * TOP_K), lambda i: (i, 0, 0), memory_space=pltpu.SMEM),
                  pl.BlockSpec((1, 1, nt * TOP_K), lambda i: (jnp.minimum(i + 1, steps - 1), 0, 0),
                               memory_space=pltpu.SMEM),
                  pl.BlockSpec((nt, 2 * TOP_K), lambda i: (i, 0)),
                  pl.BlockSpec((nt, d), lambda i: (i, 0)),
                  pl.BlockSpec((1, 6, d), lambda i: (i // per_batch, 0, 0)),
                  pl.BlockSpec((1, d), lambda i: (0, 0)),
                  pl.BlockSpec(memory_space=pl.ANY)],
        out_specs=pl.BlockSpec((nt, d), lambda i: (i, 0)),
        out_shape=jax.ShapeDtypeStruct((t, d), F32),
        scratch_shapes=[pltpu.VMEM((2, TOP_K, nt * ROW_SUB, LANES), F32),
                        pltpu.SemaphoreType.DMA((2,))],
        compiler_params=pltpu.CompilerParams(
            dimension_semantics=("arbitrary",), vmem_limit_bytes=VMEM_LIMIT),
        name="combine",
    )(dest2, dest2, gates, x1, mod, normf_w, y)


def kernel(x, c, w_ada, b_ada, norm1_w, w_in, lam_q1, lam_k1, lam_q2, lam_k2, subln_w, rel_bias,
           w_out, norm2_w, w_router, b_router, w1, b1, w2, b2, normf_w):
    batch, seq, d = x.shape
    t = batch * seq
    x2 = x.reshape(t, d)

    c_pad = jnp.zeros((8, d), F32).at[:batch].set(c)
    mod = _ada(c_pad, w_ada[0], b_ada[0][None, :])[:batch].reshape(batch, 6, d)

    q_r, k_r, v_r, g_r, q_d, k_d, vt_d = _in_proj(x2, mod, norm1_w[0][None, :],
                                                 w_in[0].astype(BF16), seq)
    y_r = _retention(q_r, k_r, v_r, g_r, batch, seq)
    y_d = _diff_attention(q_d, k_d, vt_d, rel_bias, lam_q1, lam_k1, lam_q2, lam_k2,
                          subln_w, batch, seq)

    x1, hp, meta, gates, counts = _out_router(y_r, y_d, x2, mod, norm2_w[0][None, :],
                                              w_out[0].astype(BF16), w_router[0],
                                              b_router[0][:, None], seq)

    bm = EXPERT_ROWS
    padded = (counts + bm - 1) // bm * bm
    pad_end = jnp.cumsum(padded)
    pad_start = pad_end - padded
    hot_e = meta[None, :TOP_K, :] == jnp.arange(N_EXPERTS, dtype=jnp.int32)[:, None, None]
    dest_t = jnp.sum(jnp.where(hot_e, pad_start[:, None, None], 0), axis=0) + meta[TOP_K:, :]
    n_rows = (t * TOP_K // bm + N_EXPERTS) * bm
    nb = n_rows // bm
    block_start = jnp.arange(nb, dtype=jnp.int32) * bm
    block_e = jnp.minimum(jnp.sum((pad_end[None, :] <= block_start[:, None]).astype(jnp.int32), axis=1),
                          N_EXPERTS - 1)
    n_valid = (pad_end[-1:] // bm).astype(jnp.int32)
    slot_tok = _slot_tokens(dest_t, n_rows)

    owns = padded > 0
    ids = jnp.arange(N_EXPERTS, dtype=jnp.int32)
    later = jnp.logical_and(owns[None, :], ids[None, :] > ids[:, None])
    next_owner = jnp.min(jnp.where(later, ids[None, :], N_EXPERTS), axis=1)
    next_owner = jnp.where(next_owner == N_EXPERTS, -1, next_owner).astype(jnp.int32)
    ordinal = (jnp.cumsum(owns.astype(jnp.int32)) - 1).astype(jnp.int32)
    next_e = jnp.concatenate([next_owner, jnp.maximum(ordinal, 0)])

    ys = _experts(block_e, n_valid, next_e, slot_tok, hp, _w1_prep(w1[0]), _pair_split_bias(b1[0]),
                  w2[0], b2[0][:, None, :])
    out = _combine(dest_t, gates, x1, mod, normf_w[None, :], ys, seq)
    return out.reshape(batch, seq, d)
```

```python
import math

import jax
import jax.numpy as jnp
from jax import lax
from jax.experimental import pallas as pl
from jax.experimental.pallas import tpu as pltpu
from jax.experimental.pallas import tpu_sc as plsc

F32 = jnp.float32
BF16 = jnp.bfloat16

D_MODEL = 1024
RET_HEADS = 4
RET_KEY_DIM = 64
RET_VAL_DIM = 128
RET_QK_W = RET_HEADS * RET_KEY_DIM
RET_V_W = RET_HEADS * RET_VAL_DIM
RET_CHUNK = 128
DIFF_HEADS = 4
DIFF_HEAD_DIM = 64
DIFF_QK_W = DIFF_HEADS * 2 * DIFF_HEAD_DIM
DIFF_V_W = DIFF_HEADS * 2 * DIFF_HEAD_DIM
IN_SIZES = (RET_QK_W, RET_QK_W, RET_V_W, RET_V_W, DIFF_QK_W, DIFF_QK_W, DIFF_V_W)
REL_BUCKETS = 32
REL_MAX_DIST = 128
N_EXPERTS = 32
TOP_K = 4
SWIGLU_ALPHA = 1.702
SWIGLU_LIMIT = 7.0
NORM_EPS = 1e-6
LAMBDA_INIT = 0.8 - 0.6 * math.exp(-0.3 * 0)

LANES = 128
ROW_SUB = D_MODEL // LANES
NEG_BIG = -1e30
LOG2_E = math.log2(math.e)
ONES_ROWS = 16
VMEM_LIMIT = 56 * 1024 * 1024

ROW_TILE = 512
RET_ROWS = 512
ATT_BLOCK = 512
ATT_STRIP = 256
ATT_HEADS = 4
EXPERT_ROWS = 256
PERM_TILE = 256
COMBINE_TOKENS = 256
DMA_QUEUES = 2
SC_CORES = 2
SC_SUBCORES = 16
SC_LANES = 16
SC_SLOT_CHUNK = 4096


def _rms(x):
    return x * lax.rsqrt(jnp.mean(x * x, axis=-1, keepdims=True) + NORM_EPS)


def _store_row_tiles(ref, x):
    rows = x.shape[0]
    for s in range(ROW_SUB):
        ref[pl.ds(s, rows, stride=ROW_SUB), :] = x[:, s * LANES:(s + 1) * LANES]


def _pack_halves(x):
    half = x.shape[1] // 2
    lo = lax.bitcast_convert_type(x[:, :half].astype(BF16).astype(F32), jnp.uint32)
    hi = lax.bitcast_convert_type(x[:, half:].astype(BF16).astype(F32), jnp.uint32)
    return (hi & jnp.uint32(0xFFFF0000)) | (lo >> 16)


def _unpack_halves(w):
    lo = lax.bitcast_convert_type(w << 16, F32).astype(BF16)
    hi = lax.bitcast_convert_type(w & jnp.uint32(0xFFFF0000), F32).astype(BF16)
    return jnp.concatenate([lo, hi], axis=1)


def _ada_kernel(c_ref, w_ref, b_ref, o_ref):
    c = c_ref[...]
    cond = c * jax.nn.sigmoid(c)
    o_ref[...] = jnp.dot(cond, w_ref[...], precision=lax.Precision.HIGHEST,
                         preferred_element_type=F32) + b_ref[...]


def _ada(c_pad, w_ada, b_ada):
    rows, d = c_pad.shape
    n = w_ada.shape[1]
    tn = 1024
    return pl.pallas_call(
        _ada_kernel,
        grid=(n // tn,),
        in_specs=[pl.BlockSpec((rows, d), lambda j: (0, 0)),
                  pl.BlockSpec((d, tn), lambda j: (0, j)),
                  pl.BlockSpec((1, tn), lambda j: (0, j))],
        out_specs=pl.BlockSpec((rows, tn), lambda j: (0, j)),
        out_shape=jax.ShapeDtypeStruct((rows, n), F32),
        name="ada",
    )(c_pad, w_ada, b_ada)


def _in_proj_kernel(x_ref, mod_ref, nw_ref, w_ref, *o_refs):
    x = x_ref[...]
    shift = mod_ref[0, 0:1, :]
    scale = mod_ref[0, 1:2, :]
    h = (_rms(x) * nw_ref[...]) * (1.0 + scale) + shift
    hb = h.astype(BF16)
    off = 0
    for o_ref, width in zip(o_refs[:-1], IN_SIZES[:-1]):
        o_ref[...] = jnp.dot(hb, w_ref[:, off:off + width],
                             preferred_element_type=F32).astype(o_ref.dtype)
        off += width
    vt_ref = o_refs[-1]
    v = jnp.dot(hb, w_ref[:, off:], preferred_element_type=F32)
    hw = 2 * DIFF_HEAD_DIM
    tail = (lax.broadcasted_iota(jnp.int32, (ONES_ROWS, v.shape[0]), 0) == 0).astype(vt_ref.dtype)
    for hd in range(DIFF_HEADS):
        vt_ref[0, hd, 0, 0:hw, :] = v[:, hd * hw:(hd + 1) * hw].T.astype(vt_ref.dtype)
        vt_ref[0, hd, 0, hw:, :] = tail


def _in_proj(x2, mod, norm_w, w_in_bf16, seq):
    t, d = x2.shape
    tm = min(ROW_TILE, seq)
    assert tm == min(ATT_BLOCK, seq)
    per_batch = seq // tm
    in_w = w_in_bf16.shape[1]
    hw = 2 * DIFF_HEAD_DIM
    vt_shape = (t // seq, DIFF_HEADS, per_batch, hw + ONES_ROWS, tm)
    return pl.pallas_call(
        _in_proj_kernel,
        grid=(t // tm,),
        in_specs=[pl.BlockSpec((tm, d), lambda i: (i, 0)),
                  pl.BlockSpec((1, 6, d), lambda i: (i // per_batch, 0, 0)),
                  pl.BlockSpec((1, d), lambda i: (0, 0)),
                  pl.BlockSpec((d, in_w), lambda i: (0, 0))],
        out_specs=([pl.BlockSpec((tm, w), lambda i: (i, 0)) for w in IN_SIZES[:-1]]
                   + [pl.BlockSpec((1,) + vt_shape[1:2] + (1,) + vt_shape[3:],
                                   lambda i: (i // per_batch, 0, i % per_batch, 0, 0))]),
        out_shape=([jax.ShapeDtypeStruct((t, w), BF16) for w in IN_SIZES[:-1]]
                   + [jax.ShapeDtypeStruct(vt_shape, BF16)]),
        compiler_params=pltpu.CompilerParams(vmem_limit_bytes=VMEM_LIMIT),
        name="in_proj",
    )(x2, mod, norm_w, w_in_bf16)


def _rotary(x, cos, sin_even, sin_odd):
    nxt = pltpu.roll(x, LANES - 1, 1)
    prv = pltpu.roll(x, 1, 1)
    return x * cos + nxt * sin_even + prv * sin_odd


def _ret_kernel(q_ref, k_ref, v_ref, g_ref, cos_ref, sine_ref, sino_ref,
                hmask_ref, xi_ref, zeta_ref, dmask_ref, gch_ref, o_ref, state_ref):
    @pl.when(pl.program_id(1) == 0)
    def _():
        state_ref[...] = jnp.zeros_like(state_ref)

    n_sub = q_ref.shape[0] // RET_CHUNK
    for c in range(n_sub):
        rows = slice(c * RET_CHUNK, (c + 1) * RET_CHUNK)
        for pair in range(RET_HEADS // 2):
            lanes = slice(pair * LANES, (pair + 1) * LANES)
            cos = cos_ref[rows, :]
            sine = sine_ref[rows, :]
            sino = sino_ref[rows, :]
            qr = _rotary(q_ref[rows, lanes].astype(F32), cos, sine, sino)
            kr = _rotary(k_ref[rows, lanes].astype(F32), cos, sine, sino) * (RET_KEY_DIM ** -0.5)
            qb = qr.astype(BF16)
            for hh in range(2):
                h = 2 * pair + hh
                vcols = slice(h * RET_VAL_DIM, (h + 1) * RET_VAL_DIM)
                v = v_ref[rows, vcols]
                km = (kr * hmask_ref[h]).astype(BF16)
                scores = lax.dot_general(qb, km, (((1,), (1,)), ((), ())),
                                         preferred_element_type=F32) * dmask_ref[h]
                inner = jnp.dot(scores.astype(BF16), v, preferred_element_type=F32)
                qx = (qr * xi_ref[h]).astype(BF16)
                state = state_ref[h]
                cross = jnp.dot(qx, state.astype(BF16), preferred_element_type=F32)
                kz = (kr * zeta_ref[h]).astype(BF16)
                kv = lax.dot_general(kz, v, (((0,), (0,)), ((), ())),
                                     preferred_element_type=F32)
                state_ref[h] = state * gch_ref[h] + kv
                y = _rms(inner + cross)
                g = g_ref[rows, vcols].astype(F32)
                o_ref[rows, vcols] = (g * jax.nn.sigmoid(g) * y).astype(o_ref.dtype)


def _retention_tables(seq):
    dk, c, nh = RET_KEY_DIM, RET_CHUNK, RET_HEADS
    pos = jnp.arange(seq, dtype=F32)
    inv_freq = 1.0 / (10000.0 ** jnp.linspace(0.0, 1.0, dk // 2, dtype=F32))
    ang = pos[:, None] * jnp.repeat(inv_freq, 2)[None, :]
    sin = jnp.tile(jnp.sin(ang), (1, LANES // dk))
    cos = jnp.tile(jnp.cos(ang), (1, LANES // dk))
    even = (jnp.arange(LANES) % 2 == 0)[None, :]
    sin_even = jnp.where(even, -sin, 0.0)
    sin_odd = jnp.where(even, 0.0, sin)
    log_g = jnp.log(1.0 - 2.0 ** (-5.0 - jnp.arange(nh, dtype=F32)))
    i = jnp.arange(c, dtype=F32)
    rel = i[:, None] - i[None, :]
    dmask = jnp.where(rel[None] >= 0,
                      jnp.exp(jnp.maximum(rel, 0.0)[None] * log_g[:, None, None]), 0.0)
    zeta = jnp.exp((c - 1.0 - i)[None, :] * log_g[:, None])
    xi = jnp.exp((i + 1.0)[None, :] * log_g[:, None])
    g_chunk = jnp.exp(c * log_g)
    lane = jnp.arange(LANES)
    hmask = jnp.stack([((lane // dk) == (h % 2)).astype(F32) for h in range(nh)])[:, None, :]
    xi_t = xi[:, :, None] * hmask
    zeta_t = zeta[:, :, None] * hmask
    gch = jnp.broadcast_to(g_chunk[:, None, None], (nh, 1, LANES))
    return cos, sin_even, sin_odd, hmask, xi_t, zeta_t, dmask, gch


def _retention(q, k, v, g, batch, seq):
    t = q.shape[0]
    rb = min(RET_ROWS, seq)
    per_batch = seq // rb
    cos, sin_even, sin_odd, hmask, xi_t, zeta_t, dmask, gch = _retention_tables(seq)
    row = lambda w: pl.BlockSpec((rb, w), lambda b, j: (b * per_batch + j, 0))
    tab = lambda w: pl.BlockSpec((rb, w), lambda b, j: (j, 0))
    full = lambda a: pl.BlockSpec(a.shape, lambda b, j: (0,) * a.ndim)
    return pl.pallas_call(
        _ret_kernel,
        grid=(batch, per_batch),
        in_specs=[row(RET_QK_W), row(RET_QK_W), row(RET_V_W), row(RET_V_W),
                  tab(LANES), tab(LANES), tab(LANES),
                  full(hmask), full(xi_t), full(zeta_t), full(dmask), full(gch)],
        out_specs=row(RET_V_W),
        out_shape=jax.ShapeDtypeStruct((t, RET_V_W), BF16),
        scratch_shapes=[pltpu.VMEM((RET_HEADS, LANES, RET_VAL_DIM), F32)],
        compiler_params=pltpu.CompilerParams(
            dimension_semantics=("arbitrary", "arbitrary"), vmem_limit_bytes=VMEM_LIMIT),
        name="retention",
    )(q, k, v, g, cos, sin_even, sin_odd, hmask, xi_t, zeta_t, dmask, gch)


def _t5_bucket(rel):
    n = jnp.maximum(rel, 0)
    max_exact = REL_BUCKETS // 2
    nf = jnp.maximum(n, 1).astype(F32)
    large = max_exact + (jnp.log(nf / max_exact) / math.log(REL_MAX_DIST / max_exact)
                         * (REL_BUCKETS - max_exact)).astype(jnp.int32)
    large = jnp.minimum(large, REL_BUCKETS - 1)
    return jnp.where(n < max_exact, n, large)


def _bias_tiles(rel_bias, blk):
    far = rel_bias[REL_BUCKETS - 1]
    period = 2 * blk + 1
    dist = jnp.arange(period, dtype=jnp.int32)
    hot = (_t5_bucket(dist)[:, None] == jnp.arange(REL_BUCKETS, dtype=jnp.int32)).astype(F32)
    by_dist = (jnp.einsum('db,bh->hd', hot, rel_bias, precision=lax.Precision.HIGHEST)
               - far[:, None]) * LOG2_E
    wrapped = jnp.tile(by_dist, (1, blk))[:, :blk * (period - 1)].reshape(-1, blk, period - 1)
    r = jnp.arange(blk, dtype=jnp.int32)
    visible = (r[None, :] - r[:, None]) >= 0
    b0 = jnp.where(visible[None], wrapped[:, :, :blk], NEG_BIG)
    b1 = wrapped[:, :, blk:]
    return b0, b1


def _attn_kernel(q_ref, k_ref, vt_ref, b0_ref, b1_ref, lq1_ref, lk1_ref, lq2_ref, lk2_ref,
                 sw_ref, o_ref, m_ref, acc_ref):
    blk = q_ref.shape[0]
    hw = 2 * DIFF_HEAD_DIM
    i = pl.program_id(2)
    lane = lax.broadcasted_iota(jnp.int32, (1, LANES), 1)
    qm = []
    for hd in range(ATT_HEADS):
        q = (q_ref[:, hd * hw:(hd + 1) * hw].astype(F32) * (DIFF_HEAD_DIM ** -0.5 * LOG2_E)).astype(BF16)
        zero = jnp.zeros_like(q)
        qm.append((jnp.where(lane < DIFF_HEAD_DIM, q, zero), jnp.where(lane >= DIFF_HEAD_DIM, q, zero)))

    m_ref[...] = jnp.full_like(m_ref, NEG_BIG)
    acc_ref[...] = jnp.zeros_like(acc_ref)

    def step(blocks):
        kbs = [k_ref[pl.ds(pl.multiple_of(j * blk, blk), blk), :] for j, _, _ in blocks]
        chains = [(hd, mi, qs) for hd in range(ATT_HEADS) for mi in range(2)
                  for qs in range(blk // ATT_STRIP)]
        scores = []
        for hd, mi, qs in chains:
            qc = slice(qs * ATT_STRIP, (qs + 1) * ATT_STRIP)
            row = []
            for (_, bias, diagonal), kb in zip(blocks, kbs):
                keys = (qs + 1) * ATT_STRIP if diagonal else blk
                s = lax.dot_general(kb[:keys, hd * hw:(hd + 1) * hw], qm[hd][mi][qc, :],
                                    (((1,), (1,)), ((), ())), preferred_element_type=F32)
                row.append(s if bias is None else s + bias[hd, :keys, qc])
            scores.append(row)
        stats = []
        for (hd, mi, qs), row in zip(chains, scores):
            qc = slice(qs * ATT_STRIP, (qs + 1) * ATT_STRIP)
            m_old = m_ref[hd, mi, :, qc]
            m_new = m_old
            for s in row:
                m_new = jnp.maximum(m_new, jnp.max(s, axis=0, keepdims=True))
            stats.append((jnp.exp2(m_old - m_new), [jnp.exp2(s - m_new).astype(BF16) for s in row], m_new))
        for (hd, mi, qs), (alpha, ps, m_new) in zip(chains, stats):
            qc = slice(qs * ATT_STRIP, (qs + 1) * ATT_STRIP)
            pv = None
            for (j, _, _), p in zip(blocks, ps):
                part = jnp.dot(vt_ref[0, hd, j, :, 0:p.shape[0]], p, preferred_element_type=F32)
                pv = part if pv is None else pv + part
            acc_ref[hd, mi, :, qc] = alpha * acc_ref[hd, mi, :, qc] + pv
            m_ref[hd, mi, :, qc] = m_new

    n_far = jnp.maximum(i - 1, 0)

    def far_pair(pair, carry):
        step([(2 * pair, None, False), (2 * pair + 1, None, False)])
        return carry

    lax.fori_loop(0, lax.shift_right_logical(n_far, 1), far_pair, 0)

    @pl.when(lax.rem(n_far, 2) == 1)
    def _():
        step([(n_far - 1, None, False)])

    @pl.when(i >= 1)
    def _():
        step([(i - 1, b1_ref, False), (i, b0_ref, True)])

    @pl.when(i == 0)
    def _():
        step([(i, b0_ref, True)])

    lam = (jnp.exp(jnp.sum(lq1_ref[...] * lk1_ref[...], axis=-1, keepdims=True))
           - jnp.exp(jnp.sum(lq2_ref[...] * lk2_ref[...], axis=-1, keepdims=True))
           + LAMBDA_INIT)
    for hd in range(ATT_HEADS):
        a = (acc_ref[hd, 0, :hw, :] / acc_ref[hd, 0, hw:hw + 1, :]
             - lam * (acc_ref[hd, 1, :hw, :] / acc_ref[hd, 1, hw:hw + 1, :]))
        a = a * lax.rsqrt(jnp.mean(a * a, axis=0, keepdims=True) + NORM_EPS)
        o_ref[:, hd * hw:(hd + 1) * hw] = (a.T * sw_ref[...] * (1.0 - LAMBDA_INIT)).astype(o_ref.dtype)


def _diff_attention(q, k, vt, rel_bias, lq1, lk1, lq2, lk2, subln_w, batch, seq):
    t = q.shape[0]
    blk = min(ATT_BLOCK, seq)
    nq = seq // blk
    hw = 2 * DIFF_HEAD_DIM
    nh = ATT_HEADS
    b0, b1 = _bias_tiles(rel_bias, blk)
    small = lambda a: pl.BlockSpec(a.shape, lambda b, h, i: (0,) * a.ndim)
    return pl.pallas_call(
        _attn_kernel,
        grid=(batch, DIFF_HEADS // nh, nq),
        in_specs=[pl.BlockSpec((blk, nh * hw), lambda b, h, i: (b * nq + i, h)),
                  pl.BlockSpec((seq, nh * hw), lambda b, h, i: (b, h)),
                  pl.BlockSpec((1, nh, nq, hw + ONES_ROWS, blk), lambda b, h, i: (b, h, 0, 0, 0)),
                  pl.BlockSpec((nh, blk, blk), lambda b, h, i: (h, 0, 0)),
                  pl.BlockSpec((nh, blk, blk), lambda b, h, i: (h, 0, 0)),
                  small(lq1), small(lk1), small(lq2), small(lk2), small(subln_w)],
        out_specs=pl.BlockSpec((blk, nh * hw), lambda b, h, i: (b * nq + i, h)),
        out_shape=jax.ShapeDtypeStruct((t, DIFF_V_W), BF16),
        scratch_shapes=[pltpu.VMEM((nh, 2, 1, blk), F32), pltpu.VMEM((nh, 2, hw + ONES_ROWS, blk), F32)],
        compiler_params=pltpu.CompilerParams(
            dimension_semantics=("arbitrary", "arbitrary", "arbitrary"),
            vmem_limit_bytes=VMEM_LIMIT),
        name="diff_attn",
    )(q, k, vt, b0, b1, lq1, lk1, lq2, lk2, subln_w)


def _out_kernel(yr_ref, yd_ref, x_ref, mod_ref, nw_ref, wo_ref, wr_ref, br_ref, upper_ref,
                x1_ref, hp_ref, meta_ref, gate_ref, cnt_ref, run_ref):
    tm = x_ref.shape[0]
    ne = run_ref.shape[0]

    @pl.when(pl.program_id(0) == 0)
    def _():
        run_ref[...] = jnp.zeros_like(run_ref)

    mixed = (jnp.dot(yr_ref[...], wo_ref[0:RET_V_W, :], preferred_element_type=F32)
             + jnp.dot(yd_ref[...], wo_ref[RET_V_W:, :], preferred_element_type=F32))
    gate1 = mod_ref[0, 2:3, :]
    shift2 = mod_ref[0, 3:4, :]
    scale2 = mod_ref[0, 4:5, :]
    x1 = x_ref[...] + gate1 * mixed
    x1_ref[...] = x1
    h2 = (_rms(x1) * nw_ref[...]) * (1.0 + scale2) + shift2
    hp_ref[...] = _pack_halves(h2)

    h_hi = h2.astype(BF16)
    h_lo = (h2 - h_hi.astype(F32)).astype(BF16)
    nt_dims = (((1,), (1,)), ((), ()))
    both = lax.dot_general(wr_ref[...], h_hi, nt_dims, preferred_element_type=F32)
    low = lax.dot_general(wr_ref[0:ne, :], h_lo, nt_dims, preferred_element_type=F32)
    logits = both[0:ne, :] + both[ne:, :] + low + br_ref[...]

    row = lax.broadcasted_iota(jnp.int32, logits.shape, 0)
    work = logits
    vals, idxs, hots = [], [], []
    for _ in range(TOP_K):
        mx = jnp.max(work, axis=0, keepdims=True)
        idx = jnp.min(jnp.where(work == mx, row, ne), axis=0, keepdims=True)
        hot = row == idx
        vals.append(mx)
        idxs.append(idx)
        hots.append(hot)
        work = jnp.where(hot, -jnp.inf, work)
    exps = [jnp.exp(v - vals[0]) for v in vals]
    denom = exps[0] + exps[1] + exps[2] + exps[3]

    sel = jnp.zeros(logits.shape, F32)
    for hot in hots:
        sel = sel + hot.astype(F32)
    prefix = jnp.dot(sel.astype(BF16), upper_ref[...], preferred_element_type=F32) + run_ref[...]
    ranks = [jnp.sum(jnp.where(hot, prefix, 0.0), axis=0, keepdims=True) for hot in hots]
    run_ref[...] = run_ref[...] + jnp.sum(sel, axis=1, keepdims=True)
    cnt_ref[...] = jnp.broadcast_to(run_ref[...], cnt_ref.shape).astype(jnp.int32)

    meta_ref[...] = jnp.concatenate(idxs + [r.astype(jnp.int32) for r in ranks], axis=0)
    gate_ref[...] = jnp.concatenate([e / denom for e in exps] + [jnp.zeros_like(denom)] * TOP_K, axis=0)


def _out_router(y_r, y_d, x2, mod, norm_w, w_out_bf16, w_router, b_router, seq):
    t, d = x2.shape
    tm = min(ROW_TILE, seq)
    per_batch = seq // tm
    ne = w_router.shape[1]
    w_hi = w_router.astype(BF16)
    w_lo = (w_router - w_hi.astype(F32)).astype(BF16)
    wr_t = jnp.concatenate([w_hi, w_lo], axis=1).T
    idx = jnp.arange(tm, dtype=jnp.int32)
    upper = (idx[:, None] < idx[None, :]).astype(BF16)
    row = lambda w: pl.BlockSpec((tm, w), lambda i: (i, 0))
    col = lambda h: pl.BlockSpec((h, tm), lambda i: (0, i))
    const = lambda a: pl.BlockSpec(a.shape, lambda i: (0,) * a.ndim)
    x1, hp, meta_t, gates_t, counts = pl.pallas_call(
        _out_kernel,
        grid=(t // tm,),
        in_specs=[row(RET_V_W), row(DIFF_V_W), row(d),
                  pl.BlockSpec((1, 6, d), lambda i: (i // per_batch, 0, 0)),
                  const(norm_w), const(w_out_bf16), const(wr_t), const(b_router), const(upper)],
        out_specs=[row(d), row(d // 2), col(2 * TOP_K), col(2 * TOP_K),
                   pl.BlockSpec((ne, LANES), lambda i: (0, 0))],
        out_shape=[jax.ShapeDtypeStruct((t, d), F32),
                   jax.ShapeDtypeStruct((t, d // 2), jnp.uint32),
                   jax.ShapeDtypeStruct((2 * TOP_K, t), jnp.int32),
                   jax.ShapeDtypeStruct((2 * TOP_K, t), F32),
                   jax.ShapeDtypeStruct((ne, LANES), jnp.int32)],
        scratch_shapes=[pltpu.VMEM((ne, 1), F32)],
        compiler_params=pltpu.CompilerParams(
            dimension_semantics=("arbitrary",), vmem_limit_bytes=VMEM_LIMIT),
        name="out_router",
    )(y_r, y_d, x2, mod, norm_w, w_out_bf16, wr_t, b_router, upper)
    return x1, hp, meta_t, gates_t.T, counts[:, 0]


def _w1_prep_kernel(w_ref, p_ref, o_ref):
    for s in range(w_ref.shape[2] // PERM_TILE):
        cols = slice(s * PERM_TILE, (s + 1) * PERM_TILE)
        o_ref[0, :, cols] = jnp.dot(w_ref[0, :, cols].astype(BF16), p_ref[...],
                                    preferred_element_type=F32).astype(BF16)


def _pair_split_matrix():
    i = jnp.arange(PERM_TILE)[:, None]
    j = jnp.arange(PERM_TILE)[None, :]
    half = PERM_TILE // 2
    src = jnp.where(j < half, 2 * j, 2 * (j - half) + 1)
    return (i == src).astype(BF16)


def _w1_prep(w1):
    e, d, f2 = w1.shape
    tn = 1024
    return pl.pallas_call(
        _w1_prep_kernel,
        grid=(e, f2 // tn),
        in_specs=[pl.BlockSpec((1, d, tn), lambda i, j: (i, 0, j)),
                  pl.BlockSpec((PERM_TILE, PERM_TILE), lambda i, j: (0, 0))],
        out_specs=pl.BlockSpec((1, d, tn), lambda i, j: (i, 0, j)),
        out_shape=jax.ShapeDtypeStruct((e, d, f2), BF16),
        compiler_params=pltpu.CompilerParams(vmem_limit_bytes=VMEM_LIMIT),
        name="w1_prep",
    )(w1, _pair_split_matrix())


def _pair_split_bias(b1):
    e, f2 = b1.shape
    nt = f2 // PERM_TILE
    g = b1[:, 0::2].reshape(e, nt, 1, PERM_TILE // 2)
    l = b1[:, 1::2].reshape(e, nt, 1, PERM_TILE // 2)
    return jnp.concatenate([g, l], axis=2).reshape(e, 1, f2)


def _slot_tokens(dest_t, n_rows):
    n_tok = dest_t.shape[1]
    assert n_tok & (n_tok - 1) == 0
    flat = dest_t.reshape(-1)
    n_slots = flat.shape[0]
    chunk = min(SC_SLOT_CHUNK, n_slots)
    mesh = plsc.VectorSubcoreMesh(core_axis_name="c", subcore_axis_name="s",
                                  num_cores=SC_CORES, num_subcores=SC_SUBCORES)

    def body(dest_hbm, out_hbm, out_v, dest_v):
        first = jnp.logical_and(lax.axis_index("c") == 0, lax.axis_index("s") == 0)

        @pl.when(first)
        def _():
            zeros = jnp.zeros((SC_LANES,), jnp.int32)

            @pl.loop(0, n_rows // SC_LANES)
            def _(g):
                out_v[pl.ds(g * SC_LANES, SC_LANES)] = zeros

            lane = lax.iota(jnp.int32, SC_LANES)

            @pl.loop(0, n_slots // chunk)
            def _(c):
                pltpu.sync_copy(dest_hbm.at[pl.ds(c * chunk, chunk)], dest_v)

                @pl.loop(0, chunk // SC_LANES)
                def _(g):
                    rows = dest_v[pl.ds(g * SC_LANES, SC_LANES)]
                    slot = lane + (c * chunk + g * SC_LANES)
                    plsc.store_scatter(out_v, [rows], slot & (n_tok - 1))

            pltpu.sync_copy(out_v, out_hbm)

    return pl.kernel(
        body,
        out_type=jax.ShapeDtypeStruct((n_rows,), jnp.int32),
        mesh=mesh,
        scratch_types=[pltpu.VMEM((n_rows,), jnp.int32), pltpu.VMEM((chunk,), jnp.int32)],
        compiler_params=pltpu.CompilerParams(needs_layout_passes=False),
        name="slot_tokens",
    )(flat)


def _expert_kernel(be_ref, nv_ref, nexte_ref, tcur_ref, tnxt_ref, hp_ref, w1_ref, b1_ref, w2_ref, b2_ref,
                   o_ref, hbuf, xb0, xb1, w1buf, w2buf, w2b, sem, wsems):
    b = pl.program_id(0)
    nv = nv_ref[0]
    bm = xb0.shape[0]

    def gather(tok_ref, dst):
        for r in range(bm):
            dst[pl.ds(r, 1), :] = hbuf[pl.ds(tok_ref[0, 0, r], 1), :]

    def weight_copies(e, slot):
        return (pltpu.make_async_copy(w1_ref.at[e], w1buf.at[slot], wsems.at[0, slot]),
                pltpu.make_async_copy(w2_ref.at[e], w2buf.at[slot], wsems.at[1, slot]))

    e = be_ref[b]
    prev_e = be_ref[jnp.maximum(b - 1, 0)]
    first_of_expert = jnp.logical_and(b < nv, jnp.logical_or(b == 0, e != prev_e))
    slot = lax.rem(nexte_ref[N_EXPERTS + e], 2)

    @pl.when(b == 0)
    def _():
        for cp in weight_copies(e, slot):
            cp.start()
        load = pltpu.make_async_copy(hp_ref, hbuf, sem)
        load.start()
        load.wait()
        gather(tcur_ref, xb0)

    @pl.when(first_of_expert)
    def _():
        for cp in weight_copies(e, slot):
            cp.wait()
        nxt = nexte_ref[e]

        @pl.when(nxt >= 0)
        def _():
            for cp in weight_copies(nxt, 1 - slot):
                cp.start()

        w2b[...] = w2buf[slot].astype(BF16)

    def run(cur, nxt_rows):
        x = _unpack_halves(cur[...])
        half = PERM_TILE // 2
        acts = []
        for j in range(w1buf.shape[2] // PERM_TILE):
            cols = slice(j * PERM_TILE, (j + 1) * PERM_TILE)
            h = jnp.dot(x, w1buf[slot, :, cols], preferred_element_type=F32) + b1_ref[0, :, cols]
            glu = jnp.minimum(h[:, :half], SWIGLU_LIMIT)
            lin = jnp.clip(h[:, half:], -SWIGLU_LIMIT, SWIGLU_LIMIT)
            acts.append((glu * jax.nn.sigmoid(SWIGLU_ALPHA * glu) * (lin + 1.0)).astype(BF16))
        act = jnp.concatenate(acts, axis=1)
        y = jnp.dot(act, w2b[...], preferred_element_type=F32) + b2_ref[0]
        _store_row_tiles(o_ref, y)
        gather(tnxt_ref, nxt_rows)

    even = lax.rem(b, 2) == 0

    @pl.when(jnp.logical_and(b < nv, even))
    def _():
        run(xb0, xb1)

    @pl.when(jnp.logical_and(b < nv, jnp.logical_not(even)))
    def _():
        run(xb1, xb0)

    @pl.when(b >= nv)
    def _():
        o_ref[...] = jnp.zeros_like(o_ref)


def _experts(block_e, n_valid, next_e, slot_tok, hp, w1p, b1p, w2, b2):
    d = D_MODEL
    f2 = w1p.shape[2]
    f = w2.shape[1]
    bm = EXPERT_ROWS
    nb = slot_tok.shape[0] // bm
    tok3 = slot_tok.reshape(nb, 1, bm)
    exp = lambda b, be, nv, ne: be[jnp.maximum(jnp.minimum(b, nv[0] - 1), 0)]
    grid_spec = pltpu.PrefetchScalarGridSpec(
        num_scalar_prefetch=3,
        grid=(nb,),
        in_specs=[pl.BlockSpec((1, 1, bm), lambda b, be, nv, ne: (b, 0, 0), memory_space=pltpu.SMEM),
                  pl.BlockSpec((1, 1, bm), lambda b, be, nv, ne: (jnp.minimum(b + 1, nb - 1), 0, 0),
                               memory_space=pltpu.SMEM),
                  pl.BlockSpec(memory_space=pl.ANY),
                  pl.BlockSpec(memory_space=pl.ANY),
                  pl.BlockSpec((1, 1, f2), lambda b, be, nv, ne: (exp(b, be, nv, ne), 0, 0)),
                  pl.BlockSpec(memory_space=pl.ANY),
                  pl.BlockSpec((1, 1, d), lambda b, be, nv, ne: (exp(b, be, nv, ne), 0, 0))],
        out_specs=pl.BlockSpec((bm * ROW_SUB, LANES), lambda b, be, nv, ne: (b, 0)),
        scratch_shapes=[pltpu.VMEM(hp.shape, hp.dtype), pltpu.VMEM((bm, hp.shape[1]), hp.dtype),
                        pltpu.VMEM((bm, hp.shape[1]), hp.dtype),
                        pltpu.VMEM((2,) + w1p.shape[1:], w1p.dtype), pltpu.VMEM((2,) + w2.shape[1:], w2.dtype),
                        pltpu.VMEM(w2.shape[1:], BF16),
                        pltpu.SemaphoreType.DMA(()), pltpu.SemaphoreType.DMA((2, 2))],
    )
    return pl.pallas_call(
        _expert_kernel,
        grid_spec=grid_spec,
        out_shape=jax.ShapeDtypeStruct((nb * bm * ROW_SUB, LANES), F32),
        compiler_params=pltpu.CompilerParams(
            dimension_semantics=("arbitrary",), vmem_limit_bytes=VMEM_LIMIT),
        name="experts",
    )(block_e, n_valid, next_e, tok3, tok3, hp, w1p, b1p, w2, b2)


def _combine_kernel(dcur_ref, dnxt_ref, gate_ref, x1_ref, mod_ref, nf_ref, y_ref, o_ref, buf, sems):
    nt = x1_ref.shape[0]
    i = pl.program_id(0)
    n = pl.num_programs(0)
    slot = lax.rem(i, 2)

    def row_copy(src_row, s, kk, r):
        return pltpu.make_async_copy(y_ref.at[pl.ds(src_row * ROW_SUB, ROW_SUB), :],
                                     buf.at[s, kk, pl.ds(r * ROW_SUB, ROW_SUB), :], sems.at[s])

    def fetch(d_ref, s):
        for j in range(nt):
            for kk in range(TOP_K):
                row_copy(d_ref[0, 0, kk * nt + j], s, kk, j).start(priority=kk % DMA_QUEUES)

    def wait_all(s):
        for kk in range(TOP_K):
            pltpu.make_async_copy(y_ref.at[pl.ds(0, nt * ROW_SUB), :], buf.at[s, kk], sems.at[s]).wait()

    @pl.when(i == 0)
    def _():
        fetch(dcur_ref, 0)

    fetch(dnxt_ref, 1 - slot)
    wait_all(slot)

    g = gate_ref[...]
    parts = []
    for s in range(ROW_SUB):
        acc = buf[slot, 0, pl.ds(s, nt, stride=ROW_SUB), :] * g[:, 0:1]
        for kk in range(1, TOP_K):
            acc = acc + buf[slot, kk, pl.ds(s, nt, stride=ROW_SUB), :] * g[:, kk:kk + 1]
        parts.append(acc)
    moe = jnp.concatenate(parts, axis=1)
    gate2 = mod_ref[0, 5:6, :]
    x2 = x1_ref[...] + gate2 * moe
    o_ref[...] = _rms(x2) * nf_ref[...]

    @pl.when(i == n - 1)
    def _():
        wait_all(1 - slot)


def _combine(dest_t, gates, x1, mod, normf_w, y, seq):
    t, d = x1.shape
    nt = min(COMBINE_TOKENS, seq)
    steps = t // nt
    per_batch = seq // nt
    dest2 = jnp.transpose(dest_t.reshape(TOP_K, steps, nt), (1, 0, 2)).reshape(steps, 1, nt * TOP_K)
    return pl.pallas_call(
        _combine_kernel,
        grid=(steps,),
        in_specs=[pl.BlockSpec((1, 1, nt * TOP_K), lambda i: (i, 0, 0), memory_space=pltpu.SMEM),
                  pl.BlockSpec((1, 1, nt * TOP_K), lambda i: (jnp.minimum(i + 1, steps - 1), 0, 0),
                               memory_space=pltpu.SMEM),
                  pl.BlockSpec((nt, 2 * TOP_K), lambda i: (i, 0)),
                  pl.BlockSpec((nt, d), lambda i: (i, 0)),
                  pl.BlockSpec((1, 6, d), lambda i: (i // per_batch, 0, 0)),
                  pl.BlockSpec((1, d), lambda i: (0, 0)),
                  pl.BlockSpec(memory_space=pl.ANY)],
        out_specs=pl.BlockSpec((nt, d), lambda i: (i, 0)),
        out_shape=jax.ShapeDtypeStruct((t, d), F32),
        scratch_shapes=[pltpu.VMEM((2, TOP_K, nt * ROW_SUB, LANES), F32),
                        pltpu.SemaphoreType.DMA((2,))],
        compiler_params=pltpu.CompilerParams(
            dimension_semantics=("arbitrary",), vmem_limit_bytes=VMEM_LIMIT),
        name="combine",
    )(dest2, dest2, gates, x1, mod, normf_w, y)


def kernel(x, c, w_ada, b_ada, norm1_w, w_in, lam_q1, lam_k1, lam_q2, lam_k2, subln_w, rel_bias,
           w_out, norm2_w, w_router, b_router, w1, b1, w2, b2, normf_w):
    batch, seq, d = x.shape
    t = batch * seq
    x2 = x.reshape(t, d)

    c_pad = jnp.zeros((8, d), F32).at[:batch].set(c)
    mod = _ada(c_pad, w_ada[0], b_ada[0][None, :])[:batch].reshape(batch, 6, d)

    q_r, k_r, v_r, g_r, q_d, k_d, vt_d = _in_proj(x2, mod, norm1_w[0][None, :],
                                                 w_in[0].astype(BF16), seq)
    y_r = _retention(q_r, k_r, v_r, g_r, batch, seq)
    y_d = _diff_attention(q_d, k_d, vt_d, rel_bias, lam_q1, lam_k1, lam_q2, lam_k2,
                          subln_w, batch, seq)

    x1, hp, meta, gates, counts = _out_router(y_r, y_d, x2, mod, norm2_w[0][None, :],
                                              w_out[0].astype(BF16), w_router[0],
                                              b_router[0][:, None], seq)

    bm = EXPERT_ROWS
    padded = (counts + bm - 1) // bm * bm
    pad_end = jnp.cumsum(padded)
    pad_start = pad_end - padded
    hot_e = meta[None, :TOP_K, :] == jnp.arange(N_EXPERTS, dtype=jnp.int32)[:, None, None]
    dest_t = jnp.sum(jnp.where(hot_e, pad_start[:, None, None], 0), axis=0) + meta[TOP_K:, :]
    n_rows = (t * TOP_K // bm + N_EXPERTS) * bm
    nb = n_rows // bm
    block_start = jnp.arange(nb, dtype=jnp.int32) * bm
    block_e = jnp.minimum(jnp.sum((pad_end[None, :] <= block_start[:, None]).astype(jnp.int32), axis=1),
                          N_EXPERTS - 1)
    n_valid = (pad_end[-1:] // bm).astype(jnp.int32)
    slot_tok = _slot_tokens(dest_t, n_rows)

    owns = padded > 0
    ids = jnp.arange(N_EXPERTS, dtype=jnp.int32)
    later = jnp.logical_and(owns[None, :], ids[None, :] > ids[:, None])
    next_owner = jnp.min(jnp.where(later, ids[None, :], N_EXPERTS), axis=1)
    next_owner = jnp.where(next_owner == N_EXPERTS, -1, next_owner).astype(jnp.int32)
    ordinal = (jnp.cumsum(owns.astype(jnp.int32)) - 1).astype(jnp.int32)
    next_e = jnp.concatenate([next_owner, jnp.maximum(ordinal, 0)])

    ys = _experts(block_e, n_valid, next_e, slot_tok, hp, _w1_prep(w1[0]), _pair_split_bias(b1[0]),
                  w2[0], b2[0][:, None, :])
    out = _combine(dest_t, gates, x1, mod, normf_w[None, :], ys, seq)
    return out.reshape(batch, seq, d)
```

```python
import math

import jax
import jax.numpy as jnp
from jax import lax
from jax.experimental import pallas as pl
from jax.experimental.pallas import tpu as pltpu
from jax.experimental.pallas import tpu_sc as plsc

F32 = jnp.float32
BF16 = jnp.bfloat16

D_MODEL = 1024
RET_HEADS = 4
RET_KEY_DIM = 64
RET_VAL_DIM = 128
RET_QK_W = RET_HEADS * RET_KEY_DIM
RET_V_W = RET_HEADS * RET_VAL_DIM
RET_CHUNK = 128
DIFF_HEADS = 4
DIFF_HEAD_DIM = 64
DIFF_QK_W = DIFF_HEADS * 2 * DIFF_HEAD_DIM
DIFF_V_W = DIFF_HEADS * 2 * DIFF_HEAD_DIM
IN_SIZES = (RET_QK_W, RET_QK_W, RET_V_W, RET_V_W, DIFF_QK_W, DIFF_QK_W, DIFF_V_W)
REL_BUCKETS = 32
REL_MAX_DIST = 128
N_EXPERTS = 32
TOP_K = 4
SWIGLU_ALPHA = 1.702
SWIGLU_LIMIT = 7.0
NORM_EPS = 1e-6
LAMBDA_INIT = 0.8 - 0.6 * math.exp(-0.3 * 0)

LANES = 128
ROW_SUB = D_MODEL // LANES
NEG_BIG = -1e30
LOG2_E = math.log2(math.e)
ONES_ROWS = 16
VMEM_LIMIT = 56 * 1024 * 1024

ROW_TILE = 512
RET_ROWS = 512
ATT_BLOCK = 512
ATT_STRIP = 256
ATT_HEADS = 4
EXPERT_ROWS = 256
PERM_TILE = 256
COMBINE_TOKENS = 256
DMA_QUEUES = 2
SC_CORES = 2
SC_SUBCORES = 16
SC_LANES = 16
SC_SLOT_CHUNK = 4096


def _rms(x):
    return x * lax.rsqrt(jnp.mean(x * x, axis=-1, keepdims=True) + NORM_EPS)


def _store_row_tiles(ref, x):
    rows = x.shape[0]
    for s in range(ROW_SUB):
        ref[pl.ds(s, rows, stride=ROW_SUB), :] = x[:, s * LANES:(s + 1) * LANES]


def _pack_halves(x):
    half = x.shape[1] // 2
    lo = lax.bitcast_convert_type(x[:, :half].astype(BF16).astype(F32), jnp.uint32)
    hi = lax.bitcast_convert_type(x[:, half:].astype(BF16).astype(F32), jnp.uint32)
    return (hi & jnp.uint32(0xFFFF0000)) | (lo >> 16)


def _unpack_halves(w):
    lo = lax.bitcast_convert_type(w << 16, F32).astype(BF16)
    hi = lax.bitcast_convert_type(w & jnp.uint32(0xFFFF0000), F32).astype(BF16)
    return jnp.concatenate([lo, hi], axis=1)


def _ada_kernel(c_ref, w_ref, b_ref, o_ref):
    c = c_ref[...]
    cond = c * jax.nn.sigmoid(c)
    o_ref[...] = jnp.dot(cond, w_ref[...], precision=lax.Precision.HIGHEST,
                         preferred_element_type=F32) + b_ref[...]


def _ada(c_pad, w_ada, b_ada):
    rows, d = c_pad.shape
    n = w_ada.shape[1]
    tn = 1024
    return pl.pallas_call(
        _ada_kernel,
        grid=(n // tn,),
        in_specs=[pl.BlockSpec((rows, d), lambda j: (0, 0)),
                  pl.BlockSpec((d, tn), lambda j: (0, j)),
                  pl.BlockSpec((1, tn), lambda j: (0, j))],
        out_specs=pl.BlockSpec((rows, tn), lambda j: (0, j)),
        out_shape=jax.ShapeDtypeStruct((rows, n), F32),
        name="ada",
    )(c_pad, w_ada, b_ada)


def _in_proj_kernel(x_ref, mod_ref, nw_ref, w_ref, *o_refs):
    x = x_ref[...]
    shift = mod_ref[0, 0:1, :]
    scale = mod_ref[0, 1:2, :]
    h = (_rms(x) * nw_ref[...]) * (1.0 + scale) + shift
    hb = h.astype(BF16)
    off = 0
    for o_ref, width in zip(o_refs[:-1], IN_SIZES[:-1]):
        o_ref[...] = jnp.dot(hb, w_ref[:, off:off + width],
                             preferred_element_type=F32).astype(o_ref.dtype)
        off += width
    vt_ref = o_refs[-1]
    v = jnp.dot(hb, w_ref[:, off:], preferred_element_type=F32)
    hw = 2 * DIFF_HEAD_DIM
    tail = (lax.broadcasted_iota(jnp.int32, (ONES_ROWS, v.shape[0]), 0) == 0).astype(vt_ref.dtype)
    for hd in range(DIFF_HEADS):
        vt_ref[0, hd, 0, 0:hw, :] = v[:, hd * hw:(hd + 1) * hw].T.astype(vt_ref.dtype)
        vt_ref[0, hd, 0, hw:, :] = tail


def _in_proj(x2, mod, norm_w, w_in_bf16, seq):
    t, d = x2.shape
    tm = min(ROW_TILE, seq)
    assert tm == min(ATT_BLOCK, seq)
    per_batch = seq // tm
    in_w = w_in_bf16.shape[1]
    hw = 2 * DIFF_HEAD_DIM
    vt_shape = (t // seq, DIFF_HEADS, per_batch, hw + ONES_ROWS, tm)
    return pl.pallas_call(
        _in_proj_kernel,
        grid=(t // tm,),
        in_specs=[pl.BlockSpec((tm, d), lambda i: (i, 0)),
                  pl.BlockSpec((1, 6, d), lambda i: (i // per_batch, 0, 0)),
                  pl.BlockSpec((1, d), lambda i: (0, 0)),
                  pl.BlockSpec((d, in_w), lambda i: (0, 0))],
        out_specs=([pl.BlockSpec((tm, w), lambda i: (i, 0)) for w in IN_SIZES[:-1]]
                   + [pl.BlockSpec((1,) + vt_shape[1:2] + (1,) + vt_shape[3:],
                                   lambda i: (i // per_batch, 0, i % per_batch, 0, 0))]),
        out_shape=([jax.ShapeDtypeStruct((t, w), BF16) for w in IN_SIZES[:-1]]
                   + [jax.ShapeDtypeStruct(vt_shape, BF16)]),
        compiler_params=pltpu.CompilerParams(vmem_limit_bytes=VMEM_LIMIT),
        name="in_proj",
    )(x2, mod, norm_w, w_in_bf16)


def _split_w1_tiles(w_ref, p_ref, o_ref):
    for e in range(w_ref.shape[0]):
        for s in range(w_ref.shape[2] // PERM_TILE):
            cols = slice(s * PERM_TILE, (s + 1) * PERM_TILE)
            o_ref[e, :, cols] = jnp.dot(w_ref[e, :, cols].astype(BF16), p_ref[...],
                                        preferred_element_type=F32).astype(BF16)


def _pair_split_matrix():
    i = jnp.arange(PERM_TILE)[:, None]
    j = jnp.arange(PERM_TILE)[None, :]
    half = PERM_TILE // 2
    src = jnp.where(j < half, 2 * j, 2 * (j - half) + 1)
    return (i == src).astype(BF16)


def _rotary(x, cos, sin_even, sin_odd):
    nxt = pltpu.roll(x, LANES - 1, 1)
    prv = pltpu.roll(x, 1, 1)
    return x * cos + nxt * sin_even + prv * sin_odd


def _ret_kernel(q_ref, k_ref, v_ref, g_ref, cos_ref, sine_ref, sino_ref,
                hmask_ref, xi_ref, zeta_ref, dmask_ref, gch_ref, w1_ref, perm_ref, o_ref, w1p_ref, state_ref):
    _split_w1_tiles(w1_ref, perm_ref, w1p_ref)

    @pl.when(pl.program_id(1) == 0)
    def _():
        state_ref[...] = jnp.zeros_like(state_ref)

    n_sub = q_ref.shape[0] // RET_CHUNK
    for c in range(n_sub):
        rows = slice(c * RET_CHUNK, (c + 1) * RET_CHUNK)
        for pair in range(RET_HEADS // 2):
            lanes = slice(pair * LANES, (pair + 1) * LANES)
            cos = cos_ref[rows, :]
            sine = sine_ref[rows, :]
            sino = sino_ref[rows, :]
            qr = _rotary(q_ref[rows, lanes].astype(F32), cos, sine, sino)
            kr = _rotary(k_ref[rows, lanes].astype(F32), cos, sine, sino) * (RET_KEY_DIM ** -0.5)
            qb = qr.astype(BF16)
            for hh in range(2):
                h = 2 * pair + hh
                vcols = slice(h * RET_VAL_DIM, (h + 1) * RET_VAL_DIM)
                v = v_ref[rows, vcols]
                km = (kr * hmask_ref[h]).astype(BF16)
                scores = lax.dot_general(qb, km, (((1,), (1,)), ((), ())),
                                         preferred_element_type=F32) * dmask_ref[h]
                inner = jnp.dot(scores.astype(BF16), v, preferred_element_type=F32)
                qx = (qr * xi_ref[h]).astype(BF16)
                state = state_ref[h]
                cross = jnp.dot(qx, state.astype(BF16), preferred_element_type=F32)
                kz = (kr * zeta_ref[h]).astype(BF16)
                kv = lax.dot_general(kz, v, (((0,), (0,)), ((), ())),
                                     preferred_element_type=F32)
                state_ref[h] = state * gch_ref[h] + kv
                y = _rms(inner + cross)
                g = g_ref[rows, vcols].astype(F32)
                o_ref[rows, vcols] = (g * jax.nn.sigmoid(g) * y).astype(o_ref.dtype)


def _retention_tables(seq):
    dk, c, nh = RET_KEY_DIM, RET_CHUNK, RET_HEADS
    pos = jnp.arange(seq, dtype=F32)
    inv_freq = 1.0 / (10000.0 ** jnp.linspace(0.0, 1.0, dk // 2, dtype=F32))
    ang = pos[:, None] * jnp.repeat(inv_freq, 2)[None, :]
    sin = jnp.tile(jnp.sin(ang), (1, LANES // dk))
    cos = jnp.tile(jnp.cos(ang), (1, LANES // dk))
    even = (jnp.arange(LANES) % 2 == 0)[None, :]
    sin_even = jnp.where(even, -sin, 0.0)
    sin_odd = jnp.where(even, 0.0, sin)
    log_g = jnp.log(1.0 - 2.0 ** (-5.0 - jnp.arange(nh, dtype=F32)))
    i = jnp.arange(c, dtype=F32)
    rel = i[:, None] - i[None, :]
    dmask = jnp.where(rel[None] >= 0,
                      jnp.exp(jnp.maximum(rel, 0.0)[None] * log_g[:, None, None]), 0.0)
    zeta = jnp.exp((c - 1.0 - i)[None, :] * log_g[:, None])
    xi = jnp.exp((i + 1.0)[None, :] * log_g[:, None])
    g_chunk = jnp.exp(c * log_g)
    lane = jnp.arange(LANES)
    hmask = jnp.stack([((lane // dk) == (h % 2)).astype(F32) for h in range(nh)])[:, None, :]
    xi_t = xi[:, :, None] * hmask
    zeta_t = zeta[:, :, None] * hmask
    gch = jnp.broadcast_to(g_chunk[:, None, None], (nh, 1, LANES))
    return cos, sin_even, sin_odd, hmask, xi_t, zeta_t, dmask, gch


def _retention(q, k, v, g, w1, batch, seq):
    t = q.shape[0]
    rb = min(RET_ROWS, seq)
    per_batch = seq // rb
    cos, sin_even, sin_odd, hmask, xi_t, zeta_t, dmask, gch = _retention_tables(seq)
    perm = _pair_split_matrix()
    per_step, rest = divmod(w1.shape[0], batch * per_batch)
    assert rest == 0 and per_step >= 1
    expert = pl.BlockSpec((per_step,) + w1.shape[1:], lambda b, j: (b * per_batch + j, 0, 0))
    row = lambda w: pl.BlockSpec((rb, w), lambda b, j: (b * per_batch + j, 0))
    tab = lambda w: pl.BlockSpec((rb, w), lambda b, j: (j, 0))
    full = lambda a: pl.BlockSpec(a.shape, lambda b, j: (0,) * a.ndim)
    return pl.pallas_call(
        _ret_kernel,
        grid=(batch, per_batch),
        in_specs=[row(RET_QK_W), row(RET_QK_W), row(RET_V_W), row(RET_V_W),
                  tab(LANES), tab(LANES), tab(LANES),
                  full(hmask), full(xi_t), full(zeta_t), full(dmask), full(gch), expert, full(perm)],
        out_specs=[row(RET_V_W), expert],
        out_shape=[jax.ShapeDtypeStruct((t, RET_V_W), BF16), jax.ShapeDtypeStruct(w1.shape, BF16)],
        scratch_shapes=[pltpu.VMEM((RET_HEADS, LANES, RET_VAL_DIM), F32)],
        compiler_params=pltpu.CompilerParams(
            dimension_semantics=("arbitrary", "arbitrary"), vmem_limit_bytes=VMEM_LIMIT),
        name="retention",
    )(q, k, v, g, cos, sin_even, sin_odd, hmask, xi_t, zeta_t, dmask, gch, w1, perm)


def _t5_bucket(rel):
    n = jnp.maximum(rel, 0)
    max_exact = REL_BUCKETS // 2
    nf = jnp.maximum(n, 1).astype(F32)
    large = max_exact + (jnp.log(nf / max_exact) / math.log(REL_MAX_DIST / max_exact)
                         * (REL_BUCKETS - max_exact)).astype(jnp.int32)
    large = jnp.minimum(large, REL_BUCKETS - 1)
    return jnp.where(n < max_exact, n, large)


def _bias_tiles(rel_bias, blk):
    r = jnp.arange(blk, dtype=jnp.int32)
    far = rel_bias[REL_BUCKETS - 1]
    rel0 = r[None, :] - r[:, None]
    rel1 = rel0 + blk
    buckets = jnp.arange(REL_BUCKETS, dtype=jnp.int32)

    def tile(rel):
        hot = (_t5_bucket(rel)[:, :, None] == buckets).astype(F32)
        return jnp.einsum('krb,bh->hkr', hot, rel_bias, precision=lax.Precision.HIGHEST)

    b0 = jnp.where(rel0[None] >= 0, (tile(rel0) - far[:, None, None]) * LOG2_E, NEG_BIG)
    b1 = (tile(rel1) - far[:, None, None]) * LOG2_E
    return b0, b1


def _attn_kernel(q_ref, k_ref, vt_ref, b0_ref, b1_ref, lq1_ref, lk1_ref, lq2_ref, lk2_ref,
                 sw_ref, o_ref, m_ref, acc_ref):
    blk = q_ref.shape[0]
    hw = 2 * DIFF_HEAD_DIM
    i = pl.program_id(2)
    lane = lax.broadcasted_iota(jnp.int32, (1, LANES), 1)
    qm = []
    for hd in range(ATT_HEADS):
        q = (q_ref[:, hd * hw:(hd + 1) * hw].astype(F32) * (DIFF_HEAD_DIM ** -0.5 * LOG2_E)).astype(BF16)
        zero = jnp.zeros_like(q)
        qm.append((jnp.where(lane < DIFF_HEAD_DIM, q, zero), jnp.where(lane >= DIFF_HEAD_DIM, q, zero)))

    m_ref[...] = jnp.full_like(m_ref, NEG_BIG)
    acc_ref[...] = jnp.zeros_like(acc_ref)

    def step(blocks):
        kbs = [k_ref[pl.ds(pl.multiple_of(j * blk, blk), blk), :] for j, _, _ in blocks]
        chains = [(hd, mi, qs) for hd in range(ATT_HEADS) for mi in range(2)
                  for qs in range(blk // ATT_STRIP)]
        scores = []
        for hd, mi, qs in chains:
            qc = slice(qs * ATT_STRIP, (qs + 1) * ATT_STRIP)
            row = []
            for (_, bias, diagonal), kb in zip(blocks, kbs):
                keys = (qs + 1) * ATT_STRIP if diagonal else blk
                s = lax.dot_general(kb[:keys, hd * hw:(hd + 1) * hw], qm[hd][mi][qc, :],
                                    (((1,), (1,)), ((), ())), preferred_element_type=F32)
                row.append(s if bias is None else s + bias[hd, :keys, qc])
            scores.append(row)
        stats = []
        for (hd, mi, qs), row in zip(chains, scores):
            qc = slice(qs * ATT_STRIP, (qs + 1) * ATT_STRIP)
            m_old = m_ref[hd, mi, :, qc]
            m_new = m_old
            for s in row:
                m_new = jnp.maximum(m_new, jnp.max(s, axis=0, keepdims=True))
            stats.append((jnp.exp2(m_old - m_new), [jnp.exp2(s - m_new).astype(BF16) for s in row], m_new))
        for (hd, mi, qs), (alpha, ps, m_new) in zip(chains, stats):
            qc = slice(qs * ATT_STRIP, (qs + 1) * ATT_STRIP)
            pv = None
            for (j, _, _), p in zip(blocks, ps):
                part = jnp.dot(vt_ref[0, hd, j, :, 0:p.shape[0]], p, preferred_element_type=F32)
                pv = part if pv is None else pv + part
            acc_ref[hd, mi, :, qc] = alpha * acc_ref[hd, mi, :, qc] + pv
            m_ref[hd, mi, :, qc] = m_new

    n_far = jnp.maximum(i - 1, 0)

    def far_pair(pair, carry):
        step([(2 * pair, None, False), (2 * pair + 1, None, False)])
        return carry

    lax.fori_loop(0, lax.shift_right_logical(n_far, 1), far_pair, 0)

    @pl.when(lax.rem(n_far, 2) == 1)
    def _():
        step([(n_far - 1, None, False)])

    @pl.when(i >= 1)
    def _():
        step([(i - 1, b1_ref, False), (i, b0_ref, True)])

    @pl.when(i == 0)
    def _():
        step([(i, b0_ref, True)])

    lam = (jnp.exp(jnp.sum(lq1_ref[...] * lk1_ref[...], axis=-1, keepdims=True))
           - jnp.exp(jnp.sum(lq2_ref[...] * lk2_ref[...], axis=-1, keepdims=True))
           + LAMBDA_INIT)
    for hd in range(ATT_HEADS):
        a = (acc_ref[hd, 0, :hw, :] / acc_ref[hd, 0, hw:hw + 1, :]
             - lam * (acc_ref[hd, 1, :hw, :] / acc_ref[hd, 1, hw:hw + 1, :]))
        a = a * lax.rsqrt(jnp.mean(a * a, axis=0, keepdims=True) + NORM_EPS)
        o_ref[:, hd * hw:(hd + 1) * hw] = (a.T * sw_ref[...] * (1.0 - LAMBDA_INIT)).astype(o_ref.dtype)


def _diff_attention(q, k, vt, rel_bias, lq1, lk1, lq2, lk2, subln_w, batch, seq):
    t = q.shape[0]
    blk = min(ATT_BLOCK, seq)
    nq = seq // blk
    hw = 2 * DIFF_HEAD_DIM
    nh = ATT_HEADS
    b0, b1 = _bias_tiles(rel_bias, blk)
    small = lambda a: pl.BlockSpec(a.shape, lambda b, h, i: (0,) * a.ndim)
    return pl.pallas_call(
        _attn_kernel,
        grid=(batch, DIFF_HEADS // nh, nq),
        in_specs=[pl.BlockSpec((blk, nh * hw), lambda b, h, i: (b * nq + i, h)),
                  pl.BlockSpec((seq, nh * hw), lambda b, h, i: (b, h)),
                  pl.BlockSpec((1, nh, nq, hw + ONES_ROWS, blk), lambda b, h, i: (b, h, 0, 0, 0)),
                  pl.BlockSpec((nh, blk, blk), lambda b, h, i: (h, 0, 0)),
                  pl.BlockSpec((nh, blk, blk), lambda b, h, i: (h, 0, 0)),
                  small(lq1), small(lk1), small(lq2), small(lk2), small(subln_w)],
        out_specs=pl.BlockSpec((blk, nh * hw), lambda b, h, i: (b * nq + i, h)),
        out_shape=jax.ShapeDtypeStruct((t, DIFF_V_W), BF16),
        scratch_shapes=[pltpu.VMEM((nh, 2, 1, blk), F32), pltpu.VMEM((nh, 2, hw + ONES_ROWS, blk), F32)],
        compiler_params=pltpu.CompilerParams(
            dimension_semantics=("arbitrary", "arbitrary", "arbitrary"),
            vmem_limit_bytes=VMEM_LIMIT),
        name="diff_attn",
    )(q, k, vt, b0, b1, lq1, lk1, lq2, lk2, subln_w)


def _out_kernel(yr_ref, yd_ref, x_ref, mod_ref, nw_ref, wo_ref, wr_ref, br_ref, upper_ref,
                x1_ref, hp_ref, meta_ref, gate_ref, cnt_ref, run_ref):
    tm = x_ref.shape[0]
    ne = run_ref.shape[0]

    @pl.when(pl.program_id(0) == 0)
    def _():
        run_ref[...] = jnp.zeros_like(run_ref)

    mixed = (jnp.dot(yr_ref[...], wo_ref[0:RET_V_W, :], preferred_element_type=F32)
             + jnp.dot(yd_ref[...], wo_ref[RET_V_W:, :], preferred_element_type=F32))
    gate1 = mod_ref[0, 2:3, :]
    shift2 = mod_ref[0, 3:4, :]
    scale2 = mod_ref[0, 4:5, :]
    x1 = x_ref[...] + gate1 * mixed
    x1_ref[...] = x1
    h2 = (_rms(x1) * nw_ref[...]) * (1.0 + scale2) + shift2
    hp_ref[...] = _pack_halves(h2)

    h_hi = h2.astype(BF16)
    h_lo = (h2 - h_hi.astype(F32)).astype(BF16)
    nt_dims = (((1,), (1,)), ((), ()))
    both = lax.dot_general(wr_ref[...], h_hi, nt_dims, preferred_element_type=F32)
    low = lax.dot_general(wr_ref[0:ne, :], h_lo, nt_dims, preferred_element_type=F32)
    logits = both[0:ne, :] + both[ne:, :] + low + br_ref[...]

    row = lax.broadcasted_iota(jnp.int32, logits.shape, 0)
    work = logits
    vals, idxs, hots = [], [], []
    for _ in range(TOP_K):
        mx = jnp.max(work, axis=0, keepdims=True)
        idx = jnp.min(jnp.where(work == mx, row, ne), axis=0, keepdims=True)
        hot = row == idx
        vals.append(mx)
        idxs.append(idx)
        hots.append(hot)
        work = jnp.where(hot, -jnp.inf, work)
    exps = [jnp.exp(v - vals[0]) for v in vals]
    denom = exps[0] + exps[1] + exps[2] + exps[3]

    sel = jnp.zeros(logits.shape, F32)
    for hot in hots:
        sel = sel + hot.astype(F32)
    prefix = jnp.dot(sel.astype(BF16), upper_ref[...], preferred_element_type=F32) + run_ref[...]
    ranks = [jnp.sum(jnp.where(hot, prefix, 0.0), axis=0, keepdims=True) for hot in hots]
    run_ref[...] = run_ref[...] + jnp.sum(sel, axis=1, keepdims=True)
    cnt_ref[...] = jnp.broadcast_to(run_ref[...], cnt_ref.shape).astype(jnp.int32)

    meta_ref[...] = jnp.concatenate(idxs + [r.astype(jnp.int32) for r in ranks], axis=0)
    gate_ref[...] = jnp.concatenate([e / denom for e in exps] + [jnp.zeros_like(denom)] * TOP_K, axis=0)


def _out_router(y_r, y_d, x2, mod, norm_w, w_out_bf16, w_router, b_router, seq):
    t, d = x2.shape
    tm = min(ROW_TILE, seq)
    per_batch = seq // tm
    ne = w_router.shape[1]
    w_hi = w_router.astype(BF16)
    w_lo = (w_router - w_hi.astype(F32)).astype(BF16)
    wr_t = jnp.concatenate([w_hi, w_lo], axis=1).T
    idx = jnp.arange(tm, dtype=jnp.int32)
    upper = (idx[:, None] < idx[None, :]).astype(BF16)
    row = lambda w: pl.BlockSpec((tm, w), lambda i: (i, 0))
    col = lambda h: pl.BlockSpec((h, tm), lambda i: (0, i))
    const = lambda a: pl.BlockSpec(a.shape, lambda i: (0,) * a.ndim)
    x1, hp, meta_t, gates_t, counts = pl.pallas_call(
        _out_kernel,
        grid=(t // tm,),
        in_specs=[row(RET_V_W), row(DIFF_V_W), row(d),
                  pl.BlockSpec((1, 6, d), lambda i: (i // per_batch, 0, 0)),
                  const(norm_w), const(w_out_bf16), const(wr_t), const(b_router), const(upper)],
        out_specs=[row(d), row(d // 2), col(2 * TOP_K), col(2 * TOP_K),
                   pl.BlockSpec((ne, LANES), lambda i: (0, 0))],
        out_shape=[jax.ShapeDtypeStruct((t, d), F32),
                   jax.ShapeDtypeStruct((t, d // 2), jnp.uint32),
                   jax.ShapeDtypeStruct((2 * TOP_K, t), jnp.int32),
                   jax.ShapeDtypeStruct((2 * TOP_K, t), F32),
                   jax.ShapeDtypeStruct((ne, LANES), jnp.int32)],
        scratch_shapes=[pltpu.VMEM((ne, 1), F32)],
        compiler_params=pltpu.CompilerParams(
            dimension_semantics=("arbitrary",), vmem_limit_bytes=VMEM_LIMIT),
        name="out_router",
    )(y_r, y_d, x2, mod, norm_w, w_out_bf16, wr_t, b_router, upper)
    return x1, hp, meta_t, gates_t.T, counts[:, 0]


def _pair_split_bias(b1):
    e, f2 = b1.shape
    nt = f2 // PERM_TILE
    g = b1[:, 0::2].reshape(e, nt, 1, PERM_TILE // 2)
    l = b1[:, 1::2].reshape(e, nt, 1, PERM_TILE // 2)
    return jnp.concatenate([g, l], axis=2).reshape(e, 1, f2)


def _slot_tokens(dest_t, n_rows):
    n_tok = dest_t.shape[1]
    assert n_tok & (n_tok - 1) == 0
    flat = dest_t.reshape(-1)
    n_slots = flat.shape[0]
    chunk = min(SC_SLOT_CHUNK, n_slots)
    mesh = plsc.VectorSubcoreMesh(core_axis_name="c", subcore_axis_name="s",
                                  num_cores=SC_CORES, num_subcores=SC_SUBCORES)

    def body(dest_hbm, out_hbm, out_v, dest_v):
        first = jnp.logical_and(lax.axis_index("c") == 0, lax.axis_index("s") == 0)

        @pl.when(first)
        def _():
            zeros = jnp.zeros((SC_LANES,), jnp.int32)

            @pl.loop(0, n_rows // SC_LANES)
            def _(g):
                out_v[pl.ds(g * SC_LANES, SC_LANES)] = zeros

            lane = lax.iota(jnp.int32, SC_LANES)

            @pl.loop(0, n_slots // chunk)
            def _(c):
                pltpu.sync_copy(dest_hbm.at[pl.ds(c * chunk, chunk)], dest_v)

                @pl.loop(0, chunk // SC_LANES)
                def _(g):
                    rows = dest_v[pl.ds(g * SC_LANES, SC_LANES)]
                    slot = lane + (c * chunk + g * SC_LANES)
                    plsc.store_scatter(out_v, [rows], slot & (n_tok - 1))

            pltpu.sync_copy(out_v, out_hbm)

    return pl.kernel(
        body,
        out_type=jax.ShapeDtypeStruct((n_rows,), jnp.int32),
        mesh=mesh,
        scratch_types=[pltpu.VMEM((n_rows,), jnp.int32), pltpu.VMEM((chunk,), jnp.int32)],
        compiler_params=pltpu.CompilerParams(needs_layout_passes=False),
        name="slot_tokens",
    )(flat)


def _expert_kernel(be_ref, nv_ref, nexte_ref, tcur_ref, tnxt_ref, hp_ref, w1_ref, b1_ref, w2_ref, b2_ref,
                   o_ref, hbuf, xb0, xb1, w1buf, w2buf, w2b, sem, wsems):
    b = pl.program_id(0)
    nv = nv_ref[0]
    bm = xb0.shape[0]

    def gather(tok_ref, dst):
        for r in range(bm):
            dst[pl.ds(r, 1), :] = hbuf[pl.ds(tok_ref[0, 0, r], 1), :]

    def weight_copies(e, slot):
        return (pltpu.make_async_copy(w1_ref.at[e], w1buf.at[slot], wsems.at[0, slot]),
                pltpu.make_async_copy(w2_ref.at[e], w2buf.at[slot], wsems.at[1, slot]))

    e = be_ref[b]
    prev_e = be_ref[jnp.maximum(b - 1, 0)]
    first_of_expert = jnp.logical_and(b < nv, jnp.logical_or(b == 0, e != prev_e))
    slot = lax.rem(nexte_ref[N_EXPERTS + e], 2)

    @pl.when(b == 0)
    def _():
        for cp in weight_copies(e, slot):
            cp.start()
        load = pltpu.make_async_copy(hp_ref, hbuf, sem)
        load.start()
        load.wait()
        gather(tcur_ref, xb0)

    @pl.when(first_of_expert)
    def _():
        for cp in weight_copies(e, slot):
            cp.wait()
        nxt = nexte_ref[e]

        @pl.when(nxt >= 0)
        def _():
            for cp in weight_copies(nxt, 1 - slot):
                cp.start()

        w2b[...] = w2buf[slot].astype(BF16)

    def run(cur, nxt_rows):
        x = _unpack_halves(cur[...])
        half = PERM_TILE // 2
        acts = []
        for j in range(w1buf.shape[2] // PERM_TILE):
            cols = slice(j * PERM_TILE, (j + 1) * PERM_TILE)
            h = jnp.dot(x, w1buf[slot, :, cols], preferred_element_type=F32) + b1_ref[0, :, cols]
            glu = jnp.minimum(h[:, :half], SWIGLU_LIMIT)
            lin = jnp.clip(h[:, half:], -SWIGLU_LIMIT, SWIGLU_LIMIT)
            acts.append((glu * jax.nn.sigmoid(SWIGLU_ALPHA * glu) * (lin + 1.0)).astype(BF16))
        act = jnp.concatenate(acts, axis=1)
        y = jnp.dot(act, w2b[...], preferred_element_type=F32) + b2_ref[0]
        _store_row_tiles(o_ref, y)
        gather(tnxt_ref, nxt_rows)

    even = lax.rem(b, 2) == 0

    @pl.when(jnp.logical_and(b < nv, even))
    def _():
        run(xb0, xb1)

    @pl.when(jnp.logical_and(b < nv, jnp.logical_not(even)))
    def _():
        run(xb1, xb0)

    @pl.when(b >= nv)
    def _():
        o_ref[...] = jnp.zeros_like(o_ref)


def _experts(block_e, n_valid, next_e, slot_tok, hp, w1p, b1p, w2, b2):
    d = D_MODEL
    f2 = w1p.shape[2]
    f = w2.shape[1]
    bm = EXPERT_ROWS
    nb = slot_tok.shape[0] // bm
    tok3 = slot_tok.reshape(nb, 1, bm)
    exp = lambda b, be, nv, ne: be[jnp.maximum(jnp.minimum(b, nv[0] - 1), 0)]
    grid_spec = pltpu.PrefetchScalarGridSpec(
        num_scalar_prefetch=3,
        grid=(nb,),
        in_specs=[pl.BlockSpec((1, 1, bm), lambda b, be, nv, ne: (b, 0, 0), memory_space=pltpu.SMEM),
                  pl.BlockSpec((1, 1, bm), lambda b, be, nv, ne: (jnp.minimum(b + 1, nb - 1), 0, 0),
                               memory_space=pltpu.SMEM),
                  pl.BlockSpec(memory_space=pl.ANY),
                  pl.BlockSpec(memory_space=pl.ANY),
                  pl.BlockSpec((1, 1, f2), lambda b, be, nv, ne: (exp(b, be, nv, ne), 0, 0)),
                  pl.BlockSpec(memory_space=pl.ANY),
                  pl.BlockSpec((1, 1, d), lambda b, be, nv, ne: (exp(b, be, nv, ne), 0, 0))],
        out_specs=pl.BlockSpec((bm * ROW_SUB, LANES), lambda b, be, nv, ne: (b, 0)),
        scratch_shapes=[pltpu.VMEM(hp.shape, hp.dtype), pltpu.VMEM((bm, hp.shape[1]), hp.dtype),
                        pltpu.VMEM((bm, hp.shape[1]), hp.dtype),
                        pltpu.VMEM((2,) + w1p.shape[1:], w1p.dtype), pltpu.VMEM((2,) + w2.shape[1:], w2.dtype),
                        pltpu.VMEM(w2.shape[1:], BF16),
                        pltpu.SemaphoreType.DMA(()), pltpu.SemaphoreType.DMA((2, 2))],
    )
    return pl.pallas_call(
        _expert_kernel,
        grid_spec=grid_spec,
        out_shape=jax.ShapeDtypeStruct((nb * bm * ROW_SUB, LANES), F32),
        compiler_params=pltpu.CompilerParams(
            dimension_semantics=("arbitrary",), vmem_limit_bytes=VMEM_LIMIT),
        name="experts",
    )(block_e, n_valid, next_e, tok3, tok3, hp, w1p, b1p, w2, b2)


def _combine_kernel(dcur_ref, dnxt_ref, gate_ref, x1_ref, mod_ref, nf_ref, y_ref, o_ref, buf, sems):
    nt = x1_ref.shape[0]
    i = pl.program_id(0)
    n = pl.num_programs(0)
    slot = lax.rem(i, 2)

    def row_copy(src_row, s, kk, r):
        return pltpu.make_async_copy(y_ref.at[pl.ds(src_row * ROW_SUB, ROW_SUB), :],
                                     buf.at[s, kk, pl.ds(r * ROW_SUB, ROW_SUB), :], sems.at[s])

    def fetch(d_ref, s):
        for j in range(nt):
            for kk in range(TOP_K):
                row_copy(d_ref[0, 0, kk * nt + j], s, kk, j).start(priority=kk % DMA_QUEUES)

    def wait_all(s):
        for kk in range(TOP_K):
            pltpu.make_async_copy(y_ref.at[pl.ds(0, nt * ROW_SUB), :], buf.at[s, kk], sems.at[s]).wait()

    @pl.when(i == 0)
    def _():
        fetch(dcur_ref, 0)

    fetch(dnxt_ref, 1 - slot)
    wait_all(slot)

    g = gate_ref[...]
    parts = []
    for s in range(ROW_SUB):
        acc = buf[slot, 0, pl.ds(s, nt, stride=ROW_SUB), :] * g[:, 0:1]
        for kk in range(1, TOP_K):
            acc = acc + buf[slot, kk, pl.ds(s, nt, stride=ROW_SUB), :] * g[:, kk:kk + 1]
        parts.append(acc)
    moe = jnp.concatenate(parts, axis=1)
    gate2 = mod_ref[0, 5:6, :]
    x2 = x1_ref[...] + gate2 * moe
    o_ref[...] = _rms(x2) * nf_ref[...]

    @pl.when(i == n - 1)
    def _():
        wait_all(1 - slot)


def _combine(dest_t, gates, x1, mod, normf_w, y, seq):
    t, d = x1.shape
    nt = min(COMBINE_TOKENS, seq)
    steps = t // nt
    per_batch = seq // nt
    dest2 = jnp.transpose(dest_t.reshape(TOP_K, steps, nt), (1, 0, 2)).reshape(steps, 1, nt * TOP_K)
    return pl.pallas_call(
        _combine_kernel,
        grid=(steps,),
        in_specs=[pl.BlockSpec((1, 1, nt * TOP_K), lambda i: (i, 0, 0), memory_space=pltpu.SMEM),
                  pl.BlockSpec((1, 1, nt * TOP_K), lambda i: (jnp.minimum(i + 1, steps - 1), 0, 0),
                               memory_space=pltpu.SMEM),
                  pl.BlockSpec((nt, 2 * TOP_K), lambda i: (i, 0)),
                  pl.BlockSpec((nt, d), lambda i: (i, 0)),
                  pl.BlockSpec((1, 6, d), lambda i: (i // per_batch, 0, 0)),
                  pl.BlockSpec((1, d), lambda i: (0, 0)),
                  pl.BlockSpec(memory_space=pl.ANY)],
        out_specs=pl.BlockSpec((nt, d), lambda i: (i, 0)),
        out_shape=jax.ShapeDtypeStruct((t, d), F32),
        scratch_shapes=[pltpu.VMEM((2, TOP_K, nt * ROW_SUB, LANES), F32),
                        pltpu.SemaphoreType.DMA((2,))],
        compiler_params=pltpu.CompilerParams(
            dimension_semantics=("arbitrary",), vmem_limit_bytes=VMEM_LIMIT),
        name="combine",
    )(dest2, dest2, gates, x1, mod, normf_w, y)


def kernel(x, c, w_ada, b_ada, norm1_w, w_in, lam_q1, lam_k1, lam_q2, lam_k2, subln_w, rel_bias,
           w_out, norm2_w, w_router, b_router, w1, b1, w2, b2, normf_w):
    batch, seq, d = x.shape
    t = batch * seq
    x2 = x.reshape(t, d)

    c_pad = jnp.zeros((8, d), F32).at[:batch].set(c)
    mod = _ada(c_pad, w_ada[0], b_ada[0][None, :])[:batch].reshape(batch, 6, d)

    q_r, k_r, v_r, g_r, q_d, k_d, vt_d = _in_proj(x2, mod, norm1_w[0][None, :],
                                                 w_in[0].astype(BF16), seq)
    y_r, w1p = _retention(q_r, k_r, v_r, g_r, w1[0], batch, seq)
    y_d = _diff_attention(q_d, k_d, vt_d, rel_bias, lam_q1, lam_k1, lam_q2, lam_k2,
                          subln_w, batch, seq)

    x1, hp, meta, gates, counts = _out_router(y_r, y_d, x2, mod, norm2_w[0][None, :],
                                              w_out[0].astype(BF16), w_router[0],
                                              b_router[0][:, None], seq)

    bm = EXPERT_ROWS
    padded = (counts + bm - 1) // bm * bm
    pad_end = jnp.cumsum(padded)
    pad_start = pad_end - padded
    hot_e = meta[None, :TOP_K, :] == jnp.arange(N_EXPERTS, dtype=jnp.int32)[:, None, None]
    dest_t = jnp.sum(jnp.where(hot_e, pad_start[:, None, None], 0), axis=0) + meta[TOP_K:, :]
    n_rows = (t * TOP_K // bm + N_EXPERTS) * bm
    nb = n_rows // bm
    block_start = jnp.arange(nb, dtype=jnp.int32) * bm
    block_e = jnp.minimum(jnp.sum((pad_end[None, :] <= block_start[:, None]).astype(jnp.int32), axis=1),
                          N_EXPERTS - 1)
    n_valid = (pad_end[-1:] // bm).astype(jnp.int32)
    slot_tok = _slot_tokens(dest_t, n_rows)

    owns = padded > 0
    ids = jnp.arange(N_EXPERTS, dtype=jnp.int32)
    later = jnp.logical_and(owns[None, :], ids[None, :] > ids[:, None])
    next_owner = jnp.min(jnp.where(later, ids[None, :], N_EXPERTS), axis=1)
    next_owner = jnp.where(next_owner == N_EXPERTS, -1, next_owner).astype(jnp.int32)
    ordinal = (jnp.cumsum(owns.astype(jnp.int32)) - 1).astype(jnp.int32)
    next_e = jnp.concatenate([next_owner, jnp.maximum(ordinal, 0)])

    ys = _experts(block_e, n_valid, next_e, slot_tok, hp, w1p, _pair_split_bias(b1[0]),
                  w2[0], b2[0][:, None, :])
    out = _combine(dest_t, gates, x1, mod, normf_w[None, :], ys, seq)
    return out.reshape(batch, seq, d)
```

```python
import math

import jax
import jax.numpy as jnp
from jax import lax
from jax.experimental import pallas as pl
from jax.experimental.pallas import tpu as pltpu
from jax.experimental.pallas import tpu_sc as plsc

F32 = jnp.float32
BF16 = jnp.bfloat16

D_MODEL = 1024
RET_HEADS = 4
RET_KEY_DIM = 64
RET_VAL_DIM = 128
RET_QK_W = RET_HEADS * RET_KEY_DIM
RET_V_W = RET_HEADS * RET_VAL_DIM
RET_CHUNK = 128
DIFF_HEADS = 4
DIFF_HEAD_DIM = 64
DIFF_QK_W = DIFF_HEADS * 2 * DIFF_HEAD_DIM
DIFF_V_W = DIFF_HEADS * 2 * DIFF_HEAD_DIM
IN_SIZES = (RET_QK_W, RET_QK_W, RET_V_W, RET_V_W, DIFF_QK_W, DIFF_QK_W, DIFF_V_W)
REL_BUCKETS = 32
REL_MAX_DIST = 128
N_EXPERTS = 32
TOP_K = 4
SWIGLU_ALPHA = 1.702
SWIGLU_LIMIT = 7.0
NORM_EPS = 1e-6
LAMBDA_INIT = 0.8 - 0.6 * math.exp(-0.3 * 0)

LANES = 128
ROW_SUB = D_MODEL // LANES
NEG_BIG = -1e30
LOG2_E = math.log2(math.e)
ONES_ROWS = 16
VMEM_LIMIT = 56 * 1024 * 1024

ROW_TILE = 512
RET_ROWS = 512
ATT_BLOCK = 512
ATT_STRIP = 256
ATT_HEADS = 4
EXPERT_ROWS = 256
PERM_TILE = 256
COMBINE_TOKENS = 256
DMA_QUEUES = 2
SC_CORES = 2
SC_SUBCORES = 16
SC_LANES = 16
SC_SLOT_CHUNK = 32768


def _rms(x):
    return x * lax.rsqrt(jnp.mean(x * x, axis=-1, keepdims=True) + NORM_EPS)


def _store_row_tiles(ref, x):
    rows = x.shape[0]
    for s in range(ROW_SUB):
        ref[pl.ds(s, rows, stride=ROW_SUB), :] = x[:, s * LANES:(s + 1) * LANES]


def _pack_halves(x):
    half = x.shape[1] // 2
    lo = lax.bitcast_convert_type(x[:, :half].astype(BF16).astype(F32), jnp.uint32)
    hi = lax.bitcast_convert_type(x[:, half:].astype(BF16).astype(F32), jnp.uint32)
    return (hi & jnp.uint32(0xFFFF0000)) | (lo >> 16)


def _unpack_halves(w):
    lo = lax.bitcast_convert_type(w << 16, F32).astype(BF16)
    hi = lax.bitcast_convert_type(w & jnp.uint32(0xFFFF0000), F32).astype(BF16)
    return jnp.concatenate([lo, hi], axis=1)


def _ada_kernel(c_ref, w_ref, b_ref, o_ref):
    c = c_ref[...]
    cond = c * jax.nn.sigmoid(c)
    o_ref[...] = jnp.dot(cond, w_ref[...], precision=lax.Precision.HIGHEST,
                         preferred_element_type=F32) + b_ref[...]


def _ada(c_pad, w_ada, b_ada):
    rows, d = c_pad.shape
    n = w_ada.shape[1]
    tn = 1024
    return pl.pallas_call(
        _ada_kernel,
        grid=(n // tn,),
        in_specs=[pl.BlockSpec((rows, d), lambda j: (0, 0)),
                  pl.BlockSpec((d, tn), lambda j: (0, j)),
                  pl.BlockSpec((1, tn), lambda j: (0, j))],
        out_specs=pl.BlockSpec((rows, tn), lambda j: (0, j)),
        out_shape=jax.ShapeDtypeStruct((rows, n), F32),
        name="ada",
    )(c_pad, w_ada, b_ada)


def _in_proj_kernel(x_ref, mod_ref, nw_ref, w_ref, w1_ref, perm_ref, *o_refs):
    o_refs, w1p_ref = o_refs[:-1], o_refs[-1]
    _split_w1_tiles(w1_ref, perm_ref, w1p_ref)
    x = x_ref[...]
    shift = mod_ref[0, 0:1, :]
    scale = mod_ref[0, 1:2, :]
    h = (_rms(x) * nw_ref[...]) * (1.0 + scale) + shift
    hb = h.astype(BF16)
    off = 0
    for o_ref, width in zip(o_refs[:-1], IN_SIZES[:-1]):
        o_ref[...] = jnp.dot(hb, w_ref[:, off:off + width],
                             preferred_element_type=F32).astype(o_ref.dtype)
        off += width
    vt_ref = o_refs[-1]
    v = jnp.dot(hb, w_ref[:, off:], preferred_element_type=F32)
    hw = 2 * DIFF_HEAD_DIM
    tail = (lax.broadcasted_iota(jnp.int32, (ONES_ROWS, v.shape[0]), 0) == 0).astype(vt_ref.dtype)
    for hd in range(DIFF_HEADS):
        vt_ref[0, hd, 0, 0:hw, :] = v[:, hd * hw:(hd + 1) * hw].T.astype(vt_ref.dtype)
        vt_ref[0, hd, 0, hw:, :] = tail


def _in_proj(x2, mod, norm_w, w_in_bf16, w1, seq):
    t, d = x2.shape
    tm = min(ROW_TILE, seq)
    assert tm == min(ATT_BLOCK, seq)
    per_batch = seq // tm
    in_w = w_in_bf16.shape[1]
    hw = 2 * DIFF_HEAD_DIM
    vt_shape = (t // seq, DIFF_HEADS, per_batch, hw + ONES_ROWS, tm)
    perm = _pair_split_matrix()
    per_step, rest = divmod(w1.shape[0], t // tm)
    assert rest == 0 and per_step >= 1
    expert = pl.BlockSpec((per_step,) + w1.shape[1:], lambda i: (i, 0, 0))
    return pl.pallas_call(
        _in_proj_kernel,
        grid=(t // tm,),
        in_specs=[pl.BlockSpec((tm, d), lambda i: (i, 0)),
                  pl.BlockSpec((1, 6, d), lambda i: (i // per_batch, 0, 0)),
                  pl.BlockSpec((1, d), lambda i: (0, 0)),
                  pl.BlockSpec((d, in_w), lambda i: (0, 0)),
                  expert, pl.BlockSpec(perm.shape, lambda i: (0, 0))],
        out_specs=([pl.BlockSpec((tm, w), lambda i: (i, 0)) for w in IN_SIZES[:-1]]
                   + [pl.BlockSpec((1,) + vt_shape[1:2] + (1,) + vt_shape[3:],
                                   lambda i: (i // per_batch, 0, i % per_batch, 0, 0)), expert]),
        out_shape=([jax.ShapeDtypeStruct((t, w), BF16) for w in IN_SIZES[:-1]]
                   + [jax.ShapeDtypeStruct(vt_shape, BF16), jax.ShapeDtypeStruct(w1.shape, BF16)]),
        compiler_params=pltpu.CompilerParams(vmem_limit_bytes=VMEM_LIMIT),
        name="in_proj",
    )(x2, mod, norm_w, w_in_bf16, w1, perm)


def _split_w1_tiles(w_ref, p_ref, o_ref):
    for e in range(w_ref.shape[0]):
        for s in range(w_ref.shape[2] // PERM_TILE):
            cols = slice(s * PERM_TILE, (s + 1) * PERM_TILE)
            o_ref[e, :, cols] = jnp.dot(w_ref[e, :, cols].astype(BF16), p_ref[...],
                                        preferred_element_type=F32).astype(BF16)


def _pair_split_matrix():
    i = jnp.arange(PERM_TILE)[:, None]
    j = jnp.arange(PERM_TILE)[None, :]
    half = PERM_TILE // 2
    src = jnp.where(j < half, 2 * j, 2 * (j - half) + 1)
    return (i == src).astype(BF16)


def _rotary(x, cos, sin_even, sin_odd):
    nxt = pltpu.roll(x, LANES - 1, 1)
    prv = pltpu.roll(x, 1, 1)
    return x * cos + nxt * sin_even + prv * sin_odd


def _ret_kernel(q_ref, k_ref, v_ref, g_ref, cos_ref, sine_ref, sino_ref,
                hmask_ref, xi_ref, zeta_ref, dmask_ref, gch_ref, o_ref, state_ref):
    @pl.when(pl.program_id(1) == 0)
    def _():
        state_ref[...] = jnp.zeros_like(state_ref)

    n_sub = q_ref.shape[0] // RET_CHUNK
    for c in range(n_sub):
        rows = slice(c * RET_CHUNK, (c + 1) * RET_CHUNK)
        for pair in range(RET_HEADS // 2):
            lanes = slice(pair * LANES, (pair + 1) * LANES)
            cos = cos_ref[rows, :]
            sine = sine_ref[rows, :]
            sino = sino_ref[rows, :]
            qr = _rotary(q_ref[rows, lanes].astype(F32), cos, sine, sino)
            kr = _rotary(k_ref[rows, lanes].astype(F32), cos, sine, sino) * (RET_KEY_DIM ** -0.5)
            qb = qr.astype(BF16)
            for hh in range(2):
                h = 2 * pair + hh
                vcols = slice(h * RET_VAL_DIM, (h + 1) * RET_VAL_DIM)
                v = v_ref[rows, vcols]
                km = (kr * hmask_ref[h]).astype(BF16)
                scores = lax.dot_general(qb, km, (((1,), (1,)), ((), ())),
                                         preferred_element_type=F32) * dmask_ref[h]
                inner = jnp.dot(scores.astype(BF16), v, preferred_element_type=F32)
                qx = (qr * xi_ref[h]).astype(BF16)
                state = state_ref[h]
                cross = jnp.dot(qx, state.astype(BF16), preferred_element_type=F32)
                kz = (kr * zeta_ref[h]).astype(BF16)
                kv = lax.dot_general(kz, v, (((0,), (0,)), ((), ())),
                                     preferred_element_type=F32)
                state_ref[h] = state * gch_ref[h] + kv
                y = _rms(inner + cross)
                g = g_ref[rows, vcols].astype(F32)
                o_ref[rows, vcols] = (g * jax.nn.sigmoid(g) * y).astype(o_ref.dtype)


def _retention_tables(seq):
    dk, c, nh = RET_KEY_DIM, RET_CHUNK, RET_HEADS
    pos = jnp.arange(seq, dtype=F32)
    inv_freq = 1.0 / (10000.0 ** jnp.linspace(0.0, 1.0, dk // 2, dtype=F32))
    ang = pos[:, None] * jnp.repeat(inv_freq, 2)[None, :]
    sin = jnp.tile(jnp.sin(ang), (1, LANES // dk))
    cos = jnp.tile(jnp.cos(ang), (1, LANES // dk))
    even = (jnp.arange(LANES) % 2 == 0)[None, :]
    sin_even = jnp.where(even, -sin, 0.0)
    sin_odd = jnp.where(even, 0.0, sin)
    log_g = jnp.log(1.0 - 2.0 ** (-5.0 - jnp.arange(nh, dtype=F32)))
    i = jnp.arange(c, dtype=F32)
    rel = i[:, None] - i[None, :]
    dmask = jnp.where(rel[None] >= 0,
                      jnp.exp(jnp.maximum(rel, 0.0)[None] * log_g[:, None, None]), 0.0)
    zeta = jnp.exp((c - 1.0 - i)[None, :] * log_g[:, None])
    xi = jnp.exp((i + 1.0)[None, :] * log_g[:, None])
    g_chunk = jnp.exp(c * log_g)
    lane = jnp.arange(LANES)
    hmask = jnp.stack([((lane // dk) == (h % 2)).astype(F32) for h in range(nh)])[:, None, :]
    xi_t = xi[:, :, None] * hmask
    zeta_t = zeta[:, :, None] * hmask
    gch = jnp.broadcast_to(g_chunk[:, None, None], (nh, 1, LANES))
    return cos, sin_even, sin_odd, hmask, xi_t, zeta_t, dmask, gch


def _retention(q, k, v, g, batch, seq):
    t = q.shape[0]
    rb = min(RET_ROWS, seq)
    per_batch = seq // rb
    cos, sin_even, sin_odd, hmask, xi_t, zeta_t, dmask, gch = _retention_tables(seq)
    row = lambda w: pl.BlockSpec((rb, w), lambda b, j: (b * per_batch + j, 0))
    tab = lambda w: pl.BlockSpec((rb, w), lambda b, j: (j, 0))
    full = lambda a: pl.BlockSpec(a.shape, lambda b, j: (0,) * a.ndim)
    return pl.pallas_call(
        _ret_kernel,
        grid=(batch, per_batch),
        in_specs=[row(RET_QK_W), row(RET_QK_W), row(RET_V_W), row(RET_V_W),
                  tab(LANES), tab(LANES), tab(LANES),
                  full(hmask), full(xi_t), full(zeta_t), full(dmask), full(gch)],
        out_specs=row(RET_V_W),
        out_shape=jax.ShapeDtypeStruct((t, RET_V_W), BF16),
        scratch_shapes=[pltpu.VMEM((RET_HEADS, LANES, RET_VAL_DIM), F32)],
        compiler_params=pltpu.CompilerParams(
            dimension_semantics=("arbitrary", "arbitrary"), vmem_limit_bytes=VMEM_LIMIT),
        name="retention",
    )(q, k, v, g, cos, sin_even, sin_odd, hmask, xi_t, zeta_t, dmask, gch)


def _t5_bucket(rel):
    n = jnp.maximum(rel, 0)
    max_exact = REL_BUCKETS // 2
    nf = jnp.maximum(n, 1).astype(F32)
    large = max_exact + (jnp.log(nf / max_exact) / math.log(REL_MAX_DIST / max_exact)
                         * (REL_BUCKETS - max_exact)).astype(jnp.int32)
    large = jnp.minimum(large, REL_BUCKETS - 1)
    return jnp.where(n < max_exact, n, large)


def _bias_tiles(rel_bias, blk):
    r = jnp.arange(blk, dtype=jnp.int32)
    far = rel_bias[REL_BUCKETS - 1]
    rel0 = r[None, :] - r[:, None]
    rel1 = rel0 + blk
    buckets = jnp.arange(REL_BUCKETS, dtype=jnp.int32)

    def tile(rel):
        hot = (_t5_bucket(rel)[:, :, None] == buckets).astype(F32)
        return jnp.einsum('krb,bh->hkr', hot, rel_bias, precision=lax.Precision.HIGHEST)

    b0 = jnp.where(rel0[None] >= 0, (tile(rel0) - far[:, None, None]) * LOG2_E, NEG_BIG)
    b1 = (tile(rel1) - far[:, None, None]) * LOG2_E
    return b0, b1


def _attn_kernel(q_ref, k_ref, vt_ref, b0_ref, b1_ref, lq1_ref, lk1_ref, lq2_ref, lk2_ref,
                 sw_ref, o_ref, m_ref, acc_ref):
    blk = q_ref.shape[0]
    hw = 2 * DIFF_HEAD_DIM
    i = pl.program_id(2)
    lane = lax.broadcasted_iota(jnp.int32, (1, LANES), 1)
    qm = []
    for hd in range(ATT_HEADS):
        q = (q_ref[:, hd * hw:(hd + 1) * hw].astype(F32) * (DIFF_HEAD_DIM ** -0.5 * LOG2_E)).astype(BF16)
        zero = jnp.zeros_like(q)
        qm.append((jnp.where(lane < DIFF_HEAD_DIM, q, zero), jnp.where(lane >= DIFF_HEAD_DIM, q, zero)))

    m_ref[...] = jnp.full_like(m_ref, NEG_BIG)
    acc_ref[...] = jnp.zeros_like(acc_ref)

    def step(blocks):
        kbs = [k_ref[pl.ds(pl.multiple_of(j * blk, blk), blk), :] for j, _, _ in blocks]
        chains = [(hd, mi, qs) for hd in range(ATT_HEADS) for mi in range(2)
                  for qs in range(blk // ATT_STRIP)]
        scores = []
        for hd, mi, qs in chains:
            qc = slice(qs * ATT_STRIP, (qs + 1) * ATT_STRIP)
            row = []
            for (_, bias, diagonal), kb in zip(blocks, kbs):
                keys = (qs + 1) * ATT_STRIP if diagonal else blk
                s = lax.dot_general(kb[:keys, hd * hw:(hd + 1) * hw], qm[hd][mi][qc, :],
                                    (((1,), (1,)), ((), ())), preferred_element_type=F32)
                row.append(s if bias is None else s + bias[hd, :keys, qc])
            scores.append(row)
        stats = []
        for (hd, mi, qs), row in zip(chains, scores):
            qc = slice(qs * ATT_STRIP, (qs + 1) * ATT_STRIP)
            m_old = m_ref[hd, mi, :, qc]
            m_new = m_old
            for s in row:
                m_new = jnp.maximum(m_new, jnp.max(s, axis=0, keepdims=True))
            stats.append((jnp.exp2(m_old - m_new), [jnp.exp2(s - m_new).astype(BF16) for s in row], m_new))
        for (hd, mi, qs), (alpha, ps, m_new) in zip(chains, stats):
            qc = slice(qs * ATT_STRIP, (qs + 1) * ATT_STRIP)
            pv = None
            for (j, _, _), p in zip(blocks, ps):
                part = jnp.dot(vt_ref[0, hd, j, :, 0:p.shape[0]], p, preferred_element_type=F32)
                pv = part if pv is None else pv + part
            acc_ref[hd, mi, :, qc] = alpha * acc_ref[hd, mi, :, qc] + pv
            m_ref[hd, mi, :, qc] = m_new

    n_far = jnp.maximum(i - 1, 0)

    def far_pair(pair, carry):
        step([(2 * pair, None, False), (2 * pair + 1, None, False)])
        return carry

    lax.fori_loop(0, lax.shift_right_logical(n_far, 1), far_pair, 0)

    @pl.when(lax.rem(n_far, 2) == 1)
    def _():
        step([(n_far - 1, None, False)])

    @pl.when(i >= 1)
    def _():
        step([(i - 1, b1_ref, False), (i, b0_ref, True)])

    @pl.when(i == 0)
    def _():
        step([(i, b0_ref, True)])

    lam = (jnp.exp(jnp.sum(lq1_ref[...] * lk1_ref[...], axis=-1, keepdims=True))
           - jnp.exp(jnp.sum(lq2_ref[...] * lk2_ref[...], axis=-1, keepdims=True))
           + LAMBDA_INIT)
    for hd in range(ATT_HEADS):
        a = (acc_ref[hd, 0, :hw, :] / acc_ref[hd, 0, hw:hw + 1, :]
             - lam * (acc_ref[hd, 1, :hw, :] / acc_ref[hd, 1, hw:hw + 1, :]))
        a = a * lax.rsqrt(jnp.mean(a * a, axis=0, keepdims=True) + NORM_EPS)
        o_ref[:, hd * hw:(hd + 1) * hw] = (a.T * sw_ref[...] * (1.0 - LAMBDA_INIT)).astype(o_ref.dtype)


def _diff_attention(q, k, vt, rel_bias, lq1, lk1, lq2, lk2, subln_w, batch, seq):
    t = q.shape[0]
    blk = min(ATT_BLOCK, seq)
    nq = seq // blk
    hw = 2 * DIFF_HEAD_DIM
    nh = ATT_HEADS
    b0, b1 = _bias_tiles(rel_bias, blk)
    small = lambda a: pl.BlockSpec(a.shape, lambda b, h, i: (0,) * a.ndim)
    return pl.pallas_call(
        _attn_kernel,
        grid=(batch, DIFF_HEADS // nh, nq),
        in_specs=[pl.BlockSpec((blk, nh * hw), lambda b, h, i: (b * nq + i, h)),
                  pl.BlockSpec((seq, nh * hw), lambda b, h, i: (b, h)),
                  pl.BlockSpec((1, nh, nq, hw + ONES_ROWS, blk), lambda b, h, i: (b, h, 0, 0, 0)),
                  pl.BlockSpec((nh, blk, blk), lambda b, h, i: (h, 0, 0)),
                  pl.BlockSpec((nh, blk, blk), lambda b, h, i: (h, 0, 0)),
                  small(lq1), small(lk1), small(lq2), small(lk2), small(subln_w)],
        out_specs=pl.BlockSpec((blk, nh * hw), lambda b, h, i: (b * nq + i, h)),
        out_shape=jax.ShapeDtypeStruct((t, DIFF_V_W), BF16),
        scratch_shapes=[pltpu.VMEM((nh, 2, 1, blk), F32), pltpu.VMEM((nh, 2, hw + ONES_ROWS, blk), F32)],
        compiler_params=pltpu.CompilerParams(
            dimension_semantics=("arbitrary", "arbitrary", "arbitrary"),
            vmem_limit_bytes=VMEM_LIMIT),
        name="diff_attn",
    )(q, k, vt, b0, b1, lq1, lk1, lq2, lk2, subln_w)


def _out_kernel(yr_ref, yd_ref, x_ref, mod_ref, nw_ref, wo_ref, wr_ref, br_ref, upper_ref,
                x1_ref, hp_ref, meta_ref, gate_ref, cnt_ref, run_ref):
    tm = x_ref.shape[0]
    ne = run_ref.shape[0]

    @pl.when(pl.program_id(0) == 0)
    def _():
        run_ref[...] = jnp.zeros_like(run_ref)

    mixed = (jnp.dot(yr_ref[...], wo_ref[0:RET_V_W, :], preferred_element_type=F32)
             + jnp.dot(yd_ref[...], wo_ref[RET_V_W:, :], preferred_element_type=F32))
    gate1 = mod_ref[0, 2:3, :]
    shift2 = mod_ref[0, 3:4, :]
    scale2 = mod_ref[0, 4:5, :]
    x1 = x_ref[...] + gate1 * mixed
    x1_ref[...] = x1
    h2 = (_rms(x1) * nw_ref[...]) * (1.0 + scale2) + shift2
    hp_ref[...] = _pack_halves(h2)

    h_hi = h2.astype(BF16)
    h_lo = (h2 - h_hi.astype(F32)).astype(BF16)
    nt_dims = (((1,), (1,)), ((), ()))
    both = lax.dot_general(wr_ref[...], h_hi, nt_dims, preferred_element_type=F32)
    low = lax.dot_general(wr_ref[0:ne, :], h_lo, nt_dims, preferred_element_type=F32)
    logits = both[0:ne, :] + both[ne:, :] + low + br_ref[...]

    row = lax.broadcasted_iota(jnp.int32, logits.shape, 0)
    work = logits
    vals, idxs, hots = [], [], []
    for _ in range(TOP_K):
        mx = jnp.max(work, axis=0, keepdims=True)
        idx = jnp.min(jnp.where(work == mx, row, ne), axis=0, keepdims=True)
        hot = row == idx
        vals.append(mx)
        idxs.append(idx)
        hots.append(hot)
        work = jnp.where(hot, -jnp.inf, work)
    exps = [jnp.exp(v - vals[0]) for v in vals]
    denom = exps[0] + exps[1] + exps[2] + exps[3]

    sel = jnp.zeros(logits.shape, F32)
    for hot in hots:
        sel = sel + hot.astype(F32)
    prefix = jnp.dot(sel.astype(BF16), upper_ref[...], preferred_element_type=F32) + run_ref[...]
    ranks = [jnp.sum(jnp.where(hot, prefix, 0.0), axis=0, keepdims=True) for hot in hots]
    run_ref[...] = run_ref[...] + jnp.sum(sel, axis=1, keepdims=True)
    cnt_ref[...] = jnp.broadcast_to(run_ref[...], cnt_ref.shape).astype(jnp.int32)

    meta_ref[...] = jnp.concatenate(idxs + [r.astype(jnp.int32) for r in ranks], axis=0)
    gate_ref[...] = jnp.concatenate([e / denom for e in exps] + [jnp.zeros_like(denom)] * TOP_K, axis=0)


def _out_router(y_r, y_d, x2, mod, norm_w, w_out_bf16, w_router, b_router, seq):
    t, d = x2.shape
    tm = min(ROW_TILE, seq)
    per_batch = seq // tm
    ne = w_router.shape[1]
    w_hi = w_router.astype(BF16)
    w_lo = (w_router - w_hi.astype(F32)).astype(BF16)
    wr_t = jnp.concatenate([w_hi, w_lo], axis=1).T
    idx = jnp.arange(tm, dtype=jnp.int32)
    upper = (idx[:, None] < idx[None, :]).astype(BF16)
    row = lambda w: pl.BlockSpec((tm, w), lambda i: (i, 0))
    col = lambda h: pl.BlockSpec((h, tm), lambda i: (0, i))
    const = lambda a: pl.BlockSpec(a.shape, lambda i: (0,) * a.ndim)
    x1, hp, meta_t, gates_t, counts = pl.pallas_call(
        _out_kernel,
        grid=(t // tm,),
        in_specs=[row(RET_V_W), row(DIFF_V_W), row(d),
                  pl.BlockSpec((1, 6, d), lambda i: (i // per_batch, 0, 0)),
                  const(norm_w), const(w_out_bf16), const(wr_t), const(b_router), const(upper)],
        out_specs=[row(d), row(d // 2), col(2 * TOP_K), col(2 * TOP_K),
                   pl.BlockSpec((ne, LANES), lambda i: (0, 0))],
        out_shape=[jax.ShapeDtypeStruct((t, d), F32),
                   jax.ShapeDtypeStruct((t, d // 2), jnp.uint32),
                   jax.ShapeDtypeStruct((2 * TOP_K, t), jnp.int32),
                   jax.ShapeDtypeStruct((2 * TOP_K, t), F32),
                   jax.ShapeDtypeStruct((ne, LANES), jnp.int32)],
        scratch_shapes=[pltpu.VMEM((ne, 1), F32)],
        compiler_params=pltpu.CompilerParams(
            dimension_semantics=("arbitrary",), vmem_limit_bytes=VMEM_LIMIT),
        name="out_router",
    )(y_r, y_d, x2, mod, norm_w, w_out_bf16, wr_t, b_router, upper)
    return x1, hp, meta_t, gates_t.T, counts[:, 0]


def _pair_split_bias(b1):
    e, f2 = b1.shape
    nt = f2 // PERM_TILE
    g = b1[:, 0::2].reshape(e, nt, 1, PERM_TILE // 2)
    l = b1[:, 1::2].reshape(e, nt, 1, PERM_TILE // 2)
    return jnp.concatenate([g, l], axis=2).reshape(e, 1, f2)


def _slot_tokens(dest_t, n_rows):
    n_tok = dest_t.shape[1]
    assert n_tok & (n_tok - 1) == 0
    flat = dest_t.reshape(-1)
    n_slots = flat.shape[0]
    chunk = min(SC_SLOT_CHUNK, n_slots)
    mesh = plsc.VectorSubcoreMesh(core_axis_name="c", subcore_axis_name="s",
                                  num_cores=SC_CORES, num_subcores=SC_SUBCORES)

    def body(dest_hbm, out_hbm, out_v, dest_v):
        first = jnp.logical_and(lax.axis_index("c") == 0, lax.axis_index("s") == 0)

        @pl.when(first)
        def _():
            zeros = jnp.zeros((SC_LANES,), jnp.int32)

            @pl.loop(0, n_rows // SC_LANES)
            def _(g):
                out_v[pl.ds(g * SC_LANES, SC_LANES)] = zeros

            lane = lax.iota(jnp.int32, SC_LANES)

            @pl.loop(0, n_slots // chunk)
            def _(c):
                pltpu.sync_copy(dest_hbm.at[pl.ds(c * chunk, chunk)], dest_v)

                @pl.loop(0, chunk // SC_LANES)
                def _(g):
                    rows = dest_v[pl.ds(g * SC_LANES, SC_LANES)]
                    slot = lane + (c * chunk + g * SC_LANES)
                    plsc.store_scatter(out_v, [rows], slot & (n_tok - 1))

            pltpu.sync_copy(out_v, out_hbm)

    return pl.kernel(
        body,
        out_type=jax.ShapeDtypeStruct((n_rows,), jnp.int32),
        mesh=mesh,
        scratch_types=[pltpu.VMEM((n_rows,), jnp.int32), pltpu.VMEM((chunk,), jnp.int32)],
        compiler_params=pltpu.CompilerParams(needs_layout_passes=False),
        name="slot_tokens",
    )(flat)


def _expert_kernel(be_ref, nv_ref, nexte_ref, tcur_ref, tnxt_ref, hp_ref, w1_ref, b1_ref, w2_ref, b2_ref,
                   o_ref, hbuf, xb0, xb1, w1buf, w2buf, w2b, sem, wsems):
    b = pl.program_id(0)
    nv = nv_ref[0]
    bm = xb0.shape[0]

    def gather(tok_ref, dst):
        for r in range(bm):
            dst[pl.ds(r, 1), :] = hbuf[pl.ds(tok_ref[0, 0, r], 1), :]

    def weight_copies(e, slot):
        return (pltpu.make_async_copy(w1_ref.at[e], w1buf.at[slot], wsems.at[0, slot]),
                pltpu.make_async_copy(w2_ref.at[e], w2buf.at[slot], wsems.at[1, slot]))

    e = be_ref[b]
    prev_e = be_ref[jnp.maximum(b - 1, 0)]
    first_of_expert = jnp.logical_and(b < nv, jnp.logical_or(b == 0, e != prev_e))
    slot = lax.rem(nexte_ref[N_EXPERTS + e], 2)

    @pl.when(b == 0)
    def _():
        for cp in weight_copies(e, slot):
            cp.start()
        load = pltpu.make_async_copy(hp_ref, hbuf, sem)
        load.start()
        load.wait()
        gather(tcur_ref, xb0)

    @pl.when(first_of_expert)
    def _():
        for cp in weight_copies(e, slot):
            cp.wait()
        nxt = nexte_ref[e]

        @pl.when(nxt >= 0)
        def _():
            for cp in weight_copies(nxt, 1 - slot):
                cp.start()

        w2b[...] = w2buf[slot].astype(BF16)

    def run(cur, nxt_rows):
        x = _unpack_halves(cur[...])
        half = PERM_TILE // 2
        acts = []
        for j in range(w1buf.shape[2] // PERM_TILE):
            cols = slice(j * PERM_TILE, (j + 1) * PERM_TILE)
            h = jnp.dot(x, w1buf[slot, :, cols], preferred_element_type=F32) + b1_ref[0, :, cols]
            glu = jnp.minimum(h[:, :half], SWIGLU_LIMIT)
            lin = jnp.clip(h[:, half:], -SWIGLU_LIMIT, SWIGLU_LIMIT)
            acts.append((glu * jax.nn.sigmoid(SWIGLU_ALPHA * glu) * (lin + 1.0)).astype(BF16))
        act = jnp.concatenate(acts, axis=1)
        y = jnp.dot(act, w2b[...], preferred_element_type=F32) + b2_ref[0]
        _store_row_tiles(o_ref, y)
        gather(tnxt_ref, nxt_rows)

    even = lax.rem(b, 2) == 0

    @pl.when(jnp.logical_and(b < nv, even))
    def _():
        run(xb0, xb1)

    @pl.when(jnp.logical_and(b < nv, jnp.logical_not(even)))
    def _():
        run(xb1, xb0)

    @pl.when(b >= nv)
    def _():
        o_ref[...] = jnp.zeros_like(o_ref)


def _experts(block_e, n_valid, next_e, slot_tok, hp, w1p, b1p, w2, b2):
    d = D_MODEL
    f2 = w1p.shape[2]
    f = w2.shape[1]
    bm = EXPERT_ROWS
    nb = slot_tok.shape[0] // bm
    tok3 = slot_tok.reshape(nb, 1, bm)
    exp = lambda b, be, nv, ne: be[jnp.maximum(jnp.minimum(b, nv[0] - 1), 0)]
    grid_spec = pltpu.PrefetchScalarGridSpec(
        num_scalar_prefetch=3,
        grid=(nb,),
        in_specs=[pl.BlockSpec((1, 1, bm), lambda b, be, nv, ne: (b, 0, 0), memory_space=pltpu.SMEM),
                  pl.BlockSpec((1, 1, bm), lambda b, be, nv, ne: (jnp.minimum(b + 1, nb - 1), 0, 0),
                               memory_space=pltpu.SMEM),
                  pl.BlockSpec(memory_space=pl.ANY),
                  pl.BlockSpec(memory_space=pl.ANY),
                  pl.BlockSpec((1, 1, f2), lambda b, be, nv, ne: (exp(b, be, nv, ne), 0, 0)),
                  pl.BlockSpec(memory_space=pl.ANY),
                  pl.BlockSpec((1, 1, d), lambda b, be, nv, ne: (exp(b, be, nv, ne), 0, 0))],
        out_specs=pl.BlockSpec((bm * ROW_SUB, LANES), lambda b, be, nv, ne: (b, 0)),
        scratch_shapes=[pltpu.VMEM(hp.shape, hp.dtype), pltpu.VMEM((bm, hp.shape[1]), hp.dtype),
                        pltpu.VMEM((bm, hp.shape[1]), hp.dtype),
                        pltpu.VMEM((2,) + w1p.shape[1:], w1p.dtype), pltpu.VMEM((2,) + w2.shape[1:], w2.dtype),
                        pltpu.VMEM(w2.shape[1:], BF16),
                        pltpu.SemaphoreType.DMA(()), pltpu.SemaphoreType.DMA((2, 2))],
    )
    return pl.pallas_call(
        _expert_kernel,
        grid_spec=grid_spec,
        out_shape=jax.ShapeDtypeStruct((nb * bm * ROW_SUB, LANES), F32),
        compiler_params=pltpu.CompilerParams(
            dimension_semantics=("arbitrary",), vmem_limit_bytes=VMEM_LIMIT),
        name="experts",
    )(block_e, n_valid, next_e, tok3, tok3, hp, w1p, b1p, w2, b2)


def _combine_kernel(dcur_ref, dnxt_ref, gate_ref, x1_ref, mod_ref, nf_ref, y_ref, o_ref, buf, sems):
    nt = x1_ref.shape[0]
    i = pl.program_id(0)
    n = pl.num_programs(0)
    slot = lax.rem(i, 2)

    def row_copy(src_row, s, kk, r):
        return pltpu.make_async_copy(y_ref.at[pl.ds(src_row * ROW_SUB, ROW_SUB), :],
                                     buf.at[s, kk, pl.ds(r * ROW_SUB, ROW_SUB), :], sems.at[s])

    def fetch(d_ref, s):
        for j in range(nt):
            for kk in range(TOP_K):
                row_copy(d_ref[0, 0, kk * nt + j], s, kk, j).start(priority=kk % DMA_QUEUES)

    def wait_all(s):
        for kk in range(TOP_K):
            pltpu.make_async_copy(y_ref.at[pl.ds(0, nt * ROW_SUB), :], buf.at[s, kk], sems.at[s]).wait()

    @pl.when(i == 0)
    def _():
        fetch(dcur_ref, 0)

    fetch(dnxt_ref, 1 - slot)
    wait_all(slot)

    g = gate_ref[...]
    parts = []
    for s in range(ROW_SUB):
        acc = buf[slot, 0, pl.ds(s, nt, stride=ROW_SUB), :] * g[:, 0:1]
        for kk in range(1, TOP_K):
            acc = acc + buf[slot, kk, pl.ds(s, nt, stride=ROW_SUB), :] * g[:, kk:kk + 1]
        parts.append(acc)
    moe = jnp.concatenate(parts, axis=1)
    gate2 = mod_ref[0, 5:6, :]
    x2 = x1_ref[...] + gate2 * moe
    o_ref[...] = _rms(x2) * nf_ref[...]

    @pl.when(i == n - 1)
    def _():
        wait_all(1 - slot)


def _combine(dest_t, gates, x1, mod, normf_w, y, seq):
    t, d = x1.shape
    nt = min(COMBINE_TOKENS, seq)
    steps = t // nt
    per_batch = seq // nt
    dest2 = jnp.transpose(dest_t.reshape(TOP_K, steps, nt), (1, 0, 2)).reshape(steps, 1, nt * TOP_K)
    return pl.pallas_call(
        _combine_kernel,
        grid=(steps,),
        in_specs=[pl.BlockSpec((1, 1, nt * TOP_K), lambda i: (i, 0, 0), memory_space=pltpu.SMEM),
                  pl.BlockSpec((1, 1, nt * TOP_K), lambda i: (jnp.minimum(i + 1, steps - 1), 0, 0),
                               memory_space=pltpu.SMEM),
                  pl.BlockSpec((nt, 2 * TOP_K), lambda i: (i, 0)),
                  pl.BlockSpec((nt, d), lambda i: (i, 0)),
                  pl.BlockSpec((1, 6, d), lambda i: (i // per_batch, 0, 0)),
                  pl.BlockSpec((1, d), lambda i: (0, 0)),
                  pl.BlockSpec(memory_space=pl.ANY)],
        out_specs=pl.BlockSpec((nt, d), lambda i: (i, 0)),
        out_shape=jax.ShapeDtypeStruct((t, d), F32),
        scratch_shapes=[pltpu.VMEM((2, TOP_K, nt * ROW_SUB, LANES), F32),
                        pltpu.SemaphoreType.DMA((2,))],
        compiler_params=pltpu.CompilerParams(
            dimension_semantics=("arbitrary",), vmem_limit_bytes=VMEM_LIMIT),
        name="combine",
    )(dest2, dest2, gates, x1, mod, normf_w, y)


def kernel(x, c, w_ada, b_ada, norm1_w, w_in, lam_q1, lam_k1, lam_q2, lam_k2, subln_w, rel_bias,
           w_out, norm2_w, w_router, b_router, w1, b1, w2, b2, normf_w):
    batch, seq, d = x.shape
    t = batch * seq
    x2 = x.reshape(t, d)

    c_pad = jnp.zeros((8, d), F32).at[:batch].set(c)
    mod = _ada(c_pad, w_ada[0], b_ada[0][None, :])[:batch].reshape(batch, 6, d)

    q_r, k_r, v_r, g_r, q_d, k_d, vt_d, w1p = _in_proj(x2, mod, norm1_w[0][None, :],
                                                      w_in[0].astype(BF16), w1[0], seq)
    y_r = _retention(q_r, k_r, v_r, g_r, batch, seq)
    y_d = _diff_attention(q_d, k_d, vt_d, rel_bias, lam_q1, lam_k1, lam_q2, lam_k2,
                          subln_w, batch, seq)

    x1, hp, meta, gates, counts = _out_router(y_r, y_d, x2, mod, norm2_w[0][None, :],
                                              w_out[0].astype(BF16), w_router[0],
                                              b_router[0][:, None], seq)

    bm = EXPERT_ROWS
    padded = (counts + bm - 1) // bm * bm
    pad_end = jnp.cumsum(padded)
    pad_start = pad_end - padded
    hot_e = meta[None, :TOP_K, :] == jnp.arange(N_EXPERTS, dtype=jnp.int32)[:, None, None]
    dest_t = jnp.sum(jnp.where(hot_e, pad_start[:, None, None], 0), axis=0) + meta[TOP_K:, :]
    n_rows = (t * TOP_K // bm + N_EXPERTS) * bm
    nb = n_rows // bm
    block_start = jnp.arange(nb, dtype=jnp.int32) * bm
    block_e = jnp.minimum(jnp.sum((pad_end[None, :] <= block_start[:, None]).astype(jnp.int32), axis=1),
                          N_EXPERTS - 1)
    n_valid = (pad_end[-1:] // bm).astype(jnp.int32)
    slot_tok = _slot_tokens(dest_t, n_rows)

    owns = padded > 0
    ids = jnp.arange(N_EXPERTS, dtype=jnp.int32)
    later = jnp.logical_and(owns[None, :], ids[None, :] > ids[:, None])
    next_owner = jnp.min(jnp.where(later, ids[None, :], N_EXPERTS), axis=1)
    next_owner = jnp.where(next_owner == N_EXPERTS, -1, next_owner).astype(jnp.int32)
    ordinal = (jnp.cumsum(owns.astype(jnp.int32)) - 1).astype(jnp.int32)
    next_e = jnp.concatenate([next_owner, jnp.maximum(ordinal, 0)])

    ys = _experts(block_e, n_valid, next_e, slot_tok, hp, w1p, _pair_split_bias(b1[0]),
                  w2[0], b2[0][:, None, :])
    out = _combine(dest_t, gates, x1, mod, normf_w[None, :], ys, seq)
    return out.reshape(batch, seq, d)
```

```python
import math

import jax
import jax.numpy as jnp
from jax import lax
from jax.experimental import pallas as pl
from jax.experimental.pallas import tpu as pltpu
from jax.experimental.pallas import tpu_sc as plsc

F32 = jnp.float32
BF16 = jnp.bfloat16

D_MODEL = 1024
RET_HEADS = 4
RET_KEY_DIM = 64
RET_VAL_DIM = 128
RET_QK_W = RET_HEADS * RET_KEY_DIM
RET_V_W = RET_HEADS * RET_VAL_DIM
RET_CHUNK = 128
DIFF_HEADS = 4
DIFF_HEAD_DIM = 64
DIFF_QK_W = DIFF_HEADS * 2 * DIFF_HEAD_DIM
DIFF_V_W = DIFF_HEADS * 2 * DIFF_HEAD_DIM
IN_SIZES = (RET_QK_W, RET_QK_W, RET_V_W, RET_V_W, DIFF_QK_W, DIFF_QK_W, DIFF_V_W)
REL_BUCKETS = 32
REL_MAX_DIST = 128
N_EXPERTS = 32
TOP_K = 4
SWIGLU_ALPHA = 1.702
SWIGLU_LIMIT = 7.0
NORM_EPS = 1e-6
LAMBDA_INIT = 0.8 - 0.6 * math.exp(-0.3 * 0)

LANES = 128
ROW_SUB = D_MODEL // LANES
NEG_BIG = -1e30
LOG2_E = math.log2(math.e)
ONES_ROWS = 16
VMEM_LIMIT = 56 * 1024 * 1024

ROW_TILE = 512
RET_ROWS = 512
ATT_BLOCK = 512
ATT_STRIP = 256
ATT_HEADS = 4
EXPERT_ROWS = 256
PERM_TILE = 256
COMBINE_TOKENS = 512
DMA_QUEUES = 2
SC_CORES = 2
SC_SUBCORES = 16
SC_LANES = 16
SC_UNROLL = 8
SC_SLOT_CHUNK = 32768


def _rms(x):
    return x * lax.rsqrt(jnp.mean(x * x, axis=-1, keepdims=True) + NORM_EPS)


def _store_row_tiles(ref, x):
    rows = x.shape[0]
    for s in range(ROW_SUB):
        ref[pl.ds(s, rows, stride=ROW_SUB), :] = x[:, s * LANES:(s + 1) * LANES]


def _pack_halves(x):
    half = x.shape[1] // 2
    lo = lax.bitcast_convert_type(x[:, :half].astype(BF16).astype(F32), jnp.uint32)
    hi = lax.bitcast_convert_type(x[:, half:].astype(BF16).astype(F32), jnp.uint32)
    return (hi & jnp.uint32(0xFFFF0000)) | (lo >> 16)


def _unpack_halves(w):
    lo = lax.bitcast_convert_type(w << 16, F32).astype(BF16)
    hi = lax.bitcast_convert_type(w & jnp.uint32(0xFFFF0000), F32).astype(BF16)
    return jnp.concatenate([lo, hi], axis=1)


def _ada_kernel(c_ref, w_ref, b_ref, o_ref):
    c = c_ref[...]
    cond = c * jax.nn.sigmoid(c)
    o_ref[...] = jnp.dot(cond, w_ref[...], precision=lax.Precision.HIGHEST,
                         preferred_element_type=F32) + b_ref[...]


def _ada(c_pad, w_ada, b_ada):
    rows, d = c_pad.shape
    n = w_ada.shape[1]
    tn = 1024
    return pl.pallas_call(
        _ada_kernel,
        grid=(n // tn,),
        in_specs=[pl.BlockSpec((rows, d), lambda j: (0, 0)),
                  pl.BlockSpec((d, tn), lambda j: (0, j)),
                  pl.BlockSpec((1, tn), lambda j: (0, j))],
        out_specs=pl.BlockSpec((rows, tn), lambda j: (0, j)),
        out_shape=jax.ShapeDtypeStruct((rows, n), F32),
        name="ada",
    )(c_pad, w_ada, b_ada)


def _in_proj_kernel(x_ref, mod_ref, nw_ref, w_ref, w1_ref, perm_ref, *o_refs):
    o_refs, w1p_ref = o_refs[:-1], o_refs[-1]
    _split_w1_tiles(w1_ref, perm_ref, w1p_ref)
    x = x_ref[...]
    shift = mod_ref[0, 0:1, :]
    scale = mod_ref[0, 1:2, :]
    h = (_rms(x) * nw_ref[...]) * (1.0 + scale) + shift
    hb = h.astype(BF16)
    off = 0
    for o_ref, width in zip(o_refs[:-1], IN_SIZES[:-1]):
        o_ref[...] = jnp.dot(hb, w_ref[:, off:off + width],
                             preferred_element_type=F32).astype(o_ref.dtype)
        off += width
    vt_ref = o_refs[-1]
    v = jnp.dot(hb, w_ref[:, off:], preferred_element_type=F32)
    hw = 2 * DIFF_HEAD_DIM
    tail = (lax.broadcasted_iota(jnp.int32, (ONES_ROWS, v.shape[0]), 0) == 0).astype(vt_ref.dtype)
    for hd in range(DIFF_HEADS):
        vt_ref[0, hd, 0, 0:hw, :] = v[:, hd * hw:(hd + 1) * hw].T.astype(vt_ref.dtype)
        vt_ref[0, hd, 0, hw:, :] = tail


def _in_proj(x2, mod, norm_w, w_in_bf16, w1, seq):
    t, d = x2.shape
    tm = min(ROW_TILE, seq)
    assert tm == min(ATT_BLOCK, seq)
    per_batch = seq // tm
    in_w = w_in_bf16.shape[1]
    hw = 2 * DIFF_HEAD_DIM
    vt_shape = (t // seq, DIFF_HEADS, per_batch, hw + ONES_ROWS, tm)
    perm = _pair_split_matrix()
    per_step, rest = divmod(w1.shape[0], t // tm)
    assert rest == 0 and per_step >= 1
    expert = pl.BlockSpec((per_step,) + w1.shape[1:], lambda i: (i, 0, 0))
    return pl.pallas_call(
        _in_proj_kernel,
        grid=(t // tm,),
        in_specs=[pl.BlockSpec((tm, d), lambda i: (i, 0)),
                  pl.BlockSpec((1, 6, d), lambda i: (i // per_batch, 0, 0)),
                  pl.BlockSpec((1, d), lambda i: (0, 0)),
                  pl.BlockSpec((d, in_w), lambda i: (0, 0)),
                  expert, pl.BlockSpec(perm.shape, lambda i: (0, 0))],
        out_specs=([pl.BlockSpec((tm, w), lambda i: (i, 0)) for w in IN_SIZES[:-1]]
                   + [pl.BlockSpec((1,) + vt_shape[1:2] + (1,) + vt_shape[3:],
                                   lambda i: (i // per_batch, 0, i % per_batch, 0, 0)), expert]),
        out_shape=([jax.ShapeDtypeStruct((t, w), BF16) for w in IN_SIZES[:-1]]
                   + [jax.ShapeDtypeStruct(vt_shape, BF16), jax.ShapeDtypeStruct(w1.shape, BF16)]),
        compiler_params=pltpu.CompilerParams(vmem_limit_bytes=VMEM_LIMIT),
        name="in_proj",
    )(x2, mod, norm_w, w_in_bf16, w1, perm)


def _split_w1_tiles(w_ref, p_ref, o_ref):
    for e in range(w_ref.shape[0]):
        for s in range(w_ref.shape[2] // PERM_TILE):
            cols = slice(s * PERM_TILE, (s + 1) * PERM_TILE)
            o_ref[e, :, cols] = jnp.dot(w_ref[e, :, cols].astype(BF16), p_ref[...],
                                        preferred_element_type=F32).astype(BF16)


def _pair_split_matrix():
    i = jnp.arange(PERM_TILE)[:, None]
    j = jnp.arange(PERM_TILE)[None, :]
    half = PERM_TILE // 2
    src = jnp.where(j < half, 2 * j, 2 * (j - half) + 1)
    return (i == src).astype(BF16)


def _rotary(x, cos, sin_even, sin_odd):
    nxt = pltpu.roll(x, LANES - 1, 1)
    prv = pltpu.roll(x, 1, 1)
    return x * cos + nxt * sin_even + prv * sin_odd


def _ret_kernel(q_ref, k_ref, v_ref, g_ref, cos_ref, sine_ref, sino_ref,
                hmask_ref, xi_ref, zeta_ref, dmask_ref, gch_ref, o_ref, state_ref):
    @pl.when(pl.program_id(1) == 0)
    def _():
        state_ref[...] = jnp.zeros_like(state_ref)

    n_sub = q_ref.shape[0] // RET_CHUNK
    for c in range(n_sub):
        rows = slice(c * RET_CHUNK, (c + 1) * RET_CHUNK)
        for pair in range(RET_HEADS // 2):
            lanes = slice(pair * LANES, (pair + 1) * LANES)
            cos = cos_ref[rows, :]
            sine = sine_ref[rows, :]
            sino = sino_ref[rows, :]
            qr = _rotary(q_ref[rows, lanes].astype(F32), cos, sine, sino)
            kr = _rotary(k_ref[rows, lanes].astype(F32), cos, sine, sino) * (RET_KEY_DIM ** -0.5)
            qb = qr.astype(BF16)
            for hh in range(2):
                h = 2 * pair + hh
                vcols = slice(h * RET_VAL_DIM, (h + 1) * RET_VAL_DIM)
                v = v_ref[rows, vcols]
                km = (kr * hmask_ref[h]).astype(BF16)
                scores = lax.dot_general(qb, km, (((1,), (1,)), ((), ())),
                                         preferred_element_type=F32) * dmask_ref[h]
                inner = jnp.dot(scores.astype(BF16), v, preferred_element_type=F32)
                qx = (qr * xi_ref[h]).astype(BF16)
                state = state_ref[h]
                cross = jnp.dot(qx, state.astype(BF16), preferred_element_type=F32)
                kz = (kr * zeta_ref[h]).astype(BF16)
                kv = lax.dot_general(kz, v, (((0,), (0,)), ((), ())),
                                     preferred_element_type=F32)
                state_ref[h] = state * gch_ref[h] + kv
                y = _rms(inner + cross)
                g = g_ref[rows, vcols].astype(F32)
                o_ref[rows, vcols] = (g * jax.nn.sigmoid(g) * y).astype(o_ref.dtype)


def _retention_tables(seq):
    dk, c, nh = RET_KEY_DIM, RET_CHUNK, RET_HEADS
    pos = jnp.arange(seq, dtype=F32)
    inv_freq = 1.0 / (10000.0 ** jnp.linspace(0.0, 1.0, dk // 2, dtype=F32))
    ang = pos[:, None] * jnp.repeat(inv_freq, 2)[None, :]
    sin = jnp.tile(jnp.sin(ang), (1, LANES // dk))
    cos = jnp.tile(jnp.cos(ang), (1, LANES // dk))
    even = (jnp.arange(LANES) % 2 == 0)[None, :]
    sin_even = jnp.where(even, -sin, 0.0)
    sin_odd = jnp.where(even, 0.0, sin)
    log_g = jnp.log(1.0 - 2.0 ** (-5.0 - jnp.arange(nh, dtype=F32)))
    i = jnp.arange(c, dtype=F32)
    rel = i[:, None] - i[None, :]
    dmask = jnp.where(rel[None] >= 0,
                      jnp.exp(jnp.maximum(rel, 0.0)[None] * log_g[:, None, None]), 0.0)
    zeta = jnp.exp((c - 1.0 - i)[None, :] * log_g[:, None])
    xi = jnp.exp((i + 1.0)[None, :] * log_g[:, None])
    g_chunk = jnp.exp(c * log_g)
    lane = jnp.arange(LANES)
    hmask = jnp.stack([((lane // dk) == (h % 2)).astype(F32) for h in range(nh)])[:, None, :]
    xi_t = xi[:, :, None] * hmask
    zeta_t = zeta[:, :, None] * hmask
    gch = jnp.broadcast_to(g_chunk[:, None, None], (nh, 1, LANES))
    return cos, sin_even, sin_odd, hmask, xi_t, zeta_t, dmask, gch


def _retention(q, k, v, g, batch, seq):
    t = q.shape[0]
    rb = min(RET_ROWS, seq)
    per_batch = seq // rb
    cos, sin_even, sin_odd, hmask, xi_t, zeta_t, dmask, gch = _retention_tables(seq)
    row = lambda w: pl.BlockSpec((rb, w), lambda b, j: (b * per_batch + j, 0))
    tab = lambda w: pl.BlockSpec((rb, w), lambda b, j: (j, 0))
    full = lambda a: pl.BlockSpec(a.shape, lambda b, j: (0,) * a.ndim)
    return pl.pallas_call(
        _ret_kernel,
        grid=(batch, per_batch),
        in_specs=[row(RET_QK_W), row(RET_QK_W), row(RET_V_W), row(RET_V_W),
                  tab(LANES), tab(LANES), tab(LANES),
                  full(hmask), full(xi_t), full(zeta_t), full(dmask), full(gch)],
        out_specs=row(RET_V_W),
        out_shape=jax.ShapeDtypeStruct((t, RET_V_W), BF16),
        scratch_shapes=[pltpu.VMEM((RET_HEADS, LANES, RET_VAL_DIM), F32)],
        compiler_params=pltpu.CompilerParams(
            dimension_semantics=("arbitrary", "arbitrary"), vmem_limit_bytes=VMEM_LIMIT),
        name="retention",
    )(q, k, v, g, cos, sin_even, sin_odd, hmask, xi_t, zeta_t, dmask, gch)


def _t5_bucket(rel):
    n = jnp.maximum(rel, 0)
    max_exact = REL_BUCKETS // 2
    nf = jnp.maximum(n, 1).astype(F32)
    large = max_exact + (jnp.log(nf / max_exact) / math.log(REL_MAX_DIST / max_exact)
                         * (REL_BUCKETS - max_exact)).astype(jnp.int32)
    large = jnp.minimum(large, REL_BUCKETS - 1)
    return jnp.where(n < max_exact, n, large)


def _bias_tiles(rel_bias, blk):
    r = jnp.arange(blk, dtype=jnp.int32)
    far = rel_bias[REL_BUCKETS - 1]
    rel0 = r[None, :] - r[:, None]
    rel1 = rel0 + blk
    buckets = jnp.arange(REL_BUCKETS, dtype=jnp.int32)

    def tile(rel):
        hot = (_t5_bucket(rel)[:, :, None] == buckets).astype(F32)
        return jnp.einsum('krb,bh->hkr', hot, rel_bias, precision=lax.Precision.HIGHEST)

    b0 = jnp.where(rel0[None] >= 0, (tile(rel0) - far[:, None, None]) * LOG2_E, NEG_BIG)
    b1 = (tile(rel1) - far[:, None, None]) * LOG2_E
    return b0, b1


def _attn_kernel(q_ref, k_ref, vt_ref, b0_ref, b1_ref, lq1_ref, lk1_ref, lq2_ref, lk2_ref,
                 sw_ref, o_ref, m_ref, acc_ref):
    blk = q_ref.shape[0]
    hw = 2 * DIFF_HEAD_DIM
    i = pl.program_id(2)
    lane = lax.broadcasted_iota(jnp.int32, (1, LANES), 1)
    qm = []
    for hd in range(ATT_HEADS):
        q = (q_ref[:, hd * hw:(hd + 1) * hw].astype(F32) * (DIFF_HEAD_DIM ** -0.5 * LOG2_E)).astype(BF16)
        zero = jnp.zeros_like(q)
        qm.append((jnp.where(lane < DIFF_HEAD_DIM, q, zero), jnp.where(lane >= DIFF_HEAD_DIM, q, zero)))

    m_ref[...] = jnp.full_like(m_ref, NEG_BIG)
    acc_ref[...] = jnp.zeros_like(acc_ref)

    def step(blocks):
        kbs = [k_ref[pl.ds(pl.multiple_of(j * blk, blk), blk), :] for j, _, _ in blocks]
        chains = [(hd, mi, qs) for hd in range(ATT_HEADS) for mi in range(2)
                  for qs in range(blk // ATT_STRIP)]
        scores = []
        for hd, mi, qs in chains:
            qc = slice(qs * ATT_STRIP, (qs + 1) * ATT_STRIP)
            row = []
            for (_, bias, diagonal), kb in zip(blocks, kbs):
                keys = (qs + 1) * ATT_STRIP if diagonal else blk
                s = lax.dot_general(kb[:keys, hd * hw:(hd + 1) * hw], qm[hd][mi][qc, :],
                                    (((1,), (1,)), ((), ())), preferred_element_type=F32)
                row.append(s if bias is None else s + bias[hd, :keys, qc])
            scores.append(row)
        stats = []
        for (hd, mi, qs), row in zip(chains, scores):
            qc = slice(qs * ATT_STRIP, (qs + 1) * ATT_STRIP)
            m_old = m_ref[hd, mi, :, qc]
            m_new = m_old
            for s in row:
                m_new = jnp.maximum(m_new, jnp.max(s, axis=0, keepdims=True))
            stats.append((jnp.exp2(m_old - m_new), [jnp.exp2(s - m_new).astype(BF16) for s in row], m_new))
        for (hd, mi, qs), (alpha, ps, m_new) in zip(chains, stats):
            qc = slice(qs * ATT_STRIP, (qs + 1) * ATT_STRIP)
            pv = None
            for (j, _, _), p in zip(blocks, ps):
                part = jnp.dot(vt_ref[0, hd, j, :, 0:p.shape[0]], p, preferred_element_type=F32)
                pv = part if pv is None else pv + part
            acc_ref[hd, mi, :, qc] = alpha * acc_ref[hd, mi, :, qc] + pv
            m_ref[hd, mi, :, qc] = m_new

    n_far = jnp.maximum(i - 1, 0)

    def far_pair(pair, carry):
        step([(2 * pair, None, False), (2 * pair + 1, None, False)])
        return carry

    lax.fori_loop(0, lax.shift_right_logical(n_far, 1), far_pair, 0)

    @pl.when(lax.rem(n_far, 2) == 1)
    def _():
        step([(n_far - 1, None, False)])

    @pl.when(i >= 1)
    def _():
        step([(i - 1, b1_ref, False), (i, b0_ref, True)])

    @pl.when(i == 0)
    def _():
        step([(i, b0_ref, True)])

    lam = (jnp.exp(jnp.sum(lq1_ref[...] * lk1_ref[...], axis=-1, keepdims=True))
           - jnp.exp(jnp.sum(lq2_ref[...] * lk2_ref[...], axis=-1, keepdims=True))
           + LAMBDA_INIT)
    for hd in range(ATT_HEADS):
        a = (acc_ref[hd, 0, :hw, :] / acc_ref[hd, 0, hw:hw + 1, :]
             - lam * (acc_ref[hd, 1, :hw, :] / acc_ref[hd, 1, hw:hw + 1, :]))
        a = a * lax.rsqrt(jnp.mean(a * a, axis=0, keepdims=True) + NORM_EPS)
        o_ref[:, hd * hw:(hd + 1) * hw] = (a.T * sw_ref[...] * (1.0 - LAMBDA_INIT)).astype(o_ref.dtype)


def _diff_attention(q, k, vt, rel_bias, lq1, lk1, lq2, lk2, subln_w, batch, seq):
    t = q.shape[0]
    blk = min(ATT_BLOCK, seq)
    nq = seq // blk
    hw = 2 * DIFF_HEAD_DIM
    nh = ATT_HEADS
    b0, b1 = _bias_tiles(rel_bias, blk)
    small = lambda a: pl.BlockSpec(a.shape, lambda b, h, i: (0,) * a.ndim)
    return pl.pallas_call(
        _attn_kernel,
        grid=(batch, DIFF_HEADS // nh, nq),
        in_specs=[pl.BlockSpec((blk, nh * hw), lambda b, h, i: (b * nq + i, h)),
                  pl.BlockSpec((seq, nh * hw), lambda b, h, i: (b, h)),
                  pl.BlockSpec((1, nh, nq, hw + ONES_ROWS, blk), lambda b, h, i: (b, h, 0, 0, 0)),
                  pl.BlockSpec((nh, blk, blk), lambda b, h, i: (h, 0, 0)),
                  pl.BlockSpec((nh, blk, blk), lambda b, h, i: (h, 0, 0)),
                  small(lq1), small(lk1), small(lq2), small(lk2), small(subln_w)],
        out_specs=pl.BlockSpec((blk, nh * hw), lambda b, h, i: (b * nq + i, h)),
        out_shape=jax.ShapeDtypeStruct((t, DIFF_V_W), BF16),
        scratch_shapes=[pltpu.VMEM((nh, 2, 1, blk), F32), pltpu.VMEM((nh, 2, hw + ONES_ROWS, blk), F32)],
        compiler_params=pltpu.CompilerParams(
            dimension_semantics=("arbitrary", "arbitrary", "arbitrary"),
            vmem_limit_bytes=VMEM_LIMIT),
        name="diff_attn",
    )(q, k, vt, b0, b1, lq1, lk1, lq2, lk2, subln_w)


def _out_kernel(yr_ref, yd_ref, x_ref, mod_ref, nw_ref, wo_ref, wr_ref, br_ref, upper_ref,
                x1_ref, hp_ref, meta_ref, gate_ref, cnt_ref, run_ref):
    tm = x_ref.shape[0]
    ne = run_ref.shape[0]

    @pl.when(pl.program_id(0) == 0)
    def _():
        run_ref[...] = jnp.zeros_like(run_ref)

    mixed = (jnp.dot(yr_ref[...], wo_ref[0:RET_V_W, :], preferred_element_type=F32)
             + jnp.dot(yd_ref[...], wo_ref[RET_V_W:, :], preferred_element_type=F32))
    gate1 = mod_ref[0, 2:3, :]
    shift2 = mod_ref[0, 3:4, :]
    scale2 = mod_ref[0, 4:5, :]
    x1 = x_ref[...] + gate1 * mixed
    x1_ref[...] = x1
    h2 = (_rms(x1) * nw_ref[...]) * (1.0 + scale2) + shift2
    hp_ref[...] = _pack_halves(h2)

    h_hi = h2.astype(BF16)
    h_lo = (h2 - h_hi.astype(F32)).astype(BF16)
    nt_dims = (((1,), (1,)), ((), ()))
    both = lax.dot_general(wr_ref[...], h_hi, nt_dims, preferred_element_type=F32)
    low = lax.dot_general(wr_ref[0:ne, :], h_lo, nt_dims, preferred_element_type=F32)
    logits = both[0:ne, :] + both[ne:, :] + low + br_ref[...]

    row = lax.broadcasted_iota(jnp.int32, logits.shape, 0)
    work = logits
    vals, idxs, hots = [], [], []
    for _ in range(TOP_K):
        mx = jnp.max(work, axis=0, keepdims=True)
        idx = jnp.min(jnp.where(work == mx, row, ne), axis=0, keepdims=True)
        hot = row == idx
        vals.append(mx)
        idxs.append(idx)
        hots.append(hot)
        work = jnp.where(hot, -jnp.inf, work)
    exps = [jnp.exp(v - vals[0]) for v in vals]
    denom = exps[0] + exps[1] + exps[2] + exps[3]

    sel = jnp.zeros(logits.shape, F32)
    for hot in hots:
        sel = sel + hot.astype(F32)
    prefix = jnp.dot(sel.astype(BF16), upper_ref[...], preferred_element_type=F32) + run_ref[...]
    ranks = [jnp.sum(jnp.where(hot, prefix, 0.0), axis=0, keepdims=True) for hot in hots]
    run_ref[...] = run_ref[...] + jnp.sum(sel, axis=1, keepdims=True)
    cnt_ref[...] = jnp.broadcast_to(run_ref[...], cnt_ref.shape).astype(jnp.int32)

    meta_ref[...] = jnp.concatenate(idxs + [r.astype(jnp.int32) for r in ranks], axis=0)
    gate_ref[...] = jnp.concatenate([e / denom for e in exps] + [jnp.zeros_like(denom)] * TOP_K, axis=0)


def _out_router(y_r, y_d, x2, mod, norm_w, w_out_bf16, w_router, b_router, seq):
    t, d = x2.shape
    tm = min(ROW_TILE, seq)
    per_batch = seq // tm
    ne = w_router.shape[1]
    w_hi = w_router.astype(BF16)
    w_lo = (w_router - w_hi.astype(F32)).astype(BF16)
    wr_t = jnp.concatenate([w_hi, w_lo], axis=1).T
    idx = jnp.arange(tm, dtype=jnp.int32)
    upper = (idx[:, None] < idx[None, :]).astype(BF16)
    row = lambda w: pl.BlockSpec((tm, w), lambda i: (i, 0))
    col = lambda h: pl.BlockSpec((h, tm), lambda i: (0, i))
    const = lambda a: pl.BlockSpec(a.shape, lambda i: (0,) * a.ndim)
    x1, hp, meta_t, gates_t, counts = pl.pallas_call(
        _out_kernel,
        grid=(t // tm,),
        in_specs=[row(RET_V_W), row(DIFF_V_W), row(d),
                  pl.BlockSpec((1, 6, d), lambda i: (i // per_batch, 0, 0)),
                  const(norm_w), const(w_out_bf16), const(wr_t), const(b_router), const(upper)],
        out_specs=[row(d), row(d // 2), col(2 * TOP_K), col(2 * TOP_K),
                   pl.BlockSpec((ne, LANES), lambda i: (0, 0))],
        out_shape=[jax.ShapeDtypeStruct((t, d), F32),
                   jax.ShapeDtypeStruct((t, d // 2), jnp.uint32),
                   jax.ShapeDtypeStruct((2 * TOP_K, t), jnp.int32),
                   jax.ShapeDtypeStruct((2 * TOP_K, t), F32),
                   jax.ShapeDtypeStruct((ne, LANES), jnp.int32)],
        scratch_shapes=[pltpu.VMEM((ne, 1), F32)],
        compiler_params=pltpu.CompilerParams(
            dimension_semantics=("arbitrary",), vmem_limit_bytes=VMEM_LIMIT),
        name="out_router",
    )(y_r, y_d, x2, mod, norm_w, w_out_bf16, wr_t, b_router, upper)
    return x1, hp, meta_t, gates_t.T, counts[:, 0]


def _pair_split_bias(b1):
    e, f2 = b1.shape
    nt = f2 // PERM_TILE
    g = b1[:, 0::2].reshape(e, nt, 1, PERM_TILE // 2)
    l = b1[:, 1::2].reshape(e, nt, 1, PERM_TILE // 2)
    return jnp.concatenate([g, l], axis=2).reshape(e, 1, f2)


def _slot_tokens(dest_t, n_rows):
    n_tok = dest_t.shape[1]
    assert n_tok & (n_tok - 1) == 0
    flat = dest_t.reshape(-1)
    n_slots = flat.shape[0]
    chunk = min(SC_SLOT_CHUNK, n_slots)
    mesh = plsc.VectorSubcoreMesh(core_axis_name="c", subcore_axis_name="s",
                                  num_cores=SC_CORES, num_subcores=SC_SUBCORES)

    def body(dest_hbm, out_hbm, out_v, dest_v):
        first = jnp.logical_and(lax.axis_index("c") == 0, lax.axis_index("s") == 0)

        @pl.when(first)
        def _():
            zeros = jnp.zeros((SC_LANES,), jnp.int32)

            @plsc.parallel_loop(0, n_rows // SC_LANES, unroll=SC_UNROLL)
            def _(g):
                out_v[pl.ds(g * SC_LANES, SC_LANES)] = zeros

            lane = lax.iota(jnp.int32, SC_LANES)

            @pl.loop(0, n_slots // chunk)
            def _(c):
                pltpu.sync_copy(dest_hbm.at[pl.ds(c * chunk, chunk)], dest_v)

                @plsc.parallel_loop(0, chunk // SC_LANES, unroll=SC_UNROLL)
                def _(g):
                    rows = dest_v[pl.ds(g * SC_LANES, SC_LANES)]
                    slot = lane + (c * chunk + g * SC_LANES)
                    plsc.store_scatter(out_v, [rows], slot & (n_tok - 1))

            pltpu.sync_copy(out_v, out_hbm)

    return pl.kernel(
        body,
        out_type=jax.ShapeDtypeStruct((n_rows,), jnp.int32),
        mesh=mesh,
        scratch_types=[pltpu.VMEM((n_rows,), jnp.int32), pltpu.VMEM((chunk,), jnp.int32)],
        compiler_params=pltpu.CompilerParams(needs_layout_passes=False),
        name="slot_tokens",
    )(flat)


def _expert_kernel(be_ref, nv_ref, nexte_ref, tcur_ref, tnxt_ref, hp_ref, w1_ref, b1_ref, w2_ref, b2_ref,
                   o_ref, hbuf, xb0, xb1, w1buf, w2buf, w2b, sem, wsems):
    b = pl.program_id(0)
    nv = nv_ref[0]
    bm = xb0.shape[0]

    def gather(tok_ref, dst):
        for r in range(bm):
            dst[pl.ds(r, 1), :] = hbuf[pl.ds(tok_ref[0, 0, r], 1), :]

    def weight_copies(e, slot):
        return (pltpu.make_async_copy(w1_ref.at[e], w1buf.at[slot], wsems.at[0, slot]),
                pltpu.make_async_copy(w2_ref.at[e], w2buf.at[slot], wsems.at[1, slot]))

    e = be_ref[b]
    prev_e = be_ref[jnp.maximum(b - 1, 0)]
    first_of_expert = jnp.logical_and(b < nv, jnp.logical_or(b == 0, e != prev_e))
    slot = lax.rem(nexte_ref[N_EXPERTS + e], 2)

    @pl.when(b == 0)
    def _():
        for cp in weight_copies(e, slot):
            cp.start()
        load = pltpu.make_async_copy(hp_ref, hbuf, sem)
        load.start()
        load.wait()
        gather(tcur_ref, xb0)

    @pl.when(first_of_expert)
    def _():
        for cp in weight_copies(e, slot):
            cp.wait()
        nxt = nexte_ref[e]

        @pl.when(nxt >= 0)
        def _():
            for cp in weight_copies(nxt, 1 - slot):
                cp.start()

        w2b[...] = w2buf[slot].astype(BF16)

    def run(cur, nxt_rows):
        x = _unpack_halves(cur[...])
        half = PERM_TILE // 2
        acts = []
        for j in range(w1buf.shape[2] // PERM_TILE):
            cols = slice(j * PERM_TILE, (j + 1) * PERM_TILE)
            h = jnp.dot(x, w1buf[slot, :, cols], preferred_element_type=F32) + b1_ref[0, :, cols]
            glu = jnp.minimum(h[:, :half], SWIGLU_LIMIT)
            lin = jnp.clip(h[:, half:], -SWIGLU_LIMIT, SWIGLU_LIMIT)
            acts.append((glu * jax.nn.sigmoid(SWIGLU_ALPHA * glu) * (lin + 1.0)).astype(BF16))
        act = jnp.concatenate(acts, axis=1)
        y = jnp.dot(act, w2b[...], preferred_element_type=F32) + b2_ref[0]
        _store_row_tiles(o_ref, y)
        gather(tnxt_ref, nxt_rows)

    even = lax.rem(b, 2) == 0

    @pl.when(jnp.logical_and(b < nv, even))
    def _():
        run(xb0, xb1)

    @pl.when(jnp.logical_and(b < nv, jnp.logical_not(even)))
    def _():
        run(xb1, xb0)

    @pl.when(b >= nv)
    def _():
        o_ref[...] = jnp.zeros_like(o_ref)


def _experts(block_e, n_valid, next_e, slot_tok, hp, w1p, b1p, w2, b2):
    d = D_MODEL
    f2 = w1p.shape[2]
    f = w2.shape[1]
    bm = EXPERT_ROWS
    nb = slot_tok.shape[0] // bm
    tok3 = slot_tok.reshape(nb, 1, bm)
    exp = lambda b, be, nv, ne: be[jnp.maximum(jnp.minimum(b, nv[0] - 1), 0)]
    grid_spec = pltpu.PrefetchScalarGridSpec(
        num_scalar_prefetch=3,
        grid=(nb,),
        in_specs=[pl.BlockSpec((1, 1, bm), lambda b, be, nv, ne: (b, 0, 0), memory_space=pltpu.SMEM),
                  pl.BlockSpec((1, 1, bm), lambda b, be, nv, ne: (jnp.minimum(b + 1, nb - 1), 0, 0),
                               memory_space=pltpu.SMEM),
                  pl.BlockSpec(memory_space=pl.ANY),
                  pl.BlockSpec(memory_space=pl.ANY),
                  pl.BlockSpec((1, 1, f2), lambda b, be, nv, ne: (exp(b, be, nv, ne), 0, 0)),
                  pl.BlockSpec(memory_space=pl.ANY),
                  pl.BlockSpec((1, 1, d), lambda b, be, nv, ne: (exp(b, be, nv, ne), 0, 0))],
        out_specs=pl.BlockSpec((bm * ROW_SUB, LANES), lambda b, be, nv, ne: (b, 0)),
        scratch_shapes=[pltpu.VMEM(hp.shape, hp.dtype), pltpu.VMEM((bm, hp.shape[1]), hp.dtype),
                        pltpu.VMEM((bm, hp.shape[1]), hp.dtype),
                        pltpu.VMEM((2,) + w1p.shape[1:], w1p.dtype), pltpu.VMEM((2,) + w2.shape[1:], w2.dtype),
                        pltpu.VMEM(w2.shape[1:], BF16),
                        pltpu.SemaphoreType.DMA(()), pltpu.SemaphoreType.DMA((2, 2))],
    )
    return pl.pallas_call(
        _expert_kernel,
        grid_spec=grid_spec,
        out_shape=jax.ShapeDtypeStruct((nb * bm * ROW_SUB, LANES), F32),
        compiler_params=pltpu.CompilerParams(
            dimension_semantics=("arbitrary",), vmem_limit_bytes=VMEM_LIMIT),
        name="experts",
    )(block_e, n_valid, next_e, tok3, tok3, hp, w1p, b1p, w2, b2)


def _combine_kernel(dcur_ref, dnxt_ref, gate_ref, x1_ref, mod_ref, nf_ref, y_ref, o_ref, buf, sems):
    nt = x1_ref.shape[0]
    i = pl.program_id(0)
    n = pl.num_programs(0)
    slot = lax.rem(i, 2)

    def row_copy(src_row, s, kk, r):
        return pltpu.make_async_copy(y_ref.at[pl.ds(src_row * ROW_SUB, ROW_SUB), :],
                                     buf.at[s, kk, pl.ds(r * ROW_SUB, ROW_SUB), :], sems.at[s])

    def fetch(d_ref, s):
        for j in range(nt):
            for kk in range(TOP_K):
                row_copy(d_ref[0, 0, kk * nt + j], s, kk, j).start(priority=kk % DMA_QUEUES)

    def wait_all(s):
        for kk in range(TOP_K):
            pltpu.make_async_copy(y_ref.at[pl.ds(0, nt * ROW_SUB), :], buf.at[s, kk], sems.at[s]).wait()

    @pl.when(i == 0)
    def _():
        fetch(dcur_ref, 0)

    fetch(dnxt_ref, 1 - slot)
    wait_all(slot)

    g = gate_ref[...]
    parts = []
    for s in range(ROW_SUB):
        acc = buf[slot, 0, pl.ds(s, nt, stride=ROW_SUB), :] * g[:, 0:1]
        for kk in range(1, TOP_K):
            acc = acc + buf[slot, kk, pl.ds(s, nt, stride=ROW_SUB), :] * g[:, kk:kk + 1]
        parts.append(acc)
    moe = jnp.concatenate(parts, axis=1)
    gate2 = mod_ref[0, 5:6, :]
    x2 = x1_ref[...] + gate2 * moe
    o_ref[...] = _rms(x2) * nf_ref[...]

    @pl.when(i == n - 1)
    def _():
        wait_all(1 - slot)


def _combine(dest_t, gates, x1, mod, normf_w, y, seq):
    t, d = x1.shape
    nt = min(COMBINE_TOKENS, seq)
    steps = t // nt
    per_batch = seq // nt
    dest2 = jnp.transpose(dest_t.reshape(TOP_K, steps, nt), (1, 0, 2)).reshape(steps, 1, nt * TOP_K)
    return pl.pallas_call(
        _combine_kernel,
        grid=(steps,),
        in_specs=[pl.BlockSpec((1, 1, nt * TOP_K), lambda i: (i, 0, 0), memory_space=pltpu.SMEM),
                  pl.BlockSpec((1, 1, nt * TOP_K), lambda i: (jnp.minimum(i + 1, steps - 1), 0, 0),
                               memory_space=pltpu.SMEM),
                  pl.BlockSpec((nt, 2 * TOP_K), lambda i: (i, 0)),
                  pl.BlockSpec((nt, d), lambda i: (i, 0)),
                  pl.BlockSpec((1, 6, d), lambda i: (i // per_batch, 0, 0)),
                  pl.BlockSpec((1, d), lambda i: (0, 0)),
                  pl.BlockSpec(memory_space=pl.ANY)],
        out_specs=pl.BlockSpec((nt, d), lambda i: (i, 0)),
        out_shape=jax.ShapeDtypeStruct((t, d), F32),
        scratch_shapes=[pltpu.VMEM((2, TOP_K, nt * ROW_SUB, LANES), F32),
                        pltpu.SemaphoreType.DMA((2,))],
        compiler_params=pltpu.CompilerParams(
            dimension_semantics=("arbitrary",), vmem_limit_bytes=VMEM_LIMIT),
        name="combine",
    )(dest2, dest2, gates, x1, mod, normf_w, y)


def kernel(x, c, w_ada, b_ada, norm1_w, w_in, lam_q1, lam_k1, lam_q2, lam_k2, subln_w, rel_bias,
           w_out, norm2_w, w_router, b_router, w1, b1, w2, b2, normf_w):
    batch, seq, d = x.shape
    t = batch * seq
    x2 = x.reshape(t, d)

    c_pad = jnp.zeros((8, d), F32).at[:batch].set(c)
    mod = _ada(c_pad, w_ada[0], b_ada[0][None, :])[:batch].reshape(batch, 6, d)

    q_r, k_r, v_r, g_r, q_d, k_d, vt_d, w1p = _in_proj(x2, mod, norm1_w[0][None, :],
                                                      w_in[0].astype(BF16), w1[0], seq)
    y_r = _retention(q_r, k_r, v_r, g_r, batch, seq)
    y_d = _diff_attention(q_d, k_d, vt_d, rel_bias, lam_q1, lam_k1, lam_q2, lam_k2,
                          subln_w, batch, seq)

    x1, hp, meta, gates, counts = _out_router(y_r, y_d, x2, mod, norm2_w[0][None, :],
                                              w_out[0].astype(BF16), w_router[0],
                                              b_router[0][:, None], seq)

    bm = EXPERT_ROWS
    padded = (counts + bm - 1) // bm * bm
    pad_end = jnp.cumsum(padded)
    pad_start = pad_end - padded
    hot_e = meta[None, :TOP_K, :] == jnp.arange(N_EXPERTS, dtype=jnp.int32)[:, None, None]
    dest_t = jnp.sum(jnp.where(hot_e, pad_start[:, None, None], 0), axis=0) + meta[TOP_K:, :]
    n_rows = (t * TOP_K // bm + N_EXPERTS) * bm
    nb = n_rows // bm
    block_start = jnp.arange(nb, dtype=jnp.int32) * bm
    block_e = jnp.minimum(jnp.sum((pad_end[None, :] <= block_start[:, None]).astype(jnp.int32), axis=1),
                          N_EXPERTS - 1)
    n_valid = (pad_end[-1:] // bm).astype(jnp.int32)
    slot_tok = _slot_tokens(dest_t, n_rows)

    owns = padded > 0
    ids = jnp.arange(N_EXPERTS, dtype=jnp.int32)
    later = jnp.logical_and(owns[None, :], ids[None, :] > ids[:, None])
    next_owner = jnp.min(jnp.where(later, ids[None, :], N_EXPERTS), axis=1)
    next_owner = jnp.where(next_owner == N_EXPERTS, -1, next_owner).astype(jnp.int32)
    ordinal = (jnp.cumsum(owns.astype(jnp.int32)) - 1).astype(jnp.int32)
    next_e = jnp.concatenate([next_owner, jnp.maximum(ordinal, 0)])

    ys = _experts(block_e, n_valid, next_e, slot_tok, hp, w1p, _pair_split_bias(b1[0]),
                  w2[0], b2[0][:, None, :])
    out = _combine(dest_t, gates, x1, mod, normf_w[None, :], ys, seq)
    return out.reshape(batch, seq, d)
```

```python
import math

import jax
import jax.numpy as jnp
from jax import lax
from jax.experimental import pallas as pl
from jax.experimental.pallas import tpu as pltpu
from jax.experimental.pallas import tpu_sc as plsc

F32 = jnp.float32
BF16 = jnp.bfloat16

D_MODEL = 1024
RET_HEADS = 4
RET_KEY_DIM = 64
RET_VAL_DIM = 128
RET_QK_W = RET_HEADS * RET_KEY_DIM
RET_V_W = RET_HEADS * RET_VAL_DIM
RET_CHUNK = 128
DIFF_HEADS = 4
DIFF_HEAD_DIM = 64
DIFF_QK_W = DIFF_HEADS * 2 * DIFF_HEAD_DIM
DIFF_V_W = DIFF_HEADS * 2 * DIFF_HEAD_DIM
IN_SIZES = (RET_QK_W, RET_QK_W, RET_V_W, RET_V_W, DIFF_QK_W, DIFF_QK_W, DIFF_V_W)
REL_BUCKETS = 32
REL_MAX_DIST = 128
N_EXPERTS = 32
TOP_K = 4
SWIGLU_ALPHA = 1.702
SWIGLU_LIMIT = 7.0
NORM_EPS = 1e-6
LAMBDA_INIT = 0.8 - 0.6 * math.exp(-0.3 * 0)

LANES = 128
ROW_SUB = D_MODEL // LANES
NEG_BIG = -1e30
LOG2_E = math.log2(math.e)
ONES_ROWS = 16
VMEM_LIMIT = 56 * 1024 * 1024

ROW_TILE = 512
RET_ROWS = 512
ATT_BLOCK = 512
ATT_STRIP = 256
ATT_HEADS = 4
EXPERT_ROWS = 256
PERM_TILE = 256
COMBINE_TOKENS = 256
DMA_QUEUES = 2
SC_CORES = 2
SC_SUBCORES = 16
SC_LANES = 16
SC_UNROLL = 8
SC_SLOT_CHUNK = 32768


def _rms(x):
    return x * lax.rsqrt(jnp.mean(x * x, axis=-1, keepdims=True) + NORM_EPS)


def _store_row_tiles(ref, x):
    rows = x.shape[0]
    for s in range(ROW_SUB):
        ref[pl.ds(s, rows, stride=ROW_SUB), :] = x[:, s * LANES:(s + 1) * LANES]


def _pack_halves(x):
    half = x.shape[1] // 2
    lo = lax.bitcast_convert_type(x[:, :half].astype(BF16).astype(F32), jnp.uint32)
    hi = lax.bitcast_convert_type(x[:, half:].astype(BF16).astype(F32), jnp.uint32)
    return (hi & jnp.uint32(0xFFFF0000)) | (lo >> 16)


def _unpack_halves(w):
    lo = lax.bitcast_convert_type(w << 16, F32).astype(BF16)
    hi = lax.bitcast_convert_type(w & jnp.uint32(0xFFFF0000), F32).astype(BF16)
    return jnp.concatenate([lo, hi], axis=1)


def _ada_kernel(c_ref, w_ref, b_ref, o_ref):
    c = c_ref[...]
    cond = c * jax.nn.sigmoid(c)
    o_ref[...] = jnp.dot(cond, w_ref[...], precision=lax.Precision.HIGHEST,
                         preferred_element_type=F32) + b_ref[...]


def _ada(c_pad, w_ada, b_ada):
    rows, d = c_pad.shape
    n = w_ada.shape[1]
    tn = 1024
    return pl.pallas_call(
        _ada_kernel,
        grid=(n // tn,),
        in_specs=[pl.BlockSpec((rows, d), lambda j: (0, 0)),
                  pl.BlockSpec((d, tn), lambda j: (0, j)),
                  pl.BlockSpec((1, tn), lambda j: (0, j))],
        out_specs=pl.BlockSpec((rows, tn), lambda j: (0, j)),
        out_shape=jax.ShapeDtypeStruct((rows, n), F32),
        name="ada",
    )(c_pad, w_ada, b_ada)


def _in_proj_kernel(x_ref, mod_ref, nw_ref, w_ref, w1_ref, perm_ref, *o_refs):
    o_refs, w1p_ref = o_refs[:-1], o_refs[-1]
    _split_w1_tiles(w1_ref, perm_ref, w1p_ref)
    x = x_ref[...]
    shift = mod_ref[0, 0:1, :]
    scale = mod_ref[0, 1:2, :]
    h = (_rms(x) * nw_ref[...]) * (1.0 + scale) + shift
    hb = h.astype(BF16)
    off = 0
    for o_ref, width in zip(o_refs[:-1], IN_SIZES[:-1]):
        o_ref[...] = jnp.dot(hb, w_ref[:, off:off + width],
                             preferred_element_type=F32).astype(o_ref.dtype)
        off += width
    vt_ref = o_refs[-1]
    v = jnp.dot(hb, w_ref[:, off:], preferred_element_type=F32)
    hw = 2 * DIFF_HEAD_DIM
    tail = (lax.broadcasted_iota(jnp.int32, (ONES_ROWS, v.shape[0]), 0) == 0).astype(vt_ref.dtype)
    for hd in range(DIFF_HEADS):
        vt_ref[0, hd, 0, 0:hw, :] = v[:, hd * hw:(hd + 1) * hw].T.astype(vt_ref.dtype)
        vt_ref[0, hd, 0, hw:, :] = tail


def _in_proj(x2, mod, norm_w, w_in_bf16, w1, seq):
    t, d = x2.shape
    tm = min(ROW_TILE, seq)
    assert tm == min(ATT_BLOCK, seq)
    per_batch = seq // tm
    in_w = w_in_bf16.shape[1]
    hw = 2 * DIFF_HEAD_DIM
    vt_shape = (t // seq, DIFF_HEADS, per_batch, hw + ONES_ROWS, tm)
    perm = _pair_split_matrix()
    per_step, rest = divmod(w1.shape[0], t // tm)
    assert rest == 0 and per_step >= 1
    expert = pl.BlockSpec((per_step,) + w1.shape[1:], lambda i: (i, 0, 0))
    return pl.pallas_call(
        _in_proj_kernel,
        grid=(t // tm,),
        in_specs=[pl.BlockSpec((tm, d), lambda i: (i, 0)),
                  pl.BlockSpec((1, 6, d), lambda i: (i // per_batch, 0, 0)),
                  pl.BlockSpec((1, d), lambda i: (0, 0)),
                  pl.BlockSpec((d, in_w), lambda i: (0, 0)),
                  expert, pl.BlockSpec(perm.shape, lambda i: (0, 0))],
        out_specs=([pl.BlockSpec((tm, w), lambda i: (i, 0)) for w in IN_SIZES[:-1]]
                   + [pl.BlockSpec((1,) + vt_shape[1:2] + (1,) + vt_shape[3:],
                                   lambda i: (i // per_batch, 0, i % per_batch, 0, 0)), expert]),
        out_shape=([jax.ShapeDtypeStruct((t, w), BF16) for w in IN_SIZES[:-1]]
                   + [jax.ShapeDtypeStruct(vt_shape, BF16), jax.ShapeDtypeStruct(w1.shape, BF16)]),
        compiler_params=pltpu.CompilerParams(vmem_limit_bytes=VMEM_LIMIT),
        name="in_proj",
    )(x2, mod, norm_w, w_in_bf16, w1, perm)


def _split_w1_tiles(w_ref, p_ref, o_ref):
    for e in range(w_ref.shape[0]):
        for s in range(w_ref.shape[2] // PERM_TILE):
            cols = slice(s * PERM_TILE, (s + 1) * PERM_TILE)
            o_ref[e, :, cols] = jnp.dot(w_ref[e, :, cols].astype(BF16), p_ref[...],
                                        preferred_element_type=F32).astype(BF16)


def _pair_split_matrix():
    i = jnp.arange(PERM_TILE)[:, None]
    j = jnp.arange(PERM_TILE)[None, :]
    half = PERM_TILE // 2
    src = jnp.where(j < half, 2 * j, 2 * (j - half) + 1)
    return (i == src).astype(BF16)


def _rotary(x, cos, sin_even, sin_odd):
    nxt = pltpu.roll(x, LANES - 1, 1)
    prv = pltpu.roll(x, 1, 1)
    return x * cos + nxt * sin_even + prv * sin_odd


def _ret_kernel(q_ref, k_ref, v_ref, g_ref, cos_ref, sine_ref, sino_ref,
                hmask_ref, xi_ref, zeta_ref, dmask_ref, gch_ref, o_ref, state_ref):
    @pl.when(pl.program_id(1) == 0)
    def _():
        state_ref[...] = jnp.zeros_like(state_ref)

    n_sub = q_ref.shape[0] // RET_CHUNK
    for c in range(n_sub):
        rows = slice(c * RET_CHUNK, (c + 1) * RET_CHUNK)
        for pair in range(RET_HEADS // 2):
            lanes = slice(pair * LANES, (pair + 1) * LANES)
            cos = cos_ref[rows, :]
            sine = sine_ref[rows, :]
            sino = sino_ref[rows, :]
            qr = _rotary(q_ref[rows, lanes].astype(F32), cos, sine, sino)
            kr = _rotary(k_ref[rows, lanes].astype(F32), cos, sine, sino) * (RET_KEY_DIM ** -0.5)
            qb = qr.astype(BF16)
            for hh in range(2):
                h = 2 * pair + hh
                vcols = slice(h * RET_VAL_DIM, (h + 1) * RET_VAL_DIM)
                v = v_ref[rows, vcols]
                km = (kr * hmask_ref[h]).astype(BF16)
                scores = lax.dot_general(qb, km, (((1,), (1,)), ((), ())),
                                         preferred_element_type=F32) * dmask_ref[h]
                inner = jnp.dot(scores.astype(BF16), v, preferred_element_type=F32)
                qx = (qr * xi_ref[h]).astype(BF16)
                state = state_ref[h]
                cross = jnp.dot(qx, state.astype(BF16), preferred_element_type=F32)
                kz = (kr * zeta_ref[h]).astype(BF16)
                kv = lax.dot_general(kz, v, (((0,), (0,)), ((), ())),
                                     preferred_element_type=F32)
                state_ref[h] = state * gch_ref[h] + kv
                y = _rms(inner + cross)
                g = g_ref[rows, vcols].astype(F32)
                o_ref[rows, vcols] = (g * jax.nn.sigmoid(g) * y).astype(o_ref.dtype)


def _retention_tables(seq):
    dk, c, nh = RET_KEY_DIM, RET_CHUNK, RET_HEADS
    pos = jnp.arange(seq, dtype=F32)
    inv_freq = 1.0 / (10000.0 ** jnp.linspace(0.0, 1.0, dk // 2, dtype=F32))
    ang = pos[:, None] * jnp.repeat(inv_freq, 2)[None, :]
    sin = jnp.tile(jnp.sin(ang), (1, LANES // dk))
    cos = jnp.tile(jnp.cos(ang), (1, LANES // dk))
    even = (jnp.arange(LANES) % 2 == 0)[None, :]
    sin_even = jnp.where(even, -sin, 0.0)
    sin_odd = jnp.where(even, 0.0, sin)
    log_g = jnp.log(1.0 - 2.0 ** (-5.0 - jnp.arange(nh, dtype=F32)))
    i = jnp.arange(c, dtype=F32)
    rel = i[:, None] - i[None, :]
    dmask = jnp.where(rel[None] >= 0,
                      jnp.exp(jnp.maximum(rel, 0.0)[None] * log_g[:, None, None]), 0.0)
    zeta = jnp.exp((c - 1.0 - i)[None, :] * log_g[:, None])
    xi = jnp.exp((i + 1.0)[None, :] * log_g[:, None])
    g_chunk = jnp.exp(c * log_g)
    lane = jnp.arange(LANES)
    hmask = jnp.stack([((lane // dk) == (h % 2)).astype(F32) for h in range(nh)])[:, None, :]
    xi_t = xi[:, :, None] * hmask
    zeta_t = zeta[:, :, None] * hmask
    gch = jnp.broadcast_to(g_chunk[:, None, None], (nh, 1, LANES))
    return cos, sin_even, sin_odd, hmask, xi_t, zeta_t, dmask, gch


def _retention(q, k, v, g, batch, seq):
    t = q.shape[0]
    rb = min(RET_ROWS, seq)
    per_batch = seq // rb
    cos, sin_even, sin_odd, hmask, xi_t, zeta_t, dmask, gch = _retention_tables(seq)
    row = lambda w: pl.BlockSpec((rb, w), lambda b, j: (b * per_batch + j, 0))
    tab = lambda w: pl.BlockSpec((rb, w), lambda b, j: (j, 0))
    full = lambda a: pl.BlockSpec(a.shape, lambda b, j: (0,) * a.ndim)
    return pl.pallas_call(
        _ret_kernel,
        grid=(batch, per_batch),
        in_specs=[row(RET_QK_W), row(RET_QK_W), row(RET_V_W), row(RET_V_W),
                  tab(LANES), tab(LANES), tab(LANES),
                  full(hmask), full(xi_t), full(zeta_t), full(dmask), full(gch)],
        out_specs=row(RET_V_W),
        out_shape=jax.ShapeDtypeStruct((t, RET_V_W), BF16),
        scratch_shapes=[pltpu.VMEM((RET_HEADS, LANES, RET_VAL_DIM), F32)],
        compiler_params=pltpu.CompilerParams(
            dimension_semantics=("arbitrary", "arbitrary"), vmem_limit_bytes=VMEM_LIMIT),
        name="retention",
    )(q, k, v, g, cos, sin_even, sin_odd, hmask, xi_t, zeta_t, dmask, gch)


def _t5_bucket(rel):
    n = jnp.maximum(rel, 0)
    max_exact = REL_BUCKETS // 2
    nf = jnp.maximum(n, 1).astype(F32)
    large = max_exact + (jnp.log(nf / max_exact) / math.log(REL_MAX_DIST / max_exact)
                         * (REL_BUCKETS - max_exact)).astype(jnp.int32)
    large = jnp.minimum(large, REL_BUCKETS - 1)
    return jnp.where(n < max_exact, n, large)


def _bias_tiles(rel_bias, blk):
    r = jnp.arange(blk, dtype=jnp.int32)
    far = rel_bias[REL_BUCKETS - 1]
    rel0 = r[None, :] - r[:, None]
    rel1 = rel0 + blk
    buckets = jnp.arange(REL_BUCKETS, dtype=jnp.int32)

    def tile(rel):
        hot = (_t5_bucket(rel)[:, :, None] == buckets).astype(F32)
        return jnp.einsum('krb,bh->hkr', hot, rel_bias, precision=lax.Precision.HIGHEST)

    b0 = jnp.where(rel0[None] >= 0, (tile(rel0) - far[:, None, None]) * LOG2_E, NEG_BIG)
    b1 = (tile(rel1) - far[:, None, None]) * LOG2_E
    return b0, b1


def _attn_kernel(q_ref, k_ref, vt_ref, b0_ref, b1_ref, lq1_ref, lk1_ref, lq2_ref, lk2_ref,
                 sw_ref, o_ref, m_ref, acc_ref):
    blk = q_ref.shape[0]
    hw = 2 * DIFF_HEAD_DIM
    i = pl.program_id(2)
    lane = lax.broadcasted_iota(jnp.int32, (1, LANES), 1)
    qm = []
    for hd in range(ATT_HEADS):
        q = (q_ref[:, hd * hw:(hd + 1) * hw].astype(F32) * (DIFF_HEAD_DIM ** -0.5 * LOG2_E)).astype(BF16)
        zero = jnp.zeros_like(q)
        qm.append((jnp.where(lane < DIFF_HEAD_DIM, q, zero), jnp.where(lane >= DIFF_HEAD_DIM, q, zero)))

    m_ref[...] = jnp.full_like(m_ref, NEG_BIG)
    acc_ref[...] = jnp.zeros_like(acc_ref)

    def step(blocks):
        kbs = [k_ref[pl.ds(pl.multiple_of(j * blk, blk), blk), :] for j, _, _ in blocks]
        chains = [(hd, mi, qs) for hd in range(ATT_HEADS) for mi in range(2)
                  for qs in range(blk // ATT_STRIP)]
        scores = []
        for hd, mi, qs in chains:
            qc = slice(qs * ATT_STRIP, (qs + 1) * ATT_STRIP)
            row = []
            for (_, bias, diagonal), kb in zip(blocks, kbs):
                keys = (qs + 1) * ATT_STRIP if diagonal else blk
                s = lax.dot_general(kb[:keys, hd * hw:(hd + 1) * hw], qm[hd][mi][qc, :],
                                    (((1,), (1,)), ((), ())), preferred_element_type=F32)
                row.append(s if bias is None else s + bias[hd, :keys, qc])
            scores.append(row)
        stats = []
        for (hd, mi, qs), row in zip(chains, scores):
            qc = slice(qs * ATT_STRIP, (qs + 1) * ATT_STRIP)
            m_old = m_ref[hd, mi, :, qc]
            m_new = m_old
            for s in row:
                m_new = jnp.maximum(m_new, jnp.max(s, axis=0, keepdims=True))
            stats.append((jnp.exp2(m_old - m_new), [jnp.exp2(s - m_new).astype(BF16) for s in row], m_new))
        for (hd, mi, qs), (alpha, ps, m_new) in zip(chains, stats):
            qc = slice(qs * ATT_STRIP, (qs + 1) * ATT_STRIP)
            pv = None
            for (j, _, _), p in zip(blocks, ps):
                part = jnp.dot(vt_ref[0, hd, j, :, 0:p.shape[0]], p, preferred_element_type=F32)
                pv = part if pv is None else pv + part
            acc_ref[hd, mi, :, qc] = alpha * acc_ref[hd, mi, :, qc] + pv
            m_ref[hd, mi, :, qc] = m_new

    n_far = jnp.maximum(i - 1, 0)

    def far_pair(pair, carry):
        step([(2 * pair, None, False), (2 * pair + 1, None, False)])
        return carry

    lax.fori_loop(0, lax.shift_right_logical(n_far, 1), far_pair, 0)

    @pl.when(lax.rem(n_far, 2) == 1)
    def _():
        step([(n_far - 1, None, False)])

    @pl.when(i >= 1)
    def _():
        step([(i - 1, b1_ref, False), (i, b0_ref, True)])

    @pl.when(i == 0)
    def _():
        step([(i, b0_ref, True)])

    lam = (jnp.exp(jnp.sum(lq1_ref[...] * lk1_ref[...], axis=-1, keepdims=True))
           - jnp.exp(jnp.sum(lq2_ref[...] * lk2_ref[...], axis=-1, keepdims=True))
           + LAMBDA_INIT)
    for hd in range(ATT_HEADS):
        a = (acc_ref[hd, 0, :hw, :] / acc_ref[hd, 0, hw:hw + 1, :]
             - lam * (acc_ref[hd, 1, :hw, :] / acc_ref[hd, 1, hw:hw + 1, :]))
        a = a * lax.rsqrt(jnp.mean(a * a, axis=0, keepdims=True) + NORM_EPS)
        o_ref[:, hd * hw:(hd + 1) * hw] = (a.T * sw_ref[...] * (1.0 - LAMBDA_INIT)).astype(o_ref.dtype)


def _diff_attention(q, k, vt, rel_bias, lq1, lk1, lq2, lk2, subln_w, batch, seq):
    t = q.shape[0]
    blk = min(ATT_BLOCK, seq)
    nq = seq // blk
    hw = 2 * DIFF_HEAD_DIM
    nh = ATT_HEADS
    b0, b1 = _bias_tiles(rel_bias, blk)
    small = lambda a: pl.BlockSpec(a.shape, lambda b, h, i: (0,) * a.ndim)
    return pl.pallas_call(
        _attn_kernel,
        grid=(batch, DIFF_HEADS // nh, nq),
        in_specs=[pl.BlockSpec((blk, nh * hw), lambda b, h, i: (b * nq + i, h)),
                  pl.BlockSpec((seq, nh * hw), lambda b, h, i: (b, h)),
                  pl.BlockSpec((1, nh, nq, hw + ONES_ROWS, blk), lambda b, h, i: (b, h, 0, 0, 0)),
                  pl.BlockSpec((nh, blk, blk), lambda b, h, i: (h, 0, 0)),
                  pl.BlockSpec((nh, blk, blk), lambda b, h, i: (h, 0, 0)),
                  small(lq1), small(lk1), small(lq2), small(lk2), small(subln_w)],
        out_specs=pl.BlockSpec((blk, nh * hw), lambda b, h, i: (b * nq + i, h)),
        out_shape=jax.ShapeDtypeStruct((t, DIFF_V_W), BF16),
        scratch_shapes=[pltpu.VMEM((nh, 2, 1, blk), F32), pltpu.VMEM((nh, 2, hw + ONES_ROWS, blk), F32)],
        compiler_params=pltpu.CompilerParams(
            dimension_semantics=("arbitrary", "arbitrary", "arbitrary"),
            vmem_limit_bytes=VMEM_LIMIT),
        name="diff_attn",
    )(q, k, vt, b0, b1, lq1, lk1, lq2, lk2, subln_w)


def _out_kernel(yr_ref, yd_ref, x_ref, mod_ref, nw_ref, wo_ref, wr_ref, br_ref, upper_ref,
                x1_ref, hp_ref, meta_ref, gate_ref, cnt_ref, run_ref):
    tm = x_ref.shape[0]
    ne = run_ref.shape[0]

    @pl.when(pl.program_id(0) == 0)
    def _():
        run_ref[...] = jnp.zeros_like(run_ref)

    mixed = (jnp.dot(yr_ref[...], wo_ref[0:RET_V_W, :], preferred_element_type=F32)
             + jnp.dot(yd_ref[...], wo_ref[RET_V_W:, :], preferred_element_type=F32))
    gate1 = mod_ref[0, 2:3, :]
    shift2 = mod_ref[0, 3:4, :]
    scale2 = mod_ref[0, 4:5, :]
    x1 = x_ref[...] + gate1 * mixed
    x1_ref[...] = x1
    h2 = (_rms(x1) * nw_ref[...]) * (1.0 + scale2) + shift2
    hp_ref[...] = _pack_halves(h2)

    h_hi = h2.astype(BF16)
    h_lo = (h2 - h_hi.astype(F32)).astype(BF16)
    nt_dims = (((1,), (1,)), ((), ()))
    both = lax.dot_general(wr_ref[...], h_hi, nt_dims, preferred_element_type=F32)
    low = lax.dot_general(wr_ref[0:ne, :], h_lo, nt_dims, preferred_element_type=F32)
    logits = both[0:ne, :] + both[ne:, :] + low + br_ref[...]

    row = lax.broadcasted_iota(jnp.int32, logits.shape, 0)
    work = logits
    vals, idxs, hots = [], [], []
    for _ in range(TOP_K):
        mx = jnp.max(work, axis=0, keepdims=True)
        idx = jnp.min(jnp.where(work == mx, row, ne), axis=0, keepdims=True)
        hot = row == idx
        vals.append(mx)
        idxs.append(idx)
        hots.append(hot)
        work = jnp.where(hot, -jnp.inf, work)
    exps = [jnp.exp(v - vals[0]) for v in vals]
    denom = exps[0] + exps[1] + exps[2] + exps[3]

    sel = jnp.zeros(logits.shape, F32)
    for hot in hots:
        sel = sel + hot.astype(F32)
    prefix = jnp.dot(sel.astype(BF16), upper_ref[...], preferred_element_type=F32) + run_ref[...]
    ranks = [jnp.sum(jnp.where(hot, prefix, 0.0), axis=0, keepdims=True) for hot in hots]
    run_ref[...] = run_ref[...] + jnp.sum(sel, axis=1, keepdims=True)
    cnt_ref[...] = jnp.broadcast_to(run_ref[...], cnt_ref.shape).astype(jnp.int32)

    meta_ref[...] = jnp.concatenate(idxs + [r.astype(jnp.int32) for r in ranks], axis=0)
    gate_ref[...] = jnp.concatenate([e / denom for e in exps] + [jnp.zeros_like(denom)] * TOP_K, axis=0)


def _out_router(y_r, y_d, x2, mod, norm_w, w_out_bf16, w_router, b_router, seq):
    t, d = x2.shape
    tm = min(ROW_TILE, seq)
    per_batch = seq // tm
    ne = w_router.shape[1]
    w_hi = w_router.astype(BF16)
    w_lo = (w_router - w_hi.astype(F32)).astype(BF16)
    wr_t = jnp.concatenate([w_hi, w_lo], axis=1).T
    idx = jnp.arange(tm, dtype=jnp.int32)
    upper = (idx[:, None] < idx[None, :]).astype(BF16)
    row = lambda w: pl.BlockSpec((tm, w), lambda i: (i, 0))
    col = lambda h: pl.BlockSpec((h, tm), lambda i: (0, i))
    const = lambda a: pl.BlockSpec(a.shape, lambda i: (0,) * a.ndim)
    x1, hp, meta_t, gates_t, counts = pl.pallas_call(
        _out_kernel,
        grid=(t // tm,),
        in_specs=[row(RET_V_W), row(DIFF_V_W), row(d),
                  pl.BlockSpec((1, 6, d), lambda i: (i // per_batch, 0, 0)),
                  const(norm_w), const(w_out_bf16), const(wr_t), const(b_router), const(upper)],
        out_specs=[row(d), row(d // 2), col(2 * TOP_K), col(2 * TOP_K),
                   pl.BlockSpec((ne, LANES), lambda i: (0, 0))],
        out_shape=[jax.ShapeDtypeStruct((t, d), F32),
                   jax.ShapeDtypeStruct((t, d // 2), jnp.uint32),
                   jax.ShapeDtypeStruct((2 * TOP_K, t), jnp.int32),
                   jax.ShapeDtypeStruct((2 * TOP_K, t), F32),
                   jax.ShapeDtypeStruct((ne, LANES), jnp.int32)],
        scratch_shapes=[pltpu.VMEM((ne, 1), F32)],
        compiler_params=pltpu.CompilerParams(
            dimension_semantics=("arbitrary",), vmem_limit_bytes=VMEM_LIMIT),
        name="out_router",
    )(y_r, y_d, x2, mod, norm_w, w_out_bf16, wr_t, b_router, upper)
    return x1, hp, meta_t, gates_t.T, counts[:, 0]


def _pair_split_bias(b1):
    e, f2 = b1.shape
    nt = f2 // PERM_TILE
    g = b1[:, 0::2].reshape(e, nt, 1, PERM_TILE // 2)
    l = b1[:, 1::2].reshape(e, nt, 1, PERM_TILE // 2)
    return jnp.concatenate([g, l], axis=2).reshape(e, 1, f2)


def _slot_tokens(dest_t, n_rows):
    n_tok = dest_t.shape[1]
    assert n_tok & (n_tok - 1) == 0
    flat = dest_t.reshape(-1)
    n_slots = flat.shape[0]
    chunk = min(SC_SLOT_CHUNK, n_slots)
    mesh = plsc.VectorSubcoreMesh(core_axis_name="c", subcore_axis_name="s",
                                  num_cores=SC_CORES, num_subcores=SC_SUBCORES)

    def body(dest_hbm, out_hbm, out_v, dest_v):
        first = jnp.logical_and(lax.axis_index("c") == 0, lax.axis_index("s") == 0)

        @pl.when(first)
        def _():
            zeros = jnp.zeros((SC_LANES,), jnp.int32)

            @plsc.parallel_loop(0, n_rows // SC_LANES, unroll=SC_UNROLL)
            def _(g):
                out_v[pl.ds(g * SC_LANES, SC_LANES)] = zeros

            lane = lax.iota(jnp.int32, SC_LANES)

            @pl.loop(0, n_slots // chunk)
            def _(c):
                pltpu.sync_copy(dest_hbm.at[pl.ds(c * chunk, chunk)], dest_v)

                @plsc.parallel_loop(0, chunk // SC_LANES, unroll=SC_UNROLL)
                def _(g):
                    rows = dest_v[pl.ds(g * SC_LANES, SC_LANES)]
                    slot = lane + (c * chunk + g * SC_LANES)
                    plsc.store_scatter(out_v, [rows], slot & (n_tok - 1))

            pltpu.sync_copy(out_v, out_hbm)

    return pl.kernel(
        body,
        out_type=jax.ShapeDtypeStruct((n_rows,), jnp.int32),
        mesh=mesh,
        scratch_types=[pltpu.VMEM((n_rows,), jnp.int32), pltpu.VMEM((chunk,), jnp.int32)],
        compiler_params=pltpu.CompilerParams(needs_layout_passes=False),
        name="slot_tokens",
    )(flat)


def _expert_kernel(be_ref, nv_ref, nexte_ref, tcur_ref, tnxt_ref, hp_ref, w1_ref, b1_ref, w2_ref, b2_ref,
                   o_ref, hbuf, xb0, xb1, w1buf, w2buf, w2b, sem, wsems):
    b = pl.program_id(0)
    nv = nv_ref[0]
    bm = xb0.shape[0]

    def gather(tok_ref, dst):
        for r in range(bm):
            dst[pl.ds(r, 1), :] = hbuf[pl.ds(tok_ref[0, 0, r], 1), :]

    def weight_copies(e, slot):
        return (pltpu.make_async_copy(w1_ref.at[e], w1buf.at[slot], wsems.at[0, slot]),
                pltpu.make_async_copy(w2_ref.at[e], w2buf.at[slot], wsems.at[1, slot]))

    e = be_ref[b]
    prev_e = be_ref[jnp.maximum(b - 1, 0)]
    first_of_expert = jnp.logical_and(b < nv, jnp.logical_or(b == 0, e != prev_e))
    slot = lax.rem(nexte_ref[N_EXPERTS + e], 2)

    @pl.when(b == 0)
    def _():
        for cp in weight_copies(e, slot):
            cp.start()
        load = pltpu.make_async_copy(hp_ref, hbuf, sem)
        load.start()
        load.wait()
        gather(tcur_ref, xb0)

    @pl.when(first_of_expert)
    def _():
        for cp in weight_copies(e, slot):
            cp.wait()
        nxt = nexte_ref[e]

        @pl.when(nxt >= 0)
        def _():
            for cp in weight_copies(nxt, 1 - slot):
                cp.start()

        w2b[...] = w2buf[slot].astype(BF16)

    def run(cur, nxt_rows):
        x = _unpack_halves(cur[...])
        half = PERM_TILE // 2
        acts = []
        for j in range(w1buf.shape[2] // PERM_TILE):
            cols = slice(j * PERM_TILE, (j + 1) * PERM_TILE)
            h = jnp.dot(x, w1buf[slot, :, cols], preferred_element_type=F32) + b1_ref[0, :, cols]
            glu = jnp.minimum(h[:, :half], SWIGLU_LIMIT)
            lin = jnp.clip(h[:, half:], -SWIGLU_LIMIT, SWIGLU_LIMIT)
            acts.append((glu * jax.nn.sigmoid(SWIGLU_ALPHA * glu) * (lin + 1.0)).astype(BF16))
        act = jnp.concatenate(acts, axis=1)
        y = jnp.dot(act, w2b[...], preferred_element_type=F32) + b2_ref[0]
        _store_row_tiles(o_ref, y)
        gather(tnxt_ref, nxt_rows)

    even = lax.rem(b, 2) == 0

    @pl.when(jnp.logical_and(b < nv, even))
    def _():
        run(xb0, xb1)

    @pl.when(jnp.logical_and(b < nv, jnp.logical_not(even)))
    def _():
        run(xb1, xb0)

    @pl.when(b >= nv)
    def _():
        o_ref[...] = jnp.zeros_like(o_ref)


def _experts(block_e, n_valid, next_e, slot_tok, hp, w1p, b1p, w2, b2):
    d = D_MODEL
    f2 = w1p.shape[2]
    f = w2.shape[1]
    bm = EXPERT_ROWS
    nb = slot_tok.shape[0] // bm
    tok3 = slot_tok.reshape(nb, 1, bm)
    exp = lambda b, be, nv, ne: be[jnp.maximum(jnp.minimum(b, nv[0] - 1), 0)]
    grid_spec = pltpu.PrefetchScalarGridSpec(
        num_scalar_prefetch=3,
        grid=(nb,),
        in_specs=[pl.BlockSpec((1, 1, bm), lambda b, be, nv, ne: (b, 0, 0), memory_space=pltpu.SMEM),
                  pl.BlockSpec((1, 1, bm), lambda b, be, nv, ne: (jnp.minimum(b + 1, nb - 1), 0, 0),
                               memory_space=pltpu.SMEM),
                  pl.BlockSpec(memory_space=pl.ANY),
                  pl.BlockSpec(memory_space=pl.ANY),
                  pl.BlockSpec((1, 1, f2), lambda b, be, nv, ne: (exp(b, be, nv, ne), 0, 0)),
                  pl.BlockSpec(memory_space=pl.ANY),
                  pl.BlockSpec((1, 1, d), lambda b, be, nv, ne: (exp(b, be, nv, ne), 0, 0))],
        out_specs=pl.BlockSpec((bm * ROW_SUB, LANES), lambda b, be, nv, ne: (b, 0)),
        scratch_shapes=[pltpu.VMEM(hp.shape, hp.dtype), pltpu.VMEM((bm, hp.shape[1]), hp.dtype),
                        pltpu.VMEM((bm, hp.shape[1]), hp.dtype),
                        pltpu.VMEM((2,) + w1p.shape[1:], w1p.dtype), pltpu.VMEM((2,) + w2.shape[1:], w2.dtype),
                        pltpu.VMEM(w2.shape[1:], BF16),
                        pltpu.SemaphoreType.DMA(()), pltpu.SemaphoreType.DMA((2, 2))],
    )
    return pl.pallas_call(
        _expert_kernel,
        grid_spec=grid_spec,
        out_shape=jax.ShapeDtypeStruct((nb * bm * ROW_SUB, LANES), F32),
        compiler_params=pltpu.CompilerParams(
            dimension_semantics=("arbitrary",), vmem_limit_bytes=VMEM_LIMIT),
        name="experts",
    )(block_e, n_valid, next_e, tok3, tok3, hp, w1p, b1p, w2, b2)


def _combine_kernel(dcur_ref, dnxt_ref, gate_ref, x1_ref, mod_ref, nf_ref, y_ref, o_ref, buf, sems):
    nt = x1_ref.shape[0]
    i = pl.program_id(0)
    n = pl.num_programs(0)
    slot = lax.rem(i, 2)

    def row_copy(src_row, s, kk, r):
        return pltpu.make_async_copy(y_ref.at[pl.ds(src_row * ROW_SUB, ROW_SUB), :],
                                     buf.at[s, kk, pl.ds(r * ROW_SUB, ROW_SUB), :], sems.at[s])

    def fetch(d_ref, s):
        for j in range(nt):
            for kk in range(TOP_K):
                row_copy(d_ref[0, 0, kk * nt + j], s, kk, j).start(priority=kk % DMA_QUEUES)

    def wait_all(s):
        for kk in range(TOP_K):
            pltpu.make_async_copy(y_ref.at[pl.ds(0, nt * ROW_SUB), :], buf.at[s, kk], sems.at[s]).wait()

    @pl.when(i == 0)
    def _():
        fetch(dcur_ref, 0)

    fetch(dnxt_ref, 1 - slot)
    wait_all(slot)

    g = gate_ref[...]
    parts = []
    for s in range(ROW_SUB):
        acc = buf[slot, 0, pl.ds(s, nt, stride=ROW_SUB), :] * g[:, 0:1]
        for kk in range(1, TOP_K):
            acc = acc + buf[slot, kk, pl.ds(s, nt, stride=ROW_SUB), :] * g[:, kk:kk + 1]
        parts.append(acc)
    moe = jnp.concatenate(parts, axis=1)
    gate2 = mod_ref[0, 5:6, :]
    x2 = x1_ref[...] + gate2 * moe
    o_ref[...] = _rms(x2) * nf_ref[...]

    @pl.when(i == n - 1)
    def _():
        wait_all(1 - slot)


def _combine(dest_t, gates, x1, mod, normf_w, y, seq):
    t, d = x1.shape
    nt = min(COMBINE_TOKENS, seq)
    steps = t // nt
    per_batch = seq // nt
    dest2 = jnp.transpose(dest_t.reshape(TOP_K, steps, nt), (1, 0, 2)).reshape(steps, 1, nt * TOP_K)
    return pl.pallas_call(
        _combine_kernel,
        grid=(steps,),
        in_specs=[pl.BlockSpec((1, 1, nt * TOP_K), lambda i: (i, 0, 0), memory_space=pltpu.SMEM),
                  pl.BlockSpec((1, 1, nt * TOP_K), lambda i: (jnp.minimum(i + 1, steps - 1), 0, 0),
                               memory_space=pltpu.SMEM),
                  pl.BlockSpec((nt, 2 * TOP_K), lambda i: (i, 0)),
                  pl.BlockSpec((nt, d), lambda i: (i, 0)),
                  pl.BlockSpec((1, 6, d), lambda i: (i // per_batch, 0, 0)),
                  pl.BlockSpec((1, d), lambda i: (0, 0)),
                  pl.BlockSpec(memory_space=pl.ANY)],
        out_specs=pl.BlockSpec((nt, d), lambda i: (i, 0)),
        out_shape=jax.ShapeDtypeStruct((t, d), F32),
        scratch_shapes=[pltpu.VMEM((2, TOP_K, nt * ROW_SUB, LANES), F32),
                        pltpu.SemaphoreType.DMA((2,))],
        compiler_params=pltpu.CompilerParams(
            dimension_semantics=("arbitrary",), vmem_limit_bytes=VMEM_LIMIT),
        name="combine",
    )(dest2, dest2, gates, x1, mod, normf_w, y)


def kernel(x, c, w_ada, b_ada, norm1_w, w_in, lam_q1, lam_k1, lam_q2, lam_k2, subln_w, rel_bias,
           w_out, norm2_w, w_router, b_router, w1, b1, w2, b2, normf_w):
    batch, seq, d = x.shape
    t = batch * seq
    x2 = x.reshape(t, d)

    c_pad = jnp.zeros((8, d), F32).at[:batch].set(c)
    mod = _ada(c_pad, w_ada[0], b_ada[0][None, :])[:batch].reshape(batch, 6, d)

    q_r, k_r, v_r, g_r, q_d, k_d, vt_d, w1p = _in_proj(x2, mod, norm1_w[0][None, :],
                                                      w_in[0].astype(BF16), w1[0], seq)
    y_r = _retention(q_r, k_r, v_r, g_r, batch, seq)
    y_d = _diff_attention(q_d, k_d, vt_d, rel_bias, lam_q1, lam_k1, lam_q2, lam_k2,
                          subln_w, batch, seq)

    x1, hp, meta, gates, counts = _out_router(y_r, y_d, x2, mod, norm2_w[0][None, :],
                                              w_out[0].astype(BF16), w_router[0],
                                              b_router[0][:, None], seq)

    bm = EXPERT_ROWS
    padded = (counts + bm - 1) // bm * bm
    pad_end = jnp.cumsum(padded)
    pad_start = pad_end - padded
    hot_e = meta[None, :TOP_K, :] == jnp.arange(N_EXPERTS, dtype=jnp.int32)[:, None, None]
    dest_t = jnp.sum(jnp.where(hot_e, pad_start[:, None, None], 0), axis=0) + meta[TOP_K:, :]
    n_rows = (t * TOP_K // bm + N_EXPERTS) * bm
    nb = n_rows // bm
    block_start = jnp.arange(nb, dtype=jnp.int32) * bm
    block_e = jnp.minimum(jnp.sum((pad_end[None, :] <= block_start[:, None]).astype(jnp.int32), axis=1),
                          N_EXPERTS - 1)
    n_valid = (pad_end[-1:] // bm).astype(jnp.int32)
    slot_tok = _slot_tokens(dest_t, n_rows)

    owns = padded > 0
    ids = jnp.arange(N_EXPERTS, dtype=jnp.int32)
    later = jnp.logical_and(owns[None, :], ids[None, :] > ids[:, None])
    next_owner = jnp.min(jnp.where(later, ids[None, :], N_EXPERTS), axis=1)
    next_owner = jnp.where(next_owner == N_EXPERTS, -1, next_owner).astype(jnp.int32)
    ordinal = (jnp.cumsum(owns.astype(jnp.int32)) - 1).astype(jnp.int32)
    next_e = jnp.concatenate([next_owner, jnp.maximum(ordinal, 0)])

    ys = _experts(block_e, n_valid, next_e, slot_tok, hp, w1p, _pair_split_bias(b1[0]),
                  w2[0], b2[0][:, None, :])
    out = _combine(dest_t, gates, x1, mod, normf_w[None, :], ys, seq)
    return out.reshape(batch, seq, d)
```

```python
import math

import jax
import jax.numpy as jnp
from jax import lax
from jax.experimental import pallas as pl
from jax.experimental.pallas import tpu as pltpu
from jax.experimental.pallas import tpu_sc as plsc

F32 = jnp.float32
BF16 = jnp.bfloat16

D_MODEL = 1024
RET_HEADS = 4
RET_KEY_DIM = 64
RET_VAL_DIM = 128
RET_QK_W = RET_HEADS * RET_KEY_DIM
RET_V_W = RET_HEADS * RET_VAL_DIM
RET_CHUNK = 128
DIFF_HEADS = 4
DIFF_HEAD_DIM = 64
DIFF_QK_W = DIFF_HEADS * 2 * DIFF_HEAD_DIM
DIFF_V_W = DIFF_HEADS * 2 * DIFF_HEAD_DIM
IN_SIZES = (RET_QK_W, RET_QK_W, RET_V_W, RET_V_W, DIFF_QK_W, DIFF_QK_W, DIFF_V_W)
REL_BUCKETS = 32
REL_MAX_DIST = 128
N_EXPERTS = 32
TOP_K = 4
SWIGLU_ALPHA = 1.702
SWIGLU_LIMIT = 7.0
NORM_EPS = 1e-6
LAMBDA_INIT = 0.8 - 0.6 * math.exp(-0.3 * 0)

LANES = 128
ROW_SUB = D_MODEL // LANES
NEG_BIG = -1e30
LOG2_E = math.log2(math.e)
ONES_ROWS = 16
VMEM_LIMIT = 56 * 1024 * 1024

ROW_TILE = 512
RET_ROWS = 512
ATT_BLOCK = 512
ATT_STRIP = 256
ATT_HEADS = 4
EXPERT_ROWS = 256
PERM_TILE = 256
COMBINE_TOKENS = 256
DMA_QUEUES = 2
SC_CORES = 2
SC_SUBCORES = 16
SC_LANES = 16
SC_UNROLL = 8
SC_SLOT_CHUNK = 32768


def _rms(x):
    return x * lax.rsqrt(jnp.mean(x * x, axis=-1, keepdims=True) + NORM_EPS)


def _store_row_tiles(ref, x):
    rows = x.shape[0]
    for s in range(ROW_SUB):
        ref[pl.ds(s, rows, stride=ROW_SUB), :] = x[:, s * LANES:(s + 1) * LANES]


def _pack_halves(x):
    half = x.shape[1] // 2
    lo = lax.bitcast_convert_type(x[:, :half].astype(BF16).astype(F32), jnp.uint32)
    hi = lax.bitcast_convert_type(x[:, half:].astype(BF16).astype(F32), jnp.uint32)
    return (hi & jnp.uint32(0xFFFF0000)) | (lo >> 16)


def _unpack_halves(w):
    lo = lax.bitcast_convert_type(w << 16, F32).astype(BF16)
    hi = lax.bitcast_convert_type(w & jnp.uint32(0xFFFF0000), F32).astype(BF16)
    return jnp.concatenate([lo, hi], axis=1)


def _ada_kernel(c_ref, w_ref, b_ref, o_ref):
    c = c_ref[...]
    cond = c * jax.nn.sigmoid(c)
    o_ref[...] = jnp.dot(cond, w_ref[...], precision=lax.Precision.HIGHEST,
                         preferred_element_type=F32) + b_ref[...]


def _ada(c_pad, w_ada, b_ada):
    rows, d = c_pad.shape
    n = w_ada.shape[1]
    tn = 1024
    return pl.pallas_call(
        _ada_kernel,
        grid=(n // tn,),
        in_specs=[pl.BlockSpec((rows, d), lambda j: (0, 0)),
                  pl.BlockSpec((d, tn), lambda j: (0, j)),
                  pl.BlockSpec((1, tn), lambda j: (0, j))],
        out_specs=pl.BlockSpec((rows, tn), lambda j: (0, j)),
        out_shape=jax.ShapeDtypeStruct((rows, n), F32),
        name="ada",
    )(c_pad, w_ada, b_ada)


def _in_proj_kernel(x_ref, mod_ref, nw_ref, w_ref, w1_ref, perm_ref, *o_refs):
    o_refs, w1p_ref = o_refs[:-1], o_refs[-1]
    _split_w1_tiles(w1_ref, perm_ref, w1p_ref)
    x = x_ref[...]
    shift = mod_ref[0, 0:1, :]
    scale = mod_ref[0, 1:2, :]
    h = (_rms(x) * nw_ref[...]) * (1.0 + scale) + shift
    hb = h.astype(BF16)
    off = 0
    for o_ref, width in zip(o_refs[:-1], IN_SIZES[:-1]):
        o_ref[...] = jnp.dot(hb, w_ref[:, off:off + width],
                             preferred_element_type=F32).astype(o_ref.dtype)
        off += width
    vt_ref = o_refs[-1]
    v = jnp.dot(hb, w_ref[:, off:], preferred_element_type=F32)
    hw = 2 * DIFF_HEAD_DIM
    tail = (lax.broadcasted_iota(jnp.int32, (ONES_ROWS, v.shape[0]), 0) == 0).astype(vt_ref.dtype)
    for hd in range(DIFF_HEADS):
        vt_ref[0, hd, 0, 0:hw, :] = v[:, hd * hw:(hd + 1) * hw].T.astype(vt_ref.dtype)
        vt_ref[0, hd, 0, hw:, :] = tail


def _in_proj(x2, mod, norm_w, w_in_bf16, w1, seq):
    t, d = x2.shape
    tm = min(ROW_TILE, seq)
    assert tm == min(ATT_BLOCK, seq)
    per_batch = seq // tm
    in_w = w_in_bf16.shape[1]
    hw = 2 * DIFF_HEAD_DIM
    vt_shape = (t // seq, DIFF_HEADS, per_batch, hw + ONES_ROWS, tm)
    perm = _pair_split_matrix()
    per_step, rest = divmod(w1.shape[0], t // tm)
    assert rest == 0 and per_step >= 1
    expert = pl.BlockSpec((per_step,) + w1.shape[1:], lambda i: (i, 0, 0))
    return pl.pallas_call(
        _in_proj_kernel,
        grid=(t // tm,),
        in_specs=[pl.BlockSpec((tm, d), lambda i: (i, 0)),
                  pl.BlockSpec((1, 6, d), lambda i: (i // per_batch, 0, 0)),
                  pl.BlockSpec((1, d), lambda i: (0, 0)),
                  pl.BlockSpec((d, in_w), lambda i: (0, 0)),
                  expert, pl.BlockSpec(perm.shape, lambda i: (0, 0))],
        out_specs=([pl.BlockSpec((tm, w), lambda i: (i, 0)) for w in IN_SIZES[:-1]]
                   + [pl.BlockSpec((1,) + vt_shape[1:2] + (1,) + vt_shape[3:],
                                   lambda i: (i // per_batch, 0, i % per_batch, 0, 0)), expert]),
        out_shape=([jax.ShapeDtypeStruct((t, w), BF16) for w in IN_SIZES[:-1]]
                   + [jax.ShapeDtypeStruct(vt_shape, BF16), jax.ShapeDtypeStruct(w1.shape, BF16)]),
        compiler_params=pltpu.CompilerParams(vmem_limit_bytes=VMEM_LIMIT),
        name="in_proj",
    )(x2, mod, norm_w, w_in_bf16, w1, perm)


def _split_w1_tiles(w_ref, p_ref, o_ref):
    for e in range(w_ref.shape[0]):
        for s in range(w_ref.shape[2] // PERM_TILE):
            cols = slice(s * PERM_TILE, (s + 1) * PERM_TILE)
            o_ref[e, :, cols] = jnp.dot(w_ref[e, :, cols].astype(BF16), p_ref[...],
                                        preferred_element_type=F32).astype(BF16)


def _pair_split_matrix():
    i = jnp.arange(PERM_TILE)[:, None]
    j = jnp.arange(PERM_TILE)[None, :]
    half = PERM_TILE // 2
    src = jnp.where(j < half, 2 * j, 2 * (j - half) + 1)
    return (i == src).astype(BF16)


def _rotary(x, cos, sin_even, sin_odd):
    nxt = pltpu.roll(x, LANES - 1, 1)
    prv = pltpu.roll(x, 1, 1)
    return x * cos + nxt * sin_even + prv * sin_odd


def _ret_kernel(q_ref, k_ref, v_ref, g_ref, cos_ref, sine_ref, sino_ref,
                hmask_ref, xi_ref, zeta_ref, dmask_ref, gch_ref, o_ref, state_ref):
    @pl.when(pl.program_id(1) == 0)
    def _():
        state_ref[...] = jnp.zeros_like(state_ref)

    n_sub = q_ref.shape[0] // RET_CHUNK
    for c in range(n_sub):
        rows = slice(c * RET_CHUNK, (c + 1) * RET_CHUNK)
        for pair in range(RET_HEADS // 2):
            lanes = slice(pair * LANES, (pair + 1) * LANES)
            cos = cos_ref[rows, :]
            sine = sine_ref[rows, :]
            sino = sino_ref[rows, :]
            qr = _rotary(q_ref[rows, lanes].astype(F32), cos, sine, sino)
            kr = _rotary(k_ref[rows, lanes].astype(F32), cos, sine, sino) * (RET_KEY_DIM ** -0.5)
            qb = qr.astype(BF16)
            for hh in range(2):
                h = 2 * pair + hh
                vcols = slice(h * RET_VAL_DIM, (h + 1) * RET_VAL_DIM)
                v = v_ref[rows, vcols]
                km = (kr * hmask_ref[h]).astype(BF16)
                scores = lax.dot_general(qb, km, (((1,), (1,)), ((), ())),
                                         preferred_element_type=F32) * dmask_ref[h]
                inner = jnp.dot(scores.astype(BF16), v, preferred_element_type=F32)
                qx = (qr * xi_ref[h]).astype(BF16)
                state = state_ref[h]
                cross = jnp.dot(qx, state.astype(BF16), preferred_element_type=F32)
                kz = (kr * zeta_ref[h]).astype(BF16)
                kv = lax.dot_general(kz, v, (((0,), (0,)), ((), ())),
                                     preferred_element_type=F32)
                state_ref[h] = state * gch_ref[h] + kv
                y = _rms(inner + cross)
                g = g_ref[rows, vcols].astype(F32)
                o_ref[rows, vcols] = (g * jax.nn.sigmoid(g) * y).astype(o_ref.dtype)


def _retention_tables(seq):
    dk, c, nh = RET_KEY_DIM, RET_CHUNK, RET_HEADS
    pos = jnp.arange(seq, dtype=F32)
    inv_freq = 1.0 / (10000.0 ** jnp.linspace(0.0, 1.0, dk // 2, dtype=F32))
    ang = pos[:, None] * jnp.repeat(inv_freq, 2)[None, :]
    sin = jnp.tile(jnp.sin(ang), (1, LANES // dk))
    cos = jnp.tile(jnp.cos(ang), (1, LANES // dk))
    even = (jnp.arange(LANES) % 2 == 0)[None, :]
    sin_even = jnp.where(even, -sin, 0.0)
    sin_odd = jnp.where(even, 0.0, sin)
    log_g = jnp.log(1.0 - 2.0 ** (-5.0 - jnp.arange(nh, dtype=F32)))
    i = jnp.arange(c, dtype=F32)
    rel = i[:, None] - i[None, :]
    dmask = jnp.where(rel[None] >= 0,
                      jnp.exp(jnp.maximum(rel, 0.0)[None] * log_g[:, None, None]), 0.0)
    zeta = jnp.exp((c - 1.0 - i)[None, :] * log_g[:, None])
    xi = jnp.exp((i + 1.0)[None, :] * log_g[:, None])
    g_chunk = jnp.exp(c * log_g)
    lane = jnp.arange(LANES)
    hmask = jnp.stack([((lane // dk) == (h % 2)).astype(F32) for h in range(nh)])[:, None, :]
    xi_t = xi[:, :, None] * hmask
    zeta_t = zeta[:, :, None] * hmask
    gch = jnp.broadcast_to(g_chunk[:, None, None], (nh, 1, LANES))
    return cos, sin_even, sin_odd, hmask, xi_t, zeta_t, dmask, gch


def _retention(q, k, v, g, batch, seq):
    t = q.shape[0]
    rb = min(RET_ROWS, seq)
    per_batch = seq // rb
    cos, sin_even, sin_odd, hmask, xi_t, zeta_t, dmask, gch = _retention_tables(seq)
    row = lambda w: pl.BlockSpec((rb, w), lambda b, j: (b * per_batch + j, 0))
    tab = lambda w: pl.BlockSpec((rb, w), lambda b, j: (j, 0))
    full = lambda a: pl.BlockSpec(a.shape, lambda b, j: (0,) * a.ndim)
    return pl.pallas_call(
        _ret_kernel,
        grid=(batch, per_batch),
        in_specs=[row(RET_QK_W), row(RET_QK_W), row(RET_V_W), row(RET_V_W),
                  tab(LANES), tab(LANES), tab(LANES),
                  full(hmask), full(xi_t), full(zeta_t), full(dmask), full(gch)],
        out_specs=row(RET_V_W),
        out_shape=jax.ShapeDtypeStruct((t, RET_V_W), BF16),
        scratch_shapes=[pltpu.VMEM((RET_HEADS, LANES, RET_VAL_DIM), F32)],
        compiler_params=pltpu.CompilerParams(
            dimension_semantics=("arbitrary", "arbitrary"), vmem_limit_bytes=VMEM_LIMIT),
        name="retention",
    )(q, k, v, g, cos, sin_even, sin_odd, hmask, xi_t, zeta_t, dmask, gch)


def _t5_bucket(rel):
    n = jnp.maximum(rel, 0)
    max_exact = REL_BUCKETS // 2
    nf = jnp.maximum(n, 1).astype(F32)
    large = max_exact + (jnp.log(nf / max_exact) / math.log(REL_MAX_DIST / max_exact)
                         * (REL_BUCKETS - max_exact)).astype(jnp.int32)
    large = jnp.minimum(large, REL_BUCKETS - 1)
    return jnp.where(n < max_exact, n, large)


def _far_distance():
    n = 0
    for dist in range(2 * REL_MAX_DIST):
        if dist < REL_BUCKETS // 2:
            bucket = dist
        else:
            bucket = min(REL_BUCKETS // 2 + int(math.log(dist / (REL_BUCKETS // 2))
                                                 / math.log(REL_MAX_DIST / (REL_BUCKETS // 2))
                                                 * (REL_BUCKETS - REL_BUCKETS // 2)), REL_BUCKETS - 1)
        if bucket < REL_BUCKETS - 1:
            n = dist + 1
    return n + 1


def _bias_tiles(rel_bias, blk):
    far = rel_bias[REL_BUCKETS - 1]
    buckets = jnp.arange(REL_BUCKETS, dtype=jnp.int32)
    heads = rel_bias.shape[1]
    near = _far_distance()
    sub = min(LANES, blk)
    r = jnp.arange(sub, dtype=jnp.int32)

    def tile(offset, masked):
        rows = []
        for kb in range(blk // sub):
            cols = []
            for rb in range(blk // sub):
                lo = (rb - kb) * sub - (sub - 1) + offset
                hi = (rb - kb) * sub + (sub - 1) + offset
                if masked and hi < 0:
                    cols.append(jnp.full((heads, sub, sub), NEG_BIG, F32))
                elif lo >= near:
                    cols.append(jnp.zeros((heads, sub, sub), F32))
                else:
                    rel = r[None, :] - r[:, None] + ((rb - kb) * sub + offset)
                    hot = (_t5_bucket(rel)[:, :, None] == buckets).astype(F32)
                    val = (jnp.einsum('krb,bh->hkr', hot, rel_bias, precision=lax.Precision.HIGHEST)
                           - far[:, None, None]) * LOG2_E
                    cols.append(jnp.where(rel[None] >= 0, val, NEG_BIG) if masked else val)
            rows.append(jnp.concatenate(cols, axis=2))
        return jnp.concatenate(rows, axis=1)

    return tile(0, True), tile(blk, False)


def _attn_kernel(q_ref, k_ref, vt_ref, b0_ref, b1_ref, lq1_ref, lk1_ref, lq2_ref, lk2_ref,
                 sw_ref, o_ref, m_ref, acc_ref):
    blk = q_ref.shape[0]
    hw = 2 * DIFF_HEAD_DIM
    i = pl.program_id(2)
    lane = lax.broadcasted_iota(jnp.int32, (1, LANES), 1)
    qm = []
    for hd in range(ATT_HEADS):
        q = (q_ref[:, hd * hw:(hd + 1) * hw].astype(F32) * (DIFF_HEAD_DIM ** -0.5 * LOG2_E)).astype(BF16)
        zero = jnp.zeros_like(q)
        qm.append((jnp.where(lane < DIFF_HEAD_DIM, q, zero), jnp.where(lane >= DIFF_HEAD_DIM, q, zero)))

    m_ref[...] = jnp.full_like(m_ref, NEG_BIG)
    acc_ref[...] = jnp.zeros_like(acc_ref)

    def step(blocks):
        kbs = [k_ref[pl.ds(pl.multiple_of(j * blk, blk), blk), :] for j, _, _ in blocks]
        chains = [(hd, mi, qs) for hd in range(ATT_HEADS) for mi in range(2)
                  for qs in range(blk // ATT_STRIP)]
        scores = []
        for hd, mi, qs in chains:
            qc = slice(qs * ATT_STRIP, (qs + 1) * ATT_STRIP)
            row = []
            for (_, bias, diagonal), kb in zip(blocks, kbs):
                keys = (qs + 1) * ATT_STRIP if diagonal else blk
                s = lax.dot_general(kb[:keys, hd * hw:(hd + 1) * hw], qm[hd][mi][qc, :],
                                    (((1,), (1,)), ((), ())), preferred_element_type=F32)
                row.append(s if bias is None else s + bias[hd, :keys, qc])
            scores.append(row)
        stats = []
        for (hd, mi, qs), row in zip(chains, scores):
            qc = slice(qs * ATT_STRIP, (qs + 1) * ATT_STRIP)
            m_old = m_ref[hd, mi, :, qc]
            m_new = m_old
            for s in row:
                m_new = jnp.maximum(m_new, jnp.max(s, axis=0, keepdims=True))
            stats.append((jnp.exp2(m_old - m_new), [jnp.exp2(s - m_new).astype(BF16) for s in row], m_new))
        for (hd, mi, qs), (alpha, ps, m_new) in zip(chains, stats):
            qc = slice(qs * ATT_STRIP, (qs + 1) * ATT_STRIP)
            pv = None
            for (j, _, _), p in zip(blocks, ps):
                part = jnp.dot(vt_ref[0, hd, j, :, 0:p.shape[0]], p, preferred_element_type=F32)
                pv = part if pv is None else pv + part
            acc_ref[hd, mi, :, qc] = alpha * acc_ref[hd, mi, :, qc] + pv
            m_ref[hd, mi, :, qc] = m_new

    n_far = jnp.maximum(i - 1, 0)

    def far_pair(pair, carry):
        step([(2 * pair, None, False), (2 * pair + 1, None, False)])
        return carry

    lax.fori_loop(0, lax.shift_right_logical(n_far, 1), far_pair, 0)

    @pl.when(lax.rem(n_far, 2) == 1)
    def _():
        step([(n_far - 1, None, False)])

    @pl.when(i >= 1)
    def _():
        step([(i - 1, b1_ref, False), (i, b0_ref, True)])

    @pl.when(i == 0)
    def _():
        step([(i, b0_ref, True)])

    lam = (jnp.exp(jnp.sum(lq1_ref[...] * lk1_ref[...], axis=-1, keepdims=True))
           - jnp.exp(jnp.sum(lq2_ref[...] * lk2_ref[...], axis=-1, keepdims=True))
           + LAMBDA_INIT)
    for hd in range(ATT_HEADS):
        a = (acc_ref[hd, 0, :hw, :] / acc_ref[hd, 0, hw:hw + 1, :]
             - lam * (acc_ref[hd, 1, :hw, :] / acc_ref[hd, 1, hw:hw + 1, :]))
        a = a * lax.rsqrt(jnp.mean(a * a, axis=0, keepdims=True) + NORM_EPS)
        o_ref[:, hd * hw:(hd + 1) * hw] = (a.T * sw_ref[...] * (1.0 - LAMBDA_INIT)).astype(o_ref.dtype)


def _diff_attention(q, k, vt, rel_bias, lq1, lk1, lq2, lk2, subln_w, batch, seq):
    t = q.shape[0]
    blk = min(ATT_BLOCK, seq)
    nq = seq // blk
    hw = 2 * DIFF_HEAD_DIM
    nh = ATT_HEADS
    b0, b1 = _bias_tiles(rel_bias, blk)
    small = lambda a: pl.BlockSpec(a.shape, lambda b, h, i: (0,) * a.ndim)
    return pl.pallas_call(
        _attn_kernel,
        grid=(batch, DIFF_HEADS // nh, nq),
        in_specs=[pl.BlockSpec((blk, nh * hw), lambda b, h, i: (b * nq + i, h)),
                  pl.BlockSpec((seq, nh * hw), lambda b, h, i: (b, h)),
                  pl.BlockSpec((1, nh, nq, hw + ONES_ROWS, blk), lambda b, h, i: (b, h, 0, 0, 0)),
                  pl.BlockSpec((nh, blk, blk), lambda b, h, i: (h, 0, 0)),
                  pl.BlockSpec((nh, blk, blk), lambda b, h, i: (h, 0, 0)),
                  small(lq1), small(lk1), small(lq2), small(lk2), small(subln_w)],
        out_specs=pl.BlockSpec((blk, nh * hw), lambda b, h, i: (b * nq + i, h)),
        out_shape=jax.ShapeDtypeStruct((t, DIFF_V_W), BF16),
        scratch_shapes=[pltpu.VMEM((nh, 2, 1, blk), F32), pltpu.VMEM((nh, 2, hw + ONES_ROWS, blk), F32)],
        compiler_params=pltpu.CompilerParams(
            dimension_semantics=("arbitrary", "arbitrary", "arbitrary"),
            vmem_limit_bytes=VMEM_LIMIT),
        name="diff_attn",
    )(q, k, vt, b0, b1, lq1, lk1, lq2, lk2, subln_w)


def _out_kernel(yr_ref, yd_ref, x_ref, mod_ref, nw_ref, wo_ref, wr_ref, br_ref, upper_ref,
                x1_ref, hp_ref, meta_ref, gate_ref, cnt_ref, run_ref):
    tm = x_ref.shape[0]
    ne = run_ref.shape[0]

    @pl.when(pl.program_id(0) == 0)
    def _():
        run_ref[...] = jnp.zeros_like(run_ref)

    mixed = (jnp.dot(yr_ref[...], wo_ref[0:RET_V_W, :], preferred_element_type=F32)
             + jnp.dot(yd_ref[...], wo_ref[RET_V_W:, :], preferred_element_type=F32))
    gate1 = mod_ref[0, 2:3, :]
    shift2 = mod_ref[0, 3:4, :]
    scale2 = mod_ref[0, 4:5, :]
    x1 = x_ref[...] + gate1 * mixed
    x1_ref[...] = x1
    h2 = (_rms(x1) * nw_ref[...]) * (1.0 + scale2) + shift2
    hp_ref[...] = _pack_halves(h2)

    h_hi = h2.astype(BF16)
    h_lo = (h2 - h_hi.astype(F32)).astype(BF16)
    nt_dims = (((1,), (1,)), ((), ()))
    both = lax.dot_general(wr_ref[...], h_hi, nt_dims, preferred_element_type=F32)
    low = lax.dot_general(wr_ref[0:ne, :], h_lo, nt_dims, preferred_element_type=F32)
    logits = both[0:ne, :] + both[ne:, :] + low + br_ref[...]

    row = lax.broadcasted_iota(jnp.int32, logits.shape, 0)
    work = logits
    vals, idxs, hots = [], [], []
    for _ in range(TOP_K):
        mx = jnp.max(work, axis=0, keepdims=True)
        idx = jnp.min(jnp.where(work == mx, row, ne), axis=0, keepdims=True)
        hot = row == idx
        vals.append(mx)
        idxs.append(idx)
        hots.append(hot)
        work = jnp.where(hot, -jnp.inf, work)
    exps = [jnp.exp(v - vals[0]) for v in vals]
    denom = exps[0] + exps[1] + exps[2] + exps[3]

    sel = jnp.zeros(logits.shape, F32)
    for hot in hots:
        sel = sel + hot.astype(F32)
    prefix = jnp.dot(sel.astype(BF16), upper_ref[...], preferred_element_type=F32) + run_ref[...]
    ranks = [jnp.sum(jnp.where(hot, prefix, 0.0), axis=0, keepdims=True) for hot in hots]
    run_ref[...] = run_ref[...] + jnp.sum(sel, axis=1, keepdims=True)
    cnt_ref[...] = jnp.broadcast_to(run_ref[...], cnt_ref.shape).astype(jnp.int32)

    meta_ref[...] = jnp.concatenate(idxs + [r.astype(jnp.int32) for r in ranks], axis=0)
    gate_ref[...] = jnp.concatenate([e / denom for e in exps] + [jnp.zeros_like(denom)] * TOP_K, axis=0)


def _out_router(y_r, y_d, x2, mod, norm_w, w_out_bf16, w_router, b_router, seq):
    t, d = x2.shape
    tm = min(ROW_TILE, seq)
    per_batch = seq // tm
    ne = w_router.shape[1]
    w_hi = w_router.astype(BF16)
    w_lo = (w_router - w_hi.astype(F32)).astype(BF16)
    wr_t = jnp.concatenate([w_hi, w_lo], axis=1).T
    idx = jnp.arange(tm, dtype=jnp.int32)
    upper = (idx[:, None] < idx[None, :]).astype(BF16)
    row = lambda w: pl.BlockSpec((tm, w), lambda i: (i, 0))
    col = lambda h: pl.BlockSpec((h, tm), lambda i: (0, i))
    const = lambda a: pl.BlockSpec(a.shape, lambda i: (0,) * a.ndim)
    x1, hp, meta_t, gates_t, counts = pl.pallas_call(
        _out_kernel,
        grid=(t // tm,),
        in_specs=[row(RET_V_W), row(DIFF_V_W), row(d),
                  pl.BlockSpec((1, 6, d), lambda i: (i // per_batch, 0, 0)),
                  const(norm_w), const(w_out_bf16), const(wr_t), const(b_router), const(upper)],
        out_specs=[row(d), row(d // 2), col(2 * TOP_K), col(2 * TOP_K),
                   pl.BlockSpec((ne, LANES), lambda i: (0, 0))],
        out_shape=[jax.ShapeDtypeStruct((t, d), F32),
                   jax.ShapeDtypeStruct((t, d // 2), jnp.uint32),
                   jax.ShapeDtypeStruct((2 * TOP_K, t), jnp.int32),
                   jax.ShapeDtypeStruct((2 * TOP_K, t), F32),
                   jax.ShapeDtypeStruct((ne, LANES), jnp.int32)],
        scratch_shapes=[pltpu.VMEM((ne, 1), F32)],
        compiler_params=pltpu.CompilerParams(
            dimension_semantics=("arbitrary",), vmem_limit_bytes=VMEM_LIMIT),
        name="out_router",
    )(y_r, y_d, x2, mod, norm_w, w_out_bf16, wr_t, b_router, upper)
    return x1, hp, meta_t, gates_t.T, counts[:, 0]


def _pair_split_bias(b1):
    e, f2 = b1.shape
    nt = f2 // PERM_TILE
    g = b1[:, 0::2].reshape(e, nt, 1, PERM_TILE // 2)
    l = b1[:, 1::2].reshape(e, nt, 1, PERM_TILE // 2)
    return jnp.concatenate([g, l], axis=2).reshape(e, 1, f2)


def _slot_tokens(dest_t, n_rows):
    n_tok = dest_t.shape[1]
    assert n_tok & (n_tok - 1) == 0
    flat = dest_t.reshape(-1)
    n_slots = flat.shape[0]
    chunk = min(SC_SLOT_CHUNK, n_slots)
    mesh = plsc.VectorSubcoreMesh(core_axis_name="c", subcore_axis_name="s",
                                  num_cores=SC_CORES, num_subcores=SC_SUBCORES)

    def body(dest_hbm, out_hbm, out_v, dest_v):
        first = jnp.logical_and(lax.axis_index("c") == 0, lax.axis_index("s") == 0)

        @pl.when(first)
        def _():
            zeros = jnp.zeros((SC_LANES,), jnp.int32)

            @plsc.parallel_loop(0, n_rows // SC_LANES, unroll=SC_UNROLL)
            def _(g):
                out_v[pl.ds(g * SC_LANES, SC_LANES)] = zeros

            lane = lax.iota(jnp.int32, SC_LANES)

            @pl.loop(0, n_slots // chunk)
            def _(c):
                pltpu.sync_copy(dest_hbm.at[pl.ds(c * chunk, chunk)], dest_v)

                @plsc.parallel_loop(0, chunk // SC_LANES, unroll=SC_UNROLL)
                def _(g):
                    rows = dest_v[pl.ds(g * SC_LANES, SC_LANES)]
                    slot = lane + (c * chunk + g * SC_LANES)
                    plsc.store_scatter(out_v, [rows], slot & (n_tok - 1))

            pltpu.sync_copy(out_v, out_hbm)

    return pl.kernel(
        body,
        out_type=jax.ShapeDtypeStruct((n_rows,), jnp.int32),
        mesh=mesh,
        scratch_types=[pltpu.VMEM((n_rows,), jnp.int32), pltpu.VMEM((chunk,), jnp.int32)],
        compiler_params=pltpu.CompilerParams(needs_layout_passes=False),
        name="slot_tokens",
    )(flat)


def _expert_kernel(be_ref, nv_ref, nexte_ref, tcur_ref, tnxt_ref, hp_ref, w1_ref, b1_ref, w2_ref, b2_ref,
                   o_ref, hbuf, xb0, xb1, w1buf, w2buf, w2b, sem, wsems):
    b = pl.program_id(0)
    nv = nv_ref[0]
    bm = xb0.shape[0]

    def gather(tok_ref, dst):
        for r in range(bm):
            dst[pl.ds(r, 1), :] = hbuf[pl.ds(tok_ref[0, 0, r], 1), :]

    def weight_copies(e, slot):
        return (pltpu.make_async_copy(w1_ref.at[e], w1buf.at[slot], wsems.at[0, slot]),
                pltpu.make_async_copy(w2_ref.at[e], w2buf.at[slot], wsems.at[1, slot]))

    e = be_ref[b]
    prev_e = be_ref[jnp.maximum(b - 1, 0)]
    first_of_expert = jnp.logical_and(b < nv, jnp.logical_or(b == 0, e != prev_e))
    slot = lax.rem(nexte_ref[N_EXPERTS + e], 2)

    @pl.when(b == 0)
    def _():
        for cp in weight_copies(e, slot):
            cp.start()
        load = pltpu.make_async_copy(hp_ref, hbuf, sem)
        load.start()
        load.wait()
        gather(tcur_ref, xb0)

    @pl.when(first_of_expert)
    def _():
        for cp in weight_copies(e, slot):
            cp.wait()
        nxt = nexte_ref[e]

        @pl.when(nxt >= 0)
        def _():
            for cp in weight_copies(nxt, 1 - slot):
                cp.start()

        w2b[...] = w2buf[slot].astype(BF16)

    def run(cur, nxt_rows):
        x = _unpack_halves(cur[...])
        half = PERM_TILE // 2
        acts = []
        for j in range(w1buf.shape[2] // PERM_TILE):
            cols = slice(j * PERM_TILE, (j + 1) * PERM_TILE)
            h = jnp.dot(x, w1buf[slot, :, cols], preferred_element_type=F32) + b1_ref[0, :, cols]
            glu = jnp.minimum(h[:, :half], SWIGLU_LIMIT)
            lin = jnp.clip(h[:, half:], -SWIGLU_LIMIT, SWIGLU_LIMIT)
            acts.append((glu * jax.nn.sigmoid(SWIGLU_ALPHA * glu) * (lin + 1.0)).astype(BF16))
        act = jnp.concatenate(acts, axis=1)
        y = jnp.dot(act, w2b[...], preferred_element_type=F32) + b2_ref[0]
        _store_row_tiles(o_ref, y)
        gather(tnxt_ref, nxt_rows)

    even = lax.rem(b, 2) == 0

    @pl.when(jnp.logical_and(b < nv, even))
    def _():
        run(xb0, xb1)

    @pl.when(jnp.logical_and(b < nv, jnp.logical_not(even)))
    def _():
        run(xb1, xb0)

    @pl.when(b >= nv)
    def _():
        o_ref[...] = jnp.zeros_like(o_ref)


def _experts(block_e, n_valid, next_e, slot_tok, hp, w1p, b1p, w2, b2):
    d = D_MODEL
    f2 = w1p.shape[2]
    f = w2.shape[1]
    bm = EXPERT_ROWS
    nb = slot_tok.shape[0] // bm
    tok3 = slot_tok.reshape(nb, 1, bm)
    exp = lambda b, be, nv, ne: be[jnp.maximum(jnp.minimum(b, nv[0] - 1), 0)]
    grid_spec = pltpu.PrefetchScalarGridSpec(
        num_scalar_prefetch=3,
        grid=(nb,),
        in_specs=[pl.BlockSpec((1, 1, bm), lambda b, be, nv, ne: (b, 0, 0), memory_space=pltpu.SMEM),
                  pl.BlockSpec((1, 1, bm), lambda b, be, nv, ne: (jnp.minimum(b + 1, nb - 1), 0, 0),
                               memory_space=pltpu.SMEM),
                  pl.BlockSpec(memory_space=pl.ANY),
                  pl.BlockSpec(memory_space=pl.ANY),
                  pl.BlockSpec((1, 1, f2), lambda b, be, nv, ne: (exp(b, be, nv, ne), 0, 0)),
                  pl.BlockSpec(memory_space=pl.ANY),
                  pl.BlockSpec((1, 1, d), lambda b, be, nv, ne: (exp(b, be, nv, ne), 0, 0))],
        out_specs=pl.BlockSpec((bm * ROW_SUB, LANES), lambda b, be, nv, ne: (b, 0)),
        scratch_shapes=[pltpu.VMEM(hp.shape, hp.dtype), pltpu.VMEM((bm, hp.shape[1]), hp.dtype),
                        pltpu.VMEM((bm, hp.shape[1]), hp.dtype),
                        pltpu.VMEM((2,) + w1p.shape[1:], w1p.dtype), pltpu.VMEM((2,) + w2.shape[1:], w2.dtype),
                        pltpu.VMEM(w2.shape[1:], BF16),
                        pltpu.SemaphoreType.DMA(()), pltpu.SemaphoreType.DMA((2, 2))],
    )
    return pl.pallas_call(
        _expert_kernel,
        grid_spec=grid_spec,
        out_shape=jax.ShapeDtypeStruct((nb * bm * ROW_SUB, LANES), F32),
        compiler_params=pltpu.CompilerParams(
            dimension_semantics=("arbitrary",), vmem_limit_bytes=VMEM_LIMIT),
        name="experts",
    )(block_e, n_valid, next_e, tok3, tok3, hp, w1p, b1p, w2, b2)


def _combine_kernel(dcur_ref, dnxt_ref, gate_ref, x1_ref, mod_ref, nf_ref, y_ref, o_ref, buf, sems):
    nt = x1_ref.shape[0]
    i = pl.program_id(0)
    n = pl.num_programs(0)
    slot = lax.rem(i, 2)

    def row_copy(src_row, s, kk, r):
        return pltpu.make_async_copy(y_ref.at[pl.ds(src_row * ROW_SUB, ROW_SUB), :],
                                     buf.at[s, kk, pl.ds(r * ROW_SUB, ROW_SUB), :], sems.at[s])

    def fetch(d_ref, s):
        for j in range(nt):
            for kk in range(TOP_K):
                row_copy(d_ref[0, 0, kk * nt + j], s, kk, j).start(priority=kk % DMA_QUEUES)

    def wait_all(s):
        for kk in range(TOP_K):
            pltpu.make_async_copy(y_ref.at[pl.ds(0, nt * ROW_SUB), :], buf.at[s, kk], sems.at[s]).wait()

    @pl.when(i == 0)
    def _():
        fetch(dcur_ref, 0)

    fetch(dnxt_ref, 1 - slot)
    wait_all(slot)

    g = gate_ref[...]
    parts = []
    for s in range(ROW_SUB):
        acc = buf[slot, 0, pl.ds(s, nt, stride=ROW_SUB), :] * g[:, 0:1]
        for kk in range(1, TOP_K):
            acc = acc + buf[slot, kk, pl.ds(s, nt, stride=ROW_SUB), :] * g[:, kk:kk + 1]
        parts.append(acc)
    moe = jnp.concatenate(parts, axis=1)
    gate2 = mod_ref[0, 5:6, :]
    x2 = x1_ref[...] + gate2 * moe
    o_ref[...] = _rms(x2) * nf_ref[...]

    @pl.when(i == n - 1)
    def _():
        wait_all(1 - slot)


def _combine(dest_t, gates, x1, mod, normf_w, y, seq):
    t, d = x1.shape
    nt = min(COMBINE_TOKENS, seq)
    steps = t // nt
    per_batch = seq // nt
    dest2 = jnp.transpose(dest_t.reshape(TOP_K, steps, nt), (1, 0, 2)).reshape(steps, 1, nt * TOP_K)
    return pl.pallas_call(
        _combine_kernel,
        grid=(steps,),
        in_specs=[pl.BlockSpec((1, 1, nt * TOP_K), lambda i: (i, 0, 0), memory_space=pltpu.SMEM),
                  pl.BlockSpec((1, 1, nt * TOP_K), lambda i: (jnp.minimum(i + 1, steps - 1), 0, 0),
                               memory_space=pltpu.SMEM),
                  pl.BlockSpec((nt, 2 * TOP_K), lambda i: (i, 0)),
                  pl.BlockSpec((nt, d), lambda i: (i, 0)),
                  pl.BlockSpec((1, 6, d), lambda i: (i // per_batch, 0, 0)),
                  pl.BlockSpec((1, d), lambda i: (0, 0)),
                  pl.BlockSpec(memory_space=pl.ANY)],
        out_specs=pl.BlockSpec((nt, d), lambda i: (i, 0)),
        out_shape=jax.ShapeDtypeStruct((t, d), F32),
        scratch_shapes=[pltpu.VMEM((2, TOP_K, nt * ROW_SUB, LANES), F32),
                        pltpu.SemaphoreType.DMA((2,))],
        compiler_params=pltpu.CompilerParams(
            dimension_semantics=("arbitrary",), vmem_limit_bytes=VMEM_LIMIT),
        name="combine",
    )(dest2, dest2, gates, x1, mod, normf_w, y)


def kernel(x, c, w_ada, b_ada, norm1_w, w_in, lam_q1, lam_k1, lam_q2, lam_k2, subln_w, rel_bias,
           w_out, norm2_w, w_router, b_router, w1, b1, w2, b2, normf_w):
    batch, seq, d = x.shape
    t = batch * seq
    x2 = x.reshape(t, d)

    c_pad = jnp.zeros((8, d), F32).at[:batch].set(c)
    mod = _ada(c_pad, w_ada[0], b_ada[0][None, :])[:batch].reshape(batch, 6, d)

    q_r, k_r, v_r, g_r, q_d, k_d, vt_d, w1p = _in_proj(x2, mod, norm1_w[0][None, :],
                                                      w_in[0].astype(BF16), w1[0], seq)
    y_r = _retention(q_r, k_r, v_r, g_r, batch, seq)
    y_d = _diff_attention(q_d, k_d, vt_d, rel_bias, lam_q1, lam_k1, lam_q2, lam_k2,
                          subln_w, batch, seq)

    x1, hp, meta, gates, counts = _out_router(y_r, y_d, x2, mod, norm2_w[0][None, :],
                                              w_out[0].astype(BF16), w_router[0],
                                              b_router[0][:, None], seq)

    bm = EXPERT_ROWS
    padded = (counts + bm - 1) // bm * bm
    pad_end = jnp.cumsum(padded)
    pad_start = pad_end - padded
    hot_e = meta[None, :TOP_K, :] == jnp.arange(N_EXPERTS, dtype=jnp.int32)[:, None, None]
    dest_t = jnp.sum(jnp.where(hot_e, pad_start[:, None, None], 0), axis=0) + meta[TOP_K:, :]
    n_rows = (t * TOP_K // bm + N_EXPERTS) * bm
    nb = n_rows // bm
    block_start = jnp.arange(nb, dtype=jnp.int32) * bm
    block_e = jnp.minimum(jnp.sum((pad_end[None, :] <= block_start[:, None]).astype(jnp.int32), axis=1),
                          N_EXPERTS - 1)
    n_valid = (pad_end[-1:] // bm).astype(jnp.int32)
    slot_tok = _slot_tokens(dest_t, n_rows)

    owns = padded > 0
    ids = jnp.arange(N_EXPERTS, dtype=jnp.int32)
    later = jnp.logical_and(owns[None, :], ids[None, :] > ids[:, None])
    next_owner = jnp.min(jnp.where(later, ids[None, :], N_EXPERTS), axis=1)
    next_owner = jnp.where(next_owner == N_EXPERTS, -1, next_owner).astype(jnp.int32)
    ordinal = (jnp.cumsum(owns.astype(jnp.int32)) - 1).astype(jnp.int32)
    next_e = jnp.concatenate([next_owner, jnp.maximum(ordinal, 0)])

    ys = _experts(block_e, n_valid, next_e, slot_tok, hp, w1p, _pair_split_bias(b1[0]),
                  w2[0], b2[0][:, None, :])
    out = _combine(dest_t, gates, x1, mod, normf_w[None, :], ys, seq)
    return out.reshape(batch, seq, d)
```

```python
import functools
import math

import jax
import jax.numpy as jnp
from jax import lax
from jax.experimental import pallas as pl
from jax.experimental.pallas import tpu as pltpu
from jax.experimental.pallas import tpu_sc as plsc

F32 = jnp.float32
BF16 = jnp.bfloat16

D_MODEL = 1024
RET_HEADS = 4
RET_KEY_DIM = 64
RET_VAL_DIM = 128
RET_QK_W = RET_HEADS * RET_KEY_DIM
RET_V_W = RET_HEADS * RET_VAL_DIM
RET_CHUNK = 128
DIFF_HEADS = 4
DIFF_HEAD_DIM = 64
DIFF_QK_W = DIFF_HEADS * 2 * DIFF_HEAD_DIM
DIFF_V_W = DIFF_HEADS * 2 * DIFF_HEAD_DIM
IN_SIZES = (RET_QK_W, RET_QK_W, RET_V_W, RET_V_W, DIFF_QK_W, DIFF_QK_W, DIFF_V_W)
REL_BUCKETS = 32
REL_MAX_DIST = 128
N_EXPERTS = 32
TOP_K = 4
SWIGLU_ALPHA = 1.702
SWIGLU_LIMIT = 7.0
NORM_EPS = 1e-6
LAMBDA_INIT = 0.8 - 0.6 * math.exp(-0.3 * 0)

LANES = 128
ROW_SUB = D_MODEL // LANES
NEG_BIG = -1e30
LOG2_E = math.log2(math.e)
ONES_ROWS = 16
VMEM_LIMIT = 56 * 1024 * 1024

ROW_TILE = 512
ATT_BLOCK = 512
ATT_STRIP = 256
ATT_HEADS = 4
EXPERT_ROWS = 256
PERM_TILE = 256
COMBINE_TOKENS = 256
DMA_QUEUES = 2
SC_CORES = 2
SC_SUBCORES = 16
SC_LANES = 16
SC_UNROLL = 8
SC_SLOT_CHUNK = 32768


def _rms(x):
    return x * lax.rsqrt(jnp.mean(x * x, axis=-1, keepdims=True) + NORM_EPS)


def _store_row_tiles(ref, x):
    rows = x.shape[0]
    for s in range(ROW_SUB):
        ref[pl.ds(s, rows, stride=ROW_SUB), :] = x[:, s * LANES:(s + 1) * LANES]


def _pack_halves(x):
    half = x.shape[1] // 2
    lo = lax.bitcast_convert_type(x[:, :half].astype(BF16).astype(F32), jnp.uint32)
    hi = lax.bitcast_convert_type(x[:, half:].astype(BF16).astype(F32), jnp.uint32)
    return (hi & jnp.uint32(0xFFFF0000)) | (lo >> 16)


def _unpack_halves(w):
    lo = lax.bitcast_convert_type(w << 16, F32).astype(BF16)
    hi = lax.bitcast_convert_type(w & jnp.uint32(0xFFFF0000), F32).astype(BF16)
    return jnp.concatenate([lo, hi], axis=1)


def _ada_kernel(c_ref, w_ref, b_ref, o_ref):
    c = c_ref[...]
    cond = c * jax.nn.sigmoid(c)
    o_ref[...] = jnp.dot(cond, w_ref[...], precision=lax.Precision.HIGHEST,
                         preferred_element_type=F32) + b_ref[...]


def _ada(c_pad, w_ada, b_ada):
    rows, d = c_pad.shape
    n = w_ada.shape[1]
    tn = 1024
    return pl.pallas_call(
        _ada_kernel,
        grid=(n // tn,),
        in_specs=[pl.BlockSpec((rows, d), lambda j: (0, 0)),
                  pl.BlockSpec((d, tn), lambda j: (0, j)),
                  pl.BlockSpec((1, tn), lambda j: (0, j))],
        out_specs=pl.BlockSpec((rows, tn), lambda j: (0, j)),
        out_shape=jax.ShapeDtypeStruct((rows, n), F32),
        name="ada",
    )(c_pad, w_ada, b_ada)


def _in_proj_kernel(per_batch, x_ref, mod_ref, nw_ref, w_ref, w1_ref, perm_ref,
                    cos_ref, sine_ref, sino_ref, hmask_ref, xi_ref, zeta_ref, dmask_ref, gch_ref,
                    qd_ref, kd_ref, vt_ref, yr_ref, w1p_ref, qr_s, kr_s, vr_s, gr_s, state_ref):
    _split_w1_tiles(w1_ref, perm_ref, w1p_ref)
    x = x_ref[...]
    shift = mod_ref[0, 0:1, :]
    scale = mod_ref[0, 1:2, :]
    h = (_rms(x) * nw_ref[...]) * (1.0 + scale) + shift
    hb = h.astype(BF16)
    off = 0
    for o_ref, width in zip((qr_s, kr_s, vr_s, gr_s, qd_ref, kd_ref), IN_SIZES[:-1]):
        o_ref[...] = jnp.dot(hb, w_ref[:, off:off + width],
                             preferred_element_type=F32).astype(o_ref.dtype)
        off += width
    v = jnp.dot(hb, w_ref[:, off:], preferred_element_type=F32)
    hw = 2 * DIFF_HEAD_DIM
    tail = (lax.broadcasted_iota(jnp.int32, (ONES_ROWS, v.shape[0]), 0) == 0).astype(vt_ref.dtype)
    for hd in range(DIFF_HEADS):
        vt_ref[0, hd, 0, 0:hw, :] = v[:, hd * hw:(hd + 1) * hw].T.astype(vt_ref.dtype)
        vt_ref[0, hd, 0, hw:, :] = tail
    _retention_rows(lax.rem(pl.program_id(0), per_batch) == 0, qr_s, kr_s, vr_s, gr_s,
                    cos_ref, sine_ref, sino_ref, hmask_ref, xi_ref, zeta_ref, dmask_ref, gch_ref,
                    yr_ref, state_ref)


def _in_proj(x2, mod, norm_w, w_in_bf16, w1, seq):
    t, d = x2.shape
    tm = min(ROW_TILE, seq)
    assert tm == min(ATT_BLOCK, seq)
    assert tm % RET_CHUNK == 0
    per_batch = seq // tm
    in_w = w_in_bf16.shape[1]
    hw = 2 * DIFF_HEAD_DIM
    vt_shape = (t // seq, DIFF_HEADS, per_batch, hw + ONES_ROWS, tm)
    perm = _pair_split_matrix()
    per_step, rest = divmod(w1.shape[0], t // tm)
    assert rest == 0 and per_step >= 1
    expert = pl.BlockSpec((per_step,) + w1.shape[1:], lambda i: (i, 0, 0))
    cos, sin_even, sin_odd, hmask, xi_t, zeta_t, dmask, gch = _retention_tables(seq)
    tab = pl.BlockSpec((tm, LANES), lambda i: (i % per_batch, 0))
    full = lambda a: pl.BlockSpec(a.shape, lambda i: (0,) * a.ndim)
    row = lambda w: pl.BlockSpec((tm, w), lambda i: (i, 0))
    return pl.pallas_call(
        functools.partial(_in_proj_kernel, per_batch),
        grid=(t // tm,),
        in_specs=[pl.BlockSpec((tm, d), lambda i: (i, 0)),
                  pl.BlockSpec((1, 6, d), lambda i: (i // per_batch, 0, 0)),
                  pl.BlockSpec((1, d), lambda i: (0, 0)),
                  pl.BlockSpec((d, in_w), lambda i: (0, 0)),
                  expert, full(perm), tab, tab, tab,
                  full(hmask), full(xi_t), full(zeta_t), full(dmask), full(gch)],
        out_specs=[row(DIFF_QK_W), row(DIFF_QK_W),
                   pl.BlockSpec((1,) + vt_shape[1:2] + (1,) + vt_shape[3:],
                                lambda i: (i // per_batch, 0, i % per_batch, 0, 0)),
                   row(RET_V_W), expert],
        out_shape=[jax.ShapeDtypeStruct((t, DIFF_QK_W), BF16), jax.ShapeDtypeStruct((t, DIFF_QK_W), BF16),
                   jax.ShapeDtypeStruct(vt_shape, BF16), jax.ShapeDtypeStruct((t, RET_V_W), BF16),
                   jax.ShapeDtypeStruct(w1.shape, BF16)],
        scratch_shapes=[pltpu.VMEM((tm, RET_QK_W), BF16), pltpu.VMEM((tm, RET_QK_W), BF16),
                        pltpu.VMEM((tm, RET_V_W), BF16), pltpu.VMEM((tm, RET_V_W), BF16),
                        pltpu.VMEM((RET_HEADS, LANES, RET_VAL_DIM), F32)],
        compiler_params=pltpu.CompilerParams(
            dimension_semantics=("arbitrary",), vmem_limit_bytes=VMEM_LIMIT),
        name="in_proj",
    )(x2, mod, norm_w, w_in_bf16, w1, perm, cos, sin_even, sin_odd, hmask, xi_t, zeta_t, dmask, gch)


def _split_w1_tiles(w_ref, p_ref, o_ref):
    for e in range(w_ref.shape[0]):
        for s in range(w_ref.shape[2] // PERM_TILE):
            cols = slice(s * PERM_TILE, (s + 1) * PERM_TILE)
            o_ref[e, :, cols] = jnp.dot(w_ref[e, :, cols].astype(BF16), p_ref[...],
                                        preferred_element_type=F32).astype(BF16)


def _pair_split_matrix():
    i = jnp.arange(PERM_TILE)[:, None]
    j = jnp.arange(PERM_TILE)[None, :]
    half = PERM_TILE // 2
    src = jnp.where(j < half, 2 * j, 2 * (j - half) + 1)
    return (i == src).astype(BF16)


def _rotary(x, cos, sin_even, sin_odd):
    nxt = pltpu.roll(x, LANES - 1, 1)
    prv = pltpu.roll(x, 1, 1)
    return x * cos + nxt * sin_even + prv * sin_odd


def _retention_rows(first, q_ref, k_ref, v_ref, g_ref, cos_ref, sine_ref, sino_ref,
                    hmask_ref, xi_ref, zeta_ref, dmask_ref, gch_ref, o_ref, state_ref):
    @pl.when(first)
    def _():
        state_ref[...] = jnp.zeros_like(state_ref)

    n_sub = q_ref.shape[0] // RET_CHUNK
    for c in range(n_sub):
        rows = slice(c * RET_CHUNK, (c + 1) * RET_CHUNK)
        for pair in range(RET_HEADS // 2):
            lanes = slice(pair * LANES, (pair + 1) * LANES)
            cos = cos_ref[rows, :]
            sine = sine_ref[rows, :]
            sino = sino_ref[rows, :]
            qr = _rotary(q_ref[rows, lanes].astype(F32), cos, sine, sino)
            kr = _rotary(k_ref[rows, lanes].astype(F32), cos, sine, sino) * (RET_KEY_DIM ** -0.5)
            qb = qr.astype(BF16)
            for hh in range(2):
                h = 2 * pair + hh
                vcols = slice(h * RET_VAL_DIM, (h + 1) * RET_VAL_DIM)
                v = v_ref[rows, vcols]
                km = (kr * hmask_ref[h]).astype(BF16)
                scores = lax.dot_general(qb, km, (((1,), (1,)), ((), ())),
                                         preferred_element_type=F32) * dmask_ref[h]
                inner = jnp.dot(scores.astype(BF16), v, preferred_element_type=F32)
                qx = (qr * xi_ref[h]).astype(BF16)
                state = state_ref[h]
                cross = jnp.dot(qx, state.astype(BF16), preferred_element_type=F32)
                kz = (kr * zeta_ref[h]).astype(BF16)
                kv = lax.dot_general(kz, v, (((0,), (0,)), ((), ())),
                                     preferred_element_type=F32)
                state_ref[h] = state * gch_ref[h] + kv
                y = _rms(inner + cross)
                g = g_ref[rows, vcols].astype(F32)
                o_ref[rows, vcols] = (g * jax.nn.sigmoid(g) * y).astype(o_ref.dtype)


def _retention_tables(seq):
    dk, c, nh = RET_KEY_DIM, RET_CHUNK, RET_HEADS
    pos = jnp.arange(seq, dtype=F32)
    inv_freq = 1.0 / (10000.0 ** jnp.linspace(0.0, 1.0, dk // 2, dtype=F32))
    ang = pos[:, None] * jnp.repeat(inv_freq, 2)[None, :]
    sin = jnp.tile(jnp.sin(ang), (1, LANES // dk))
    cos = jnp.tile(jnp.cos(ang), (1, LANES // dk))
    even = (jnp.arange(LANES) % 2 == 0)[None, :]
    sin_even = jnp.where(even, -sin, 0.0)
    sin_odd = jnp.where(even, 0.0, sin)
    log_g = jnp.log(1.0 - 2.0 ** (-5.0 - jnp.arange(nh, dtype=F32)))
    i = jnp.arange(c, dtype=F32)
    rel = i[:, None] - i[None, :]
    dmask = jnp.where(rel[None] >= 0,
                      jnp.exp(jnp.maximum(rel, 0.0)[None] * log_g[:, None, None]), 0.0)
    zeta = jnp.exp((c - 1.0 - i)[None, :] * log_g[:, None])
    xi = jnp.exp((i + 1.0)[None, :] * log_g[:, None])
    g_chunk = jnp.exp(c * log_g)
    lane = jnp.arange(LANES)
    hmask = jnp.stack([((lane // dk) == (h % 2)).astype(F32) for h in range(nh)])[:, None, :]
    xi_t = xi[:, :, None] * hmask
    zeta_t = zeta[:, :, None] * hmask
    gch = jnp.broadcast_to(g_chunk[:, None, None], (nh, 1, LANES))
    return cos, sin_even, sin_odd, hmask, xi_t, zeta_t, dmask, gch


def _t5_bucket(rel):
    n = jnp.maximum(rel, 0)
    max_exact = REL_BUCKETS // 2
    nf = jnp.maximum(n, 1).astype(F32)
    large = max_exact + (jnp.log(nf / max_exact) / math.log(REL_MAX_DIST / max_exact)
                         * (REL_BUCKETS - max_exact)).astype(jnp.int32)
    large = jnp.minimum(large, REL_BUCKETS - 1)
    return jnp.where(n < max_exact, n, large)


def _far_distance():
    n = 0
    for dist in range(2 * REL_MAX_DIST):
        if dist < REL_BUCKETS // 2:
            bucket = dist
        else:
            bucket = min(REL_BUCKETS // 2 + int(math.log(dist / (REL_BUCKETS // 2))
                                                 / math.log(REL_MAX_DIST / (REL_BUCKETS // 2))
                                                 * (REL_BUCKETS - REL_BUCKETS // 2)), REL_BUCKETS - 1)
        if bucket < REL_BUCKETS - 1:
            n = dist + 1
    return n + 1


def _bias_tiles(rel_bias, blk):
    far = rel_bias[REL_BUCKETS - 1]
    buckets = jnp.arange(REL_BUCKETS, dtype=jnp.int32)
    heads = rel_bias.shape[1]
    near = _far_distance()
    sub = min(LANES, blk)
    r = jnp.arange(sub, dtype=jnp.int32)

    def tile(offset, masked):
        rows = []
        for kb in range(blk // sub):
            cols = []
            for rb in range(blk // sub):
                lo = (rb - kb) * sub - (sub - 1) + offset
                hi = (rb - kb) * sub + (sub - 1) + offset
                if masked and hi < 0:
                    cols.append(jnp.full((heads, sub, sub), NEG_BIG, F32))
                elif lo >= near:
                    cols.append(jnp.zeros((heads, sub, sub), F32))
                else:
                    rel = r[None, :] - r[:, None] + ((rb - kb) * sub + offset)
                    hot = (_t5_bucket(rel)[:, :, None] == buckets).astype(F32)
                    val = (jnp.einsum('krb,bh->hkr', hot, rel_bias, precision=lax.Precision.HIGHEST)
                           - far[:, None, None]) * LOG2_E
                    cols.append(jnp.where(rel[None] >= 0, val, NEG_BIG) if masked else val)
            rows.append(jnp.concatenate(cols, axis=2))
        return jnp.concatenate(rows, axis=1)

    return tile(0, True), tile(blk, False)


def _attn_kernel(q_ref, k_ref, vt_ref, b0_ref, b1_ref, lq1_ref, lk1_ref, lq2_ref, lk2_ref,
                 sw_ref, o_ref, m_ref, acc_ref):
    blk = q_ref.shape[0]
    hw = 2 * DIFF_HEAD_DIM
    i = pl.program_id(2)
    lane = lax.broadcasted_iota(jnp.int32, (1, LANES), 1)
    qm = []
    for hd in range(ATT_HEADS):
        q = (q_ref[:, hd * hw:(hd + 1) * hw].astype(F32) * (DIFF_HEAD_DIM ** -0.5 * LOG2_E)).astype(BF16)
        zero = jnp.zeros_like(q)
        qm.append((jnp.where(lane < DIFF_HEAD_DIM, q, zero), jnp.where(lane >= DIFF_HEAD_DIM, q, zero)))

    m_ref[...] = jnp.full_like(m_ref, NEG_BIG)
    acc_ref[...] = jnp.zeros_like(acc_ref)

    def step(blocks):
        kbs = [k_ref[pl.ds(pl.multiple_of(j * blk, blk), blk), :] for j, _, _ in blocks]
        chains = [(hd, mi, qs) for hd in range(ATT_HEADS) for mi in range(2)
                  for qs in range(blk // ATT_STRIP)]
        scores = []
        for hd, mi, qs in chains:
            qc = slice(qs * ATT_STRIP, (qs + 1) * ATT_STRIP)
            row = []
            for (_, bias, diagonal), kb in zip(blocks, kbs):
                keys = (qs + 1) * ATT_STRIP if diagonal else blk
                s = lax.dot_general(kb[:keys, hd * hw:(hd + 1) * hw], qm[hd][mi][qc, :],
                                    (((1,), (1,)), ((), ())), preferred_element_type=F32)
                row.append(s if bias is None else s + bias[hd, :keys, qc])
            scores.append(row)
        stats = []
        for (hd, mi, qs), row in zip(chains, scores):
            qc = slice(qs * ATT_STRIP, (qs + 1) * ATT_STRIP)
            m_old = m_ref[hd, mi, :, qc]
            m_new = m_old
            for s in row:
                m_new = jnp.maximum(m_new, jnp.max(s, axis=0, keepdims=True))
            stats.append((jnp.exp2(m_old - m_new), [jnp.exp2(s - m_new).astype(BF16) for s in row], m_new))
        for (hd, mi, qs), (alpha, ps, m_new) in zip(chains, stats):
            qc = slice(qs * ATT_STRIP, (qs + 1) * ATT_STRIP)
            pv = None
            for (j, _, _), p in zip(blocks, ps):
                part = jnp.dot(vt_ref[0, hd, j, :, 0:p.shape[0]], p, preferred_element_type=F32)
                pv = part if pv is None else pv + part
            acc_ref[hd, mi, :, qc] = alpha * acc_ref[hd, mi, :, qc] + pv
            m_ref[hd, mi, :, qc] = m_new

    n_far = jnp.maximum(i - 1, 0)

    def far_pair(pair, carry):
        step([(2 * pair, None, False), (2 * pair + 1, None, False)])
        return carry

    lax.fori_loop(0, lax.shift_right_logical(n_far, 1), far_pair, 0)

    @pl.when(lax.rem(n_far, 2) == 1)
    def _():
        step([(n_far - 1, None, False)])

    @pl.when(i >= 1)
    def _():
        step([(i - 1, b1_ref, False), (i, b0_ref, True)])

    @pl.when(i == 0)
    def _():
        step([(i, b0_ref, True)])

    lam = (jnp.exp(jnp.sum(lq1_ref[...] * lk1_ref[...], axis=-1, keepdims=True))
           - jnp.exp(jnp.sum(lq2_ref[...] * lk2_ref[...], axis=-1, keepdims=True))
           + LAMBDA_INIT)
    for hd in range(ATT_HEADS):
        a = (acc_ref[hd, 0, :hw, :] / acc_ref[hd, 0, hw:hw + 1, :]
             - lam * (acc_ref[hd, 1, :hw, :] / acc_ref[hd, 1, hw:hw + 1, :]))
        a = a * lax.rsqrt(jnp.mean(a * a, axis=0, keepdims=True) + NORM_EPS)
        o_ref[:, hd * hw:(hd + 1) * hw] = (a.T * sw_ref[...] * (1.0 - LAMBDA_INIT)).astype(o_ref.dtype)


def _diff_attention(q, k, vt, rel_bias, lq1, lk1, lq2, lk2, subln_w, batch, seq):
    t = q.shape[0]
    blk = min(ATT_BLOCK, seq)
    nq = seq // blk
    hw = 2 * DIFF_HEAD_DIM
    nh = ATT_HEADS
    b0, b1 = _bias_tiles(rel_bias, blk)
    small = lambda a: pl.BlockSpec(a.shape, lambda b, h, i: (0,) * a.ndim)
    return pl.pallas_call(
        _attn_kernel,
        grid=(batch, DIFF_HEADS // nh, nq),
        in_specs=[pl.BlockSpec((blk, nh * hw), lambda b, h, i: (b * nq + i, h)),
                  pl.BlockSpec((seq, nh * hw), lambda b, h, i: (b, h)),
                  pl.BlockSpec((1, nh, nq, hw + ONES_ROWS, blk), lambda b, h, i: (b, h, 0, 0, 0)),
                  pl.BlockSpec((nh, blk, blk), lambda b, h, i: (h, 0, 0)),
                  pl.BlockSpec((nh, blk, blk), lambda b, h, i: (h, 0, 0)),
                  small(lq1), small(lk1), small(lq2), small(lk2), small(subln_w)],
        out_specs=pl.BlockSpec((blk, nh * hw), lambda b, h, i: (b * nq + i, h)),
        out_shape=jax.ShapeDtypeStruct((t, DIFF_V_W), BF16),
        scratch_shapes=[pltpu.VMEM((nh, 2, 1, blk), F32), pltpu.VMEM((nh, 2, hw + ONES_ROWS, blk), F32)],
        compiler_params=pltpu.CompilerParams(
            dimension_semantics=("arbitrary", "arbitrary", "arbitrary"),
            vmem_limit_bytes=VMEM_LIMIT),
        name="diff_attn",
    )(q, k, vt, b0, b1, lq1, lk1, lq2, lk2, subln_w)


def _out_kernel(yr_ref, yd_ref, x_ref, mod_ref, nw_ref, wo_ref, wr_ref, br_ref, upper_ref,
                x1_ref, hp_ref, meta_ref, gate_ref, cnt_ref, run_ref):
    tm = x_ref.shape[0]
    ne = run_ref.shape[0]

    @pl.when(pl.program_id(0) == 0)
    def _():
        run_ref[...] = jnp.zeros_like(run_ref)

    mixed = (jnp.dot(yr_ref[...], wo_ref[0:RET_V_W, :], preferred_element_type=F32)
             + jnp.dot(yd_ref[...], wo_ref[RET_V_W:, :], preferred_element_type=F32))
    gate1 = mod_ref[0, 2:3, :]
    shift2 = mod_ref[0, 3:4, :]
    scale2 = mod_ref[0, 4:5, :]
    x1 = x_ref[...] + gate1 * mixed
    x1_ref[...] = x1
    h2 = (_rms(x1) * nw_ref[...]) * (1.0 + scale2) + shift2
    hp_ref[...] = _pack_halves(h2)

    h_hi = h2.astype(BF16)
    h_lo = (h2 - h_hi.astype(F32)).astype(BF16)
    nt_dims = (((1,), (1,)), ((), ()))
    both = lax.dot_general(wr_ref[...], h_hi, nt_dims, preferred_element_type=F32)
    low = lax.dot_general(wr_ref[0:ne, :], h_lo, nt_dims, preferred_element_type=F32)
    logits = both[0:ne, :] + both[ne:, :] + low + br_ref[...]

    row = lax.broadcasted_iota(jnp.int32, logits.shape, 0)
    work = logits
    vals, idxs, hots = [], [], []
    for _ in range(TOP_K):
        mx = jnp.max(work, axis=0, keepdims=True)
        idx = jnp.min(jnp.where(work == mx, row, ne), axis=0, keepdims=True)
        hot = row == idx
        vals.append(mx)
        idxs.append(idx)
        hots.append(hot)
        work = jnp.where(hot, -jnp.inf, work)
    exps = [jnp.exp(v - vals[0]) for v in vals]
    denom = exps[0] + exps[1] + exps[2] + exps[3]

    sel = jnp.zeros(logits.shape, F32)
    for hot in hots:
        sel = sel + hot.astype(F32)
    prefix = jnp.dot(sel.astype(BF16), upper_ref[...], preferred_element_type=F32) + run_ref[...]
    ranks = [jnp.sum(jnp.where(hot, prefix, 0.0), axis=0, keepdims=True) for hot in hots]
    run_ref[...] = run_ref[...] + jnp.sum(sel, axis=1, keepdims=True)
    cnt_ref[...] = jnp.broadcast_to(run_ref[...], cnt_ref.shape).astype(jnp.int32)

    meta_ref[...] = jnp.concatenate(idxs + [r.astype(jnp.int32) for r in ranks], axis=0)
    gate_ref[...] = jnp.concatenate([e / denom for e in exps] + [jnp.zeros_like(denom)] * TOP_K, axis=0)


def _out_router(y_r, y_d, x2, mod, norm_w, w_out_bf16, w_router, b_router, seq):
    t, d = x2.shape
    tm = min(ROW_TILE, seq)
    per_batch = seq // tm
    ne = w_router.shape[1]
    w_hi = w_router.astype(BF16)
    w_lo = (w_router - w_hi.astype(F32)).astype(BF16)
    wr_t = jnp.concatenate([w_hi, w_lo], axis=1).T
    idx = jnp.arange(tm, dtype=jnp.int32)
    upper = (idx[:, None] < idx[None, :]).astype(BF16)
    row = lambda w: pl.BlockSpec((tm, w), lambda i: (i, 0))
    col = lambda h: pl.BlockSpec((h, tm), lambda i: (0, i))
    const = lambda a: pl.BlockSpec(a.shape, lambda i: (0,) * a.ndim)
    x1, hp, meta_t, gates_t, counts = pl.pallas_call(
        _out_kernel,
        grid=(t // tm,),
        in_specs=[row(RET_V_W), row(DIFF_V_W), row(d),
                  pl.BlockSpec((1, 6, d), lambda i: (i // per_batch, 0, 0)),
                  const(norm_w), const(w_out_bf16), const(wr_t), const(b_router), const(upper)],
        out_specs=[row(d), row(d // 2), col(2 * TOP_K), col(2 * TOP_K),
                   pl.BlockSpec((ne, LANES), lambda i: (0, 0))],
        out_shape=[jax.ShapeDtypeStruct((t, d), F32),
                   jax.ShapeDtypeStruct((t, d // 2), jnp.uint32),
                   jax.ShapeDtypeStruct((2 * TOP_K, t), jnp.int32),
                   jax.ShapeDtypeStruct((2 * TOP_K, t), F32),
                   jax.ShapeDtypeStruct((ne, LANES), jnp.int32)],
        scratch_shapes=[pltpu.VMEM((ne, 1), F32)],
        compiler_params=pltpu.CompilerParams(
            dimension_semantics=("arbitrary",), vmem_limit_bytes=VMEM_LIMIT),
        name="out_router",
    )(y_r, y_d, x2, mod, norm_w, w_out_bf16, wr_t, b_router, upper)
    return x1, hp, meta_t, gates_t.T, counts[:, 0]


def _pair_split_bias(b1):
    e, f2 = b1.shape
    nt = f2 // PERM_TILE
    g = b1[:, 0::2].reshape(e, nt, 1, PERM_TILE // 2)
    l = b1[:, 1::2].reshape(e, nt, 1, PERM_TILE // 2)
    return jnp.concatenate([g, l], axis=2).reshape(e, 1, f2)


def _slot_tokens(dest_t, n_rows):
    n_tok = dest_t.shape[1]
    assert n_tok & (n_tok - 1) == 0
    flat = dest_t.reshape(-1)
    n_slots = flat.shape[0]
    chunk = min(SC_SLOT_CHUNK, n_slots)
    mesh = plsc.VectorSubcoreMesh(core_axis_name="c", subcore_axis_name="s",
                                  num_cores=SC_CORES, num_subcores=SC_SUBCORES)

    def body(dest_hbm, out_hbm, out_v, dest_v):
        first = jnp.logical_and(lax.axis_index("c") == 0, lax.axis_index("s") == 0)

        @pl.when(first)
        def _():
            zeros = jnp.zeros((SC_LANES,), jnp.int32)

            @plsc.parallel_loop(0, n_rows // SC_LANES, unroll=SC_UNROLL)
            def _(g):
                out_v[pl.ds(g * SC_LANES, SC_LANES)] = zeros

            lane = lax.iota(jnp.int32, SC_LANES)

            @pl.loop(0, n_slots // chunk)
            def _(c):
                pltpu.sync_copy(dest_hbm.at[pl.ds(c * chunk, chunk)], dest_v)

                @plsc.parallel_loop(0, chunk // SC_LANES, unroll=SC_UNROLL)
                def _(g):
                    rows = dest_v[pl.ds(g * SC_LANES, SC_LANES)]
                    slot = lane + (c * chunk + g * SC_LANES)
                    plsc.store_scatter(out_v, [rows], slot & (n_tok - 1))

            pltpu.sync_copy(out_v, out_hbm)

    return pl.kernel(
        body,
        out_type=jax.ShapeDtypeStruct((n_rows,), jnp.int32),
        mesh=mesh,
        scratch_types=[pltpu.VMEM((n_rows,), jnp.int32), pltpu.VMEM((chunk,), jnp.int32)],
        compiler_params=pltpu.CompilerParams(needs_layout_passes=False),
        name="slot_tokens",
    )(flat)


def _expert_kernel(be_ref, nv_ref, nexte_ref, tcur_ref, tnxt_ref, hp_ref, w1_ref, b1_ref, w2_ref, b2_ref,
                   o_ref, hbuf, xb0, xb1, w1buf, w2buf, w2b, sem, wsems):
    b = pl.program_id(0)
    nv = nv_ref[0]
    bm = xb0.shape[0]

    def gather(tok_ref, dst):
        for r in range(bm):
            dst[pl.ds(r, 1), :] = hbuf[pl.ds(tok_ref[0, 0, r], 1), :]

    def weight_copies(e, slot):
        return (pltpu.make_async_copy(w1_ref.at[e], w1buf.at[slot], wsems.at[0, slot]),
                pltpu.make_async_copy(w2_ref.at[e], w2buf.at[slot], wsems.at[1, slot]))

    e = be_ref[b]
    prev_e = be_ref[jnp.maximum(b - 1, 0)]
    first_of_expert = jnp.logical_and(b < nv, jnp.logical_or(b == 0, e != prev_e))
    slot = lax.rem(nexte_ref[N_EXPERTS + e], 2)

    @pl.when(b == 0)
    def _():
        for cp in weight_copies(e, slot):
            cp.start()
        load = pltpu.make_async_copy(hp_ref, hbuf, sem)
        load.start()
        load.wait()
        gather(tcur_ref, xb0)

    @pl.when(first_of_expert)
    def _():
        for cp in weight_copies(e, slot):
            cp.wait()
        nxt = nexte_ref[e]

        @pl.when(nxt >= 0)
        def _():
            for cp in weight_copies(nxt, 1 - slot):
                cp.start()

        w2b[...] = w2buf[slot].astype(BF16)

    def run(cur, nxt_rows):
        x = _unpack_halves(cur[...])
        half = PERM_TILE // 2
        acts = []
        for j in range(w1buf.shape[2] // PERM_TILE):
            cols = slice(j * PERM_TILE, (j + 1) * PERM_TILE)
            h = jnp.dot(x, w1buf[slot, :, cols], preferred_element_type=F32) + b1_ref[0, :, cols]
            glu = jnp.minimum(h[:, :half], SWIGLU_LIMIT)
            lin = jnp.clip(h[:, half:], -SWIGLU_LIMIT, SWIGLU_LIMIT)
            acts.append((glu * jax.nn.sigmoid(SWIGLU_ALPHA * glu) * (lin + 1.0)).astype(BF16))
        act = jnp.concatenate(acts, axis=1)
        y = jnp.dot(act, w2b[...], preferred_element_type=F32) + b2_ref[0]
        _store_row_tiles(o_ref, y)
        gather(tnxt_ref, nxt_rows)

    even = lax.rem(b, 2) == 0

    @pl.when(jnp.logical_and(b < nv, even))
    def _():
        run(xb0, xb1)

    @pl.when(jnp.logical_and(b < nv, jnp.logical_not(even)))
    def _():
        run(xb1, xb0)

    @pl.when(b >= nv)
    def _():
        o_ref[...] = jnp.zeros_like(o_ref)


def _experts(block_e, n_valid, next_e, slot_tok, hp, w1p, b1p, w2, b2):
    d = D_MODEL
    f2 = w1p.shape[2]
    f = w2.shape[1]
    bm = EXPERT_ROWS
    nb = slot_tok.shape[0] // bm
    tok3 = slot_tok.reshape(nb, 1, bm)
    exp = lambda b, be, nv, ne: be[jnp.maximum(jnp.minimum(b, nv[0] - 1), 0)]
    grid_spec = pltpu.PrefetchScalarGridSpec(
        num_scalar_prefetch=3,
        grid=(nb,),
        in_specs=[pl.BlockSpec((1, 1, bm), lambda b, be, nv, ne: (b, 0, 0), memory_space=pltpu.SMEM),
                  pl.BlockSpec((1, 1, bm), lambda b, be, nv, ne: (jnp.minimum(b + 1, nb - 1), 0, 0),
                               memory_space=pltpu.SMEM),
                  pl.BlockSpec(memory_space=pl.ANY),
                  pl.BlockSpec(memory_space=pl.ANY),
                  pl.BlockSpec((1, 1, f2), lambda b, be, nv, ne: (exp(b, be, nv, ne), 0, 0)),
                  pl.BlockSpec(memory_space=pl.ANY),
                  pl.BlockSpec((1, 1, d), lambda b, be, nv, ne: (exp(b, be, nv, ne), 0, 0))],
        out_specs=pl.BlockSpec((bm * ROW_SUB, LANES), lambda b, be, nv, ne: (b, 0)),
        scratch_shapes=[pltpu.VMEM(hp.shape, hp.dtype), pltpu.VMEM((bm, hp.shape[1]), hp.dtype),
                        pltpu.VMEM((bm, hp.shape[1]), hp.dtype),
                        pltpu.VMEM((2,) + w1p.shape[1:], w1p.dtype), pltpu.VMEM((2,) + w2.shape[1:], w2.dtype),
                        pltpu.VMEM(w2.shape[1:], BF16),
                        pltpu.SemaphoreType.DMA(()), pltpu.SemaphoreType.DMA((2, 2))],
    )
    return pl.pallas_call(
        _expert_kernel,
        grid_spec=grid_spec,
        out_shape=jax.ShapeDtypeStruct((nb * bm * ROW_SUB, LANES), F32),
        compiler_params=pltpu.CompilerParams(
            dimension_semantics=("arbitrary",), vmem_limit_bytes=VMEM_LIMIT),
        name="experts",
    )(block_e, n_valid, next_e, tok3, tok3, hp, w1p, b1p, w2, b2)


def _combine_kernel(dcur_ref, dnxt_ref, gate_ref, x1_ref, mod_ref, nf_ref, y_ref, o_ref, buf, sems):
    nt = x1_ref.shape[0]
    i = pl.program_id(0)
    n = pl.num_programs(0)
    slot = lax.rem(i, 2)

    def row_copy(src_row, s, kk, r):
        return pltpu.make_async_copy(y_ref.at[pl.ds(src_row * ROW_SUB, ROW_SUB), :],
                                     buf.at[s, kk, pl.ds(r * ROW_SUB, ROW_SUB), :], sems.at[s])

    def fetch(d_ref, s):
        for j in range(nt):
            for kk in range(TOP_K):
                row_copy(d_ref[0, 0, kk * nt + j], s, kk, j).start(priority=kk % DMA_QUEUES)

    def wait_all(s):
        for kk in range(TOP_K):
            pltpu.make_async_copy(y_ref.at[pl.ds(0, nt * ROW_SUB), :], buf.at[s, kk], sems.at[s]).wait()

    @pl.when(i == 0)
    def _():
        fetch(dcur_ref, 0)

    fetch(dnxt_ref, 1 - slot)
    wait_all(slot)

    g = gate_ref[...]
    parts = []
    for s in range(ROW_SUB):
        acc = buf[slot, 0, pl.ds(s, nt, stride=ROW_SUB), :] * g[:, 0:1]
        for kk in range(1, TOP_K):
            acc = acc + buf[slot, kk, pl.ds(s, nt, stride=ROW_SUB), :] * g[:, kk:kk + 1]
        parts.append(acc)
    moe = jnp.concatenate(parts, axis=1)
    gate2 = mod_ref[0, 5:6, :]
    x2 = x1_ref[...] + gate2 * moe
    o_ref[...] = _rms(x2) * nf_ref[...]

    @pl.when(i == n - 1)
    def _():
        wait_all(1 - slot)


def _combine(dest_t, gates, x1, mod, normf_w, y, seq):
    t, d = x1.shape
    nt = min(COMBINE_TOKENS, seq)
    steps = t // nt
    per_batch = seq // nt
    dest2 = jnp.transpose(dest_t.reshape(TOP_K, steps, nt), (1, 0, 2)).reshape(steps, 1, nt * TOP_K)
    return pl.pallas_call(
        _combine_kernel,
        grid=(steps,),
        in_specs=[pl.BlockSpec((1, 1, nt * TOP_K), lambda i: (i, 0, 0), memory_space=pltpu.SMEM),
                  pl.BlockSpec((1, 1, nt * TOP_K), lambda i: (jnp.minimum(i + 1, steps - 1), 0, 0),
                               memory_space=pltpu.SMEM),
                  pl.BlockSpec((nt, 2 * TOP_K), lambda i: (i, 0)),
                  pl.BlockSpec((nt, d), lambda i: (i, 0)),
                  pl.BlockSpec((1, 6, d), lambda i: (i // per_batch, 0, 0)),
                  pl.BlockSpec((1, d), lambda i: (0, 0)),
                  pl.BlockSpec(memory_space=pl.ANY)],
        out_specs=pl.BlockSpec((nt, d), lambda i: (i, 0)),
        out_shape=jax.ShapeDtypeStruct((t, d), F32),
        scratch_shapes=[pltpu.VMEM((2, TOP_K, nt * ROW_SUB, LANES), F32),
                        pltpu.SemaphoreType.DMA((2,))],
        compiler_params=pltpu.CompilerParams(
            dimension_semantics=("arbitrary",), vmem_limit_bytes=VMEM_LIMIT),
        name="combine",
    )(dest2, dest2, gates, x1, mod, normf_w, y)


def kernel(x, c, w_ada, b_ada, norm1_w, w_in, lam_q1, lam_k1, lam_q2, lam_k2, subln_w, rel_bias,
           w_out, norm2_w, w_router, b_router, w1, b1, w2, b2, normf_w):
    batch, seq, d = x.shape
    t = batch * seq
    x2 = x.reshape(t, d)

    c_pad = jnp.zeros((8, d), F32).at[:batch].set(c)
    mod = _ada(c_pad, w_ada[0], b_ada[0][None, :])[:batch].reshape(batch, 6, d)

    q_d, k_d, vt_d, y_r, w1p = _in_proj(x2, mod, norm1_w[0][None, :], w_in[0].astype(BF16), w1[0], seq)
    y_d = _diff_attention(q_d, k_d, vt_d, rel_bias, lam_q1, lam_k1, lam_q2, lam_k2,
                          subln_w, batch, seq)

    x1, hp, meta, gates, counts = _out_router(y_r, y_d, x2, mod, norm2_w[0][None, :],
                                              w_out[0].astype(BF16), w_router[0],
                                              b_router[0][:, None], seq)

    bm = EXPERT_ROWS
    padded = (counts + bm - 1) // bm * bm
    pad_end = jnp.cumsum(padded)
    pad_start = pad_end - padded
    hot_e = meta[None, :TOP_K, :] == jnp.arange(N_EXPERTS, dtype=jnp.int32)[:, None, None]
    dest_t = jnp.sum(jnp.where(hot_e, pad_start[:, None, None], 0), axis=0) + meta[TOP_K:, :]
    n_rows = (t * TOP_K // bm + N_EXPERTS) * bm
    nb = n_rows // bm
    block_start = jnp.arange(nb, dtype=jnp.int32) * bm
    block_e = jnp.minimum(jnp.sum((pad_end[None, :] <= block_start[:, None]).astype(jnp.int32), axis=1),
                          N_EXPERTS - 1)
    n_valid = (pad_end[-1:] // bm).astype(jnp.int32)
    slot_tok = _slot_tokens(dest_t, n_rows)

    owns = padded > 0
    ids = jnp.arange(N_EXPERTS, dtype=jnp.int32)
    later = jnp.logical_and(owns[None, :], ids[None, :] > ids[:, None])
    next_owner = jnp.min(jnp.where(later, ids[None, :], N_EXPERTS), axis=1)
    next_owner = jnp.where(next_owner == N_EXPERTS, -1, next_owner).astype(jnp.int32)
    ordinal = (jnp.cumsum(owns.astype(jnp.int32)) - 1).astype(jnp.int32)
    next_e = jnp.concatenate([next_owner, jnp.maximum(ordinal, 0)])

    ys = _experts(block_e, n_valid, next_e, slot_tok, hp, w1p, _pair_split_bias(b1[0]),
                  w2[0], b2[0][:, None, :])
    out = _combine(dest_t, gates, x1, mod, normf_w[None, :], ys, seq)
    return out.reshape(batch, seq, d)
```
